```python
import math
import numpy as np
import jax
import jax.numpy as jnp
from jax import lax

D_MODEL = 2048
BATCH = 4
SEQ = 4096
DEPTH = 2

CHUNK = 64
Q_BLOCK = 128
ROPE_THETA = 500000.0
HEAD_DIM = 128
BRANCH_HEADS = 4
N_BRANCHES = 4
BRANCH_WIDTH = BRANCH_HEADS * HEAD_DIM
DIFF_QK = 64
DIFF_ROT = DIFF_QK // 4
MLA_Q_LORA = 512
MLA_KV_LORA = 256
MLA_NOPE = 128
MLA_ROPE = 64
N_GROUPS = 4
EXPERTS_PER_GROUP = 8
N_EXPERTS = N_GROUPS * EXPERTS_PER_GROUP
EXPERT_HIDDEN = 512
TOP_K_FINE = 2
MOE_BLOCK = 128
PLE_DIM = 256
LN_EPS = 1e-5
RMS_EPS = 1e-6
DEEPNORM_ALPHA = (2 * DEPTH) ** 0.25
DEEPNORM_BETA = (8 * DEPTH) ** -0.25

IN_WIDTHS = (
    ('diff_q', BRANCH_HEADS * 2 * DIFF_QK),
    ('diff_k', BRANCH_HEADS * 2 * DIFF_QK),
    ('diff_v', BRANCH_WIDTH),
    ('fox_q', BRANCH_WIDTH),
    ('fox_k', BRANCH_WIDTH),
    ('fox_v', BRANCH_WIDTH),
    ('fox_f', BRANCH_HEADS),
    ('mla_cq', MLA_Q_LORA),
    ('mla_ckv', MLA_KV_LORA),
    ('mla_kr', MLA_ROPE),
    ('sb_q', BRANCH_WIDTH),
    ('sb_k', BRANCH_WIDTH),
    ('sb_v', BRANCH_WIDTH),
    ('gates', N_BRANCHES * D_MODEL),
)
IN_DIM = sum(w for _, w in IN_WIDTHS)

kernel_name = 'hybrid_chunk_causal_moe_encoder'


def layer_norm(x, g, b):
    xf = x.astype(jnp.float32)
    mu = jnp.mean(xf, axis=-1, keepdims=True)
    var = jnp.mean(jnp.square(xf - mu), axis=-1, keepdims=True)
    y = (xf - mu) * lax.rsqrt(var + LN_EPS) * g.astype(jnp.float32) + b.astype(jnp.float32)
    return y.astype(x.dtype)


def rms_norm(x, g):
    xf = x.astype(jnp.float32)
    y = xf * lax.rsqrt(jnp.mean(jnp.square(xf), axis=-1, keepdims=True) + RMS_EPS) * g.astype(jnp.float32)
    return y.astype(x.dtype)


def rope(x, pos, rot_dim):
    half = rot_dim // 2
    inv_freq = ROPE_THETA ** (-jnp.arange(half, dtype=jnp.float32) / half)
    ang = pos.astype(jnp.float32)[:, :, None] * inv_freq
    ang = ang.reshape(ang.shape[:2] + (1,) * (x.ndim - 3) + (half,))
    cos, sin = jnp.cos(ang), jnp.sin(ang)
    xr = x[..., :rot_dim].astype(jnp.float32)
    x1, x2 = xr[..., :half], xr[..., half:]
    rot = jnp.concatenate([x1 * cos - x2 * sin, x2 * cos + x1 * sin], axis=-1).astype(x.dtype)
    return jnp.concatenate([rot, x[..., rot_dim:]], axis=-1)


def masked_softmax(scores, mask):
    return jax.nn.softmax(jnp.where(mask, scores, -jnp.inf), axis=-1)


def query_positions(bi):
    return bi * Q_BLOCK + jnp.arange(Q_BLOCK)


def chunk_causal_mask(bi, s):
    return (jnp.arange(s) // CHUNK)[None, :] <= (query_positions(bi) // CHUNK)[:, None]


def sweep_query_blocks(fn, *q_arrays):
    b, s = q_arrays[0].shape[:2]
    nb = s // Q_BLOCK
    blocks = tuple(jnp.moveaxis(a.reshape((b, nb, Q_BLOCK) + a.shape[2:]), 1, 0) for a in q_arrays)
    out = lax.map(lambda args: fn(*args), (jnp.arange(nb),) + blocks)
    out = jnp.moveaxis(out, 0, 1)
    return out.reshape((b, s) + out.shape[3:])


def split_columns(u):
    offsets = np.cumsum([w for _, w in IN_WIDTHS])[:-1].tolist()
    parts = jnp.split(u, offsets, axis=-1)
    return {name: part for (name, _), part in zip(IN_WIDTHS, parts)}


def differential_attention(q, k, v, lam_params, subln_g, pos, layer_idx):
    b, s = q.shape[:2]
    q = rope(q, pos, DIFF_ROT)
    k = rope(k, pos, DIFF_ROT)
    lam_init = 0.8 - 0.6 * math.exp(-0.3 * layer_idx)
    lp = lam_params.astype(jnp.float32)
    lam = jnp.exp(jnp.sum(lp[0] * lp[1])) - jnp.exp(jnp.sum(lp[2] * lp[3])) + lam_init
    scale = DIFF_QK ** -0.5

    def block(bi, qb):
        sc = jnp.einsum('bqhcd,bkhcd->bchqk', qb, k).astype(jnp.float32) * scale
        a = masked_softmax(sc, chunk_causal_mask(bi, s))
        a = a[:, 0] - lam * a[:, 1]
        return jnp.einsum('bhqk,bkhd->bqhd', a.astype(v.dtype), v)

    o = sweep_query_blocks(block, q)
    o = rms_norm(o, subln_g) * (1.0 - lam_init)
    return o.reshape(b, s, BRANCH_WIDTH)


def forgetting_attention(q, k, v, f_logit, f_bias):
    b, s = q.shape[:2]
    log_f = jax.nn.log_sigmoid(f_logit.astype(jnp.float32) + f_bias.astype(jnp.float32))
    cum = jnp.cumsum(log_f, axis=1)
    cum_k = jnp.transpose(cum, (0, 2, 1))
    scale = HEAD_DIM ** -0.5
    key_pos = jnp.arange(s)

    def block(bi, qb, cum_q):
        sc = jnp.einsum('bqhd,bkhd->bhqk', qb, k).astype(jnp.float32) * scale
        sc = sc + jnp.transpose(cum_q, (0, 2, 1))[..., None] - cum_k[:, :, None, :]
        mask = key_pos[None, :] <= query_positions(bi)[:, None]
        a = masked_softmax(sc, mask)
        return jnp.einsum('bhqk,bkhd->bqhd', a.astype(v.dtype), v)

    o = sweep_query_blocks(block, q, cum)
    return o.reshape(b, s, BRANCH_WIDTH)


def latent_attention(cq, ckv, kr, q_norm_g, kv_norm_g, w_uq, w_ukv, pos):
    b, s = cq.shape[:2]
    q = (rms_norm(cq, q_norm_g) @ w_uq).reshape(b, s, BRANCH_HEADS, MLA_NOPE + MLA_ROPE)
    q_nope = q[..., :MLA_NOPE]
    q_rope = rope(q[..., MLA_NOPE:], pos, MLA_ROPE)
    kv = (rms_norm(ckv, kv_norm_g) @ w_ukv).reshape(b, s, BRANCH_HEADS, MLA_NOPE + HEAD_DIM)
    k_nope, v = kv[..., :MLA_NOPE], kv[..., MLA_NOPE:]
    k_rope = rope(kr, pos, MLA_ROPE)
    scale = (MLA_NOPE + MLA_ROPE) ** -0.5

    def block(bi, qn, qr):
        sc = (jnp.einsum('bqhd,bkhd->bhqk', qn, k_nope)
              + jnp.einsum('bqhr,bkr->bhqk', qr, k_rope)).astype(jnp.float32) * scale
        a = masked_softmax(sc, chunk_causal_mask(bi, s))
        return jnp.einsum('bhqk,bkhd->bqhd', a.astype(v.dtype), v)

    o = sweep_query_blocks(block, q_nope, q_rope)
    return o.reshape(b, s, BRANCH_WIDTH)


def stick_breaking_attention(q, k, v):
    b, s = q.shape[:2]
    scale = HEAD_DIM ** -0.5
    key_pos = jnp.arange(s)

    def block(bi, qb):
        z = jnp.einsum('bqhd,bkhd->bhqk', qb, k).astype(jnp.float32) * scale
        mask = key_pos[None, :] < query_positions(bi)[:, None]
        log_beta = jax.nn.log_sigmoid(z)
        log_keep = jnp.where(mask, jax.nn.log_sigmoid(-z), 0.0)
        later = lax.cumsum(log_keep, axis=3, reverse=True) - log_keep
        w = jnp.where(mask, jnp.exp(log_beta + later), 0.0)
        return jnp.einsum('bhqk,bkhd->bqhd', w.astype(v.dtype), v)

    o = sweep_query_blocks(block, q)
    return o.reshape(b, s, BRANCH_WIDTH)


def hybrid_mixer(x, positions, w_in, fox_f_bias, diff_lambda, diff_subln, mla_q_norm,
                 mla_kv_norm, mla_w_uq, mla_w_ukv, w_branch, w_o, layer_idx):
    b, s, _ = x.shape
    u = split_columns(x @ w_in)

    def heads(t):
        return t.reshape(b, s, BRANCH_HEADS, HEAD_DIM)

    o_diff = differential_attention(
        u['diff_q'].reshape(b, s, BRANCH_HEADS, 2, DIFF_QK),
        u['diff_k'].reshape(b, s, BRANCH_HEADS, 2, DIFF_QK),
        heads(u['diff_v']), diff_lambda, diff_subln, positions, layer_idx)
    o_fox = forgetting_attention(heads(u['fox_q']), heads(u['fox_k']), heads(u['fox_v']),
                                 u['fox_f'], fox_f_bias)
    o_mla = latent_attention(u['mla_cq'], u['mla_ckv'], u['mla_kr'], mla_q_norm, mla_kv_norm,
                             mla_w_uq, mla_w_ukv, positions)
    o_sb = stick_breaking_attention(heads(u['sb_q']), heads(u['sb_k']), heads(u['sb_v']))

    gates = jax.nn.sigmoid(u['gates'].reshape(b, s, N_BRANCHES, D_MODEL))
    merged = jnp.zeros_like(x)
    for n, o in enumerate((o_diff, o_fox, o_mla, o_sb)):
        merged = merged + gates[:, :, n] * (o @ w_branch[n])
    return merged @ w_o


def routed_experts(xt, expert_idx, gate_w, w_gate, w_up, w_down):
    t, d = xt.shape
    a = t * TOP_K_FINE
    flat_e = expert_idx.reshape(a)
    flat_tok = jnp.repeat(jnp.arange(t, dtype=jnp.int32), TOP_K_FINE)
    flat_w = gate_w.reshape(a)
    order = jnp.argsort(flat_e)
    e_sorted, tok_sorted, w_sorted = flat_e[order], flat_tok[order], flat_w[order]
    sizes = jnp.bincount(flat_e, length=N_EXPERTS)
    padded = (sizes + MOE_BLOCK - 1) // MOE_BLOCK * MOE_BLOCK
    pad_end = jnp.cumsum(padded)
    pad_start = pad_end - padded
    start = jnp.cumsum(sizes) - sizes
    dest = pad_start[e_sorted] + jnp.arange(a) - start[e_sorted]
    n_rows = a + N_EXPERTS * MOE_BLOCK
    n_blk = n_rows // MOE_BLOCK
    rows = jnp.zeros((n_rows, d), xt.dtype).at[dest].set(xt[tok_sorted])
    blk_e = jnp.minimum(jnp.searchsorted(pad_end, jnp.arange(n_blk) * MOE_BLOCK, side='right'),
                        N_EXPERTS - 1)

    def expert_block(args):
        xb, e = args
        hid = jax.nn.silu(xb @ w_gate[e]) * (xb @ w_up[e])
        return hid @ w_down[e]

    y_rows = lax.map(expert_block, (rows.reshape(n_blk, MOE_BLOCK, d), blk_e)).reshape(n_rows, d)
    y_assign = y_rows[dest] * w_sorted[:, None].astype(xt.dtype)
    return jax.ops.segment_sum(y_assign, tok_sorted, num_segments=t)


def hierarchical_moe(h, w_rg, b_rg, w_re, b_re, w_eg, w_eu, w_ed):
    b, s, d = h.shape
    t = b * s
    xt = h.reshape(t, d)
    g_logits = (xt @ w_rg).astype(jnp.float32) + b_rg.astype(jnp.float32)
    grp = jnp.argmax(g_logits, axis=-1)
    p_grp = jnp.take_along_axis(jax.nn.softmax(g_logits, axis=-1), grp[:, None], axis=-1)
    e_logits = ((xt @ w_re).astype(jnp.float32) + b_re.astype(jnp.float32)).reshape(
        t, N_GROUPS, EXPERTS_PER_GROUP)
    e_in = e_logits[jnp.arange(t), grp]
    top_p, top_i = lax.top_k(jax.nn.softmax(e_in, axis=-1), TOP_K_FINE)
    weights = p_grp * top_p / jnp.sum(top_p, axis=-1, keepdims=True)
    expert_idx = grp[:, None].astype(jnp.int32) * EXPERTS_PER_GROUP + top_i.astype(jnp.int32)
    y = routed_experts(xt, expert_idx, weights, w_eg, w_eu, w_ed)
    return y.reshape(b, s, d)


def setup_inputs(seed: int = 0) -> dict:
    key = jax.random.key(seed)
    ks = jax.random.split(key, 32)
    d = D_MODEL

    def normal(i, shape, scale):
        return jax.random.normal(ks[i], shape, jnp.float32) * scale

    x = normal(0, (BATCH, SEQ, d), 1.0)
    p = normal(1, (DEPTH, BATCH, SEQ, PLE_DIM), 1.0)
    stream_start = jax.random.randint(ks[2], (BATCH, 1), 0, 64, dtype=jnp.int32) * CHUNK
    positions = (stream_start + jnp.arange(SEQ, dtype=jnp.int32)[None, :]).astype(jnp.int32)
    return {
        'x': x,
        'p': p,
        'positions': positions,
        'w_in': normal(3, (DEPTH, d, IN_DIM), d ** -0.5),
        'fox_f_bias': normal(4, (DEPTH, BRANCH_HEADS), 0.1),
        'diff_lambda': normal(5, (DEPTH, 4, DIFF_QK), 0.1),
        'diff_subln': 1.0 + normal(6, (DEPTH, HEAD_DIM), 0.02),
        'mla_q_norm': 1.0 + normal(7, (DEPTH, MLA_Q_LORA), 0.02),
        'mla_kv_norm': 1.0 + normal(8, (DEPTH, MLA_KV_LORA), 0.02),
        'mla_w_uq': normal(9, (DEPTH, MLA_Q_LORA, BRANCH_HEADS * (MLA_NOPE + MLA_ROPE)), MLA_Q_LORA ** -0.5),
        'mla_w_ukv': normal(10, (DEPTH, MLA_KV_LORA, BRANCH_HEADS * (MLA_NOPE + HEAD_DIM)), MLA_KV_LORA ** -0.5),
        'w_branch': normal(11, (DEPTH, N_BRANCHES, BRANCH_WIDTH, d), BRANCH_WIDTH ** -0.5),
        'w_o': normal(12, (DEPTH, d, d), DEEPNORM_BETA * d ** -0.5),
        'ln1_g': 1.0 + normal(13, (DEPTH, d), 0.02),
        'ln1_b': normal(14, (DEPTH, d), 0.02),
        'w_router_group': normal(15, (DEPTH, d, N_GROUPS), d ** -0.5),
        'b_router_group': normal(16, (DEPTH, N_GROUPS), 0.01),
        'w_router_expert': normal(17, (DEPTH, d, N_EXPERTS), d ** -0.5),
        'b_router_expert': normal(18, (DEPTH, N_EXPERTS), 0.01),
        'w_expert_gate': normal(19, (DEPTH, N_EXPERTS, d, EXPERT_HIDDEN), d ** -0.5),
        'w_expert_up': normal(20, (DEPTH, N_EXPERTS, d, EXPERT_HIDDEN), d ** -0.5),
        'w_expert_down': normal(21, (DEPTH, N_EXPERTS, EXPERT_HIDDEN, d), DEEPNORM_BETA * EXPERT_HIDDEN ** -0.5),
        'w_ple_gate': normal(22, (DEPTH, d, d), d ** -0.5),
        'w_ple_proj': normal(23, (DEPTH, PLE_DIM, d), DEEPNORM_BETA * PLE_DIM ** -0.5),
        'ln2_g': 1.0 + normal(24, (DEPTH, d), 0.02),
        'ln2_b': normal(25, (DEPTH, d), 0.02),
    }


def reference(x, p, positions, w_in, fox_f_bias, diff_lambda, diff_subln, mla_q_norm,
              mla_kv_norm, mla_w_uq, mla_w_ukv, w_branch, w_o, ln1_g, ln1_b,
              w_router_group, b_router_group, w_router_expert, b_router_expert,
              w_expert_gate, w_expert_up, w_expert_down, w_ple_gate, w_ple_proj,
              ln2_g, ln2_b):
    for i in range(DEPTH):
        mix = hybrid_mixer(x, positions, w_in[i], fox_f_bias[i], diff_lambda[i], diff_subln[i],
                           mla_q_norm[i], mla_kv_norm[i], mla_w_uq[i], mla_w_ukv[i],
                           w_branch[i], w_o[i], i)
        h = layer_norm(DEEPNORM_ALPHA * x + mix, ln1_g[i], ln1_b[i])
        ffn = hierarchical_moe(h, w_router_group[i], b_router_group[i], w_router_expert[i],
                               b_router_expert[i], w_expert_gate[i], w_expert_up[i],
                               w_expert_down[i])
        ple = jax.nn.sigmoid(h @ w_ple_gate[i]) * (p[i] @ w_ple_proj[i])
        x = layer_norm(DEEPNORM_ALPHA * h + ffn + ple, ln2_g[i], ln2_b[i])
    return x
```

```python
import functools
import math

import jax
import jax.numpy as jnp
from jax import lax
from jax.experimental import pallas as pl
from jax.experimental.pallas import tpu as pltpu

HEAD_DIM = 128
HEADS = 4
N_BRANCHES = 4
BRANCH_WIDTH = HEADS * HEAD_DIM
DIFF_QK = 64
DIFF_ROT = DIFF_QK // 4
MLA_Q_LORA = 512
MLA_KV_LORA = 256
MLA_NOPE = 128
MLA_ROPE = 64
MLA_QK_PAD = 256
N_GROUPS = 4
EXPERTS_PER_GROUP = 8
N_EXPERTS = N_GROUPS * EXPERTS_PER_GROUP
TOP_K = 2
CHUNK = 64
ROPE_THETA = 500000.0
LN_EPS = 1e-5
RMS_EPS = 1e-6
NEG_BIG = -1e30
LANES = 128
V7X_VMEM_LIMIT = 56 * 1024 * 1024

COL_GATES = 0
COL_DQ = 8192
COL_DV = COL_DQ + 1024
COL_FQ = COL_DV + 512
COL_FK = COL_FQ + 512
COL_FV = COL_FK + 512
COL_SQ = COL_FV + 512
COL_SK = COL_SQ + 512
COL_SV = COL_SK + 512
COL_CQ = COL_SV + 512
COL_CKV = COL_CQ + 512
COL_KR = COL_CKV + 256
COL_FF = COL_KR + 128
U_WIDTH = COL_FF + 128

ATT_TQ = 256
ATT_TK = 256
MOE_ROWS = 256


def _cparams(sem, vmem=V7X_VMEM_LIMIT):
    return pltpu.CompilerParams(dimension_semantics=sem, vmem_limit_bytes=vmem)


def _resident(shape, index_map):
    return pl.BlockSpec(shape, index_map, pipeline_mode=pl.Buffered(1))


def _matmul_body(x_ref, w_ref, o_ref):
    o_ref[...] = jnp.dot(x_ref[...], w_ref[...],
                         preferred_element_type=jnp.float32).astype(o_ref.dtype)


def _matmul(x, w, out_dtype, tm, tn, name):
    m, k = x.shape
    n = w.shape[1]
    tm, tn = min(tm, m), min(tn, n)
    return pl.pallas_call(
        _matmul_body,
        out_shape=jax.ShapeDtypeStruct((m, n), out_dtype),
        grid=(m // tm, n // tn),
        in_specs=[pl.BlockSpec((tm, k), lambda i, j: (i, 0)),
                  pl.BlockSpec((k, tn), lambda i, j: (0, j))],
        out_specs=pl.BlockSpec((tm, tn), lambda i, j: (i, j)),
        compiler_params=_cparams(("parallel", "arbitrary")),
        name=name,
    )(x, w)


def _split3(x):
    hi = x.astype(jnp.bfloat16)
    r1 = x - hi.astype(jnp.float32)
    mid = r1.astype(jnp.bfloat16)
    lo = (r1 - mid.astype(jnp.float32)).astype(jnp.bfloat16)
    return hi, mid, lo


def _dot(a, b):
    return jnp.dot(a, b, preferred_element_type=jnp.float32)


def _dot_nt(a, b):
    return lax.dot_general(a, b, (((1,), (1,)), ((), ())), preferred_element_type=jnp.float32)


def _log_sigmoid(z):
    return jnp.minimum(z, 0.0) - jnp.log1p(jnp.exp(-jnp.abs(z)))


def _sigmoid(z):
    return 1.0 / (1.0 + jnp.exp(-z))


def _fgate_body(x_ref, w_ref, b_ref, o_ref, carry_ref):
    @pl.when(pl.program_id(1) == 0)
    def _():
        carry_ref[...] = jnp.zeros_like(carry_ref)

    f = _dot(x_ref[...], w_ref[...]) + b_ref[...]
    log_f = _log_sigmoid(f)
    ts = log_f.shape[0]
    row = lax.broadcasted_iota(jnp.int32, (ts, ts), 0)
    col = lax.broadcasted_iota(jnp.int32, (ts, ts), 1)
    tri = (col <= row).astype(jnp.bfloat16)
    hi, mid, lo = _split3(log_f)
    cum = _dot(tri, hi) + _dot(tri, mid) + _dot(tri, lo) + carry_ref[...]
    o_ref[...] = cum
    carry_ref[...] = cum[ts - 1:ts, :]


def _fgate_cumsum(xb, w_f, b_f, batch, seq):
    t, d = xb.shape
    ts = min(512, seq)
    ns = seq // ts
    return pl.pallas_call(
        _fgate_body,
        out_shape=jax.ShapeDtypeStruct((t, LANES), jnp.float32),
        grid=(batch, ns),
        in_specs=[pl.BlockSpec((ts, d), lambda b, s: (b * ns + s, 0)),
                  _resident((d, LANES), lambda b, s: (0, 0)),
                  _resident((1, LANES), lambda b, s: (0, 0))],
        out_specs=pl.BlockSpec((ts, LANES), lambda b, s: (b * ns + s, 0)),
        scratch_shapes=[pltpu.VMEM((1, LANES), jnp.float32)],
        compiler_params=_cparams(("parallel", "arbitrary")),
        name="fgate_cumsum",
    )(xb, w_f, b_f)


def _rope_lanes(x, c, s_up, s_dn, half):
    return x * c + pltpu.roll(x, LANES - half, 1) * s_up + pltpu.roll(x, half, 1) * s_dn


def _prep_body(dqk_ref, cq_ref, ckv_ref, kr_ref, c16_ref, su16_ref, sd16_ref,
               c64_ref, su64_ref, sd64_ref, qg_ref, kvg_ref, wuq_ref, wuk_ref, wuv_ref,
               dqk_o, mq_o, mk_o, mv_o):
    c16, su16, sd16 = c16_ref[...], su16_ref[...], sd16_ref[...]
    c64, su64, sd64 = c64_ref[...], su64_ref[...], sd64_ref[...]
    for j in range(dqk_ref.shape[1] // LANES):
        sl = slice(j * LANES, (j + 1) * LANES)
        x = dqk_ref[:, sl].astype(jnp.float32)
        dqk_o[:, sl] = _rope_lanes(x, c16, su16, sd16, DIFF_ROT // 2).astype(dqk_o.dtype)

    cq = cq_ref[...].astype(jnp.float32)
    cqn = cq * lax.rsqrt(jnp.mean(cq * cq, axis=-1, keepdims=True) + RMS_EPS) * qg_ref[...]
    q = _dot(cqn.astype(jnp.bfloat16), wuq_ref[...])
    ckv = ckv_ref[...].astype(jnp.float32)
    ckvn = (ckv * lax.rsqrt(jnp.mean(ckv * ckv, axis=-1, keepdims=True) + RMS_EPS)
            * kvg_ref[...]).astype(jnp.bfloat16)
    kn = _dot(ckvn, wuk_ref[...])
    mv_o[...] = _dot(ckvn, wuv_ref[...]).astype(mv_o.dtype)
    kr = _rope_lanes(kr_ref[...].astype(jnp.float32), c64, su64, sd64, MLA_ROPE // 2)
    for h in range(HEADS):
        base = h * MLA_QK_PAD
        mq_o[:, base:base + LANES] = q[:, base:base + LANES].astype(mq_o.dtype)
        qr = _rope_lanes(q[:, base + LANES:base + 2 * LANES], c64, su64, sd64, MLA_ROPE // 2)
        mq_o[:, base + LANES:base + 2 * LANES] = qr.astype(mq_o.dtype)
        mk_o[:, base:base + LANES] = kn[:, h * LANES:(h + 1) * LANES].astype(mk_o.dtype)
        mk_o[:, base + LANES:base + 2 * LANES] = kr.astype(mk_o.dtype)


def _prep(u, tabs, q_norm, kv_norm, w_uq, w_uk, w_uv):
    t = u.shape[0]
    tm = min(512, t)
    row = lambda blk: (lambda i: (i, blk))
    const = lambda i: (0, 0)
    tab_spec = pl.BlockSpec((tm, LANES), lambda i: (i, 0))
    outs = pl.pallas_call(
        _prep_body,
        out_shape=(jax.ShapeDtypeStruct((t, 1024), jnp.bfloat16),
                   jax.ShapeDtypeStruct((t, HEADS * MLA_QK_PAD), jnp.bfloat16),
                   jax.ShapeDtypeStruct((t, HEADS * MLA_QK_PAD), jnp.bfloat16),
                   jax.ShapeDtypeStruct((t, BRANCH_WIDTH), jnp.bfloat16)),
        grid=(t // tm,),
        in_specs=[pl.BlockSpec((tm, 1024), row(COL_DQ // 1024)),
                  pl.BlockSpec((tm, MLA_Q_LORA), row(COL_CQ // MLA_Q_LORA)),
                  pl.BlockSpec((tm, MLA_KV_LORA), row(COL_CKV // MLA_KV_LORA)),
                  pl.BlockSpec((tm, LANES), row(COL_KR // LANES)),
                  tab_spec, tab_spec, tab_spec, tab_spec, tab_spec, tab_spec,
                  _resident((1, MLA_Q_LORA), const),
                  _resident((1, MLA_KV_LORA), const),
                  _resident(w_uq.shape, const),
                  _resident(w_uk.shape, const),
                  _resident(w_uv.shape, const)],
        out_specs=(pl.BlockSpec((tm, 1024), lambda i: (i, 0)),
                   pl.BlockSpec((tm, HEADS * MLA_QK_PAD), lambda i: (i, 0)),
                   pl.BlockSpec((tm, HEADS * MLA_QK_PAD), lambda i: (i, 0)),
                   pl.BlockSpec((tm, BRANCH_WIDTH), lambda i: (i, 0))),
        compiler_params=_cparams(("parallel",)),
        name="prep",
    )(u, u, u, u, *tabs, q_norm, kv_norm, w_uq, w_uk, w_uv)
    return outs


def _tile_iotas(tk, tq):
    j = lax.broadcasted_iota(jnp.int32, (tk, tq), 0)
    i = lax.broadcasted_iota(jnp.int32, (tk, tq), 1)
    return j, i


def _softmax_tile(s_t, v_t, m_ref, l_ref, acc_ref):
    m_old = m_ref[...]
    m_new = jnp.maximum(m_old, jnp.max(s_t, axis=0, keepdims=True))
    alpha = jnp.exp(m_old - m_new)
    p_t = jnp.exp(s_t - m_new)
    l_ref[...] = alpha * l_ref[...] + jnp.sum(p_t, axis=0, keepdims=True)
    pv = lax.dot_general(v_t, p_t.astype(jnp.bfloat16), (((0,), (0,)), ((), ())),
                         preferred_element_type=jnp.float32)
    acc_ref[...] = alpha * acc_ref[...] + pv
    m_ref[...] = m_new


def _attn_body(*refs, kind, scale, tq, tk, lam_init):
    if kind == "diff":
        lam_ref, q_ref, k_ref, v_ref, g_ref, o_ref, m_ref, l_ref, acc_ref = refs
    elif kind == "fox":
        q_ref, k_ref, v_ref, cq_ref, ck_ref, o_ref, m_ref, l_ref, acc_ref = refs
    elif kind == "mla":
        q_ref, k_ref, v_ref, o_ref, m_ref, l_ref, acc_ref = refs
    else:
        q_ref, k_ref, v_ref, o_ref, r_ref, acc_ref = refs
    qi = pl.program_id(2)
    q = q_ref[...]
    n_comp = 2 if kind == "diff" else 1
    if kind == "diff":
        lane = lax.broadcasted_iota(jnp.int32, q.shape, 1)
        qs = [jnp.where(lane < DIFF_QK, q, jnp.zeros_like(q)),
              jnp.where(lane >= DIFF_QK, q, jnp.zeros_like(q))]
    else:
        qs = [q]

    acc_ref[...] = jnp.zeros_like(acc_ref)
    if kind == "sb":
        r_ref[...] = jnp.zeros_like(r_ref)
    else:
        m_ref[...] = jnp.full_like(m_ref, NEG_BIG)
        l_ref[...] = jnp.zeros_like(l_ref)

    def tile(kb, masked):
        k0 = pl.multiple_of(kb * tk, tk)
        k_t = k_ref[pl.ds(k0, tk), :]
        v_t = v_ref[pl.ds(k0, tk), :]
        if masked:
            j, i = _tile_iotas(tk, tq)
            if kind in ("diff", "mla"):
                mask = (j // CHUNK) <= (i // CHUNK)
            elif kind == "fox":
                mask = j <= i
            else:
                mask = j < i
        if kind == "sb":
            z_t = _dot_nt(k_t, q) * scale
            ls = _log_sigmoid(z_t)
            lk = ls - z_t
            if masked:
                lk = jnp.where(mask, lk, 0.0)
            jj = lax.broadcasted_iota(jnp.int32, (tk, tk), 0)
            mm = lax.broadcasted_iota(jnp.int32, (tk, tk), 1)
            upper = (mm > jj).astype(jnp.bfloat16)
            hi, mid, lo = _split3(lk)
            later = r_ref[...] + _dot(upper, hi) + _dot(upper, mid) + _dot(upper, lo)
            w_t = jnp.exp(ls + later)
            if masked:
                w_t = jnp.where(mask, w_t, 0.0)
            acc_ref[...] += lax.dot_general(v_t, w_t.astype(jnp.bfloat16),
                                            (((0,), (0,)), ((), ())),
                                            preferred_element_type=jnp.float32)
            r_ref[...] += jnp.sum(lk, axis=0, keepdims=True)
            return
        for c in range(n_comp):
            s_t = _dot_nt(k_t, qs[c]) * scale
            if kind == "fox":
                ck = ck_ref[pl.ds(k0, tk), :]
                s_t = s_t + cq_ref[...] - jnp.concatenate([ck] * (tq // LANES), axis=1)
            if masked:
                s_t = jnp.where(mask, s_t, NEG_BIG)
            _softmax_tile(s_t, v_t, m_ref.at[c], l_ref.at[c], acc_ref.at[c])

    tile(qi, True)

    if kind == "sb":
        def body(t, carry):
            tile(qi - 1 - t, False)
            return carry
    else:
        def body(t, carry):
            tile(t, False)
            return carry
    lax.fori_loop(0, qi, body, 0)

    if kind == "sb":
        o_t = acc_ref[...]
    elif kind == "diff":
        o_t = acc_ref[0] / l_ref[0] - lam_ref[0] * (acc_ref[1] / l_ref[1])
    else:
        o_t = acc_ref[0] / l_ref[0]
    o = o_t.T
    if kind == "diff":
        o = (o * lax.rsqrt(jnp.mean(o * o, axis=-1, keepdims=True) + RMS_EPS)
             * g_ref[...] * (1.0 - lam_init))
    o_ref[...] = o.astype(o_ref.dtype)


def _attention(kind, batch, seq, q_arr, q_col, k_arr, k_col, v_arr, v_col, dq, scale,
               extras=(), lam=None, lam_init=0.0):
    t = q_arr.shape[0]
    tq = min(ATT_TQ, seq)
    tk = tq
    nq = seq // tq
    q_spec = pl.BlockSpec((tq, dq), lambda b, h, i, *_: (b * nq + i, q_col + h))
    k_spec = pl.BlockSpec((seq, dq), lambda b, h, i, *_: (b, k_col + h))
    v_spec = pl.BlockSpec((seq, HEAD_DIM), lambda b, h, i, *_: (b, v_col + h))
    in_specs = [q_spec, k_spec, v_spec]
    args = [q_arr, k_arr, v_arr]
    n_comp = 2 if kind == "diff" else 1
    if kind == "diff":
        in_specs.append(_resident((1, HEAD_DIM), lambda b, h, i, *_: (0, 0)))
        args.append(extras[0])
    elif kind == "fox":
        cq_rows, ck_rep = extras
        in_specs.append(pl.BlockSpec((None, 1, tq), lambda b, h, i: (b * HEADS + h, 0, i)))
        in_specs.append(pl.BlockSpec((None, seq, LANES), lambda b, h, i: (b * HEADS + h, 0, 0)))
        args += [cq_rows, ck_rep]
    if kind == "sb":
        scratch = [pltpu.VMEM((1, tq), jnp.float32), pltpu.VMEM((HEAD_DIM, tq), jnp.float32)]
    else:
        scratch = [pltpu.VMEM((n_comp, 1, tq), jnp.float32),
                   pltpu.VMEM((n_comp, 1, tq), jnp.float32),
                   pltpu.VMEM((n_comp, HEAD_DIM, tq), jnp.float32)]
    body = functools.partial(_attn_body, kind=kind, scale=scale, tq=tq, tk=tk, lam_init=lam_init)
    grid = (batch, HEADS, nq)
    out_shape = jax.ShapeDtypeStruct((t, BRANCH_WIDTH), jnp.bfloat16)
    out_spec = pl.BlockSpec((tq, HEAD_DIM), lambda b, h, i, *_: (b * nq + i, h))
    cp = _cparams(("parallel", "parallel", "arbitrary"))
    if kind == "diff":
        return pl.pallas_call(
            body, out_shape=out_shape,
            grid_spec=pltpu.PrefetchScalarGridSpec(
                num_scalar_prefetch=1, grid=grid, in_specs=in_specs, out_specs=out_spec,
                scratch_shapes=scratch),
            compiler_params=cp, name="attn_" + kind)(lam, *args)
    return pl.pallas_call(
        body, out_shape=out_shape, grid=grid, in_specs=in_specs, out_specs=out_spec,
        scratch_shapes=scratch, compiler_params=cp, name="attn_" + kind)(*args)


def _layer_norm(y, g, b):
    mu = jnp.mean(y, axis=-1, keepdims=True)
    yc = y - mu
    var = jnp.mean(yc * yc, axis=-1, keepdims=True)
    return yc * lax.rsqrt(var + LN_EPS) * g + b


def _merge_body(g_ref, od_ref, of_ref, om_ref, os_ref, wb_ref, wo_ref, x_ref, lg_ref, lb_ref,
                wrh_ref, wrl_ref, h_o, hb_o, lgt_o, *, alpha):
    d = x_ref.shape[1]
    merged = None
    for n, o_ref in enumerate((od_ref, of_ref, om_ref, os_ref)):
        gate = _sigmoid(g_ref[:, n * d:(n + 1) * d].astype(jnp.float32))
        term = gate * _dot(o_ref[...], wb_ref[n])
        merged = term if merged is None else merged + term
    mix = _dot(merged.astype(jnp.bfloat16), wo_ref[...])
    h = _layer_norm(alpha * x_ref[...] + mix, lg_ref[...], lb_ref[...])
    h_o[...] = h
    h_hi = h.astype(jnp.bfloat16)
    hb_o[...] = h_hi
    h_lo = (h - h_hi.astype(jnp.float32)).astype(jnp.bfloat16)
    lgt_o[...] = (_dot(h_hi, wrh_ref[...]) + _dot(h_lo, wrh_ref[...]) + _dot(h_hi, wrl_ref[...]))


def _merge(u, o_d, o_f, o_m, o_s, w_b, w_o, x, ln_g, ln_b, wr_hi, wr_lo, alpha):
    t, d = x.shape
    tm = min(256, t)
    row = lambda i: (i, 0)
    const2 = lambda i: (0, 0)
    o_spec = pl.BlockSpec((tm, BRANCH_WIDTH), row)
    return pl.pallas_call(
        functools.partial(_merge_body, alpha=alpha),
        out_shape=(jax.ShapeDtypeStruct((t, d), jnp.float32),
                   jax.ShapeDtypeStruct((t, d), jnp.bfloat16),
                   jax.ShapeDtypeStruct((t, LANES), jnp.float32)),
        grid=(t // tm,),
        in_specs=[pl.BlockSpec((tm, N_BRANCHES * d), row),
                  o_spec, o_spec, o_spec, o_spec,
                  _resident(w_b.shape, lambda i: (0, 0, 0)),
                  _resident(w_o.shape, const2),
                  pl.BlockSpec((tm, d), row),
                  _resident((1, d), const2), _resident((1, d), const2),
                  _resident(wr_hi.shape, const2), _resident(wr_lo.shape, const2)],
        out_specs=(pl.BlockSpec((tm, d), row), pl.BlockSpec((tm, d), row),
                   pl.BlockSpec((tm, LANES), row)),
        compiler_params=_cparams(("parallel",)),
        name="merge_ln1",
    )(u, o_d, o_f, o_m, o_s, w_b, w_o, x, ln_g, ln_b, wr_hi, wr_lo)


def _moe_body(blk_e_ref, n_used_ref, x_ref, wg_ref, wu_ref, wd_ref, y_ref):
    i = pl.program_id(0)

    @pl.when(i < n_used_ref[0])
    def _():
        x = x_ref[...]
        a = _dot(x, wg_ref[...])
        hid = (a * _sigmoid(a)) * _dot(x, wu_ref[...])
        y_ref[...] = _dot(hid.astype(jnp.bfloat16), wd_ref[...])

    @pl.when(i >= n_used_ref[0])
    def _():
        y_ref[...] = jnp.zeros_like(y_ref)


def _moe(blk_e, n_used, rows, w_g, w_u, w_d):
    n_rows, d = rows.shape
    hid = w_g.shape[2]
    bm = MOE_ROWS
    return pl.pallas_call(
        _moe_body,
        out_shape=jax.ShapeDtypeStruct((n_rows, d), jnp.float32),
        grid_spec=pltpu.PrefetchScalarGridSpec(
            num_scalar_prefetch=2, grid=(n_rows // bm,),
            in_specs=[pl.BlockSpec((bm, d), lambda i, e, n: (i, 0)),
                      pl.BlockSpec((None, d, hid), lambda i, e, n: (e[i], 0, 0)),
                      pl.BlockSpec((None, d, hid), lambda i, e, n: (e[i], 0, 0)),
                      pl.BlockSpec((None, hid, d), lambda i, e, n: (e[i], 0, 0))],
            out_specs=pl.BlockSpec((bm, d), lambda i, e, n: (i, 0))),
        compiler_params=_cparams(("arbitrary",)),
        name="moe_experts",
    )(blk_e, n_used, rows, w_g, w_u, w_d)


def _final_body(h_ref, hb_ref, f_ref, p_ref, wg_ref, wp_ref, lg_ref, lb_ref, y_o, yb_o, *, alpha):
    gate = _sigmoid(_dot(hb_ref[...], wg_ref[...]))
    proj = _dot(p_ref[...].astype(jnp.bfloat16), wp_ref[...])
    y = _layer_norm(alpha * h_ref[...] + f_ref[...] + gate * proj, lg_ref[...], lb_ref[...])
    y_o[...] = y
    yb_o[...] = y.astype(jnp.bfloat16)


def _final(h, hb, ffn, p, w_pg, w_pp, ln_g, ln_b, alpha):
    t, d = h.shape
    tm = min(256, t)
    row = lambda i: (i, 0)
    const2 = lambda i: (0, 0)
    return pl.pallas_call(
        functools.partial(_final_body, alpha=alpha),
        out_shape=(jax.ShapeDtypeStruct((t, d), jnp.float32),
                   jax.ShapeDtypeStruct((t, d), jnp.bfloat16)),
        grid=(t // tm,),
        in_specs=[pl.BlockSpec((tm, d), row), pl.BlockSpec((tm, d), row),
                  pl.BlockSpec((tm, d), row), pl.BlockSpec((tm, p.shape[1]), row),
                  _resident(w_pg.shape, const2), _resident(w_pp.shape, const2),
                  _resident((1, d), const2), _resident((1, d), const2)],
        out_specs=(pl.BlockSpec((tm, d), row), pl.BlockSpec((tm, d), row)),
        compiler_params=_cparams(("parallel",)),
        name="final_ln2",
    )(h, hb, ffn, p, w_pg, w_pp, ln_g, ln_b)


def _rope_tables(positions):
    pos = positions.reshape(-1).astype(jnp.float32)
    lane = jnp.arange(LANES)

    def tables(rot, period):
        half = rot // 2
        inv_freq = ROPE_THETA ** (-jnp.arange(half, dtype=jnp.float32) / half)
        ang = pos[:, None] * inv_freq
        cos, sin = jnp.cos(ang), jnp.sin(ang)
        lp = lane % period
        idx = lp % half
        in_rot = lp < rot
        c = jnp.where(in_rot[None, :], cos[:, idx], 1.0)
        s_up = jnp.where((lp < half)[None, :], -sin[:, idx], 0.0)
        s_dn = jnp.where(((lp >= half) & in_rot)[None, :], sin[:, idx], 0.0)
        return c, s_up, s_dn

    return tables(DIFF_ROT, DIFF_QK) + tables(MLA_ROPE, LANES)


def _reorder_w_in(w):
    d = w.shape[0]
    widths = [512, 512, 512, 512, 512, 512, HEADS, MLA_Q_LORA, MLA_KV_LORA, MLA_ROPE,
              512, 512, 512, N_BRANCHES * d]
    offs = [0]
    for wd in widths:
        offs.append(offs[-1] + wd)
    part = lambda n: w[:, offs[n]:offs[n + 1]]
    (dq, dk, dv, fq, fk, fv, ff, cq, ckv, kr, sq, sk, sv, gates) = [part(n) for n in range(14)]
    zeros = lambda n: jnp.zeros((d, n), w.dtype)
    cols = [gates, dq, dk, dv, fq, fk, fv, sq, sk, sv, cq, ckv, kr, zeros(LANES - MLA_ROPE),
            ff, zeros(LANES - HEADS)]
    w1 = jnp.concatenate(cols, axis=1).astype(jnp.bfloat16)
    w_f = jnp.concatenate([ff, zeros(LANES - HEADS)], axis=1).astype(jnp.bfloat16)
    return w1, w_f


def _route(logits, b_rg, b_re, t):
    g_logits = logits[:, :N_GROUPS] + b_rg
    grp = jnp.argmax(g_logits, axis=-1)
    p_grp = jnp.take_along_axis(jax.nn.softmax(g_logits, axis=-1), grp[:, None], axis=-1)
    e_logits = (logits[:, N_GROUPS:N_GROUPS + N_EXPERTS] + b_re).reshape(
        t, N_GROUPS, EXPERTS_PER_GROUP)
    e_in = jnp.take_along_axis(e_logits, grp[:, None, None], axis=1)[:, 0]
    top_p, top_i = lax.top_k(jax.nn.softmax(e_in, axis=-1), TOP_K)
    weights = p_grp * top_p / jnp.sum(top_p, axis=-1, keepdims=True)
    expert_idx = grp[:, None].astype(jnp.int32) * EXPERTS_PER_GROUP + top_i.astype(jnp.int32)
    return expert_idx, weights


def _dispatch_plan(expert_idx, t):
    a = t * TOP_K
    flat_e = expert_idx.reshape(a)
    onehot = (flat_e[:, None] == jnp.arange(N_EXPERTS)[None, :]).astype(jnp.int32)
    rank = jnp.take_along_axis(jnp.cumsum(onehot, axis=0), flat_e[:, None], axis=1)[:, 0] - 1
    sizes = jnp.sum(onehot, axis=0)
    padded = (sizes + MOE_ROWS - 1) // MOE_ROWS * MOE_ROWS
    pad_end = jnp.cumsum(padded)
    pad_start = pad_end - padded
    dest = (pad_start[flat_e] + rank).astype(jnp.int32)
    n_rows = a + N_EXPERTS * MOE_ROWS
    n_blk = n_rows // MOE_ROWS
    flat_tok = jnp.arange(a, dtype=jnp.int32) // TOP_K
    row_tok = jnp.zeros((n_rows,), jnp.int32).at[dest].set(flat_tok)
    blk_e = jnp.minimum(jnp.searchsorted(pad_end, jnp.arange(n_blk) * MOE_ROWS, side="right"),
                        N_EXPERTS - 1).astype(jnp.int32)
    n_used = (pad_end[-1] // MOE_ROWS).astype(jnp.int32).reshape(1)
    return dest.reshape(t, TOP_K), row_tok, blk_e, n_used


def kernel(x, p, positions, w_in, fox_f_bias, diff_lambda, diff_subln, mla_q_norm, mla_kv_norm,
           mla_w_uq, mla_w_ukv, w_branch, w_o, ln1_g, ln1_b, w_router_group, b_router_group,
           w_router_expert, b_router_expert, w_expert_gate, w_expert_up, w_expert_down,
           w_ple_gate, w_ple_proj, ln2_g, ln2_b):
    batch, seq, d = x.shape
    depth = w_in.shape[0]
    t = batch * seq
    alpha = (2 * depth) ** 0.25
    bf = jnp.bfloat16
    tabs = _rope_tables(positions)
    xf = x.reshape(t, d)
    xb = xf.astype(bf)

    for i in range(depth):
        w1, w_f = _reorder_w_in(w_in[i])
        u = _matmul(xb, w1, bf, 1024, 1536, "in_proj")

        b_f = jnp.zeros((1, LANES), jnp.float32).at[0, :HEADS].set(fox_f_bias[i])
        cum = _fgate_cumsum(xb, w_f, b_f, batch, seq)[:, :HEADS]
        cum_bh = jnp.transpose(cum.reshape(batch, seq, HEADS), (0, 2, 1)).reshape(
            batch * HEADS, seq)
        cq_rows = cum_bh.reshape(batch * HEADS, 1, seq)
        ck_rep = jnp.broadcast_to(cum_bh[:, :, None], (batch * HEADS, seq, LANES))

        uq = mla_w_uq[i].reshape(MLA_Q_LORA, HEADS, MLA_NOPE + MLA_ROPE)
        uq = jnp.pad(uq, ((0, 0), (0, 0), (0, MLA_QK_PAD - MLA_NOPE - MLA_ROPE)))
        uq = uq.reshape(MLA_Q_LORA, HEADS * MLA_QK_PAD).astype(bf)
        ukv = mla_w_ukv[i].reshape(MLA_KV_LORA, HEADS, MLA_NOPE + HEAD_DIM)
        uk = ukv[:, :, :MLA_NOPE].reshape(MLA_KV_LORA, HEADS * MLA_NOPE).astype(bf)
        uv = ukv[:, :, MLA_NOPE:].reshape(MLA_KV_LORA, HEADS * HEAD_DIM).astype(bf)
        dqk, mq, mk, mv = _prep(u, tabs, mla_q_norm[i].reshape(1, -1),
                                mla_kv_norm[i].reshape(1, -1), uq, uk, uv)

        lam_init = 0.8 - 0.6 * math.exp(-0.3 * i)
        lp = diff_lambda[i].astype(jnp.float32)
        lam = (jnp.exp(jnp.sum(lp[0] * lp[1])) - jnp.exp(jnp.sum(lp[2] * lp[3]))
               + lam_init).reshape(1)
        blk = lambda col: col // HEAD_DIM
        o_d = _attention("diff", batch, seq, dqk, 0, dqk, HEADS, u, blk(COL_DV), HEAD_DIM,
                         DIFF_QK ** -0.5, extras=(diff_subln[i].reshape(1, HEAD_DIM),),
                         lam=lam, lam_init=lam_init)
        o_f = _attention("fox", batch, seq, u, blk(COL_FQ), u, blk(COL_FK), u, blk(COL_FV),
                         HEAD_DIM, HEAD_DIM ** -0.5, extras=(cq_rows, ck_rep))
        o_m = _attention("mla", batch, seq, mq, 0, mk, 0, mv, 0, MLA_QK_PAD,
                         (MLA_NOPE + MLA_ROPE) ** -0.5)
        o_s = _attention("sb", batch, seq, u, blk(COL_SQ), u, blk(COL_SK), u, blk(COL_SV),
                         HEAD_DIM, HEAD_DIM ** -0.5)

        w_r = jnp.concatenate([w_router_group[i], w_router_expert[i]], axis=1)
        w_r = jnp.pad(w_r, ((0, 0), (0, LANES - w_r.shape[1])))
        wr_hi = w_r.astype(bf)
        wr_lo = (w_r - wr_hi.astype(jnp.float32)).astype(bf)
        h, hb, logits = _merge(u, o_d, o_f, o_m, o_s, w_branch[i].astype(bf), w_o[i].astype(bf),
                               xf, ln1_g[i].reshape(1, d), ln1_b[i].reshape(1, d),
                               wr_hi, wr_lo, alpha)

        expert_idx, weights = _route(logits, b_router_group[i], b_router_expert[i], t)
        dest, row_tok, blk_e, n_used = _dispatch_plan(expert_idx, t)
        rows = jnp.take(hb, row_tok, axis=0)
        y_rows = _moe(blk_e, n_used, rows, w_expert_gate[i].astype(bf),
                      w_expert_up[i].astype(bf), w_expert_down[i].astype(bf))
        ffn = (jnp.take(y_rows, dest[:, 0], axis=0) * weights[:, 0:1]
               + jnp.take(y_rows, dest[:, 1], axis=0) * weights[:, 1:2])

        xf, xb = _final(h, hb, ffn, p[i].reshape(t, -1), w_ple_gate[i].astype(bf),
                        w_ple_proj[i].astype(bf), ln2_g[i].reshape(1, d),
                        ln2_b[i].reshape(1, d), alpha)
    return xf.reshape(batch, seq, d)
```

```python
import functools
import math

import jax
import jax.numpy as jnp
from jax import lax
from jax.experimental import pallas as pl
from jax.experimental.pallas import tpu as pltpu

HEAD_DIM = 128
HEADS = 4
N_BRANCHES = 4
BRANCH_WIDTH = HEADS * HEAD_DIM
DIFF_QK = 64
DIFF_ROT = DIFF_QK // 4
MLA_Q_LORA = 512
MLA_KV_LORA = 256
MLA_NOPE = 128
MLA_ROPE = 64
MLA_QK_PAD = 256
N_GROUPS = 4
EXPERTS_PER_GROUP = 8
N_EXPERTS = N_GROUPS * EXPERTS_PER_GROUP
TOP_K = 2
CHUNK = 64
ROPE_THETA = 500000.0
LN_EPS = 1e-5
RMS_EPS = 1e-6
NEG_BIG = -1e30
LOG2E = 1.4426950408889634
LANES = 128
V7X_VMEM_LIMIT = 56 * 1024 * 1024

COL_GATES = 0
COL_DQ = 8192
COL_DV = COL_DQ + 1024
COL_FQ = COL_DV + 512
COL_FK = COL_FQ + 512
COL_FV = COL_FK + 512
COL_SQ = COL_FV + 512
COL_SK = COL_SQ + 512
COL_SV = COL_SK + 512
COL_CQ = COL_SV + 512
COL_CKV = COL_CQ + 512
COL_KR = COL_CKV + 256
COL_FF = COL_KR + 128
U_WIDTH = COL_FF + 128

ATT_TQ = 512
MOE_ROWS = 256


def _cparams(sem, vmem=V7X_VMEM_LIMIT):
    return pltpu.CompilerParams(dimension_semantics=sem, vmem_limit_bytes=vmem)


def _resident(shape, index_map):
    return pl.BlockSpec(shape, index_map, pipeline_mode=pl.Buffered(1))


def _matmul_body(x_ref, w_ref, c_ref, o_ref):
    acc = jnp.dot(x_ref[...], w_ref[...], preferred_element_type=jnp.float32)
    o_ref[...] = (acc * c_ref[...]).astype(o_ref.dtype)


def _matmul(x, w, col_scale, out_dtype, tm, tn, name):
    m, k = x.shape
    n = w.shape[1]
    tm, tn = min(tm, m), min(tn, n)
    return pl.pallas_call(
        _matmul_body,
        out_shape=jax.ShapeDtypeStruct((m, n), out_dtype),
        grid=(m // tm, n // tn),
        in_specs=[pl.BlockSpec((tm, k), lambda i, j: (i, 0)),
                  pl.BlockSpec((k, tn), lambda i, j: (0, j)),
                  pl.BlockSpec((1, tn), lambda i, j: (0, j))],
        out_specs=pl.BlockSpec((tm, tn), lambda i, j: (i, j)),
        compiler_params=_cparams(("parallel", "arbitrary")),
        name=name,
    )(x, w, col_scale)


def _split3(x):
    hi = x.astype(jnp.bfloat16)
    r1 = x - hi.astype(jnp.float32)
    mid = r1.astype(jnp.bfloat16)
    lo = (r1 - mid.astype(jnp.float32)).astype(jnp.bfloat16)
    return hi, mid, lo


def _split2(x):
    hi = x.astype(jnp.bfloat16)
    mid = (x - hi.astype(jnp.float32)).astype(jnp.bfloat16)
    return hi, mid


def _dot(a, b):
    return jnp.dot(a, b, preferred_element_type=jnp.float32)


def _dot_nt(a, b):
    return lax.dot_general(a, b, (((1,), (1,)), ((), ())), preferred_element_type=jnp.float32)


def _dot_tn(a, b):
    return lax.dot_general(a, b, (((0,), (0,)), ((), ())), preferred_element_type=jnp.float32)


def _log_sigmoid(z):
    neg_abs = pltpu.bitcast(pltpu.bitcast(z, jnp.uint32) | jnp.uint32(0x80000000), jnp.float32)
    return jnp.minimum(z, 0.0) - jnp.log(1.0 + jnp.exp(neg_abs))


def _sigmoid(z):
    return 1.0 / (1.0 + jnp.exp(-z))


def _fgate_body(x_ref, w_ref, b_ref, o_ref, carry_ref):
    @pl.when(pl.program_id(1) == 0)
    def _():
        carry_ref[...] = jnp.zeros_like(carry_ref)

    f = _dot(x_ref[...], w_ref[...]) + b_ref[...]
    log_f = _log_sigmoid(f)
    ts = log_f.shape[0]
    row = lax.broadcasted_iota(jnp.int32, (ts, ts), 0)
    col = lax.broadcasted_iota(jnp.int32, (ts, ts), 1)
    tri = (col <= row).astype(jnp.bfloat16)
    hi, mid, lo = _split3(log_f)
    cum = _dot(tri, hi) + _dot(tri, mid) + _dot(tri, lo) + carry_ref[...]
    o_ref[...] = cum
    carry_ref[...] = cum[ts - 1:ts, :]


def _fgate_cumsum(xb, w_f, b_f, batch, seq):
    t, d = xb.shape
    ts = min(512, seq)
    ns = seq // ts
    return pl.pallas_call(
        _fgate_body,
        out_shape=jax.ShapeDtypeStruct((t, LANES), jnp.float32),
        grid=(batch, ns),
        in_specs=[pl.BlockSpec((ts, d), lambda b, s: (b * ns + s, 0)),
                  _resident((d, LANES), lambda b, s: (0, 0)),
                  _resident((1, LANES), lambda b, s: (0, 0))],
        out_specs=pl.BlockSpec((ts, LANES), lambda b, s: (b * ns + s, 0)),
        scratch_shapes=[pltpu.VMEM((1, LANES), jnp.float32)],
        compiler_params=_cparams(("parallel", "arbitrary")),
        name="fgate_cumsum",
    )(xb, w_f, b_f)


def _rope_lanes(x, c, s_up, s_dn, half):
    return x * c + pltpu.roll(x, LANES - half, 1) * s_up + pltpu.roll(x, half, 1) * s_dn


def _prep_body(dqk_ref, cq_ref, ckv_ref, kr_ref, c16_ref, su16_ref, sd16_ref,
               c64_ref, su64_ref, sd64_ref, qg_ref, kvg_ref, wuq_ref, wuk_ref, wuv_ref,
               dqk_o, mq_o, mk_o, mv_o):
    c16, su16, sd16 = c16_ref[...], su16_ref[...], sd16_ref[...]
    c64, su64, sd64 = c64_ref[...], su64_ref[...], sd64_ref[...]
    for j in range(dqk_ref.shape[1] // LANES):
        sl = slice(j * LANES, (j + 1) * LANES)
        x = _rope_lanes(dqk_ref[:, sl].astype(jnp.float32), c16, su16, sd16, DIFF_ROT // 2)
        if j < HEADS:
            x = x * (DIFF_QK ** -0.5 * LOG2E)
        dqk_o[:, sl] = x.astype(dqk_o.dtype)

    cq = cq_ref[...].astype(jnp.float32)
    cqn = cq * lax.rsqrt(jnp.mean(cq * cq, axis=-1, keepdims=True) + RMS_EPS) * qg_ref[...]
    q = _dot(cqn.astype(jnp.bfloat16), wuq_ref[...])
    q = q * ((MLA_NOPE + MLA_ROPE) ** -0.5 * LOG2E)
    ckv = ckv_ref[...].astype(jnp.float32)
    ckvn = (ckv * lax.rsqrt(jnp.mean(ckv * ckv, axis=-1, keepdims=True) + RMS_EPS)
            * kvg_ref[...]).astype(jnp.bfloat16)
    kn = _dot(ckvn, wuk_ref[...])
    mv_o[...] = _dot(ckvn, wuv_ref[...]).astype(mv_o.dtype)
    kr = _rope_lanes(kr_ref[...].astype(jnp.float32), c64, su64, sd64, MLA_ROPE // 2)
    for h in range(HEADS):
        base = h * MLA_QK_PAD
        mq_o[:, base:base + LANES] = q[:, base:base + LANES].astype(mq_o.dtype)
        qr = _rope_lanes(q[:, base + LANES:base + 2 * LANES], c64, su64, sd64, MLA_ROPE // 2)
        mq_o[:, base + LANES:base + 2 * LANES] = qr.astype(mq_o.dtype)
        mk_o[:, base:base + LANES] = kn[:, h * LANES:(h + 1) * LANES].astype(mk_o.dtype)
        mk_o[:, base + LANES:base + 2 * LANES] = kr.astype(mk_o.dtype)


def _prep(u, tabs, q_norm, kv_norm, w_uq, w_uk, w_uv):
    t = u.shape[0]
    tm = min(512, t)
    row = lambda blk: (lambda i: (i, blk))
    const = lambda i: (0, 0)
    tab_spec = pl.BlockSpec((tm, LANES), lambda i: (i, 0))
    outs = pl.pallas_call(
        _prep_body,
        out_shape=(jax.ShapeDtypeStruct((t, 1024), jnp.bfloat16),
                   jax.ShapeDtypeStruct((t, HEADS * MLA_QK_PAD), jnp.bfloat16),
                   jax.ShapeDtypeStruct((t, HEADS * MLA_QK_PAD), jnp.bfloat16),
                   jax.ShapeDtypeStruct((t, BRANCH_WIDTH), jnp.bfloat16)),
        grid=(t // tm,),
        in_specs=[pl.BlockSpec((tm, 1024), row(COL_DQ // 1024)),
                  pl.BlockSpec((tm, MLA_Q_LORA), row(COL_CQ // MLA_Q_LORA)),
                  pl.BlockSpec((tm, MLA_KV_LORA), row(COL_CKV // MLA_KV_LORA)),
                  pl.BlockSpec((tm, LANES), row(COL_KR // LANES)),
                  tab_spec, tab_spec, tab_spec, tab_spec, tab_spec, tab_spec,
                  _resident((1, MLA_Q_LORA), const),
                  _resident((1, MLA_KV_LORA), const),
                  _resident(w_uq.shape, const),
                  _resident(w_uk.shape, const),
                  _resident(w_uv.shape, const)],
        out_specs=(pl.BlockSpec((tm, 1024), lambda i: (i, 0)),
                   pl.BlockSpec((tm, HEADS * MLA_QK_PAD), lambda i: (i, 0)),
                   pl.BlockSpec((tm, HEADS * MLA_QK_PAD), lambda i: (i, 0)),
                   pl.BlockSpec((tm, BRANCH_WIDTH), lambda i: (i, 0))),
        compiler_params=_cparams(("parallel",)),
        name="prep",
    )(u, u, u, u, *tabs, q_norm, kv_norm, w_uq, w_uk, w_uv)
    return outs


def _attn_body(*refs, kind, tq, tk, lam_init):
    softmax = kind != "sb"
    if kind == "diff":
        lam_ref, q_ref, k_ref, v_ref, g_ref, o_ref = refs[:6]
    elif kind == "fox":
        q_ref, k_ref, v_ref, cq_ref, ck_ref, o_ref = refs[:6]
    else:
        q_ref, k_ref, v_ref, o_ref = refs[:4]
    if softmax:
        s_bufs, p_bufs = refs[-8:-6], refs[-6:-4]
        m_ref, l_ref, a_ref, acc_ref = refs[-4:]
    else:
        s_bufs, p_bufs = refs[-6:-4], refs[-4:-2]
        r_ref, acc_ref = refs[-2:]
    qi = pl.program_id(2)
    q0 = qi * tq
    last_kb = 2 * qi + 1
    q = q_ref[...]
    n_comp = 2 if kind == "diff" else 1
    if kind == "diff":
        lane = lax.broadcasted_iota(jnp.int32, q.shape, 1)
        qs = [jnp.where(lane < DIFF_QK, q, jnp.zeros_like(q)),
              jnp.where(lane >= DIFF_QK, q, jnp.zeros_like(q))]
    else:
        qs = [q]

    acc_ref[...] = jnp.zeros_like(acc_ref)
    p_bufs[1][...] = jnp.zeros_like(p_bufs[1])
    if softmax:
        m_ref[...] = jnp.full_like(m_ref, NEG_BIG)
        l_ref[...] = jnp.zeros_like(l_ref)
        a_ref[...] = jnp.zeros_like(a_ref)
    else:
        r_ref[...] = jnp.zeros_like(r_ref)

    def key_rows(kb):
        if isinstance(kb, int):
            return pl.ds(kb * tk, tk)
        return pl.ds(pl.multiple_of(kb * tk, tk), tk)

    def scores_stage(kb, s_dst):
        k_t = k_ref[key_rows(kb), :]
        for c in range(n_comp):
            s_dst[c] = _dot_nt(k_t, qs[c])

    def values_stage(kb, p_src):
        v_t = v_ref[key_rows(kb), :]
        for c in range(n_comp):
            pv = _dot_tn(v_t, p_src[c])
            if softmax:
                acc_ref[c] = a_ref[c] * acc_ref[c] + pv
            else:
                acc_ref[c] = acc_ref[c] + pv

    def weights_stage(kb, s_src, p_dst, masked):
        if masked:
            j = lax.broadcasted_iota(jnp.int32, (tk, tq), 0)
            i = lax.broadcasted_iota(jnp.int32, (tk, tq), 1)
            off = q0 - kb * tk
            if kind in ("diff", "mla"):
                mask = (j // CHUNK - i // CHUNK) <= off // CHUNK
            elif kind == "fox":
                mask = (j - i) <= off
            else:
                mask = (j - i) < off
        if not softmax:
            z_t = s_src[0]
            ls = _log_sigmoid(z_t)
            lk = ls - z_t
            if masked:
                lk = jnp.where(mask, lk, 0.0)
            jj = lax.broadcasted_iota(jnp.int32, (tk, tk), 0)
            mm = lax.broadcasted_iota(jnp.int32, (tk, tk), 1)
            upper = (mm > jj).astype(jnp.bfloat16)
            hi, mid = _split2(lk)
            later = r_ref[...] + _dot(jnp.concatenate([upper, upper], axis=1),
                                      jnp.concatenate([hi, mid], axis=0))
            w_t = jnp.exp(ls + later)
            if masked:
                w_t = jnp.where(mask, w_t, 0.0)
            p_dst[0] = w_t.astype(jnp.bfloat16)
            r_ref[...] += jnp.sum(lk, axis=0, keepdims=True)
            return
        for c in range(n_comp):
            s_t = s_src[c]
            if kind == "fox":
                ck = ck_ref[key_rows(kb), :]
                s_t = s_t + cq_ref[...] - jnp.concatenate([ck] * (tq // LANES), axis=1)
            if masked:
                s_t = jnp.where(mask, s_t, NEG_BIG)
            m_old = m_ref[c]
            m_new = jnp.maximum(m_old, jnp.max(s_t, axis=0, keepdims=True))
            alpha = jnp.exp2(m_old - m_new)
            p_t = jnp.exp2(s_t - m_new)
            l_ref[c] = alpha * l_ref[c] + jnp.sum(p_t, axis=0, keepdims=True)
            m_ref[c] = m_new
            a_ref[c] = alpha
            p_dst[c] = p_t.astype(jnp.bfloat16)

    def step(kb, parity, masked, next_kb, prev_kb):
        if next_kb is not None:
            scores_stage(next_kb, s_bufs[1 - parity])
        values_stage(prev_kb, p_bufs[1 - parity])
        weights_stage(kb, s_bufs[parity], p_bufs[parity], masked)

    if softmax:
        scores_stage(0, s_bufs[0])

        def body(t, carry):
            kb = 2 * t
            step(kb, 0, False, kb + 1, jnp.maximum(kb - 1, 0))
            step(kb + 1, 1, False, kb + 2, kb)
            return carry
        lax.fori_loop(0, qi, body, 0)
        step(last_kb - 1, 0, True, last_kb, jnp.maximum(last_kb - 2, 0))
        step(last_kb, 1, True, None, last_kb - 1)
        values_stage(last_kb, p_bufs[1])
    else:
        scores_stage(last_kb, s_bufs[0])
        step(last_kb, 0, True, last_kb - 1, last_kb)
        step(last_kb - 1, 1, True, jnp.maximum(last_kb - 2, 0), last_kb)

        def body(t, carry):
            kb = last_kb - 2 - 2 * t
            step(kb, 0, False, kb - 1, kb + 1)
            step(kb - 1, 1, False, jnp.maximum(kb - 2, 0), kb)
            return carry
        lax.fori_loop(0, qi, body, 0)
        values_stage(0, p_bufs[1])

    if kind == "sb":
        o_t = acc_ref[0]
    elif kind == "diff":
        o_t = acc_ref[0] / l_ref[0] - lam_ref[0] * (acc_ref[1] / l_ref[1])
    else:
        o_t = acc_ref[0] / l_ref[0]
    o = o_t.T
    if kind == "diff":
        o = (o * lax.rsqrt(jnp.mean(o * o, axis=-1, keepdims=True) + RMS_EPS)
             * g_ref[...] * (1.0 - lam_init))
    o_ref[...] = o.astype(o_ref.dtype)


def _attention(kind, batch, seq, q_arr, q_col, k_arr, k_col, v_arr, v_col, dq,
               extras=(), lam=None, lam_init=0.0):
    t = q_arr.shape[0]
    tq = min(ATT_TQ, seq)
    tk = tq // 2
    nq = seq // tq
    q_spec = pl.BlockSpec((tq, dq), lambda b, h, i, *_: (b * nq + i, q_col + h))
    k_spec = pl.BlockSpec((seq, dq), lambda b, h, i, *_: (b, k_col + h))
    v_spec = pl.BlockSpec((seq, HEAD_DIM), lambda b, h, i, *_: (b, v_col + h))
    in_specs = [q_spec, k_spec, v_spec]
    args = [q_arr, k_arr, v_arr]
    n_comp = 2 if kind == "diff" else 1
    if kind == "diff":
        in_specs.append(_resident((1, HEAD_DIM), lambda b, h, i, *_: (0, 0)))
        args.append(extras[0])
    elif kind == "fox":
        cq_rows, ck_rep = extras
        in_specs.append(pl.BlockSpec((None, 1, tq), lambda b, h, i: (b * HEADS + h, 0, i)))
        in_specs.append(pl.BlockSpec((None, seq, LANES), lambda b, h, i: (b * HEADS + h, 0, 0)))
        args += [cq_rows, ck_rep]
    s_buf = pltpu.VMEM((n_comp, tk, tq), jnp.float32)
    p_buf = pltpu.VMEM((n_comp, tk, tq), jnp.bfloat16)
    row = pltpu.VMEM((n_comp, 1, tq), jnp.float32)
    acc = pltpu.VMEM((n_comp, HEAD_DIM, tq), jnp.float32)
    if kind == "sb":
        scratch = [s_buf, s_buf, p_buf, p_buf, pltpu.VMEM((1, tq), jnp.float32), acc]
    else:
        scratch = [s_buf, s_buf, p_buf, p_buf, row, row, row, acc]
    body = functools.partial(_attn_body, kind=kind, tq=tq, tk=tk, lam_init=lam_init)
    grid = (batch, HEADS, nq)
    out_shape = jax.ShapeDtypeStruct((t, BRANCH_WIDTH), jnp.bfloat16)
    out_spec = pl.BlockSpec((tq, HEAD_DIM), lambda b, h, i, *_: (b * nq + i, h))
    cp = _cparams(("parallel", "parallel", "arbitrary"))
    if kind == "diff":
        return pl.pallas_call(
            body, out_shape=out_shape,
            grid_spec=pltpu.PrefetchScalarGridSpec(
                num_scalar_prefetch=1, grid=grid, in_specs=in_specs, out_specs=out_spec,
                scratch_shapes=scratch),
            compiler_params=cp, name="attn_" + kind)(lam, *args)
    return pl.pallas_call(
        body, out_shape=out_shape, grid=grid, in_specs=in_specs, out_specs=out_spec,
        scratch_shapes=scratch, compiler_params=cp, name="attn_" + kind)(*args)


def _layer_norm(y, g, b):
    mu = jnp.mean(y, axis=-1, keepdims=True)
    yc = y - mu
    var = jnp.mean(yc * yc, axis=-1, keepdims=True)
    return yc * lax.rsqrt(var + LN_EPS) * g + b


def _merge_body(g_ref, od_ref, of_ref, om_ref, os_ref, wb_ref, wo_ref, x_ref, lg_ref, lb_ref,
                wrh_ref, wrl_ref, h_o, hb_o, lgt_o, *, alpha):
    d = x_ref.shape[1]
    merged = None
    for n, o_ref in enumerate((od_ref, of_ref, om_ref, os_ref)):
        gate = _sigmoid(g_ref[:, n * d:(n + 1) * d].astype(jnp.float32))
        term = gate * _dot(o_ref[...], wb_ref[n])
        merged = term if merged is None else merged + term
    mix = _dot(merged.astype(jnp.bfloat16), wo_ref[...])
    h = _layer_norm(alpha * x_ref[...] + mix, lg_ref[...], lb_ref[...])
    h_o[...] = h
    h_hi = h.astype(jnp.bfloat16)
    hb_o[...] = h_hi
    h_lo = (h - h_hi.astype(jnp.float32)).astype(jnp.bfloat16)
    lgt_o[...] = (_dot(h_hi, wrh_ref[...]) + _dot(h_lo, wrh_ref[...]) + _dot(h_hi, wrl_ref[...]))


def _merge(u, o_d, o_f, o_m, o_s, w_b, w_o, x, ln_g, ln_b, wr_hi, wr_lo, alpha):
    t, d = x.shape
    tm = min(256, t)
    row = lambda i: (i, 0)
    const2 = lambda i: (0, 0)
    o_spec = pl.BlockSpec((tm, BRANCH_WIDTH), row)
    return pl.pallas_call(
        functools.partial(_merge_body, alpha=alpha),
        out_shape=(jax.ShapeDtypeStruct((t, d), jnp.float32),
                   jax.ShapeDtypeStruct((t, d), jnp.bfloat16),
                   jax.ShapeDtypeStruct((t, LANES), jnp.float32)),
        grid=(t // tm,),
        in_specs=[pl.BlockSpec((tm, N_BRANCHES * d), row),
                  o_spec, o_spec, o_spec, o_spec,
                  _resident(w_b.shape, lambda i: (0, 0, 0)),
                  _resident(w_o.shape, const2),
                  pl.BlockSpec((tm, d), row),
                  _resident((1, d), const2), _resident((1, d), const2),
                  _resident(wr_hi.shape, const2), _resident(wr_lo.shape, const2)],
        out_specs=(pl.BlockSpec((tm, d), row), pl.BlockSpec((tm, d), row),
                   pl.BlockSpec((tm, LANES), row)),
        compiler_params=_cparams(("parallel",)),
        name="merge_ln1",
    )(u, o_d, o_f, o_m, o_s, w_b, w_o, x, ln_g, ln_b, wr_hi, wr_lo)


def _moe_body(blk_e_ref, n_used_ref, x_ref, wg_ref, wu_ref, wd_ref, y_ref):
    i = pl.program_id(0)

    @pl.when(i < n_used_ref[0])
    def _():
        x = x_ref[...]
        a = _dot(x, wg_ref[...])
        hid = (a * _sigmoid(a)) * _dot(x, wu_ref[...])
        y_ref[...] = _dot(hid.astype(jnp.bfloat16), wd_ref[...])

    @pl.when(i >= n_used_ref[0])
    def _():
        y_ref[...] = jnp.zeros_like(y_ref)


def _moe(blk_e, n_used, rows, w_g, w_u, w_d):
    n_rows, d = rows.shape
    hid = w_g.shape[2]
    bm = MOE_ROWS
    return pl.pallas_call(
        _moe_body,
        out_shape=jax.ShapeDtypeStruct((n_rows, d), jnp.float32),
        grid_spec=pltpu.PrefetchScalarGridSpec(
            num_scalar_prefetch=2, grid=(n_rows // bm,),
            in_specs=[pl.BlockSpec((bm, d), lambda i, e, n: (i, 0)),
                      pl.BlockSpec((None, d, hid), lambda i, e, n: (e[i], 0, 0)),
                      pl.BlockSpec((None, d, hid), lambda i, e, n: (e[i], 0, 0)),
                      pl.BlockSpec((None, hid, d), lambda i, e, n: (e[i], 0, 0))],
            out_specs=pl.BlockSpec((bm, d), lambda i, e, n: (i, 0))),
        compiler_params=_cparams(("arbitrary",)),
        name="moe_experts",
    )(blk_e, n_used, rows, w_g, w_u, w_d)


def _final_body(h_ref, hb_ref, f_ref, p_ref, wg_ref, wp_ref, lg_ref, lb_ref, y_o, yb_o, *, alpha):
    gate = _sigmoid(_dot(hb_ref[...], wg_ref[...]))
    proj = _dot(p_ref[...].astype(jnp.bfloat16), wp_ref[...])
    y = _layer_norm(alpha * h_ref[...] + f_ref[...] + gate * proj, lg_ref[...], lb_ref[...])
    y_o[...] = y
    yb_o[...] = y.astype(jnp.bfloat16)


def _final(h, hb, ffn, p, w_pg, w_pp, ln_g, ln_b, alpha):
    t, d = h.shape
    tm = min(256, t)
    row = lambda i: (i, 0)
    const2 = lambda i: (0, 0)
    return pl.pallas_call(
        functools.partial(_final_body, alpha=alpha),
        out_shape=(jax.ShapeDtypeStruct((t, d), jnp.float32),
                   jax.ShapeDtypeStruct((t, d), jnp.bfloat16)),
        grid=(t // tm,),
        in_specs=[pl.BlockSpec((tm, d), row), pl.BlockSpec((tm, d), row),
                  pl.BlockSpec((tm, d), row), pl.BlockSpec((tm, p.shape[1]), row),
                  _resident(w_pg.shape, const2), _resident(w_pp.shape, const2),
                  _resident((1, d), const2), _resident((1, d), const2)],
        out_specs=(pl.BlockSpec((tm, d), row), pl.BlockSpec((tm, d), row)),
        compiler_params=_cparams(("parallel",)),
        name="final_ln2",
    )(h, hb, ffn, p, w_pg, w_pp, ln_g, ln_b)


def _rope_tables(positions):
    pos = positions.reshape(-1).astype(jnp.float32)
    lane = jnp.arange(LANES)

    def tables(rot, period):
        half = rot // 2
        inv_freq = ROPE_THETA ** (-jnp.arange(half, dtype=jnp.float32) / half)
        ang = pos[:, None] * inv_freq
        cos, sin = jnp.cos(ang), jnp.sin(ang)
        lp = lane % period
        idx = lp % half
        in_rot = lp < rot
        c = jnp.where(in_rot[None, :], cos[:, idx], 1.0)
        s_up = jnp.where((lp < half)[None, :], -sin[:, idx], 0.0)
        s_dn = jnp.where(((lp >= half) & in_rot)[None, :], sin[:, idx], 0.0)
        return c, s_up, s_dn

    return tables(DIFF_ROT, DIFF_QK) + tables(MLA_ROPE, LANES)


def _reorder_w_in(w):
    d = w.shape[0]
    widths = [512, 512, 512, 512, 512, 512, HEADS, MLA_Q_LORA, MLA_KV_LORA, MLA_ROPE,
              512, 512, 512, N_BRANCHES * d]
    offs = [0]
    for wd in widths:
        offs.append(offs[-1] + wd)
    part = lambda n: w[:, offs[n]:offs[n + 1]]
    (dq, dk, dv, fq, fk, fv, ff, cq, ckv, kr, sq, sk, sv, gates) = [part(n) for n in range(14)]
    zeros = lambda n: jnp.zeros((d, n), w.dtype)
    cols = [gates, dq, dk, dv, fq, fk, fv, sq, sk, sv, cq, ckv, kr, zeros(LANES - MLA_ROPE),
            ff, zeros(LANES - HEADS)]
    w1 = jnp.concatenate(cols, axis=1).astype(jnp.bfloat16)
    w_f = jnp.concatenate([ff, zeros(LANES - HEADS)], axis=1).astype(jnp.bfloat16)
    return w1, w_f


def _route(logits, b_rg, b_re, t):
    g_logits = logits[:, :N_GROUPS] + b_rg
    grp = jnp.argmax(g_logits, axis=-1)
    p_grp = jnp.take_along_axis(jax.nn.softmax(g_logits, axis=-1), grp[:, None], axis=-1)
    e_logits = (logits[:, N_GROUPS:N_GROUPS + N_EXPERTS] + b_re).reshape(
        t, N_GROUPS, EXPERTS_PER_GROUP)
    e_in = jnp.take_along_axis(e_logits, grp[:, None, None], axis=1)[:, 0]
    top_p, top_i = lax.top_k(jax.nn.softmax(e_in, axis=-1), TOP_K)
    weights = p_grp * top_p / jnp.sum(top_p, axis=-1, keepdims=True)
    expert_idx = grp[:, None].astype(jnp.int32) * EXPERTS_PER_GROUP + top_i.astype(jnp.int32)
    return expert_idx, weights


def _dispatch_plan(expert_idx, t):
    a = t * TOP_K
    flat_e = expert_idx.reshape(a)
    onehot = (flat_e[:, None] == jnp.arange(N_EXPERTS)[None, :]).astype(jnp.int32)
    rank = jnp.take_along_axis(jnp.cumsum(onehot, axis=0), flat_e[:, None], axis=1)[:, 0] - 1
    sizes = jnp.sum(onehot, axis=0)
    padded = (sizes + MOE_ROWS - 1) // MOE_ROWS * MOE_ROWS
    pad_end = jnp.cumsum(padded)
    pad_start = pad_end - padded
    dest = (pad_start[flat_e] + rank).astype(jnp.int32)
    n_rows = a + N_EXPERTS * MOE_ROWS
    n_blk = n_rows // MOE_ROWS
    flat_tok = jnp.arange(a, dtype=jnp.int32) // TOP_K
    row_tok = jnp.zeros((n_rows,), jnp.int32).at[dest].set(flat_tok)
    blk_e = jnp.minimum(jnp.searchsorted(pad_end, jnp.arange(n_blk) * MOE_ROWS, side="right"),
                        N_EXPERTS - 1).astype(jnp.int32)
    n_used = (pad_end[-1] // MOE_ROWS).astype(jnp.int32).reshape(1)
    return dest.reshape(t, TOP_K), row_tok, blk_e, n_used


def kernel(x, p, positions, w_in, fox_f_bias, diff_lambda, diff_subln, mla_q_norm, mla_kv_norm,
           mla_w_uq, mla_w_ukv, w_branch, w_o, ln1_g, ln1_b, w_router_group, b_router_group,
           w_router_expert, b_router_expert, w_expert_gate, w_expert_up, w_expert_down,
           w_ple_gate, w_ple_proj, ln2_g, ln2_b):
    batch, seq, d = x.shape
    depth = w_in.shape[0]
    t = batch * seq
    alpha = (2 * depth) ** 0.25
    bf = jnp.bfloat16
    tabs = _rope_tables(positions)
    col_scale = jnp.ones((1, U_WIDTH), jnp.float32)
    col_scale = col_scale.at[0, COL_FQ:COL_FQ + BRANCH_WIDTH].set(HEAD_DIM ** -0.5 * LOG2E)
    col_scale = col_scale.at[0, COL_SQ:COL_SQ + BRANCH_WIDTH].set(HEAD_DIM ** -0.5)
    xf = x.reshape(t, d)
    xb = xf.astype(bf)

    for i in range(depth):
        w1, w_f = _reorder_w_in(w_in[i])
        u = _matmul(xb, w1, col_scale, bf, 1024, 1536, "in_proj")

        b_f = jnp.zeros((1, LANES), jnp.float32).at[0, :HEADS].set(fox_f_bias[i])
        cum = _fgate_cumsum(xb, w_f, b_f, batch, seq)[:, :HEADS] * LOG2E
        cum_bh = jnp.transpose(cum.reshape(batch, seq, HEADS), (0, 2, 1)).reshape(
            batch * HEADS, seq)
        cq_rows = cum_bh.reshape(batch * HEADS, 1, seq)
        ck_rep = jnp.broadcast_to(cum_bh[:, :, None], (batch * HEADS, seq, LANES))

        uq = mla_w_uq[i].reshape(MLA_Q_LORA, HEADS, MLA_NOPE + MLA_ROPE)
        uq = jnp.pad(uq, ((0, 0), (0, 0), (0, MLA_QK_PAD - MLA_NOPE - MLA_ROPE)))
        uq = uq.reshape(MLA_Q_LORA, HEADS * MLA_QK_PAD).astype(bf)
        ukv = mla_w_ukv[i].reshape(MLA_KV_LORA, HEADS, MLA_NOPE + HEAD_DIM)
        uk = ukv[:, :, :MLA_NOPE].reshape(MLA_KV_LORA, HEADS * MLA_NOPE).astype(bf)
        uv = ukv[:, :, MLA_NOPE:].reshape(MLA_KV_LORA, HEADS * HEAD_DIM).astype(bf)
        dqk, mq, mk, mv = _prep(u, tabs, mla_q_norm[i].reshape(1, -1),
                                mla_kv_norm[i].reshape(1, -1), uq, uk, uv)

        lam_init = 0.8 - 0.6 * math.exp(-0.3 * i)
        lp = diff_lambda[i].astype(jnp.float32)
        lam = (jnp.exp(jnp.sum(lp[0] * lp[1])) - jnp.exp(jnp.sum(lp[2] * lp[3]))
               + lam_init).reshape(1)
        blk = lambda col: col // HEAD_DIM
        o_d = _attention("diff", batch, seq, dqk, 0, dqk, HEADS, u, blk(COL_DV), HEAD_DIM,
                         extras=(diff_subln[i].reshape(1, HEAD_DIM),),
                         lam=lam, lam_init=lam_init)
        o_f = _attention("fox", batch, seq, u, blk(COL_FQ), u, blk(COL_FK), u, blk(COL_FV),
                         HEAD_DIM, extras=(cq_rows, ck_rep))
        o_m = _attention("mla", batch, seq, mq, 0, mk, 0, mv, 0, MLA_QK_PAD)
        o_s = _attention("sb", batch, seq, u, blk(COL_SQ), u, blk(COL_SK), u, blk(COL_SV),
                         HEAD_DIM)

        w_r = jnp.concatenate([w_router_group[i], w_router_expert[i]], axis=1)
        w_r = jnp.pad(w_r, ((0, 0), (0, LANES - w_r.shape[1])))
        wr_hi = w_r.astype(bf)
        wr_lo = (w_r - wr_hi.astype(jnp.float32)).astype(bf)
        h, hb, logits = _merge(u, o_d, o_f, o_m, o_s, w_branch[i].astype(bf), w_o[i].astype(bf),
                               xf, ln1_g[i].reshape(1, d), ln1_b[i].reshape(1, d),
                               wr_hi, wr_lo, alpha)

        expert_idx, weights = _route(logits, b_router_group[i], b_router_expert[i], t)
        dest, row_tok, blk_e, n_used = _dispatch_plan(expert_idx, t)
        rows = jnp.take(hb, row_tok, axis=0)
        y_rows = _moe(blk_e, n_used, rows, w_expert_gate[i].astype(bf),
                      w_expert_up[i].astype(bf), w_expert_down[i].astype(bf))
        ffn = (jnp.take(y_rows, dest[:, 0], axis=0) * weights[:, 0:1]
               + jnp.take(y_rows, dest[:, 1], axis=0) * weights[:, 1:2])

        xf, xb = _final(h, hb, ffn, p[i].reshape(t, -1), w_ple_gate[i].astype(bf),
                        w_ple_proj[i].astype(bf), ln2_g[i].reshape(1, d),
                        ln2_b[i].reshape(1, d), alpha)
    return xf.reshape(batch, seq, d)
```

```python
import functools
import math

import jax
import jax.numpy as jnp
from jax import lax
from jax.experimental import pallas as pl
from jax.experimental.pallas import tpu as pltpu

HEAD_DIM = 128
HEADS = 4
N_BRANCHES = 4
BRANCH_WIDTH = HEADS * HEAD_DIM
DIFF_QK = 64
DIFF_ROT = DIFF_QK // 4
MLA_Q_LORA = 512
MLA_KV_LORA = 256
MLA_NOPE = 128
MLA_ROPE = 64
MLA_QK_PAD = 256
N_GROUPS = 4
EXPERTS_PER_GROUP = 8
N_EXPERTS = N_GROUPS * EXPERTS_PER_GROUP
TOP_K = 2
CHUNK = 64
ROPE_THETA = 500000.0
LN_EPS = 1e-5
RMS_EPS = 1e-6
NEG_BIG = -1e30
LOG2E = 1.4426950408889634
LANES = 128
V7X_VMEM_LIMIT = 56 * 1024 * 1024

COL_GATES = 0
COL_DQ = 8192
COL_DV = COL_DQ + 1024
COL_FQ = COL_DV + 512
COL_FK = COL_FQ + 512
COL_FV = COL_FK + 512
COL_SQ = COL_FV + 512
COL_SK = COL_SQ + 512
COL_SV = COL_SK + 512
COL_CQ = COL_SV + 512
COL_CKV = COL_CQ + 512
COL_KR = COL_CKV + 256
COL_FF = COL_KR + 128
U_WIDTH = COL_FF + 128

ATT_TQ = 512
MOE_ROWS = 256


def _cparams(sem, vmem=V7X_VMEM_LIMIT):
    return pltpu.CompilerParams(dimension_semantics=sem, vmem_limit_bytes=vmem)


def _resident(shape, index_map):
    return pl.BlockSpec(shape, index_map, pipeline_mode=pl.Buffered(1))


def _matmul_body(x_ref, w_ref, c_ref, o_ref):
    acc = jnp.dot(x_ref[...], w_ref[...], preferred_element_type=jnp.float32)
    o_ref[...] = (acc * c_ref[...]).astype(o_ref.dtype)


def _matmul(x, w, col_scale, out_dtype, tm, tn, name):
    m, k = x.shape
    n = w.shape[1]
    tm, tn = min(tm, m), min(tn, n)
    return pl.pallas_call(
        _matmul_body,
        out_shape=jax.ShapeDtypeStruct((m, n), out_dtype),
        grid=(m // tm, n // tn),
        in_specs=[pl.BlockSpec((tm, k), lambda i, j: (i, 0)),
                  pl.BlockSpec((k, tn), lambda i, j: (0, j)),
                  pl.BlockSpec((1, tn), lambda i, j: (0, j))],
        out_specs=pl.BlockSpec((tm, tn), lambda i, j: (i, j)),
        compiler_params=_cparams(("parallel", "arbitrary")),
        name=name,
    )(x, w, col_scale)


def _split3(x):
    hi = x.astype(jnp.bfloat16)
    r1 = x - hi.astype(jnp.float32)
    mid = r1.astype(jnp.bfloat16)
    lo = (r1 - mid.astype(jnp.float32)).astype(jnp.bfloat16)
    return hi, mid, lo


def _split2(x):
    hi = x.astype(jnp.bfloat16)
    mid = (x - hi.astype(jnp.float32)).astype(jnp.bfloat16)
    return hi, mid


def _dot(a, b):
    return jnp.dot(a, b, preferred_element_type=jnp.float32)


def _dot_nt(a, b):
    return lax.dot_general(a, b, (((1,), (1,)), ((), ())), preferred_element_type=jnp.float32)


def _dot_tn(a, b):
    return lax.dot_general(a, b, (((0,), (0,)), ((), ())), preferred_element_type=jnp.float32)


def _log_sigmoid(z):
    neg_abs = pltpu.bitcast(pltpu.bitcast(z, jnp.uint32) | jnp.uint32(0x80000000), jnp.float32)
    return jnp.minimum(z, 0.0) - jnp.log(1.0 + jnp.exp(neg_abs))


def _sigmoid(z):
    return 1.0 / (1.0 + jnp.exp(-z))


def _fgate_body(x_ref, w_ref, b_ref, o_ref, carry_ref):
    @pl.when(pl.program_id(1) == 0)
    def _():
        carry_ref[...] = jnp.zeros_like(carry_ref)

    f = _dot(x_ref[...], w_ref[...]) + b_ref[...]
    log_f = _log_sigmoid(f)
    ts = log_f.shape[0]
    row = lax.broadcasted_iota(jnp.int32, (ts, ts), 0)
    col = lax.broadcasted_iota(jnp.int32, (ts, ts), 1)
    tri = (col <= row).astype(jnp.bfloat16)
    hi, mid, lo = _split3(log_f)
    cum = _dot(tri, hi) + _dot(tri, mid) + _dot(tri, lo) + carry_ref[...]
    o_ref[...] = cum
    carry_ref[...] = cum[ts - 1:ts, :]


def _fgate_cumsum(xb, w_f, b_f, batch, seq):
    t, d = xb.shape
    ts = min(512, seq)
    ns = seq // ts
    return pl.pallas_call(
        _fgate_body,
        out_shape=jax.ShapeDtypeStruct((t, LANES), jnp.float32),
        grid=(batch, ns),
        in_specs=[pl.BlockSpec((ts, d), lambda b, s: (b * ns + s, 0)),
                  _resident((d, LANES), lambda b, s: (0, 0)),
                  _resident((1, LANES), lambda b, s: (0, 0))],
        out_specs=pl.BlockSpec((ts, LANES), lambda b, s: (b * ns + s, 0)),
        scratch_shapes=[pltpu.VMEM((1, LANES), jnp.float32)],
        compiler_params=_cparams(("parallel", "arbitrary")),
        name="fgate_cumsum",
    )(xb, w_f, b_f)


def _rope_lanes(x, c, s_up, s_dn, half):
    return x * c + pltpu.roll(x, LANES - half, 1) * s_up + pltpu.roll(x, half, 1) * s_dn


def _prep_body(dqk_ref, cq_ref, ckv_ref, kr_ref, c16_ref, su16_ref, sd16_ref,
               c64_ref, su64_ref, sd64_ref, qg_ref, kvg_ref, wuq_ref, wuk_ref, wuv_ref,
               dqk_o, mq_o, mk_o, mv_o):
    c16, su16, sd16 = c16_ref[...], su16_ref[...], sd16_ref[...]
    c64, su64, sd64 = c64_ref[...], su64_ref[...], sd64_ref[...]
    for j in range(dqk_ref.shape[1] // LANES):
        sl = slice(j * LANES, (j + 1) * LANES)
        x = _rope_lanes(dqk_ref[:, sl].astype(jnp.float32), c16, su16, sd16, DIFF_ROT // 2)
        if j < HEADS:
            x = x * (DIFF_QK ** -0.5 * LOG2E)
        dqk_o[:, sl] = x.astype(dqk_o.dtype)

    cq = cq_ref[...].astype(jnp.float32)
    cqn = cq * lax.rsqrt(jnp.mean(cq * cq, axis=-1, keepdims=True) + RMS_EPS) * qg_ref[...]
    q = _dot(cqn.astype(jnp.bfloat16), wuq_ref[...])
    q = q * ((MLA_NOPE + MLA_ROPE) ** -0.5 * LOG2E)
    ckv = ckv_ref[...].astype(jnp.float32)
    ckvn = (ckv * lax.rsqrt(jnp.mean(ckv * ckv, axis=-1, keepdims=True) + RMS_EPS)
            * kvg_ref[...]).astype(jnp.bfloat16)
    kn = _dot(ckvn, wuk_ref[...])
    mv_o[...] = _dot(ckvn, wuv_ref[...]).astype(mv_o.dtype)
    kr = _rope_lanes(kr_ref[...].astype(jnp.float32), c64, su64, sd64, MLA_ROPE // 2)
    for h in range(HEADS):
        base = h * MLA_QK_PAD
        mq_o[:, base:base + LANES] = q[:, base:base + LANES].astype(mq_o.dtype)
        qr = _rope_lanes(q[:, base + LANES:base + 2 * LANES], c64, su64, sd64, MLA_ROPE // 2)
        mq_o[:, base + LANES:base + 2 * LANES] = qr.astype(mq_o.dtype)
        mk_o[:, base:base + LANES] = kn[:, h * LANES:(h + 1) * LANES].astype(mk_o.dtype)
        mk_o[:, base + LANES:base + 2 * LANES] = kr.astype(mk_o.dtype)


def _prep(u, tabs, q_norm, kv_norm, w_uq, w_uk, w_uv):
    t = u.shape[0]
    tm = min(512, t)
    row = lambda blk: (lambda i: (i, blk))
    const = lambda i: (0, 0)
    tab_spec = pl.BlockSpec((tm, LANES), lambda i: (i, 0))
    outs = pl.pallas_call(
        _prep_body,
        out_shape=(jax.ShapeDtypeStruct((t, 1024), jnp.bfloat16),
                   jax.ShapeDtypeStruct((t, HEADS * MLA_QK_PAD), jnp.bfloat16),
                   jax.ShapeDtypeStruct((t, HEADS * MLA_QK_PAD), jnp.bfloat16),
                   jax.ShapeDtypeStruct((t, BRANCH_WIDTH), jnp.bfloat16)),
        grid=(t // tm,),
        in_specs=[pl.BlockSpec((tm, 1024), row(COL_DQ // 1024)),
                  pl.BlockSpec((tm, MLA_Q_LORA), row(COL_CQ // MLA_Q_LORA)),
                  pl.BlockSpec((tm, MLA_KV_LORA), row(COL_CKV // MLA_KV_LORA)),
                  pl.BlockSpec((tm, LANES), row(COL_KR // LANES)),
                  tab_spec, tab_spec, tab_spec, tab_spec, tab_spec, tab_spec,
                  _resident((1, MLA_Q_LORA), const),
                  _resident((1, MLA_KV_LORA), const),
                  _resident(w_uq.shape, const),
                  _resident(w_uk.shape, const),
                  _resident(w_uv.shape, const)],
        out_specs=(pl.BlockSpec((tm, 1024), lambda i: (i, 0)),
                   pl.BlockSpec((tm, HEADS * MLA_QK_PAD), lambda i: (i, 0)),
                   pl.BlockSpec((tm, HEADS * MLA_QK_PAD), lambda i: (i, 0)),
                   pl.BlockSpec((tm, BRANCH_WIDTH), lambda i: (i, 0))),
        compiler_params=_cparams(("parallel",)),
        name="prep",
    )(u, u, u, u, *tabs, q_norm, kv_norm, w_uq, w_uk, w_uv)
    return outs


def _attn_body(*refs, kind, tq, tk, lam_init):
    softmax = kind != "sb"
    if kind == "diff":
        lam_ref, q_ref, k_ref, v_ref, g_ref, o_ref = refs[:6]
    elif kind == "fox":
        q_ref, k_ref, v_ref, cq_ref, ck_ref, o_ref = refs[:6]
    else:
        q_ref, k_ref, v_ref, o_ref = refs[:4]
    if softmax:
        s_bufs, p_bufs = refs[-8:-6], refs[-6:-4]
        m_ref, l_ref, a_ref, acc_ref = refs[-4:]
    else:
        s_bufs, p_bufs = refs[-6:-4], refs[-4:-2]
        r_ref, acc_ref = refs[-2:]
    qi = pl.program_id(2)
    q0 = qi * tq
    last_kb = 2 * qi + 1
    q = q_ref[...]
    n_comp = 2 if kind == "diff" else 1
    if kind == "diff":
        lane = lax.broadcasted_iota(jnp.int32, q.shape, 1)
        qs = [jnp.where(lane < DIFF_QK, q, jnp.zeros_like(q)),
              jnp.where(lane >= DIFF_QK, q, jnp.zeros_like(q))]
    else:
        qs = [q]

    acc_ref[...] = jnp.zeros_like(acc_ref)
    p_bufs[1][...] = jnp.zeros_like(p_bufs[1])
    if softmax:
        m_ref[...] = jnp.full_like(m_ref, NEG_BIG)
        l_ref[...] = jnp.zeros_like(l_ref)
        a_ref[...] = jnp.zeros_like(a_ref)
    else:
        r_ref[...] = jnp.zeros_like(r_ref)

    def key_rows(kb):
        if isinstance(kb, int):
            return pl.ds(kb * tk, tk)
        return pl.ds(pl.multiple_of(kb * tk, tk), tk)

    def scores_stage(kb, s_dst):
        k_t = k_ref[key_rows(kb), :]
        for c in range(n_comp):
            s_dst[c] = _dot_nt(k_t, qs[c])

    def values_stage(kb, p_src):
        v_t = v_ref[key_rows(kb), :]
        for c in range(n_comp):
            pv = _dot_tn(v_t, p_src[c])
            if softmax:
                acc_ref[c] = a_ref[c] * acc_ref[c] + pv
            else:
                acc_ref[c] = acc_ref[c] + pv

    def weights_stage(kb, s_src, p_dst, masked):
        if masked:
            j = lax.broadcasted_iota(jnp.int32, (tk, tq), 0)
            i = lax.broadcasted_iota(jnp.int32, (tk, tq), 1)
            off = q0 - kb * tk
            if kind in ("diff", "mla"):
                mask = (j // CHUNK - i // CHUNK) <= off // CHUNK
            elif kind == "fox":
                mask = (j - i) <= off
            else:
                mask = (j - i) < off
        if not softmax:
            z_t = s_src[0]
            ls = _log_sigmoid(z_t)
            lk = ls - z_t
            if masked:
                lk = jnp.where(mask, lk, 0.0)
            jj = lax.broadcasted_iota(jnp.int32, (tk, tk), 0)
            mm = lax.broadcasted_iota(jnp.int32, (tk, tk), 1)
            upper = (mm > jj).astype(jnp.bfloat16)
            hi, mid = _split2(lk)
            later = r_ref[...] + _dot(jnp.concatenate([upper, upper], axis=1),
                                      jnp.concatenate([hi, mid], axis=0))
            w_t = jnp.exp(ls + later)
            if masked:
                w_t = jnp.where(mask, w_t, 0.0)
            p_dst[0] = w_t.astype(jnp.bfloat16)
            r_ref[...] += jnp.sum(lk, axis=0, keepdims=True)
            return
        for c in range(n_comp):
            s_t = s_src[c]
            if kind == "fox":
                ck = ck_ref[key_rows(kb), :]
                s_t = s_t + cq_ref[...] - jnp.concatenate([ck] * (tq // LANES), axis=1)
            if masked:
                s_t = jnp.where(mask, s_t, NEG_BIG)
            m_old = m_ref[c]
            m_new = jnp.maximum(m_old, jnp.max(s_t, axis=0, keepdims=True))
            alpha = jnp.exp2(m_old - m_new)
            p_t = jnp.exp2(s_t - m_new)
            l_ref[c] = alpha * l_ref[c] + jnp.sum(p_t, axis=0, keepdims=True)
            m_ref[c] = m_new
            a_ref[c] = alpha
            p_dst[c] = p_t.astype(jnp.bfloat16)

    def step(kb, parity, masked, next_kb, prev_kb):
        if next_kb is not None:
            scores_stage(next_kb, s_bufs[1 - parity])
        values_stage(prev_kb, p_bufs[1 - parity])
        weights_stage(kb, s_bufs[parity], p_bufs[parity], masked)

    if softmax:
        scores_stage(0, s_bufs[0])

        def body(t, carry):
            kb = 2 * t
            step(kb, 0, False, kb + 1, jnp.maximum(kb - 1, 0))
            step(kb + 1, 1, False, kb + 2, kb)
            return carry
        lax.fori_loop(0, qi, body, 0)
        step(last_kb - 1, 0, True, last_kb, jnp.maximum(last_kb - 2, 0))
        step(last_kb, 1, True, None, last_kb - 1)
        values_stage(last_kb, p_bufs[1])
    else:
        scores_stage(last_kb, s_bufs[0])
        step(last_kb, 0, True, last_kb - 1, last_kb)
        step(last_kb - 1, 1, True, jnp.maximum(last_kb - 2, 0), last_kb)

        def body(t, carry):
            kb = last_kb - 2 - 2 * t
            step(kb, 0, False, kb - 1, kb + 1)
            step(kb - 1, 1, False, jnp.maximum(kb - 2, 0), kb)
            return carry
        lax.fori_loop(0, qi, body, 0)
        values_stage(0, p_bufs[1])

    if kind == "sb":
        o_t = acc_ref[0]
    elif kind == "diff":
        o_t = acc_ref[0] / l_ref[0] - lam_ref[0] * (acc_ref[1] / l_ref[1])
    else:
        o_t = acc_ref[0] / l_ref[0]
    o = o_t.T
    if kind == "diff":
        o = (o * lax.rsqrt(jnp.mean(o * o, axis=-1, keepdims=True) + RMS_EPS)
             * g_ref[...] * (1.0 - lam_init))
    o_ref[...] = o.astype(o_ref.dtype)


def _attention(kind, batch, seq, q_arr, q_col, k_arr, k_col, v_arr, v_col, dq,
               extras=(), lam=None, lam_init=0.0):
    t = q_arr.shape[0]
    tq = min(ATT_TQ, seq)
    tk = tq // 2
    nq = seq // tq
    q_spec = pl.BlockSpec((tq, dq), lambda b, h, i, *_: (b * nq + i, q_col + h))
    k_spec = pl.BlockSpec((seq, dq), lambda b, h, i, *_: (b, k_col + h))
    v_spec = pl.BlockSpec((seq, HEAD_DIM), lambda b, h, i, *_: (b, v_col + h))
    in_specs = [q_spec, k_spec, v_spec]
    args = [q_arr, k_arr, v_arr]
    n_comp = 2 if kind == "diff" else 1
    if kind == "diff":
        in_specs.append(_resident((1, HEAD_DIM), lambda b, h, i, *_: (0, 0)))
        args.append(extras[0])
    elif kind == "fox":
        cq_rows, ck_rep = extras
        in_specs.append(pl.BlockSpec((None, 1, tq), lambda b, h, i: (b * HEADS + h, 0, i)))
        in_specs.append(pl.BlockSpec((None, seq, LANES), lambda b, h, i: (b * HEADS + h, 0, 0)))
        args += [cq_rows, ck_rep]
    s_buf = pltpu.VMEM((n_comp, tk, tq), jnp.float32)
    p_buf = pltpu.VMEM((n_comp, tk, tq), jnp.bfloat16)
    row = pltpu.VMEM((n_comp, 1, tq), jnp.float32)
    acc = pltpu.VMEM((n_comp, HEAD_DIM, tq), jnp.float32)
    if kind == "sb":
        scratch = [s_buf, s_buf, p_buf, p_buf, pltpu.VMEM((1, tq), jnp.float32), acc]
    else:
        scratch = [s_buf, s_buf, p_buf, p_buf, row, row, row, acc]
    body = functools.partial(_attn_body, kind=kind, tq=tq, tk=tk, lam_init=lam_init)
    grid = (batch, HEADS, nq)
    out_shape = jax.ShapeDtypeStruct((t, BRANCH_WIDTH), jnp.bfloat16)
    out_spec = pl.BlockSpec((tq, HEAD_DIM), lambda b, h, i, *_: (b * nq + i, h))
    cp = _cparams(("parallel", "parallel", "arbitrary"))
    if kind == "diff":
        return pl.pallas_call(
            body, out_shape=out_shape,
            grid_spec=pltpu.PrefetchScalarGridSpec(
                num_scalar_prefetch=1, grid=grid, in_specs=in_specs, out_specs=out_spec,
                scratch_shapes=scratch),
            compiler_params=cp, name="attn_" + kind)(lam, *args)
    return pl.pallas_call(
        body, out_shape=out_shape, grid=grid, in_specs=in_specs, out_specs=out_spec,
        scratch_shapes=scratch, compiler_params=cp, name="attn_" + kind)(*args)


def _layer_norm(y, g, b):
    mu = jnp.mean(y, axis=-1, keepdims=True)
    yc = y - mu
    var = jnp.mean(yc * yc, axis=-1, keepdims=True)
    return yc * lax.rsqrt(var + LN_EPS) * g + b


def _merge_body(g_ref, od_ref, of_ref, om_ref, os_ref, wb_ref, wo_ref, x_ref, lg_ref, lb_ref,
                wrh_ref, wrl_ref, h_o, hb_o, lgt_o, *, alpha):
    d = x_ref.shape[1]
    merged = None
    for n, o_ref in enumerate((od_ref, of_ref, om_ref, os_ref)):
        gate = _sigmoid(g_ref[:, n * d:(n + 1) * d].astype(jnp.float32))
        term = gate * _dot(o_ref[...], wb_ref[n])
        merged = term if merged is None else merged + term
    mix = _dot(merged.astype(jnp.bfloat16), wo_ref[...])
    h = _layer_norm(alpha * x_ref[...] + mix, lg_ref[...], lb_ref[...])
    h_o[...] = h
    h_hi = h.astype(jnp.bfloat16)
    hb_o[...] = h_hi
    h_lo = (h - h_hi.astype(jnp.float32)).astype(jnp.bfloat16)
    lgt_o[...] = (_dot(h_hi, wrh_ref[...]) + _dot(h_lo, wrh_ref[...]) + _dot(h_hi, wrl_ref[...]))


def _merge(u, o_d, o_f, o_m, o_s, w_b, w_o, x, ln_g, ln_b, wr_hi, wr_lo, alpha):
    t, d = x.shape
    tm = min(256, t)
    row = lambda i: (i, 0)
    const2 = lambda i: (0, 0)
    o_spec = pl.BlockSpec((tm, BRANCH_WIDTH), row)
    return pl.pallas_call(
        functools.partial(_merge_body, alpha=alpha),
        out_shape=(jax.ShapeDtypeStruct((t, d), jnp.float32),
                   jax.ShapeDtypeStruct((t, d), jnp.bfloat16),
                   jax.ShapeDtypeStruct((t, LANES), jnp.float32)),
        grid=(t // tm,),
        in_specs=[pl.BlockSpec((tm, N_BRANCHES * d), row),
                  o_spec, o_spec, o_spec, o_spec,
                  _resident(w_b.shape, lambda i: (0, 0, 0)),
                  _resident(w_o.shape, const2),
                  pl.BlockSpec((tm, d), row),
                  _resident((1, d), const2), _resident((1, d), const2),
                  _resident(wr_hi.shape, const2), _resident(wr_lo.shape, const2)],
        out_specs=(pl.BlockSpec((tm, d), row), pl.BlockSpec((tm, d), row),
                   pl.BlockSpec((tm, LANES), row)),
        compiler_params=_cparams(("parallel",)),
        name="merge_ln1",
    )(u, o_d, o_f, o_m, o_s, w_b, w_o, x, ln_g, ln_b, wr_hi, wr_lo)


def _gather_rows(idx_ref, base, src_hbm, dst, sem, n):
    def issue(r, carry):
        row = idx_ref[base + r]
        pltpu.make_async_copy(src_hbm.at[pl.ds(row, 1), :], dst.at[pl.ds(r, 1), :], sem).start()
        return carry
    lax.fori_loop(0, n, issue, 0, unroll=8)


def _wait_rows(src_hbm, dst, sem, n):
    pltpu.make_async_copy(src_hbm.at[pl.ds(0, n), :], dst, sem).wait()


def _moe_body(blk_e_ref, n_used_ref, tok_ref, h_hbm, wg_ref, wu_ref, wd_ref, y_ref,
              xbuf, sem, wg_b, wu_b, wd_b):
    i = pl.program_id(0)
    n_used = n_used_ref[0]
    bm = xbuf.shape[1]
    slot = i % 2

    @pl.when((i == 0) & (n_used > 0))
    def _():
        _gather_rows(tok_ref, 0, h_hbm, xbuf.at[0], sem.at[0], bm)

    @pl.when(i + 1 < n_used)
    def _():
        _gather_rows(tok_ref, (i + 1) * bm, h_hbm, xbuf.at[1 - slot], sem.at[1 - slot], bm)

    @pl.when(i < n_used)
    def _():
        @pl.when((i == 0) | (blk_e_ref[i] != blk_e_ref[jnp.maximum(i - 1, 0)]))
        def _():
            wg_b[...] = wg_ref[...].astype(jnp.bfloat16)
            wu_b[...] = wu_ref[...].astype(jnp.bfloat16)
            wd_b[...] = wd_ref[...].astype(jnp.bfloat16)

        _wait_rows(h_hbm, xbuf.at[slot], sem.at[slot], bm)
        x = xbuf[slot].astype(jnp.bfloat16)
        a = _dot(x, wg_b[...])
        hid = (a * _sigmoid(a)) * _dot(x, wu_b[...])
        y_ref[...] = _dot(hid.astype(jnp.bfloat16), wd_b[...])

    @pl.when(i >= n_used)
    def _():
        y_ref[...] = jnp.zeros_like(y_ref)


def _moe(blk_e, n_used, row_tok, h, w_g, w_u, w_d):
    n_rows = row_tok.shape[0]
    d = h.shape[1]
    hid = w_g.shape[2]
    bm = MOE_ROWS
    wspec = lambda shape: pl.BlockSpec((None,) + shape, lambda i, e, n, tok: (e[i], 0, 0))
    return pl.pallas_call(
        _moe_body,
        out_shape=jax.ShapeDtypeStruct((n_rows, d), jnp.float32),
        grid_spec=pltpu.PrefetchScalarGridSpec(
            num_scalar_prefetch=3, grid=(n_rows // bm,),
            in_specs=[pl.BlockSpec(memory_space=pl.ANY),
                      wspec((d, hid)), wspec((d, hid)), wspec((hid, d))],
            out_specs=pl.BlockSpec((bm, d), lambda i, e, n, tok: (i, 0)),
            scratch_shapes=[pltpu.VMEM((2, bm, d), jnp.float32),
                            pltpu.SemaphoreType.DMA((2,)),
                            pltpu.VMEM((d, hid), jnp.bfloat16),
                            pltpu.VMEM((d, hid), jnp.bfloat16),
                            pltpu.VMEM((hid, d), jnp.bfloat16)]),
        compiler_params=_cparams(("arbitrary",)),
        name="moe_experts",
    )(blk_e, n_used, row_tok, h, w_g, w_u, w_d)


def _final_body(d0_ref, d1_ref, h_ref, hb_ref, y_hbm, gw_ref, p_ref, wg_ref, wp_ref, lg_ref,
                lb_ref, y_o, yb_o, gbuf, sem, *, alpha):
    i = pl.program_id(0)
    n = pl.num_programs(0)
    tm = h_ref.shape[0]
    slot = i % 2

    def gather(step, s):
        _gather_rows(d0_ref, step * tm, y_hbm, gbuf.at[s, 0], sem.at[s], tm)
        _gather_rows(d1_ref, step * tm, y_hbm, gbuf.at[s, 1], sem.at[s], tm)

    @pl.when(i == 0)
    def _():
        gather(0, 0)

    @pl.when(i + 1 < n)
    def _():
        gather(i + 1, 1 - slot)

    gate = _sigmoid(_dot(hb_ref[...], wg_ref[...]))
    proj = _dot(p_ref[...].astype(jnp.bfloat16), wp_ref[...])
    _wait_rows(y_hbm, gbuf.at[slot, 0], sem.at[slot], tm)
    _wait_rows(y_hbm, gbuf.at[slot, 1], sem.at[slot], tm)
    gw = gw_ref[...]
    ffn = gbuf[slot, 0] * gw[:, 0:1] + gbuf[slot, 1] * gw[:, 1:2]
    y = _layer_norm(alpha * h_ref[...] + ffn + gate * proj, lg_ref[...], lb_ref[...])
    y_o[...] = y
    yb_o[...] = y.astype(jnp.bfloat16)


def _final(dest0, dest1, h, hb, y_rows, gate_w, p, w_pg, w_pp, ln_g, ln_b, alpha):
    t, d = h.shape
    tm = min(256, t)
    row = lambda i, *_: (i, 0)
    const2 = lambda i, *_: (0, 0)
    return pl.pallas_call(
        functools.partial(_final_body, alpha=alpha),
        out_shape=(jax.ShapeDtypeStruct((t, d), jnp.float32),
                   jax.ShapeDtypeStruct((t, d), jnp.bfloat16)),
        grid_spec=pltpu.PrefetchScalarGridSpec(
            num_scalar_prefetch=2, grid=(t // tm,),
            in_specs=[pl.BlockSpec((tm, d), row), pl.BlockSpec((tm, d), row),
                      pl.BlockSpec(memory_space=pl.ANY),
                      pl.BlockSpec((tm, TOP_K), row), pl.BlockSpec((tm, p.shape[1]), row),
                      _resident(w_pg.shape, const2), _resident(w_pp.shape, const2),
                      _resident((1, d), const2), _resident((1, d), const2)],
            out_specs=(pl.BlockSpec((tm, d), row), pl.BlockSpec((tm, d), row)),
            scratch_shapes=[pltpu.VMEM((2, TOP_K, tm, d), jnp.float32),
                            pltpu.SemaphoreType.DMA((2,))]),
        compiler_params=_cparams(("arbitrary",)),
        name="final_ln2",
    )(dest0, dest1, h, hb, y_rows, gate_w, p, w_pg, w_pp, ln_g, ln_b)


def _rope_tables(positions):
    pos = positions.reshape(-1).astype(jnp.float32)
    lane = jnp.arange(LANES)

    def tables(rot, period):
        half = rot // 2
        inv_freq = ROPE_THETA ** (-jnp.arange(half, dtype=jnp.float32) / half)
        ang = pos[:, None] * inv_freq
        cos, sin = jnp.cos(ang), jnp.sin(ang)
        lp = lane % period
        idx = lp % half
        in_rot = lp < rot
        c = jnp.where(in_rot[None, :], cos[:, idx], 1.0)
        s_up = jnp.where((lp < half)[None, :], -sin[:, idx], 0.0)
        s_dn = jnp.where(((lp >= half) & in_rot)[None, :], sin[:, idx], 0.0)
        return c, s_up, s_dn

    return tables(DIFF_ROT, DIFF_QK) + tables(MLA_ROPE, LANES)


def _reorder_w_in(w):
    d = w.shape[0]
    widths = [512, 512, 512, 512, 512, 512, HEADS, MLA_Q_LORA, MLA_KV_LORA, MLA_ROPE,
              512, 512, 512, N_BRANCHES * d]
    offs = [0]
    for wd in widths:
        offs.append(offs[-1] + wd)
    part = lambda n: w[:, offs[n]:offs[n + 1]]
    (dq, dk, dv, fq, fk, fv, ff, cq, ckv, kr, sq, sk, sv, gates) = [part(n) for n in range(14)]
    bf = jnp.bfloat16
    zeros = lambda n: jnp.zeros((d, n), bf)
    cols = [gates, dq, dk, dv, fq, fk, fv, sq, sk, sv, cq, ckv, kr, zeros(LANES - MLA_ROPE),
            ff, zeros(LANES - HEADS)]
    w1 = jnp.concatenate([c.astype(bf) for c in cols], axis=1)
    w_f = jnp.concatenate([ff.astype(bf), zeros(LANES - HEADS)], axis=1)
    return w1, w_f


def _route(logits, b_rg, b_re, t):
    g_logits = logits[:, :N_GROUPS] + b_rg
    grp = jnp.argmax(g_logits, axis=-1)
    p_grp = jnp.take_along_axis(jax.nn.softmax(g_logits, axis=-1), grp[:, None], axis=-1)
    e_logits = (logits[:, N_GROUPS:N_GROUPS + N_EXPERTS] + b_re).reshape(
        t, N_GROUPS, EXPERTS_PER_GROUP)
    e_in = jnp.take_along_axis(e_logits, grp[:, None, None], axis=1)[:, 0]
    top_p, top_i = lax.top_k(jax.nn.softmax(e_in, axis=-1), TOP_K)
    weights = p_grp * top_p / jnp.sum(top_p, axis=-1, keepdims=True)
    expert_idx = grp[:, None].astype(jnp.int32) * EXPERTS_PER_GROUP + top_i.astype(jnp.int32)
    return expert_idx, weights


def _dispatch_plan(expert_idx, t):
    a = t * TOP_K
    flat_e = expert_idx.reshape(a)
    onehot = (flat_e[:, None] == jnp.arange(N_EXPERTS)[None, :]).astype(jnp.int32)
    rank = jnp.take_along_axis(jnp.cumsum(onehot, axis=0), flat_e[:, None], axis=1)[:, 0] - 1
    sizes = jnp.sum(onehot, axis=0)
    padded = (sizes + MOE_ROWS - 1) // MOE_ROWS * MOE_ROWS
    pad_end = jnp.cumsum(padded)
    pad_start = pad_end - padded
    dest = (pad_start[flat_e] + rank).astype(jnp.int32)
    n_rows = a + N_EXPERTS * MOE_ROWS
    n_blk = n_rows // MOE_ROWS
    flat_tok = jnp.arange(a, dtype=jnp.int32) // TOP_K
    row_tok = jnp.zeros((n_rows,), jnp.int32).at[dest].set(flat_tok)
    blk_start = jnp.arange(n_blk, dtype=pad_end.dtype) * MOE_ROWS
    blk_e = jnp.minimum(jnp.sum(pad_end[None, :] <= blk_start[:, None], axis=1),
                        N_EXPERTS - 1).astype(jnp.int32)
    n_used = (pad_end[-1] // MOE_ROWS).astype(jnp.int32).reshape(1)
    return dest.reshape(t, TOP_K), row_tok, blk_e, n_used


def kernel(x, p, positions, w_in, fox_f_bias, diff_lambda, diff_subln, mla_q_norm, mla_kv_norm,
           mla_w_uq, mla_w_ukv, w_branch, w_o, ln1_g, ln1_b, w_router_group, b_router_group,
           w_router_expert, b_router_expert, w_expert_gate, w_expert_up, w_expert_down,
           w_ple_gate, w_ple_proj, ln2_g, ln2_b):
    batch, seq, d = x.shape
    depth = w_in.shape[0]
    t = batch * seq
    alpha = (2 * depth) ** 0.25
    bf = jnp.bfloat16
    tabs = _rope_tables(positions)
    col_scale = jnp.ones((1, U_WIDTH), jnp.float32)
    col_scale = col_scale.at[0, COL_FQ:COL_FQ + BRANCH_WIDTH].set(HEAD_DIM ** -0.5 * LOG2E)
    col_scale = col_scale.at[0, COL_SQ:COL_SQ + BRANCH_WIDTH].set(HEAD_DIM ** -0.5)
    xf = x.reshape(t, d)
    xb = xf.astype(bf)

    for i in range(depth):
        w1, w_f = _reorder_w_in(w_in[i])
        u = _matmul(xb, w1, col_scale, bf, 1024, 1536, "in_proj")

        b_f = jnp.zeros((1, LANES), jnp.float32).at[0, :HEADS].set(fox_f_bias[i])
        cum = _fgate_cumsum(xb, w_f, b_f, batch, seq)[:, :HEADS] * LOG2E
        cum_bh = jnp.transpose(cum.reshape(batch, seq, HEADS), (0, 2, 1)).reshape(
            batch * HEADS, seq)
        cq_rows = cum_bh.reshape(batch * HEADS, 1, seq)
        ck_rep = jnp.broadcast_to(cum_bh[:, :, None], (batch * HEADS, seq, LANES))

        uq = mla_w_uq[i].reshape(MLA_Q_LORA, HEADS, MLA_NOPE + MLA_ROPE)
        uq = jnp.pad(uq, ((0, 0), (0, 0), (0, MLA_QK_PAD - MLA_NOPE - MLA_ROPE)))
        uq = uq.reshape(MLA_Q_LORA, HEADS * MLA_QK_PAD).astype(bf)
        ukv = mla_w_ukv[i].reshape(MLA_KV_LORA, HEADS, MLA_NOPE + HEAD_DIM)
        uk = ukv[:, :, :MLA_NOPE].reshape(MLA_KV_LORA, HEADS * MLA_NOPE).astype(bf)
        uv = ukv[:, :, MLA_NOPE:].reshape(MLA_KV_LORA, HEADS * HEAD_DIM).astype(bf)
        dqk, mq, mk, mv = _prep(u, tabs, mla_q_norm[i].reshape(1, -1),
                                mla_kv_norm[i].reshape(1, -1), uq, uk, uv)

        lam_init = 0.8 - 0.6 * math.exp(-0.3 * i)
        lp = diff_lambda[i].astype(jnp.float32)
        lam = (jnp.exp(jnp.sum(lp[0] * lp[1])) - jnp.exp(jnp.sum(lp[2] * lp[3]))
               + lam_init).reshape(1)
        blk = lambda col: col // HEAD_DIM
        o_d = _attention("diff", batch, seq, dqk, 0, dqk, HEADS, u, blk(COL_DV), HEAD_DIM,
                         extras=(diff_subln[i].reshape(1, HEAD_DIM),),
                         lam=lam, lam_init=lam_init)
        o_f = _attention("fox", batch, seq, u, blk(COL_FQ), u, blk(COL_FK), u, blk(COL_FV),
                         HEAD_DIM, extras=(cq_rows, ck_rep))
        o_m = _attention("mla", batch, seq, mq, 0, mk, 0, mv, 0, MLA_QK_PAD)
        o_s = _attention("sb", batch, seq, u, blk(COL_SQ), u, blk(COL_SK), u, blk(COL_SV),
                         HEAD_DIM)

        w_r = jnp.concatenate([w_router_group[i], w_router_expert[i]], axis=1)
        w_r = jnp.pad(w_r, ((0, 0), (0, LANES - w_r.shape[1])))
        wr_hi = w_r.astype(bf)
        wr_lo = (w_r - wr_hi.astype(jnp.float32)).astype(bf)
        h, hb, logits = _merge(u, o_d, o_f, o_m, o_s, w_branch[i].astype(bf), w_o[i].astype(bf),
                               xf, ln1_g[i].reshape(1, d), ln1_b[i].reshape(1, d),
                               wr_hi, wr_lo, alpha)

        expert_idx, weights = _route(logits, b_router_group[i], b_router_expert[i], t)
        dest, row_tok, blk_e, n_used = _dispatch_plan(expert_idx, t)
        y_rows = _moe(blk_e, n_used, row_tok, h, w_expert_gate[i], w_expert_up[i],
                      w_expert_down[i])
        xf, xb = _final(dest[:, 0], dest[:, 1], h, hb, y_rows, weights.astype(jnp.float32),
                        p[i].reshape(t, -1), w_ple_gate[i].astype(bf),
                        w_ple_proj[i].astype(bf), ln2_g[i].reshape(1, d),
                        ln2_b[i].reshape(1, d), alpha)
    return xf.reshape(batch, seq, d)
```

```python
import functools
import math

import jax
import jax.numpy as jnp
from jax import lax
from jax.experimental import pallas as pl
from jax.experimental.pallas import tpu as pltpu

HEAD_DIM = 128
HEADS = 4
N_BRANCHES = 4
BRANCH_WIDTH = HEADS * HEAD_DIM
DIFF_QK = 64
DIFF_ROT = DIFF_QK // 4
MLA_Q_LORA = 512
MLA_KV_LORA = 256
MLA_NOPE = 128
MLA_ROPE = 64
MLA_QK_PAD = 256
N_GROUPS = 4
EXPERTS_PER_GROUP = 8
N_EXPERTS = N_GROUPS * EXPERTS_PER_GROUP
TOP_K = 2
CHUNK = 64
ROPE_THETA = 500000.0
LN_EPS = 1e-5
RMS_EPS = 1e-6
NEG_BIG = -1e30
LOG2E = 1.4426950408889634
LANES = 128
V7X_VMEM_LIMIT = 56 * 1024 * 1024

COL_GATES = 0
COL_DQ = 8192
COL_DV = COL_DQ + 1024
COL_FQ = COL_DV + 512
COL_FK = COL_FQ + 512
COL_FV = COL_FK + 512
COL_SQ = COL_FV + 512
COL_SK = COL_SQ + 512
COL_SV = COL_SK + 512
COL_CQ = COL_SV + 512
COL_CKV = COL_CQ + 512
COL_KR = COL_CKV + 256
COL_FF = COL_KR + 128
U_WIDTH = COL_FF + 128

ATT_TQ = 512
MOE_ROWS = 256


def _cparams(sem, vmem=V7X_VMEM_LIMIT):
    return pltpu.CompilerParams(dimension_semantics=sem, vmem_limit_bytes=vmem)


def _resident(shape, index_map):
    return pl.BlockSpec(shape, index_map, pipeline_mode=pl.Buffered(1))


def _matmul_body(x_ref, w_ref, c_ref, o_ref):
    acc = jnp.dot(x_ref[...], w_ref[...], preferred_element_type=jnp.float32)
    o_ref[...] = (acc * c_ref[...]).astype(o_ref.dtype)


def _matmul(x, w, col_scale, out_dtype, tm, tn, name):
    m, k = x.shape
    n = w.shape[1]
    tm, tn = min(tm, m), min(tn, n)
    return pl.pallas_call(
        _matmul_body,
        out_shape=jax.ShapeDtypeStruct((m, n), out_dtype),
        grid=(m // tm, n // tn),
        in_specs=[pl.BlockSpec((tm, k), lambda i, j: (i, 0)),
                  pl.BlockSpec((k, tn), lambda i, j: (0, j)),
                  pl.BlockSpec((1, tn), lambda i, j: (0, j))],
        out_specs=pl.BlockSpec((tm, tn), lambda i, j: (i, j)),
        compiler_params=_cparams(("parallel", "arbitrary")),
        name=name,
    )(x, w, col_scale)


def _split3(x):
    hi = x.astype(jnp.bfloat16)
    r1 = x - hi.astype(jnp.float32)
    mid = r1.astype(jnp.bfloat16)
    lo = (r1 - mid.astype(jnp.float32)).astype(jnp.bfloat16)
    return hi, mid, lo


def _split2(x):
    hi = x.astype(jnp.bfloat16)
    mid = (x - hi.astype(jnp.float32)).astype(jnp.bfloat16)
    return hi, mid


def _dot(a, b):
    return jnp.dot(a, b, preferred_element_type=jnp.float32)


def _dot_nt(a, b):
    return lax.dot_general(a, b, (((1,), (1,)), ((), ())), preferred_element_type=jnp.float32)


def _dot_tn(a, b):
    return lax.dot_general(a, b, (((0,), (0,)), ((), ())), preferred_element_type=jnp.float32)


def _log_sigmoid(z):
    neg_abs = pltpu.bitcast(pltpu.bitcast(z, jnp.uint32) | jnp.uint32(0x80000000), jnp.float32)
    return jnp.minimum(z, 0.0) - jnp.log(1.0 + jnp.exp(neg_abs))


def _sigmoid(z):
    return 1.0 / (1.0 + jnp.exp(-z))


def _fgate_body(x_ref, w_ref, b_ref, o_ref, carry_ref):
    @pl.when(pl.program_id(1) == 0)
    def _():
        carry_ref[...] = jnp.zeros_like(carry_ref)

    f = _dot(x_ref[...], w_ref[...]) + b_ref[...]
    log_f = _log_sigmoid(f)
    ts = log_f.shape[0]
    row = lax.broadcasted_iota(jnp.int32, (ts, ts), 0)
    col = lax.broadcasted_iota(jnp.int32, (ts, ts), 1)
    tri = (col <= row).astype(jnp.bfloat16)
    hi, mid, lo = _split3(log_f)
    cum = _dot(tri, hi) + _dot(tri, mid) + _dot(tri, lo) + carry_ref[...]
    o_ref[...] = cum
    carry_ref[...] = cum[ts - 1:ts, :]


def _fgate_cumsum(xb, w_f, b_f, batch, seq):
    t, d = xb.shape
    ts = min(512, seq)
    ns = seq // ts
    return pl.pallas_call(
        _fgate_body,
        out_shape=jax.ShapeDtypeStruct((t, LANES), jnp.float32),
        grid=(batch, ns),
        in_specs=[pl.BlockSpec((ts, d), lambda b, s: (b * ns + s, 0)),
                  _resident((d, LANES), lambda b, s: (0, 0)),
                  _resident((1, LANES), lambda b, s: (0, 0))],
        out_specs=pl.BlockSpec((ts, LANES), lambda b, s: (b * ns + s, 0)),
        scratch_shapes=[pltpu.VMEM((1, LANES), jnp.float32)],
        compiler_params=_cparams(("parallel", "arbitrary")),
        name="fgate_cumsum",
    )(xb, w_f, b_f)


def _rope_lanes(x, c, s_up, s_dn, half):
    return x * c + pltpu.roll(x, LANES - half, 1) * s_up + pltpu.roll(x, half, 1) * s_dn


def _prep_body(dqk_ref, cq_ref, ckv_ref, kr_ref, c16_ref, su16_ref, sd16_ref,
               c64_ref, su64_ref, sd64_ref, qg_ref, kvg_ref, wuq_ref, wuk_ref, wuv_ref,
               dqk_o, mq_o, mk_o, mv_o):
    c16, su16, sd16 = c16_ref[...], su16_ref[...], sd16_ref[...]
    c64, su64, sd64 = c64_ref[...], su64_ref[...], sd64_ref[...]
    for j in range(dqk_ref.shape[1] // LANES):
        sl = slice(j * LANES, (j + 1) * LANES)
        x = _rope_lanes(dqk_ref[:, sl].astype(jnp.float32), c16, su16, sd16, DIFF_ROT // 2)
        if j < HEADS:
            x = x * (DIFF_QK ** -0.5 * LOG2E)
        dqk_o[:, sl] = x.astype(dqk_o.dtype)

    cq = cq_ref[...].astype(jnp.float32)
    cqn = cq * lax.rsqrt(jnp.mean(cq * cq, axis=-1, keepdims=True) + RMS_EPS) * qg_ref[...]
    q = _dot(cqn.astype(jnp.bfloat16), wuq_ref[...])
    q = q * ((MLA_NOPE + MLA_ROPE) ** -0.5 * LOG2E)
    ckv = ckv_ref[...].astype(jnp.float32)
    ckvn = (ckv * lax.rsqrt(jnp.mean(ckv * ckv, axis=-1, keepdims=True) + RMS_EPS)
            * kvg_ref[...]).astype(jnp.bfloat16)
    kn = _dot(ckvn, wuk_ref[...])
    mv_o[...] = _dot(ckvn, wuv_ref[...]).astype(mv_o.dtype)
    kr = _rope_lanes(kr_ref[...].astype(jnp.float32), c64, su64, sd64, MLA_ROPE // 2)
    for h in range(HEADS):
        base = h * MLA_QK_PAD
        mq_o[:, base:base + LANES] = q[:, base:base + LANES].astype(mq_o.dtype)
        qr = _rope_lanes(q[:, base + LANES:base + 2 * LANES], c64, su64, sd64, MLA_ROPE // 2)
        mq_o[:, base + LANES:base + 2 * LANES] = qr.astype(mq_o.dtype)
        mk_o[:, base:base + LANES] = kn[:, h * LANES:(h + 1) * LANES].astype(mk_o.dtype)
        mk_o[:, base + LANES:base + 2 * LANES] = kr.astype(mk_o.dtype)


def _prep(u, tabs, q_norm, kv_norm, w_uq, w_uk, w_uv):
    t = u.shape[0]
    tm = min(512, t)
    row = lambda blk: (lambda i: (i, blk))
    const = lambda i: (0, 0)
    tab_spec = pl.BlockSpec((tm, LANES), lambda i: (i, 0))
    outs = pl.pallas_call(
        _prep_body,
        out_shape=(jax.ShapeDtypeStruct((t, 1024), jnp.bfloat16),
                   jax.ShapeDtypeStruct((t, HEADS * MLA_QK_PAD), jnp.bfloat16),
                   jax.ShapeDtypeStruct((t, HEADS * MLA_QK_PAD), jnp.bfloat16),
                   jax.ShapeDtypeStruct((t, BRANCH_WIDTH), jnp.bfloat16)),
        grid=(t // tm,),
        in_specs=[pl.BlockSpec((tm, 1024), row(COL_DQ // 1024)),
                  pl.BlockSpec((tm, MLA_Q_LORA), row(COL_CQ // MLA_Q_LORA)),
                  pl.BlockSpec((tm, MLA_KV_LORA), row(COL_CKV // MLA_KV_LORA)),
                  pl.BlockSpec((tm, LANES), row(COL_KR // LANES)),
                  tab_spec, tab_spec, tab_spec, tab_spec, tab_spec, tab_spec,
                  _resident((1, MLA_Q_LORA), const),
                  _resident((1, MLA_KV_LORA), const),
                  _resident(w_uq.shape, const),
                  _resident(w_uk.shape, const),
                  _resident(w_uv.shape, const)],
        out_specs=(pl.BlockSpec((tm, 1024), lambda i: (i, 0)),
                   pl.BlockSpec((tm, HEADS * MLA_QK_PAD), lambda i: (i, 0)),
                   pl.BlockSpec((tm, HEADS * MLA_QK_PAD), lambda i: (i, 0)),
                   pl.BlockSpec((tm, BRANCH_WIDTH), lambda i: (i, 0))),
        compiler_params=_cparams(("parallel",)),
        name="prep",
    )(u, u, u, u, *tabs, q_norm, kv_norm, w_uq, w_uk, w_uv)
    return outs


def _attn_body(*refs, kind, tq, tk, lam_init):
    softmax = kind != "sb"
    if kind == "diff":
        lam_ref, q_ref, k_ref, v_ref, g_ref, o_ref = refs[:6]
    elif kind == "fox":
        q_ref, k_ref, v_ref, cq_ref, ck_ref, o_ref = refs[:6]
    else:
        q_ref, k_ref, v_ref, o_ref = refs[:4]
    if softmax:
        s_bufs, p_bufs = refs[-8:-6], refs[-6:-4]
        m_ref, l_ref, a_ref, acc_ref = refs[-4:]
    else:
        s_bufs, p_bufs = refs[-6:-4], refs[-4:-2]
        r_ref, acc_ref = refs[-2:]
    qi = pl.program_id(2)
    q0 = qi * tq
    last_kb = 2 * qi + 1
    q = q_ref[...]
    n_comp = 2 if kind == "diff" else 1
    if kind == "diff":
        lane = lax.broadcasted_iota(jnp.int32, q.shape, 1)
        qs = [jnp.where(lane < DIFF_QK, q, jnp.zeros_like(q)),
              jnp.where(lane >= DIFF_QK, q, jnp.zeros_like(q))]
    else:
        qs = [q]

    acc_ref[...] = jnp.zeros_like(acc_ref)
    p_bufs[1][...] = jnp.zeros_like(p_bufs[1])
    if softmax:
        m_ref[...] = jnp.full_like(m_ref, NEG_BIG)
        l_ref[...] = jnp.zeros_like(l_ref)
        a_ref[...] = jnp.zeros_like(a_ref)
    else:
        r_ref[...] = jnp.zeros_like(r_ref)

    def key_rows(kb):
        if isinstance(kb, int):
            return pl.ds(kb * tk, tk)
        return pl.ds(pl.multiple_of(kb * tk, tk), tk)

    def scores_stage(kb, s_dst):
        k_t = k_ref[key_rows(kb), :]
        for c in range(n_comp):
            s_dst[c] = _dot_nt(k_t, qs[c])

    def values_stage(kb, p_src):
        v_t = v_ref[key_rows(kb), :]
        for c in range(n_comp):
            pv = _dot_tn(v_t, p_src[c])
            if softmax:
                acc_ref[c] = a_ref[c] * acc_ref[c] + pv
            else:
                acc_ref[c] = acc_ref[c] + pv

    def weights_stage(kb, s_src, p_dst, masked):
        if masked:
            j = lax.broadcasted_iota(jnp.int32, (tk, tq), 0)
            i = lax.broadcasted_iota(jnp.int32, (tk, tq), 1)
            off = q0 - kb * tk
            if kind in ("diff", "mla"):
                mask = (j // CHUNK - i // CHUNK) <= off // CHUNK
            elif kind == "fox":
                mask = (j - i) <= off
            else:
                mask = (j - i) < off
        if not softmax:
            z_t = s_src[0]
            ls = _log_sigmoid(z_t)
            lk = ls - z_t
            if masked:
                lk = jnp.where(mask, lk, 0.0)
            jj = lax.broadcasted_iota(jnp.int32, (tk, tk), 0)
            mm = lax.broadcasted_iota(jnp.int32, (tk, tk), 1)
            upper = (mm > jj).astype(jnp.bfloat16)
            hi, mid = _split2(lk)
            later = r_ref[...] + _dot(jnp.concatenate([upper, upper], axis=1),
                                      jnp.concatenate([hi, mid], axis=0))
            w_t = jnp.exp(ls + later)
            if masked:
                w_t = jnp.where(mask, w_t, 0.0)
            p_dst[0] = w_t.astype(jnp.bfloat16)
            r_ref[...] += jnp.sum(lk, axis=0, keepdims=True)
            return
        for c in range(n_comp):
            s_t = s_src[c]
            if kind == "fox":
                ck = ck_ref[key_rows(kb), :]
                s_t = s_t + cq_ref[...] - jnp.concatenate([ck] * (tq // LANES), axis=1)
            if masked:
                s_t = jnp.where(mask, s_t, NEG_BIG)
            m_old = m_ref[c]
            m_new = jnp.maximum(m_old, jnp.max(s_t, axis=0, keepdims=True))
            alpha = jnp.exp2(m_old - m_new)
            p_t = jnp.exp2(s_t - m_new)
            l_ref[c] = alpha * l_ref[c] + jnp.sum(p_t, axis=0, keepdims=True)
            m_ref[c] = m_new
            a_ref[c] = alpha
            p_dst[c] = p_t.astype(jnp.bfloat16)

    def step(kb, parity, masked, next_kb, prev_kb):
        if next_kb is not None:
            scores_stage(next_kb, s_bufs[1 - parity])
        values_stage(prev_kb, p_bufs[1 - parity])
        weights_stage(kb, s_bufs[parity], p_bufs[parity], masked)

    if softmax:
        scores_stage(0, s_bufs[0])

        def body(t, carry):
            kb = 2 * t
            step(kb, 0, False, kb + 1, jnp.maximum(kb - 1, 0))
            step(kb + 1, 1, False, kb + 2, kb)
            return carry
        lax.fori_loop(0, qi, body, 0)
        step(last_kb - 1, 0, True, last_kb, jnp.maximum(last_kb - 2, 0))
        step(last_kb, 1, True, None, last_kb - 1)
        values_stage(last_kb, p_bufs[1])
    else:
        scores_stage(last_kb, s_bufs[0])
        step(last_kb, 0, True, last_kb - 1, last_kb)
        step(last_kb - 1, 1, True, jnp.maximum(last_kb - 2, 0), last_kb)

        def body(t, carry):
            kb = last_kb - 2 - 2 * t
            step(kb, 0, False, kb - 1, kb + 1)
            step(kb - 1, 1, False, jnp.maximum(kb - 2, 0), kb)
            return carry
        lax.fori_loop(0, qi, body, 0)
        values_stage(0, p_bufs[1])

    if kind == "sb":
        o_t = acc_ref[0]
    elif kind == "diff":
        o_t = acc_ref[0] / l_ref[0] - lam_ref[0] * (acc_ref[1] / l_ref[1])
    else:
        o_t = acc_ref[0] / l_ref[0]
    o = o_t.T
    if kind == "diff":
        o = (o * lax.rsqrt(jnp.mean(o * o, axis=-1, keepdims=True) + RMS_EPS)
             * g_ref[...] * (1.0 - lam_init))
    o_ref[...] = o.astype(o_ref.dtype)


def _attention(kind, batch, seq, q_arr, q_col, k_arr, k_col, v_arr, v_col, dq,
               extras=(), lam=None, lam_init=0.0):
    t = q_arr.shape[0]
    tq = min(ATT_TQ, seq)
    tk = tq // 2
    nq = seq // tq
    q_spec = pl.BlockSpec((tq, dq), lambda b, h, i, *_: (b * nq + i, q_col + h))
    k_spec = pl.BlockSpec((seq, dq), lambda b, h, i, *_: (b, k_col + h))
    v_spec = pl.BlockSpec((seq, HEAD_DIM), lambda b, h, i, *_: (b, v_col + h))
    in_specs = [q_spec, k_spec, v_spec]
    args = [q_arr, k_arr, v_arr]
    n_comp = 2 if kind == "diff" else 1
    if kind == "diff":
        in_specs.append(_resident((1, HEAD_DIM), lambda b, h, i, *_: (0, 0)))
        args.append(extras[0])
    elif kind == "fox":
        cq_rows, ck_rep = extras
        in_specs.append(pl.BlockSpec((None, 1, tq), lambda b, h, i: (b * HEADS + h, 0, i)))
        in_specs.append(pl.BlockSpec((None, seq, LANES), lambda b, h, i: (b * HEADS + h, 0, 0)))
        args += [cq_rows, ck_rep]
    s_buf = pltpu.VMEM((n_comp, tk, tq), jnp.float32)
    p_buf = pltpu.VMEM((n_comp, tk, tq), jnp.bfloat16)
    row = pltpu.VMEM((n_comp, 1, tq), jnp.float32)
    acc = pltpu.VMEM((n_comp, HEAD_DIM, tq), jnp.float32)
    if kind == "sb":
        scratch = [s_buf, s_buf, p_buf, p_buf, pltpu.VMEM((1, tq), jnp.float32), acc]
    else:
        scratch = [s_buf, s_buf, p_buf, p_buf, row, row, row, acc]
    body = functools.partial(_attn_body, kind=kind, tq=tq, tk=tk, lam_init=lam_init)
    grid = (batch, HEADS, nq)
    out_shape = jax.ShapeDtypeStruct((t, BRANCH_WIDTH), jnp.bfloat16)
    out_spec = pl.BlockSpec((tq, HEAD_DIM), lambda b, h, i, *_: (b * nq + i, h))
    cp = _cparams(("parallel", "parallel", "arbitrary"))
    if kind == "diff":
        return pl.pallas_call(
            body, out_shape=out_shape,
            grid_spec=pltpu.PrefetchScalarGridSpec(
                num_scalar_prefetch=1, grid=grid, in_specs=in_specs, out_specs=out_spec,
                scratch_shapes=scratch),
            compiler_params=cp, name="attn_" + kind)(lam, *args)
    return pl.pallas_call(
        body, out_shape=out_shape, grid=grid, in_specs=in_specs, out_specs=out_spec,
        scratch_shapes=scratch, compiler_params=cp, name="attn_" + kind)(*args)


def _layer_norm(y, g, b):
    mu = jnp.mean(y, axis=-1, keepdims=True)
    yc = y - mu
    var = jnp.mean(yc * yc, axis=-1, keepdims=True)
    return yc * lax.rsqrt(var + LN_EPS) * g + b


def _merge_body(g_ref, od_ref, of_ref, om_ref, os_ref, wb_ref, wo_ref, x_ref, lg_ref, lb_ref,
                wrh_ref, wrl_ref, h_o, hr_o, hb_o, lgt_o, *, alpha):
    d = x_ref.shape[1]
    merged = None
    for n, o_ref in enumerate((od_ref, of_ref, om_ref, os_ref)):
        gate = _sigmoid(g_ref[:, n * d:(n + 1) * d].astype(jnp.float32))
        term = gate * _dot(o_ref[...], wb_ref[n])
        merged = term if merged is None else merged + term
    mix = _dot(merged.astype(jnp.bfloat16), wo_ref[...])
    h = _layer_norm(alpha * x_ref[...] + mix, lg_ref[...], lb_ref[...])
    h_o[...] = h
    _to_chunked(hr_o, h)
    h_hi = h.astype(jnp.bfloat16)
    hb_o[...] = h_hi
    h_lo = (h - h_hi.astype(jnp.float32)).astype(jnp.bfloat16)
    lgt_o[...] = (_dot(h_hi, wrh_ref[...]) + _dot(h_lo, wrh_ref[...]) + _dot(h_hi, wrl_ref[...]))


def _merge(u, o_d, o_f, o_m, o_s, w_b, w_o, x, ln_g, ln_b, wr_hi, wr_lo, alpha):
    t, d = x.shape
    tm = min(256, t)
    row = lambda i: (i, 0)
    const2 = lambda i: (0, 0)
    o_spec = pl.BlockSpec((tm, BRANCH_WIDTH), row)
    return pl.pallas_call(
        functools.partial(_merge_body, alpha=alpha),
        out_shape=(jax.ShapeDtypeStruct((t, d), jnp.float32),
                   jax.ShapeDtypeStruct((t * d // LANES, LANES), jnp.float32),
                   jax.ShapeDtypeStruct((t, d), jnp.bfloat16),
                   jax.ShapeDtypeStruct((t, LANES), jnp.float32)),
        grid=(t // tm,),
        in_specs=[pl.BlockSpec((tm, N_BRANCHES * d), row),
                  o_spec, o_spec, o_spec, o_spec,
                  _resident(w_b.shape, lambda i: (0, 0, 0)),
                  _resident(w_o.shape, const2),
                  pl.BlockSpec((tm, d), row),
                  _resident((1, d), const2), _resident((1, d), const2),
                  _resident(wr_hi.shape, const2), _resident(wr_lo.shape, const2)],
        out_specs=(pl.BlockSpec((tm, d), row), pl.BlockSpec((tm * d // LANES, LANES), row),
                   pl.BlockSpec((tm, d), row), pl.BlockSpec((tm, LANES), row)),
        compiler_params=_cparams(("parallel",)),
        name="merge_ln1",
    )(u, o_d, o_f, o_m, o_s, w_b, w_o, x, ln_g, ln_b, wr_hi, wr_lo)


def _to_chunked(ref, x):
    n, d = x.shape
    nc = d // LANES
    for c in range(nc):
        ref[pl.ds(c, n, stride=nc), :] = x[:, c * LANES:(c + 1) * LANES]


def _from_chunked(ref, n, nc):
    return jnp.concatenate([ref[pl.ds(c, n, stride=nc), :] for c in range(nc)], axis=1)


def _gather_rows(idx_ref, base, src_hbm, dst, sem, n, nc):
    for r in range(n):
        row = pl.multiple_of(idx_ref[base + r] * nc, nc)
        pltpu.make_async_copy(src_hbm.at[pl.ds(row, nc), :], dst.at[pl.ds(r * nc, nc), :],
                              sem).start()


def _wait_rows(src_hbm, dst, sem, n, nc):
    pltpu.make_async_copy(src_hbm.at[pl.ds(0, n * nc), :], dst, sem).wait()


def _moe_body(blk_e_ref, n_used_ref, tok_ref, h_hbm, wg_ref, wu_ref, wd_ref, y_ref,
              xbuf, sem, wg_b, wu_b, wd_b):
    i = pl.program_id(0)
    n_used = n_used_ref[0]
    nc = wg_ref.shape[0] // LANES
    bm = xbuf.shape[1] // nc
    slot = i % 2

    @pl.when((i == 0) & (n_used > 0))
    def _():
        _gather_rows(tok_ref, 0, h_hbm, xbuf.at[0], sem.at[0], bm, nc)

    @pl.when(i + 1 < n_used)
    def _():
        _gather_rows(tok_ref, (i + 1) * bm, h_hbm, xbuf.at[1 - slot], sem.at[1 - slot], bm, nc)

    @pl.when(i < n_used)
    def _():
        @pl.when((i == 0) | (blk_e_ref[i] != blk_e_ref[jnp.maximum(i - 1, 0)]))
        def _():
            wg_b[...] = wg_ref[...].astype(jnp.bfloat16)
            wu_b[...] = wu_ref[...].astype(jnp.bfloat16)
            wd_b[...] = wd_ref[...].astype(jnp.bfloat16)

        _wait_rows(h_hbm, xbuf.at[slot], sem.at[slot], bm, nc)
        x = _from_chunked(xbuf.at[slot], bm, nc).astype(jnp.bfloat16)
        a = _dot(x, wg_b[...])
        hid = (a * _sigmoid(a)) * _dot(x, wu_b[...])
        _to_chunked(y_ref, _dot(hid.astype(jnp.bfloat16), wd_b[...]))

    @pl.when(i >= n_used)
    def _():
        y_ref[...] = jnp.zeros_like(y_ref)


def _moe(blk_e, n_used, row_tok, h_rows, w_g, w_u, w_d, layer):
    n_rows = row_tok.shape[0]
    d, hid = w_g.shape[2], w_g.shape[3]
    nc = d // LANES
    bm = MOE_ROWS
    wspec = lambda shape: pl.BlockSpec((None, None) + shape,
                                       lambda i, e, n, tok: (layer, e[i], 0, 0))
    return pl.pallas_call(
        _moe_body,
        out_shape=jax.ShapeDtypeStruct((n_rows * nc, LANES), jnp.float32),
        grid_spec=pltpu.PrefetchScalarGridSpec(
            num_scalar_prefetch=3, grid=(n_rows // bm,),
            in_specs=[pl.BlockSpec(memory_space=pl.ANY),
                      wspec((d, hid)), wspec((d, hid)), wspec((hid, d))],
            out_specs=pl.BlockSpec((bm * nc, LANES), lambda i, e, n, tok: (i, 0)),
            scratch_shapes=[pltpu.VMEM((2, bm * nc, LANES), jnp.float32),
                            pltpu.SemaphoreType.DMA((2,)),
                            pltpu.VMEM((d, hid), jnp.bfloat16),
                            pltpu.VMEM((d, hid), jnp.bfloat16),
                            pltpu.VMEM((hid, d), jnp.bfloat16)]),
        compiler_params=_cparams(("arbitrary",)),
        name="moe_experts",
    )(blk_e, n_used, row_tok, h_rows, w_g, w_u, w_d)


def _final_body(d0_ref, d1_ref, h_ref, hb_ref, y_hbm, gw_ref, p_ref, wg_ref, wp_ref, lg_ref,
                lb_ref, y_o, yb_o, gbuf, sem, *, alpha):
    i = pl.program_id(0)
    n = pl.num_programs(0)
    tm, d = h_ref.shape
    nc = d // LANES
    slot = i % 2

    def gather(step, s):
        _gather_rows(d0_ref, step * tm, y_hbm, gbuf.at[s, 0], sem.at[s], tm, nc)
        _gather_rows(d1_ref, step * tm, y_hbm, gbuf.at[s, 1], sem.at[s], tm, nc)

    @pl.when(i == 0)
    def _():
        gather(0, 0)

    @pl.when(i + 1 < n)
    def _():
        gather(i + 1, 1 - slot)

    gate = _sigmoid(_dot(hb_ref[...], wg_ref[...]))
    proj = _dot(p_ref[...].astype(jnp.bfloat16), wp_ref[...])
    _wait_rows(y_hbm, gbuf.at[slot, 0], sem.at[slot], tm, nc)
    _wait_rows(y_hbm, gbuf.at[slot, 1], sem.at[slot], tm, nc)
    gw = gw_ref[...]
    ffn = (_from_chunked(gbuf.at[slot, 0], tm, nc) * gw[:, 0:1]
           + _from_chunked(gbuf.at[slot, 1], tm, nc) * gw[:, 1:2])
    y = _layer_norm(alpha * h_ref[...] + ffn + gate * proj, lg_ref[...], lb_ref[...])
    y_o[...] = y
    yb_o[...] = y.astype(jnp.bfloat16)


def _final(dest0, dest1, h, hb, y_rows, gate_w, p, w_pg, w_pp, ln_g, ln_b, alpha):
    t, d = h.shape
    tm = min(256, t)
    row = lambda i, *_: (i, 0)
    const2 = lambda i, *_: (0, 0)
    return pl.pallas_call(
        functools.partial(_final_body, alpha=alpha),
        out_shape=(jax.ShapeDtypeStruct((t, d), jnp.float32),
                   jax.ShapeDtypeStruct((t, d), jnp.bfloat16)),
        grid_spec=pltpu.PrefetchScalarGridSpec(
            num_scalar_prefetch=2, grid=(t // tm,),
            in_specs=[pl.BlockSpec((tm, d), row), pl.BlockSpec((tm, d), row),
                      pl.BlockSpec(memory_space=pl.ANY),
                      pl.BlockSpec((tm, TOP_K), row), pl.BlockSpec((tm, p.shape[1]), row),
                      _resident(w_pg.shape, const2), _resident(w_pp.shape, const2),
                      _resident((1, d), const2), _resident((1, d), const2)],
            out_specs=(pl.BlockSpec((tm, d), row), pl.BlockSpec((tm, d), row)),
            scratch_shapes=[pltpu.VMEM((2, TOP_K, tm * d // LANES, LANES), jnp.float32),
                            pltpu.SemaphoreType.DMA((2,))]),
        compiler_params=_cparams(("arbitrary",)),
        name="final_ln2",
    )(dest0, dest1, h, hb, y_rows, gate_w, p, w_pg, w_pp, ln_g, ln_b)


def _rope_tables(positions):
    pos = positions.reshape(-1).astype(jnp.float32)
    lane = jnp.arange(LANES)

    def tables(rot, period):
        half = rot // 2
        inv_freq = ROPE_THETA ** (-jnp.arange(half, dtype=jnp.float32) / half)
        ang = pos[:, None] * inv_freq
        cos, sin = jnp.cos(ang), jnp.sin(ang)
        lp = lane % period
        idx = lp % half
        in_rot = lp < rot
        c = jnp.where(in_rot[None, :], cos[:, idx], 1.0)
        s_up = jnp.where((lp < half)[None, :], -sin[:, idx], 0.0)
        s_dn = jnp.where(((lp >= half) & in_rot)[None, :], sin[:, idx], 0.0)
        return c, s_up, s_dn

    return tables(DIFF_ROT, DIFF_QK) + tables(MLA_ROPE, LANES)


def _w_in_segments(d):
    widths = [512, 512, 512, 512, 512, 512, HEADS, MLA_Q_LORA, MLA_KV_LORA, MLA_ROPE,
              512, 512, 512, N_BRANCHES * d]
    dsts = [COL_DQ, COL_DQ + 512, COL_DV, COL_FQ, COL_FK, COL_FV, COL_FF, COL_CQ, COL_CKV,
            COL_KR, COL_SQ, COL_SK, COL_SV, COL_GATES]
    segs, src = [], 0
    for wd, dst in zip(widths, dsts):
        segs.append((src, dst, wd))
        src += wd
    return segs


def _w_in_body(w_ref, o_ref, f_ref, *, segs):
    for src, dst, wd in segs:
        o_ref[:, dst:dst + wd] = w_ref[:, src:src + wd].astype(o_ref.dtype)
    rows = o_ref.shape[0]
    o_ref[:, COL_KR + MLA_ROPE:COL_KR + LANES] = jnp.zeros((rows, LANES - MLA_ROPE), o_ref.dtype)
    o_ref[:, COL_FF + HEADS:COL_FF + LANES] = jnp.zeros((rows, LANES - HEADS), o_ref.dtype)
    f_ref[...] = o_ref[:, COL_FF:COL_FF + LANES]


def _reorder_w_in(w_in, layer):
    _, d, n_in = w_in.shape
    tr = 128
    return pl.pallas_call(
        functools.partial(_w_in_body, segs=_w_in_segments(d)),
        out_shape=(jax.ShapeDtypeStruct((d, U_WIDTH), jnp.bfloat16),
                   jax.ShapeDtypeStruct((d, LANES), jnp.bfloat16)),
        grid=(d // tr,),
        in_specs=[pl.BlockSpec((None, tr, n_in), lambda r: (layer, r, 0))],
        out_specs=(pl.BlockSpec((tr, U_WIDTH), lambda r: (r, 0)),
                   pl.BlockSpec((tr, LANES), lambda r: (r, 0))),
        compiler_params=_cparams(("parallel",)),
        name="w_in_relayout",
    )(w_in)


def _route(logits, b_rg, b_re, t):
    g_logits = logits[:, :N_GROUPS] + b_rg
    grp = jnp.argmax(g_logits, axis=-1)
    p_grp = jnp.take_along_axis(jax.nn.softmax(g_logits, axis=-1), grp[:, None], axis=-1)
    e_logits = (logits[:, N_GROUPS:N_GROUPS + N_EXPERTS] + b_re).reshape(
        t, N_GROUPS, EXPERTS_PER_GROUP)
    e_in = jnp.take_along_axis(e_logits, grp[:, None, None], axis=1)[:, 0]
    top_p, top_i = lax.top_k(jax.nn.softmax(e_in, axis=-1), TOP_K)
    weights = p_grp * top_p / jnp.sum(top_p, axis=-1, keepdims=True)
    expert_idx = grp[:, None].astype(jnp.int32) * EXPERTS_PER_GROUP + top_i.astype(jnp.int32)
    return expert_idx, weights


def _dispatch_plan(expert_idx, t):
    a = t * TOP_K
    flat_e = expert_idx.reshape(a)
    onehot = (flat_e[:, None] == jnp.arange(N_EXPERTS)[None, :]).astype(jnp.int32)
    rank = jnp.take_along_axis(jnp.cumsum(onehot, axis=0), flat_e[:, None], axis=1)[:, 0] - 1
    sizes = jnp.sum(onehot, axis=0)
    padded = (sizes + MOE_ROWS - 1) // MOE_ROWS * MOE_ROWS
    pad_end = jnp.cumsum(padded)
    pad_start = pad_end - padded
    dest = (pad_start[flat_e] + rank).astype(jnp.int32)
    n_rows = a + N_EXPERTS * MOE_ROWS
    n_blk = n_rows // MOE_ROWS
    flat_tok = jnp.arange(a, dtype=jnp.int32) // TOP_K
    row_tok = jnp.zeros((n_rows,), jnp.int32).at[dest].set(flat_tok)
    blk_start = jnp.arange(n_blk, dtype=pad_end.dtype) * MOE_ROWS
    blk_e = jnp.minimum(jnp.sum(pad_end[None, :] <= blk_start[:, None], axis=1),
                        N_EXPERTS - 1).astype(jnp.int32)
    n_used = (pad_end[-1] // MOE_ROWS).astype(jnp.int32).reshape(1)
    return dest.reshape(t, TOP_K), row_tok, blk_e, n_used


def kernel(x, p, positions, w_in, fox_f_bias, diff_lambda, diff_subln, mla_q_norm, mla_kv_norm,
           mla_w_uq, mla_w_ukv, w_branch, w_o, ln1_g, ln1_b, w_router_group, b_router_group,
           w_router_expert, b_router_expert, w_expert_gate, w_expert_up, w_expert_down,
           w_ple_gate, w_ple_proj, ln2_g, ln2_b):
    batch, seq, d = x.shape
    depth = w_in.shape[0]
    t = batch * seq
    alpha = (2 * depth) ** 0.25
    bf = jnp.bfloat16
    tabs = _rope_tables(positions)
    col_scale = jnp.ones((1, U_WIDTH), jnp.float32)
    col_scale = col_scale.at[0, COL_FQ:COL_FQ + BRANCH_WIDTH].set(HEAD_DIM ** -0.5 * LOG2E)
    col_scale = col_scale.at[0, COL_SQ:COL_SQ + BRANCH_WIDTH].set(HEAD_DIM ** -0.5)
    xf = x.reshape(t, d)
    xb = xf.astype(bf)

    for i in range(depth):
        w1, w_f = _reorder_w_in(w_in, i)
        u = _matmul(xb, w1, col_scale, bf, 1024, 1536, "in_proj")

        b_f = jnp.zeros((1, LANES), jnp.float32).at[0, :HEADS].set(fox_f_bias[i])
        cum = _fgate_cumsum(xb, w_f, b_f, batch, seq)[:, :HEADS] * LOG2E
        cum_bh = jnp.transpose(cum.reshape(batch, seq, HEADS), (0, 2, 1)).reshape(
            batch * HEADS, seq)
        cq_rows = cum_bh.reshape(batch * HEADS, 1, seq)
        ck_rep = jnp.broadcast_to(cum_bh[:, :, None], (batch * HEADS, seq, LANES))

        uq = mla_w_uq[i].reshape(MLA_Q_LORA, HEADS, MLA_NOPE + MLA_ROPE)
        uq = jnp.pad(uq, ((0, 0), (0, 0), (0, MLA_QK_PAD - MLA_NOPE - MLA_ROPE)))
        uq = uq.reshape(MLA_Q_LORA, HEADS * MLA_QK_PAD).astype(bf)
        ukv = mla_w_ukv[i].reshape(MLA_KV_LORA, HEADS, MLA_NOPE + HEAD_DIM)
        uk = ukv[:, :, :MLA_NOPE].reshape(MLA_KV_LORA, HEADS * MLA_NOPE).astype(bf)
        uv = ukv[:, :, MLA_NOPE:].reshape(MLA_KV_LORA, HEADS * HEAD_DIM).astype(bf)
        dqk, mq, mk, mv = _prep(u, tabs, mla_q_norm[i].reshape(1, -1),
                                mla_kv_norm[i].reshape(1, -1), uq, uk, uv)

        lam_init = 0.8 - 0.6 * math.exp(-0.3 * i)
        lp = diff_lambda[i].astype(jnp.float32)
        lam = (jnp.exp(jnp.sum(lp[0] * lp[1])) - jnp.exp(jnp.sum(lp[2] * lp[3]))
               + lam_init).reshape(1)
        blk = lambda col: col // HEAD_DIM
        o_d = _attention("diff", batch, seq, dqk, 0, dqk, HEADS, u, blk(COL_DV), HEAD_DIM,
                         extras=(diff_subln[i].reshape(1, HEAD_DIM),),
                         lam=lam, lam_init=lam_init)
        o_f = _attention("fox", batch, seq, u, blk(COL_FQ), u, blk(COL_FK), u, blk(COL_FV),
                         HEAD_DIM, extras=(cq_rows, ck_rep))
        o_m = _attention("mla", batch, seq, mq, 0, mk, 0, mv, 0, MLA_QK_PAD)
        o_s = _attention("sb", batch, seq, u, blk(COL_SQ), u, blk(COL_SK), u, blk(COL_SV),
                         HEAD_DIM)

        w_r = jnp.concatenate([w_router_group[i], w_router_expert[i]], axis=1)
        w_r = jnp.pad(w_r, ((0, 0), (0, LANES - w_r.shape[1])))
        wr_hi = w_r.astype(bf)
        wr_lo = (w_r - wr_hi.astype(jnp.float32)).astype(bf)
        h, h_rows, hb, logits = _merge(
            u, o_d, o_f, o_m, o_s, w_branch[i].astype(bf), w_o[i].astype(bf), xf,
            ln1_g[i].reshape(1, d), ln1_b[i].reshape(1, d), wr_hi, wr_lo, alpha)

        expert_idx, weights = _route(logits, b_router_group[i], b_router_expert[i], t)
        dest, row_tok, blk_e, n_used = _dispatch_plan(expert_idx, t)
        y_rows = _moe(blk_e, n_used, row_tok, h_rows, w_expert_gate, w_expert_up,
                      w_expert_down, i)
        xf, xb = _final(dest[:, 0], dest[:, 1], h, hb, y_rows, weights.astype(jnp.float32),
                        p[i].reshape(t, -1), w_ple_gate[i].astype(bf),
                        w_ple_proj[i].astype(bf), ln2_g[i].reshape(1, d),
                        ln2_b[i].reshape(1, d), alpha)
    return xf.reshape(batch, seq, d)
```

```python
import functools
import math

import jax
import jax.numpy as jnp
from jax import lax
from jax.experimental import pallas as pl
from jax.experimental.pallas import tpu as pltpu

HEAD_DIM = 128
HEADS = 4
N_BRANCHES = 4
BRANCH_WIDTH = HEADS * HEAD_DIM
DIFF_QK = 64
DIFF_ROT = DIFF_QK // 4
MLA_Q_LORA = 512
MLA_KV_LORA = 256
MLA_NOPE = 128
MLA_ROPE = 64
MLA_QK_PAD = 256
N_GROUPS = 4
EXPERTS_PER_GROUP = 8
N_EXPERTS = N_GROUPS * EXPERTS_PER_GROUP
TOP_K = 2
CHUNK = 64
ROPE_THETA = 500000.0
LN_EPS = 1e-5
RMS_EPS = 1e-6
NEG_BIG = -1e30
LOG2E = 1.4426950408889634
LANES = 128
V7X_VMEM_LIMIT = 56 * 1024 * 1024

COL_GATES = 0
COL_DQ = 8192
COL_DV = COL_DQ + 1024
COL_FQ = COL_DV + 512
COL_FK = COL_FQ + 512
COL_FV = COL_FK + 512
COL_SQ = COL_FV + 512
COL_SK = COL_SQ + 512
COL_SV = COL_SK + 512
COL_CQ = COL_SV + 512
COL_CKV = COL_CQ + 512
COL_KR = COL_CKV + 256
COL_FF = COL_KR + 128
U_WIDTH = COL_FF + 128

ATT_TQ = 512
MOE_ROWS = 256


def _cparams(sem, vmem=V7X_VMEM_LIMIT):
    return pltpu.CompilerParams(dimension_semantics=sem, vmem_limit_bytes=vmem)


def _resident(shape, index_map):
    return pl.BlockSpec(shape, index_map, pipeline_mode=pl.Buffered(1))


def _matmul_body(x_ref, w_ref, c_ref, o_ref):
    acc = jnp.dot(x_ref[...], w_ref[...], preferred_element_type=jnp.float32)
    o_ref[...] = (acc * c_ref[...]).astype(o_ref.dtype)


def _matmul(x, w, col_scale, out_dtype, tm, tn, name):
    m, k = x.shape
    n = w.shape[1]
    tm, tn = min(tm, m), min(tn, n)
    return pl.pallas_call(
        _matmul_body,
        out_shape=jax.ShapeDtypeStruct((m, n), out_dtype),
        grid=(m // tm, n // tn),
        in_specs=[pl.BlockSpec((tm, k), lambda i, j: (i, 0)),
                  pl.BlockSpec((k, tn), lambda i, j: (0, j)),
                  pl.BlockSpec((1, tn), lambda i, j: (0, j))],
        out_specs=pl.BlockSpec((tm, tn), lambda i, j: (i, j)),
        compiler_params=_cparams(("parallel", "arbitrary")),
        name=name,
    )(x, w, col_scale)


def _split3(x):
    hi = x.astype(jnp.bfloat16)
    r1 = x - hi.astype(jnp.float32)
    mid = r1.astype(jnp.bfloat16)
    lo = (r1 - mid.astype(jnp.float32)).astype(jnp.bfloat16)
    return hi, mid, lo


def _split2(x):
    hi = x.astype(jnp.bfloat16)
    mid = (x - hi.astype(jnp.float32)).astype(jnp.bfloat16)
    return hi, mid


def _dot(a, b):
    return jnp.dot(a, b, preferred_element_type=jnp.float32)


def _dot_nt(a, b):
    return lax.dot_general(a, b, (((1,), (1,)), ((), ())), preferred_element_type=jnp.float32)


def _dot_tn(a, b):
    return lax.dot_general(a, b, (((0,), (0,)), ((), ())), preferred_element_type=jnp.float32)


def _log_sigmoid(z):
    neg_abs = pltpu.bitcast(pltpu.bitcast(z, jnp.uint32) | jnp.uint32(0x80000000), jnp.float32)
    return jnp.minimum(z, 0.0) - jnp.log(1.0 + jnp.exp(neg_abs))


def _sigmoid(z):
    return 1.0 / (1.0 + jnp.exp(-z))


def _fgate_body(x_ref, w_ref, b_ref, o_ref, carry_ref):
    @pl.when(pl.program_id(1) == 0)
    def _():
        carry_ref[...] = jnp.zeros_like(carry_ref)

    f = _dot(x_ref[...], w_ref[...]) + b_ref[...]
    log_f = _log_sigmoid(f)
    ts = log_f.shape[0]
    row = lax.broadcasted_iota(jnp.int32, (ts, ts), 0)
    col = lax.broadcasted_iota(jnp.int32, (ts, ts), 1)
    tri = (col <= row).astype(jnp.bfloat16)
    hi, mid, lo = _split3(log_f)
    cum = _dot(tri, hi) + _dot(tri, mid) + _dot(tri, lo) + carry_ref[...]
    o_ref[...] = cum
    carry_ref[...] = cum[ts - 1:ts, :]


def _fgate_cumsum(xb, w_f, b_f, batch, seq):
    t, d = xb.shape
    ts = min(512, seq)
    ns = seq // ts
    return pl.pallas_call(
        _fgate_body,
        out_shape=jax.ShapeDtypeStruct((t, LANES), jnp.float32),
        grid=(batch, ns),
        in_specs=[pl.BlockSpec((ts, d), lambda b, s: (b * ns + s, 0)),
                  _resident((d, LANES), lambda b, s: (0, 0)),
                  _resident((1, LANES), lambda b, s: (0, 0))],
        out_specs=pl.BlockSpec((ts, LANES), lambda b, s: (b * ns + s, 0)),
        scratch_shapes=[pltpu.VMEM((1, LANES), jnp.float32)],
        compiler_params=_cparams(("parallel", "arbitrary")),
        name="fgate_cumsum",
    )(xb, w_f, b_f)


def _rope_lanes(x, c, s_up, s_dn, half):
    return x * c + pltpu.roll(x, LANES - half, 1) * s_up + pltpu.roll(x, half, 1) * s_dn


def _prep_body(dqk_ref, cq_ref, ckv_ref, kr_ref, c16_ref, su16_ref, sd16_ref,
               c64_ref, su64_ref, sd64_ref, qg_ref, kvg_ref, wuq_ref, wuk_ref, wuv_ref,
               dqk_o, mq_o, mk_o, mv_o):
    c16, su16, sd16 = c16_ref[...], su16_ref[...], sd16_ref[...]
    c64, su64, sd64 = c64_ref[...], su64_ref[...], sd64_ref[...]
    for j in range(dqk_ref.shape[1] // LANES):
        sl = slice(j * LANES, (j + 1) * LANES)
        x = _rope_lanes(dqk_ref[:, sl].astype(jnp.float32), c16, su16, sd16, DIFF_ROT // 2)
        if j < HEADS:
            x = x * (DIFF_QK ** -0.5 * LOG2E)
        dqk_o[:, sl] = x.astype(dqk_o.dtype)

    cq = cq_ref[...].astype(jnp.float32)
    cqn = cq * lax.rsqrt(jnp.mean(cq * cq, axis=-1, keepdims=True) + RMS_EPS) * qg_ref[...]
    q = _dot(cqn.astype(jnp.bfloat16), wuq_ref[...])
    q = q * ((MLA_NOPE + MLA_ROPE) ** -0.5 * LOG2E)
    ckv = ckv_ref[...].astype(jnp.float32)
    ckvn = (ckv * lax.rsqrt(jnp.mean(ckv * ckv, axis=-1, keepdims=True) + RMS_EPS)
            * kvg_ref[...]).astype(jnp.bfloat16)
    kn = _dot(ckvn, wuk_ref[...])
    mv_o[...] = _dot(ckvn, wuv_ref[...]).astype(mv_o.dtype)
    kr = _rope_lanes(kr_ref[...].astype(jnp.float32), c64, su64, sd64, MLA_ROPE // 2)
    for h in range(HEADS):
        base = h * MLA_QK_PAD
        mq_o[:, base:base + LANES] = q[:, base:base + LANES].astype(mq_o.dtype)
        qr = _rope_lanes(q[:, base + LANES:base + 2 * LANES], c64, su64, sd64, MLA_ROPE // 2)
        mq_o[:, base + LANES:base + 2 * LANES] = qr.astype(mq_o.dtype)
        mk_o[:, base:base + LANES] = kn[:, h * LANES:(h + 1) * LANES].astype(mk_o.dtype)
        mk_o[:, base + LANES:base + 2 * LANES] = kr.astype(mk_o.dtype)


def _prep(u, tabs, q_norm, kv_norm, w_uq, w_uk, w_uv):
    t = u.shape[0]
    tm = min(512, t)
    row = lambda blk: (lambda i: (i, blk))
    const = lambda i: (0, 0)
    tab_spec = pl.BlockSpec((tm, LANES), lambda i: (i, 0))
    outs = pl.pallas_call(
        _prep_body,
        out_shape=(jax.ShapeDtypeStruct((t, 1024), jnp.bfloat16),
                   jax.ShapeDtypeStruct((t, HEADS * MLA_QK_PAD), jnp.bfloat16),
                   jax.ShapeDtypeStruct((t, HEADS * MLA_QK_PAD), jnp.bfloat16),
                   jax.ShapeDtypeStruct((t, BRANCH_WIDTH), jnp.bfloat16)),
        grid=(t // tm,),
        in_specs=[pl.BlockSpec((tm, 1024), row(COL_DQ // 1024)),
                  pl.BlockSpec((tm, MLA_Q_LORA), row(COL_CQ // MLA_Q_LORA)),
                  pl.BlockSpec((tm, MLA_KV_LORA), row(COL_CKV // MLA_KV_LORA)),
                  pl.BlockSpec((tm, LANES), row(COL_KR // LANES)),
                  tab_spec, tab_spec, tab_spec, tab_spec, tab_spec, tab_spec,
                  _resident((1, MLA_Q_LORA), const),
                  _resident((1, MLA_KV_LORA), const),
                  _resident(w_uq.shape, const),
                  _resident(w_uk.shape, const),
                  _resident(w_uv.shape, const)],
        out_specs=(pl.BlockSpec((tm, 1024), lambda i: (i, 0)),
                   pl.BlockSpec((tm, HEADS * MLA_QK_PAD), lambda i: (i, 0)),
                   pl.BlockSpec((tm, HEADS * MLA_QK_PAD), lambda i: (i, 0)),
                   pl.BlockSpec((tm, BRANCH_WIDTH), lambda i: (i, 0))),
        compiler_params=_cparams(("parallel",)),
        name="prep",
    )(u, u, u, u, *tabs, q_norm, kv_norm, w_uq, w_uk, w_uv)
    return outs


def _attn_body(*refs, kind, tq, tk, lam_init):
    softmax = kind != "sb"
    if kind == "diff":
        lam_ref, q_ref, k_ref, v_ref, g_ref, o_ref = refs[:6]
    elif kind == "fox":
        q_ref, k_ref, v_ref, cq_ref, ck_ref, o_ref = refs[:6]
    else:
        q_ref, k_ref, v_ref, o_ref = refs[:4]
    if softmax:
        s_bufs, p_bufs = refs[-8:-6], refs[-6:-4]
        vt_ref, m_ref, a_ref, acc_ref = refs[-4:]
    else:
        s_bufs, p_bufs = refs[-7:-5], refs[-5:-3]
        vt_ref, r_ref, acc_ref = refs[-3:]
    qi = pl.program_id(2)

    @pl.when(qi == 0)
    def _():
        for kb in range(vt_ref.shape[0]):
            vt_ref[kb, 0:HEAD_DIM, :] = v_ref[kb * tk:(kb + 1) * tk, :].T
            if softmax:
                vt_ref[kb, HEAD_DIM:, :] = jnp.ones((vt_ref.shape[1] - HEAD_DIM, tk),
                                                    vt_ref.dtype)

    q0 = qi * tq
    last_kb = 2 * qi + 1
    q = q_ref[...]
    n_comp = 2 if kind == "diff" else 1
    if kind == "diff":
        lane = lax.broadcasted_iota(jnp.int32, q.shape, 1)
        qs = [jnp.where(lane < DIFF_QK, q, jnp.zeros_like(q)),
              jnp.where(lane >= DIFF_QK, q, jnp.zeros_like(q))]
    else:
        qs = [q]

    acc_ref[...] = jnp.zeros_like(acc_ref)
    p_bufs[1][...] = jnp.zeros_like(p_bufs[1])
    if softmax:
        m_ref[...] = jnp.full_like(m_ref, NEG_BIG)
        a_ref[...] = jnp.zeros_like(a_ref)
    else:
        r_ref[...] = jnp.zeros_like(r_ref)

    def key_rows(kb):
        if isinstance(kb, int):
            return pl.ds(kb * tk, tk)
        return pl.ds(pl.multiple_of(kb * tk, tk), tk)

    def scores_stage(kb, s_dst):
        k_t = k_ref[key_rows(kb), :]
        for c in range(n_comp):
            s_dst[c] = _dot_nt(k_t, qs[c])

    def values_stage(kb, p_src):
        v_t = vt_ref[kb]
        for c in range(n_comp):
            pv = _dot(v_t, p_src[c])[:acc_ref.shape[1]]
            if softmax:
                acc_ref[c] = a_ref[c] * acc_ref[c] + pv
            else:
                acc_ref[c] = acc_ref[c] + pv

    def weights_stage(kb, s_src, p_dst, masked):
        if masked:
            j = lax.broadcasted_iota(jnp.int32, (tk, tq), 0)
            i = lax.broadcasted_iota(jnp.int32, (tk, tq), 1)
            off = q0 - kb * tk
            if kind in ("diff", "mla"):
                mask = (j // CHUNK - i // CHUNK) <= off // CHUNK
            elif kind == "fox":
                mask = (j - i) <= off
            else:
                mask = (j - i) < off
        if not softmax:
            z_t = s_src[0]
            ls = _log_sigmoid(z_t)
            lk = ls - z_t
            if masked:
                lk = jnp.where(mask, lk, 0.0)
            jj = lax.broadcasted_iota(jnp.int32, (tk, tk), 0)
            mm = lax.broadcasted_iota(jnp.int32, (tk, tk), 1)
            upper = (mm > jj).astype(jnp.bfloat16)
            hi, mid = _split2(lk)
            later = r_ref[...] + _dot(jnp.concatenate([upper, upper], axis=1),
                                      jnp.concatenate([hi, mid], axis=0))
            w_t = jnp.exp(ls + later)
            if masked:
                w_t = jnp.where(mask, w_t, 0.0)
            p_dst[0] = w_t.astype(jnp.bfloat16)
            r_ref[...] += jnp.sum(lk, axis=0, keepdims=True)
            return
        for c in range(n_comp):
            s_t = s_src[c]
            if kind == "fox":
                ck = ck_ref[key_rows(kb), :]
                s_t = s_t + cq_ref[...] - jnp.concatenate([ck] * (tq // LANES), axis=1)
            if masked:
                s_t = jnp.where(mask, s_t, NEG_BIG)
            m_old = m_ref[c]
            m_new = jnp.maximum(m_old, jnp.max(s_t, axis=0, keepdims=True))
            alpha = jnp.exp2(m_old - m_new)
            p_t = jnp.exp2(s_t - m_new)
            m_ref[c] = m_new
            a_ref[c] = alpha
            p_dst[c] = p_t.astype(jnp.bfloat16)

    def step(kb, parity, masked, next_kb, prev_kb):
        if next_kb is not None:
            scores_stage(next_kb, s_bufs[1 - parity])
        values_stage(prev_kb, p_bufs[1 - parity])
        weights_stage(kb, s_bufs[parity], p_bufs[parity], masked)

    if softmax:
        scores_stage(0, s_bufs[0])

        def body(t, carry):
            kb = 2 * t
            step(kb, 0, False, kb + 1, jnp.maximum(kb - 1, 0))
            step(kb + 1, 1, False, kb + 2, kb)
            return carry
        lax.fori_loop(0, qi, body, 0)
        step(last_kb - 1, 0, True, last_kb, jnp.maximum(last_kb - 2, 0))
        step(last_kb, 1, True, None, last_kb - 1)
        values_stage(last_kb, p_bufs[1])
    else:
        scores_stage(last_kb, s_bufs[0])
        step(last_kb, 0, True, last_kb - 1, last_kb)
        step(last_kb - 1, 1, True, jnp.maximum(last_kb - 2, 0), last_kb)

        def body(t, carry):
            kb = last_kb - 2 - 2 * t
            step(kb, 0, False, kb - 1, kb + 1)
            step(kb - 1, 1, False, jnp.maximum(kb - 2, 0), kb)
            return carry
        lax.fori_loop(0, qi, body, 0)
        values_stage(0, p_bufs[1])

    def normalised(c):
        return acc_ref[c, 0:HEAD_DIM, :] / acc_ref[c, HEAD_DIM:HEAD_DIM + 1, :]

    if kind == "sb":
        o_t = acc_ref[0]
    elif kind == "diff":
        o_t = normalised(0) - lam_ref[0] * normalised(1)
    else:
        o_t = normalised(0)
    o = o_t.T
    if kind == "diff":
        o = (o * lax.rsqrt(jnp.mean(o * o, axis=-1, keepdims=True) + RMS_EPS)
             * g_ref[...] * (1.0 - lam_init))
    o_ref[...] = o.astype(o_ref.dtype)


def _attention(kind, batch, seq, q_arr, q_col, k_arr, k_col, v_arr, v_col, dq,
               extras=(), lam=None, lam_init=0.0):
    t = q_arr.shape[0]
    tq = min(ATT_TQ, seq)
    tk = tq // 2
    nq = seq // tq
    q_spec = pl.BlockSpec((tq, dq), lambda b, h, i, *_: (b * nq + i, q_col + h))
    k_spec = pl.BlockSpec((seq, dq), lambda b, h, i, *_: (b, k_col + h))
    v_spec = pl.BlockSpec((seq, HEAD_DIM), lambda b, h, i, *_: (b, v_col + h))
    in_specs = [q_spec, k_spec, v_spec]
    args = [q_arr, k_arr, v_arr]
    n_comp = 2 if kind == "diff" else 1
    if kind == "diff":
        in_specs.append(_resident((1, HEAD_DIM), lambda b, h, i, *_: (0, 0)))
        args.append(extras[0])
    elif kind == "fox":
        cq_rows, ck_rep = extras
        in_specs.append(pl.BlockSpec((None, 1, tq), lambda b, h, i: (b * HEADS + h, 0, i)))
        in_specs.append(pl.BlockSpec((None, seq, LANES), lambda b, h, i: (b * HEADS + h, 0, 0)))
        args += [cq_rows, ck_rep]
    s_buf = pltpu.VMEM((n_comp, tk, tq), jnp.float32)
    p_buf = pltpu.VMEM((n_comp, tk, tq), jnp.bfloat16)
    row = pltpu.VMEM((n_comp, 1, tq), jnp.float32)
    nk = seq // tk
    if kind == "sb":
        scratch = [s_buf, s_buf, p_buf, p_buf,
                   pltpu.VMEM((nk, HEAD_DIM, tk), jnp.bfloat16),
                   pltpu.VMEM((1, tq), jnp.float32),
                   pltpu.VMEM((1, HEAD_DIM, tq), jnp.float32)]
    else:
        scratch = [s_buf, s_buf, p_buf, p_buf,
                   pltpu.VMEM((nk, HEAD_DIM + 16, tk), jnp.bfloat16), row, row,
                   pltpu.VMEM((n_comp, HEAD_DIM + 8, tq), jnp.float32)]
    body = functools.partial(_attn_body, kind=kind, tq=tq, tk=tk, lam_init=lam_init)
    grid = (batch, HEADS, nq)
    out_shape = jax.ShapeDtypeStruct((t, BRANCH_WIDTH), jnp.bfloat16)
    out_spec = pl.BlockSpec((tq, HEAD_DIM), lambda b, h, i, *_: (b * nq + i, h))
    cp = _cparams(("parallel", "parallel", "arbitrary"))
    if kind == "diff":
        return pl.pallas_call(
            body, out_shape=out_shape,
            grid_spec=pltpu.PrefetchScalarGridSpec(
                num_scalar_prefetch=1, grid=grid, in_specs=in_specs, out_specs=out_spec,
                scratch_shapes=scratch),
            compiler_params=cp, name="attn_" + kind)(lam, *args)
    return pl.pallas_call(
        body, out_shape=out_shape, grid=grid, in_specs=in_specs, out_specs=out_spec,
        scratch_shapes=scratch, compiler_params=cp, name="attn_" + kind)(*args)


def _layer_norm(y, g, b):
    mu = jnp.mean(y, axis=-1, keepdims=True)
    yc = y - mu
    var = jnp.mean(yc * yc, axis=-1, keepdims=True)
    return yc * lax.rsqrt(var + LN_EPS) * g + b


def _merge_body(g_ref, od_ref, of_ref, om_ref, os_ref, wb_ref, wo_ref, x_ref, lg_ref, lb_ref,
                wrh_ref, wrl_ref, h_o, hr_o, hb_o, lgt_o, *, alpha):
    d = x_ref.shape[1]
    merged = None
    for n, o_ref in enumerate((od_ref, of_ref, om_ref, os_ref)):
        gate = _sigmoid(g_ref[:, n * d:(n + 1) * d].astype(jnp.float32))
        term = gate * _dot(o_ref[...], wb_ref[n])
        merged = term if merged is None else merged + term
    mix = _dot(merged.astype(jnp.bfloat16), wo_ref[...])
    h = _layer_norm(alpha * x_ref[...] + mix, lg_ref[...], lb_ref[...])
    h_o[...] = h
    _to_chunked(hr_o, h)
    h_hi = h.astype(jnp.bfloat16)
    hb_o[...] = h_hi
    h_lo = (h - h_hi.astype(jnp.float32)).astype(jnp.bfloat16)
    lgt_o[...] = (_dot(h_hi, wrh_ref[...]) + _dot(h_lo, wrh_ref[...]) + _dot(h_hi, wrl_ref[...]))


def _merge(u, o_d, o_f, o_m, o_s, w_b, w_o, x, ln_g, ln_b, wr_hi, wr_lo, alpha):
    t, d = x.shape
    tm = min(256, t)
    row = lambda i: (i, 0)
    const2 = lambda i: (0, 0)
    o_spec = pl.BlockSpec((tm, BRANCH_WIDTH), row)
    return pl.pallas_call(
        functools.partial(_merge_body, alpha=alpha),
        out_shape=(jax.ShapeDtypeStruct((t, d), jnp.float32),
                   jax.ShapeDtypeStruct((t * _chunk_pitch(d), LANES), jnp.float32),
                   jax.ShapeDtypeStruct((t, d), jnp.bfloat16),
                   jax.ShapeDtypeStruct((t, LANES), jnp.float32)),
        grid=(t // tm,),
        in_specs=[pl.BlockSpec((tm, N_BRANCHES * d), row),
                  o_spec, o_spec, o_spec, o_spec,
                  _resident(w_b.shape, lambda i: (0, 0, 0)),
                  _resident(w_o.shape, const2),
                  pl.BlockSpec((tm, d), row),
                  _resident((1, d), const2), _resident((1, d), const2),
                  _resident(wr_hi.shape, const2), _resident(wr_lo.shape, const2)],
        out_specs=(pl.BlockSpec((tm, d), row),
                   pl.BlockSpec((tm * _chunk_pitch(d), LANES), row),
                   pl.BlockSpec((tm, d), row), pl.BlockSpec((tm, LANES), row)),
        compiler_params=_cparams(("parallel",)),
        name="merge_ln1",
    )(u, o_d, o_f, o_m, o_s, w_b, w_o, x, ln_g, ln_b, wr_hi, wr_lo)


CHUNK_PAD = 4


def _chunk_pitch(d):
    return d // LANES + CHUNK_PAD


def _to_chunked(ref, x):
    n, d = x.shape
    nc, pitch = d // LANES, _chunk_pitch(d)
    for c in range(nc):
        ref[pl.ds(c, n, stride=pitch), :] = x[:, c * LANES:(c + 1) * LANES]
    for c in range(nc, pitch):
        ref[pl.ds(c, n, stride=pitch), :] = jnp.zeros((n, LANES), x.dtype)


def _from_chunked(ref, n, d):
    pitch = _chunk_pitch(d)
    return jnp.concatenate([ref[pl.ds(c, n, stride=pitch), :] for c in range(d // LANES)],
                           axis=1)


def _gather_rows(idx_ref, base, src_hbm, dst, sem, n, d):
    nc, pitch = d // LANES, _chunk_pitch(d)
    for r in range(n):
        row = pl.multiple_of(idx_ref[base + r] * pitch, CHUNK_PAD)
        pltpu.make_async_copy(src_hbm.at[pl.ds(row, nc), :], dst.at[pl.ds(r * pitch, nc), :],
                              sem).start()


def _wait_rows(src_hbm, dst, sem, n, d):
    rows = n * (d // LANES)
    pltpu.make_async_copy(src_hbm.at[pl.ds(0, rows), :], dst.at[pl.ds(0, rows), :], sem).wait()


def _moe_body(blk_e_ref, n_used_ref, tok_ref, h_hbm, wg_ref, wu_ref, wd_ref, y_ref,
              xbuf, sem, wg_b, wu_b, wd_b):
    i = pl.program_id(0)
    n_used = n_used_ref[0]
    d = wg_ref.shape[0]
    bm = xbuf.shape[1] // _chunk_pitch(d)
    slot = i % 2

    @pl.when((i == 0) & (n_used > 0))
    def _():
        _gather_rows(tok_ref, 0, h_hbm, xbuf.at[0], sem.at[0], bm, d)

    @pl.when(i + 1 < n_used)
    def _():
        _gather_rows(tok_ref, (i + 1) * bm, h_hbm, xbuf.at[1 - slot], sem.at[1 - slot], bm, d)

    @pl.when(i < n_used)
    def _():
        @pl.when((i == 0) | (blk_e_ref[i] != blk_e_ref[jnp.maximum(i - 1, 0)]))
        def _():
            wg_b[...] = wg_ref[...].astype(jnp.bfloat16)
            wu_b[...] = wu_ref[...].astype(jnp.bfloat16)
            wd_b[...] = wd_ref[...].astype(jnp.bfloat16)

        _wait_rows(h_hbm, xbuf.at[slot], sem.at[slot], bm, d)
        x = _from_chunked(xbuf.at[slot], bm, d).astype(jnp.bfloat16)
        a = _dot(x, wg_b[...])
        hid = (a * _sigmoid(a)) * _dot(x, wu_b[...])
        _to_chunked(y_ref, _dot(hid.astype(jnp.bfloat16), wd_b[...]))

    @pl.when(i >= n_used)
    def _():
        y_ref[...] = jnp.zeros_like(y_ref)


def _moe(blk_e, n_used, row_tok, h_rows, w_g, w_u, w_d, layer):
    n_rows = row_tok.shape[0]
    d, hid = w_g.shape[2], w_g.shape[3]
    pitch = _chunk_pitch(d)
    bm = MOE_ROWS
    wspec = lambda shape: pl.BlockSpec((None, None) + shape,
                                       lambda i, e, n, tok: (layer, e[i], 0, 0))
    return pl.pallas_call(
        _moe_body,
        out_shape=jax.ShapeDtypeStruct((n_rows * pitch, LANES), jnp.float32),
        grid_spec=pltpu.PrefetchScalarGridSpec(
            num_scalar_prefetch=3, grid=(n_rows // bm,),
            in_specs=[pl.BlockSpec(memory_space=pl.ANY),
                      wspec((d, hid)), wspec((d, hid)), wspec((hid, d))],
            out_specs=pl.BlockSpec((bm * pitch, LANES), lambda i, e, n, tok: (i, 0)),
            scratch_shapes=[pltpu.VMEM((2, bm * pitch, LANES), jnp.float32),
                            pltpu.SemaphoreType.DMA((2,)),
                            pltpu.VMEM((d, hid), jnp.bfloat16),
                            pltpu.VMEM((d, hid), jnp.bfloat16),
                            pltpu.VMEM((hid, d), jnp.bfloat16)]),
        compiler_params=_cparams(("arbitrary",)),
        name="moe_experts",
    )(blk_e, n_used, row_tok, h_rows, w_g, w_u, w_d)


def _final_body(d0_ref, d1_ref, h_ref, hb_ref, y_hbm, gw_ref, p_ref, wg_ref, wp_ref, lg_ref,
                lb_ref, y_o, yb_o, gbuf, sem, *, alpha):
    i = pl.program_id(0)
    n = pl.num_programs(0)
    tm, d = h_ref.shape
    slot = i % 2

    def gather(step, s):
        _gather_rows(d0_ref, step * tm, y_hbm, gbuf.at[s, 0], sem.at[s], tm, d)
        _gather_rows(d1_ref, step * tm, y_hbm, gbuf.at[s, 1], sem.at[s], tm, d)

    @pl.when(i == 0)
    def _():
        gather(0, 0)

    @pl.when(i + 1 < n)
    def _():
        gather(i + 1, 1 - slot)

    gate = _sigmoid(_dot(hb_ref[...], wg_ref[...]))
    proj = _dot(p_ref[...].astype(jnp.bfloat16), wp_ref[...])
    _wait_rows(y_hbm, gbuf.at[slot, 0], sem.at[slot], tm, d)
    _wait_rows(y_hbm, gbuf.at[slot, 1], sem.at[slot], tm, d)
    gw = gw_ref[...]
    ffn = (_from_chunked(gbuf.at[slot, 0], tm, d) * gw[:, 0:1]
           + _from_chunked(gbuf.at[slot, 1], tm, d) * gw[:, 1:2])
    y = _layer_norm(alpha * h_ref[...] + ffn + gate * proj, lg_ref[...], lb_ref[...])
    y_o[...] = y
    yb_o[...] = y.astype(jnp.bfloat16)


def _final(dest0, dest1, h, hb, y_rows, gate_w, p, w_pg, w_pp, ln_g, ln_b, alpha):
    t, d = h.shape
    tm = min(256, t)
    row = lambda i, *_: (i, 0)
    const2 = lambda i, *_: (0, 0)
    return pl.pallas_call(
        functools.partial(_final_body, alpha=alpha),
        out_shape=(jax.ShapeDtypeStruct((t, d), jnp.float32),
                   jax.ShapeDtypeStruct((t, d), jnp.bfloat16)),
        grid_spec=pltpu.PrefetchScalarGridSpec(
            num_scalar_prefetch=2, grid=(t // tm,),
            in_specs=[pl.BlockSpec((tm, d), row), pl.BlockSpec((tm, d), row),
                      pl.BlockSpec(memory_space=pl.ANY),
                      pl.BlockSpec((tm, TOP_K), row), pl.BlockSpec((tm, p.shape[1]), row),
                      _resident(w_pg.shape, const2), _resident(w_pp.shape, const2),
                      _resident((1, d), const2), _resident((1, d), const2)],
            out_specs=(pl.BlockSpec((tm, d), row), pl.BlockSpec((tm, d), row)),
            scratch_shapes=[pltpu.VMEM((2, TOP_K, tm * _chunk_pitch(d), LANES), jnp.float32),
                            pltpu.SemaphoreType.DMA((2,))]),
        compiler_params=_cparams(("arbitrary",)),
        name="final_ln2",
    )(dest0, dest1, h, hb, y_rows, gate_w, p, w_pg, w_pp, ln_g, ln_b)


def _rope_tables(positions):
    pos = positions.reshape(-1).astype(jnp.float32)
    lane = jnp.arange(LANES)

    def tables(rot, period):
        half = rot // 2
        inv_freq = ROPE_THETA ** (-jnp.arange(half, dtype=jnp.float32) / half)
        ang = pos[:, None] * inv_freq
        cos, sin = jnp.cos(ang), jnp.sin(ang)
        lp = lane % period
        idx = lp % half
        in_rot = lp < rot
        c = jnp.where(in_rot[None, :], cos[:, idx], 1.0)
        s_up = jnp.where((lp < half)[None, :], -sin[:, idx], 0.0)
        s_dn = jnp.where(((lp >= half) & in_rot)[None, :], sin[:, idx], 0.0)
        return c, s_up, s_dn

    return tables(DIFF_ROT, DIFF_QK) + tables(MLA_ROPE, LANES)


def _w_in_segments(d):
    widths = [512, 512, 512, 512, 512, 512, HEADS, MLA_Q_LORA, MLA_KV_LORA, MLA_ROPE,
              512, 512, 512, N_BRANCHES * d]
    dsts = [COL_DQ, COL_DQ + 512, COL_DV, COL_FQ, COL_FK, COL_FV, COL_FF, COL_CQ, COL_CKV,
            COL_KR, COL_SQ, COL_SK, COL_SV, COL_GATES]
    segs, src = [], 0
    for wd, dst in zip(widths, dsts):
        segs.append((src, dst, wd))
        src += wd
    return segs


def _w_in_body(w_ref, o_ref, f_ref, *, segs):
    for src, dst, wd in segs:
        o_ref[:, dst:dst + wd] = w_ref[:, src:src + wd].astype(o_ref.dtype)
    rows = o_ref.shape[0]
    o_ref[:, COL_KR + MLA_ROPE:COL_KR + LANES] = jnp.zeros((rows, LANES - MLA_ROPE), o_ref.dtype)
    o_ref[:, COL_FF + HEADS:COL_FF + LANES] = jnp.zeros((rows, LANES - HEADS), o_ref.dtype)
    f_ref[...] = o_ref[:, COL_FF:COL_FF + LANES]


def _reorder_w_in(w_in, layer):
    _, d, n_in = w_in.shape
    tr = 128
    return pl.pallas_call(
        functools.partial(_w_in_body, segs=_w_in_segments(d)),
        out_shape=(jax.ShapeDtypeStruct((d, U_WIDTH), jnp.bfloat16),
                   jax.ShapeDtypeStruct((d, LANES), jnp.bfloat16)),
        grid=(d // tr,),
        in_specs=[pl.BlockSpec((None, tr, n_in), lambda r: (layer, r, 0))],
        out_specs=(pl.BlockSpec((tr, U_WIDTH), lambda r: (r, 0)),
                   pl.BlockSpec((tr, LANES), lambda r: (r, 0))),
        compiler_params=_cparams(("parallel",)),
        name="w_in_relayout",
    )(w_in)


def _route(logits, b_rg, b_re, t):
    g_logits = logits[:, :N_GROUPS] + b_rg
    grp = jnp.argmax(g_logits, axis=-1)
    p_grp = jnp.take_along_axis(jax.nn.softmax(g_logits, axis=-1), grp[:, None], axis=-1)
    e_logits = (logits[:, N_GROUPS:N_GROUPS + N_EXPERTS] + b_re).reshape(
        t, N_GROUPS, EXPERTS_PER_GROUP)
    e_in = jnp.take_along_axis(e_logits, grp[:, None, None], axis=1)[:, 0]
    top_p, top_i = lax.top_k(jax.nn.softmax(e_in, axis=-1), TOP_K)
    weights = p_grp * top_p / jnp.sum(top_p, axis=-1, keepdims=True)
    expert_idx = grp[:, None].astype(jnp.int32) * EXPERTS_PER_GROUP + top_i.astype(jnp.int32)
    return expert_idx, weights


def _dispatch_plan(expert_idx, t):
    a = t * TOP_K
    flat_e = expert_idx.reshape(a)
    onehot = (flat_e[:, None] == jnp.arange(N_EXPERTS)[None, :]).astype(jnp.int32)
    rank = jnp.take_along_axis(jnp.cumsum(onehot, axis=0), flat_e[:, None], axis=1)[:, 0] - 1
    sizes = jnp.sum(onehot, axis=0)
    padded = (sizes + MOE_ROWS - 1) // MOE_ROWS * MOE_ROWS
    pad_end = jnp.cumsum(padded)
    pad_start = pad_end - padded
    dest = (pad_start[flat_e] + rank).astype(jnp.int32)
    n_rows = a + N_EXPERTS * MOE_ROWS
    n_blk = n_rows // MOE_ROWS
    flat_tok = jnp.arange(a, dtype=jnp.int32) // TOP_K
    row_tok = jnp.zeros((n_rows,), jnp.int32).at[dest].set(flat_tok)
    blk_start = jnp.arange(n_blk, dtype=pad_end.dtype) * MOE_ROWS
    blk_e = jnp.minimum(jnp.sum(pad_end[None, :] <= blk_start[:, None], axis=1),
                        N_EXPERTS - 1).astype(jnp.int32)
    n_used = (pad_end[-1] // MOE_ROWS).astype(jnp.int32).reshape(1)
    return dest.reshape(t, TOP_K), row_tok, blk_e, n_used


def kernel(x, p, positions, w_in, fox_f_bias, diff_lambda, diff_subln, mla_q_norm, mla_kv_norm,
           mla_w_uq, mla_w_ukv, w_branch, w_o, ln1_g, ln1_b, w_router_group, b_router_group,
           w_router_expert, b_router_expert, w_expert_gate, w_expert_up, w_expert_down,
           w_ple_gate, w_ple_proj, ln2_g, ln2_b):
    batch, seq, d = x.shape
    depth = w_in.shape[0]
    t = batch * seq
    alpha = (2 * depth) ** 0.25
    bf = jnp.bfloat16
    tabs = _rope_tables(positions)
    col_scale = jnp.ones((1, U_WIDTH), jnp.float32)
    col_scale = col_scale.at[0, COL_FQ:COL_FQ + BRANCH_WIDTH].set(HEAD_DIM ** -0.5 * LOG2E)
    col_scale = col_scale.at[0, COL_SQ:COL_SQ + BRANCH_WIDTH].set(HEAD_DIM ** -0.5)
    xf = x.reshape(t, d)
    xb = xf.astype(bf)

    for i in range(depth):
        w1, w_f = _reorder_w_in(w_in, i)
        u = _matmul(xb, w1, col_scale, bf, 1024, 1536, "in_proj")

        b_f = jnp.zeros((1, LANES), jnp.float32).at[0, :HEADS].set(fox_f_bias[i])
        cum = _fgate_cumsum(xb, w_f, b_f, batch, seq)[:, :HEADS] * LOG2E
        cum_bh = jnp.transpose(cum.reshape(batch, seq, HEADS), (0, 2, 1)).reshape(
            batch * HEADS, seq)
        cq_rows = cum_bh.reshape(batch * HEADS, 1, seq)
        ck_rep = jnp.broadcast_to(cum_bh[:, :, None], (batch * HEADS, seq, LANES))

        uq = mla_w_uq[i].reshape(MLA_Q_LORA, HEADS, MLA_NOPE + MLA_ROPE)
        uq = jnp.pad(uq, ((0, 0), (0, 0), (0, MLA_QK_PAD - MLA_NOPE - MLA_ROPE)))
        uq = uq.reshape(MLA_Q_LORA, HEADS * MLA_QK_PAD).astype(bf)
        ukv = mla_w_ukv[i].reshape(MLA_KV_LORA, HEADS, MLA_NOPE + HEAD_DIM)
        uk = ukv[:, :, :MLA_NOPE].reshape(MLA_KV_LORA, HEADS * MLA_NOPE).astype(bf)
        uv = ukv[:, :, MLA_NOPE:].reshape(MLA_KV_LORA, HEADS * HEAD_DIM).astype(bf)
        dqk, mq, mk, mv = _prep(u, tabs, mla_q_norm[i].reshape(1, -1),
                                mla_kv_norm[i].reshape(1, -1), uq, uk, uv)

        lam_init = 0.8 - 0.6 * math.exp(-0.3 * i)
        lp = diff_lambda[i].astype(jnp.float32)
        lam = (jnp.exp(jnp.sum(lp[0] * lp[1])) - jnp.exp(jnp.sum(lp[2] * lp[3]))
               + lam_init).reshape(1)
        blk = lambda col: col // HEAD_DIM
        o_d = _attention("diff", batch, seq, dqk, 0, dqk, HEADS, u, blk(COL_DV), HEAD_DIM,
                         extras=(diff_subln[i].reshape(1, HEAD_DIM),),
                         lam=lam, lam_init=lam_init)
        o_f = _attention("fox", batch, seq, u, blk(COL_FQ), u, blk(COL_FK), u, blk(COL_FV),
                         HEAD_DIM, extras=(cq_rows, ck_rep))
        o_m = _attention("mla", batch, seq, mq, 0, mk, 0, mv, 0, MLA_QK_PAD)
        o_s = _attention("sb", batch, seq, u, blk(COL_SQ), u, blk(COL_SK), u, blk(COL_SV),
                         HEAD_DIM)

        w_r = jnp.concatenate([w_router_group[i], w_router_expert[i]], axis=1)
        w_r = jnp.pad(w_r, ((0, 0), (0, LANES - w_r.shape[1])))
        wr_hi = w_r.astype(bf)
        wr_lo = (w_r - wr_hi.astype(jnp.float32)).astype(bf)
        h, h_rows, hb, logits = _merge(
            u, o_d, o_f, o_m, o_s, w_branch[i].astype(bf), w_o[i].astype(bf), xf,
            ln1_g[i].reshape(1, d), ln1_b[i].reshape(1, d), wr_hi, wr_lo, alpha)

        expert_idx, weights = _route(logits, b_router_group[i], b_router_expert[i], t)
        dest, row_tok, blk_e, n_used = _dispatch_plan(expert_idx, t)
        y_rows = _moe(blk_e, n_used, row_tok, h_rows, w_expert_gate, w_expert_up,
                      w_expert_down, i)
        xf, xb = _final(dest[:, 0], dest[:, 1], h, hb, y_rows, weights.astype(jnp.float32),
                        p[i].reshape(t, -1), w_ple_gate[i].astype(bf),
                        w_ple_proj[i].astype(bf), ln2_g[i].reshape(1, d),
                        ln2_b[i].reshape(1, d), alpha)
    return xf.reshape(batch, seq, d)
```

```python
import functools
import math

import jax
import jax.numpy as jnp
from jax import lax
from jax.experimental import pallas as pl
from jax.experimental.pallas import tpu as pltpu

HEAD_DIM = 128
HEADS = 4
N_BRANCHES = 4
BRANCH_WIDTH = HEADS * HEAD_DIM
DIFF_QK = 64
DIFF_ROT = DIFF_QK // 4
MLA_Q_LORA = 512
MLA_KV_LORA = 256
MLA_NOPE = 128
MLA_ROPE = 64
MLA_QK_PAD = 256
N_GROUPS = 4
EXPERTS_PER_GROUP = 8
N_EXPERTS = N_GROUPS * EXPERTS_PER_GROUP
TOP_K = 2
CHUNK = 64
ROPE_THETA = 500000.0
LN_EPS = 1e-5
RMS_EPS = 1e-6
NEG_BIG = -1e30
LOG2E = 1.4426950408889634
LANES = 128
V7X_VMEM_LIMIT = 56 * 1024 * 1024

COL_GATES = 0
COL_DQ = 8192
COL_DV = COL_DQ + 1024
COL_FQ = COL_DV + 512
COL_FK = COL_FQ + 512
COL_FV = COL_FK + 512
COL_SQ = COL_FV + 512
COL_SK = COL_SQ + 512
COL_SV = COL_SK + 512
COL_CQ = COL_SV + 512
COL_CKV = COL_CQ + 512
COL_KR = COL_CKV + 256
COL_FF = COL_KR + 128
U_WIDTH = COL_FF + 128

ATT_TQ = 512
MOE_ROWS = 256


def _cparams(sem, vmem=V7X_VMEM_LIMIT):
    return pltpu.CompilerParams(dimension_semantics=sem, vmem_limit_bytes=vmem)


def _resident(shape, index_map):
    return pl.BlockSpec(shape, index_map, pipeline_mode=pl.Buffered(1))


def _matmul_body(x_ref, w_ref, c_ref, o_ref):
    acc = jnp.dot(x_ref[...], w_ref[...], preferred_element_type=jnp.float32)
    o_ref[...] = (acc * c_ref[...]).astype(o_ref.dtype)


def _matmul(x, w, col_scale, out_dtype, tm, tn, name):
    m, k = x.shape
    n = w.shape[1]
    tm, tn = min(tm, m), min(tn, n)
    return pl.pallas_call(
        _matmul_body,
        out_shape=jax.ShapeDtypeStruct((m, n), out_dtype),
        grid=(m // tm, n // tn),
        in_specs=[pl.BlockSpec((tm, k), lambda i, j: (i, 0)),
                  pl.BlockSpec((k, tn), lambda i, j: (0, j)),
                  pl.BlockSpec((1, tn), lambda i, j: (0, j))],
        out_specs=pl.BlockSpec((tm, tn), lambda i, j: (i, j)),
        compiler_params=_cparams(("parallel", "arbitrary")),
        name=name,
    )(x, w, col_scale)


def _split3(x):
    hi = x.astype(jnp.bfloat16)
    r1 = x - hi.astype(jnp.float32)
    mid = r1.astype(jnp.bfloat16)
    lo = (r1 - mid.astype(jnp.float32)).astype(jnp.bfloat16)
    return hi, mid, lo


def _split2(x):
    hi = x.astype(jnp.bfloat16)
    mid = (x - hi.astype(jnp.float32)).astype(jnp.bfloat16)
    return hi, mid


def _dot(a, b):
    return jnp.dot(a, b, preferred_element_type=jnp.float32)


def _dot_nt(a, b):
    return lax.dot_general(a, b, (((1,), (1,)), ((), ())), preferred_element_type=jnp.float32)


def _dot_tn(a, b):
    return lax.dot_general(a, b, (((0,), (0,)), ((), ())), preferred_element_type=jnp.float32)


def _log_sigmoid(z):
    neg_abs = pltpu.bitcast(pltpu.bitcast(z, jnp.uint32) | jnp.uint32(0x80000000), jnp.float32)
    return jnp.minimum(z, 0.0) - jnp.log(1.0 + jnp.exp(neg_abs))


def _sigmoid(z):
    return 1.0 / (1.0 + jnp.exp(-z))


def _fgate_body(x_ref, w_ref, b_ref, o_ref, carry_ref):
    @pl.when(pl.program_id(1) == 0)
    def _():
        carry_ref[...] = jnp.zeros_like(carry_ref)

    f = _dot(x_ref[...], w_ref[...]) + b_ref[...]
    log_f = _log_sigmoid(f)
    ts = log_f.shape[0]
    row = lax.broadcasted_iota(jnp.int32, (ts, ts), 0)
    col = lax.broadcasted_iota(jnp.int32, (ts, ts), 1)
    tri = (col <= row).astype(jnp.bfloat16)
    hi, mid, lo = _split3(log_f)
    cum = _dot(tri, hi) + _dot(tri, mid) + _dot(tri, lo) + carry_ref[...]
    o_ref[...] = cum
    carry_ref[...] = cum[ts - 1:ts, :]


def _fgate_cumsum(xb, w_f, b_f, batch, seq):
    t, d = xb.shape
    ts = min(512, seq)
    ns = seq // ts
    return pl.pallas_call(
        _fgate_body,
        out_shape=jax.ShapeDtypeStruct((t, LANES), jnp.float32),
        grid=(batch, ns),
        in_specs=[pl.BlockSpec((ts, d), lambda b, s: (b * ns + s, 0)),
                  _resident((d, LANES), lambda b, s: (0, 0)),
                  _resident((1, LANES), lambda b, s: (0, 0))],
        out_specs=pl.BlockSpec((ts, LANES), lambda b, s: (b * ns + s, 0)),
        scratch_shapes=[pltpu.VMEM((1, LANES), jnp.float32)],
        compiler_params=_cparams(("parallel", "arbitrary")),
        name="fgate_cumsum",
    )(xb, w_f, b_f)


def _rope_lanes(x, c, s_up, s_dn, half):
    return x * c + pltpu.roll(x, LANES - half, 1) * s_up + pltpu.roll(x, half, 1) * s_dn


def _prep_body(dqk_ref, cq_ref, ckv_ref, kr_ref, c16_ref, su16_ref, sd16_ref,
               c64_ref, su64_ref, sd64_ref, qg_ref, kvg_ref, wuq_ref, wuk_ref, wuv_ref,
               dqk_o, mq_o, mk_o, mv_o):
    c16, su16, sd16 = c16_ref[...], su16_ref[...], sd16_ref[...]
    c64, su64, sd64 = c64_ref[...], su64_ref[...], sd64_ref[...]
    for j in range(dqk_ref.shape[1] // LANES):
        sl = slice(j * LANES, (j + 1) * LANES)
        x = _rope_lanes(dqk_ref[:, sl].astype(jnp.float32), c16, su16, sd16, DIFF_ROT // 2)
        if j < HEADS:
            x = x * (DIFF_QK ** -0.5 * LOG2E)
        dqk_o[:, sl] = x.astype(dqk_o.dtype)

    cq = cq_ref[...].astype(jnp.float32)
    cqn = cq * lax.rsqrt(jnp.mean(cq * cq, axis=-1, keepdims=True) + RMS_EPS) * qg_ref[...]
    q = _dot(cqn.astype(jnp.bfloat16), wuq_ref[...])
    q = q * ((MLA_NOPE + MLA_ROPE) ** -0.5 * LOG2E)
    ckv = ckv_ref[...].astype(jnp.float32)
    ckvn = (ckv * lax.rsqrt(jnp.mean(ckv * ckv, axis=-1, keepdims=True) + RMS_EPS)
            * kvg_ref[...]).astype(jnp.bfloat16)
    kn = _dot(ckvn, wuk_ref[...])
    mv_o[...] = _dot(ckvn, wuv_ref[...]).astype(mv_o.dtype)
    kr = _rope_lanes(kr_ref[...].astype(jnp.float32), c64, su64, sd64, MLA_ROPE // 2)
    for h in range(HEADS):
        base = h * MLA_QK_PAD
        mq_o[:, base:base + LANES] = q[:, base:base + LANES].astype(mq_o.dtype)
        qr = _rope_lanes(q[:, base + LANES:base + 2 * LANES], c64, su64, sd64, MLA_ROPE // 2)
        mq_o[:, base + LANES:base + 2 * LANES] = qr.astype(mq_o.dtype)
        mk_o[:, base:base + LANES] = kn[:, h * LANES:(h + 1) * LANES].astype(mk_o.dtype)
        mk_o[:, base + LANES:base + 2 * LANES] = kr.astype(mk_o.dtype)


def _prep(u, tabs, q_norm, kv_norm, w_uq, w_uk, w_uv):
    t = u.shape[0]
    tm = min(512, t)
    row = lambda blk: (lambda i: (i, blk))
    const = lambda i: (0, 0)
    tab_spec = pl.BlockSpec((tm, LANES), lambda i: (i, 0))
    outs = pl.pallas_call(
        _prep_body,
        out_shape=(jax.ShapeDtypeStruct((t, 1024), jnp.bfloat16),
                   jax.ShapeDtypeStruct((t, HEADS * MLA_QK_PAD), jnp.bfloat16),
                   jax.ShapeDtypeStruct((t, HEADS * MLA_QK_PAD), jnp.bfloat16),
                   jax.ShapeDtypeStruct((t, BRANCH_WIDTH), jnp.bfloat16)),
        grid=(t // tm,),
        in_specs=[pl.BlockSpec((tm, 1024), row(COL_DQ // 1024)),
                  pl.BlockSpec((tm, MLA_Q_LORA), row(COL_CQ // MLA_Q_LORA)),
                  pl.BlockSpec((tm, MLA_KV_LORA), row(COL_CKV // MLA_KV_LORA)),
                  pl.BlockSpec((tm, LANES), row(COL_KR // LANES)),
                  tab_spec, tab_spec, tab_spec, tab_spec, tab_spec, tab_spec,
                  _resident((1, MLA_Q_LORA), const),
                  _resident((1, MLA_KV_LORA), const),
                  _resident(w_uq.shape, const),
                  _resident(w_uk.shape, const),
                  _resident(w_uv.shape, const)],
        out_specs=(pl.BlockSpec((tm, 1024), lambda i: (i, 0)),
                   pl.BlockSpec((tm, HEADS * MLA_QK_PAD), lambda i: (i, 0)),
                   pl.BlockSpec((tm, HEADS * MLA_QK_PAD), lambda i: (i, 0)),
                   pl.BlockSpec((tm, BRANCH_WIDTH), lambda i: (i, 0))),
        compiler_params=_cparams(("parallel",)),
        name="prep",
    )(u, u, u, u, *tabs, q_norm, kv_norm, w_uq, w_uk, w_uv)
    return outs


def _attn_body(*refs, kind, tq, tk, lam_init):
    softmax = kind != "sb"
    if kind == "diff":
        lam_ref, q_ref, k_ref, v_ref, g_ref, o_ref = refs[:6]
    elif kind == "fox":
        q_ref, k_ref, v_ref, cq_ref, ck_ref, o_ref = refs[:6]
    else:
        q_ref, k_ref, v_ref, o_ref = refs[:4]
    if softmax:
        s_bufs, p_bufs = refs[-8:-6], refs[-6:-4]
        vt_ref, m_ref, a_ref, acc_ref = refs[-4:]
    else:
        s_bufs, p_bufs = refs[-7:-5], refs[-5:-3]
        vt_ref, r_ref, acc_ref = refs[-3:]
    qi = pl.program_id(2)

    @pl.when(qi == 0)
    def _():
        for kb in range(vt_ref.shape[0]):
            vt_ref[kb, 0:HEAD_DIM, :] = v_ref[kb * tk:(kb + 1) * tk, :].T
            if softmax:
                vt_ref[kb, HEAD_DIM:, :] = jnp.ones((vt_ref.shape[1] - HEAD_DIM, tk),
                                                    vt_ref.dtype)

    q0 = qi * tq
    last_kb = 2 * qi + 1
    q = q_ref[...]
    n_comp = 2 if kind == "diff" else 1
    if kind == "diff":
        lane = lax.broadcasted_iota(jnp.int32, q.shape, 1)
        qs = [jnp.where(lane < DIFF_QK, q, jnp.zeros_like(q)),
              jnp.where(lane >= DIFF_QK, q, jnp.zeros_like(q))]
    else:
        qs = [q]

    acc_ref[...] = jnp.zeros_like(acc_ref)
    p_bufs[1][...] = jnp.zeros_like(p_bufs[1])
    if softmax:
        m_ref[...] = jnp.full_like(m_ref, NEG_BIG)
        a_ref[...] = jnp.zeros_like(a_ref)
    else:
        r_ref[...] = jnp.zeros_like(r_ref)

    def key_rows(kb):
        if isinstance(kb, int):
            return pl.ds(kb * tk, tk)
        return pl.ds(pl.multiple_of(kb * tk, tk), tk)

    def scores_stage(kb, s_dst):
        k_t = k_ref[key_rows(kb), :]
        for c in range(n_comp):
            s_dst[c] = _dot_nt(k_t, qs[c])

    def values_stage(kb, p_src):
        v_t = vt_ref[kb]
        for c in range(n_comp):
            pv = _dot(v_t, p_src[c])[:acc_ref.shape[1]]
            if softmax:
                acc_ref[c] = a_ref[c] * acc_ref[c] + pv
            else:
                acc_ref[c] = acc_ref[c] + pv

    def weights_stage(kb, s_src, p_dst, masked):
        if masked:
            j = lax.broadcasted_iota(jnp.int32, (tk, tq), 0)
            i = lax.broadcasted_iota(jnp.int32, (tk, tq), 1)
            off = q0 - kb * tk
            if kind in ("diff", "mla"):
                mask = (j // CHUNK - i // CHUNK) <= off // CHUNK
            elif kind == "fox":
                mask = (j - i) <= off
            else:
                mask = (j - i) < off
        if not softmax:
            z_t = s_src[0]
            ls = _log_sigmoid(z_t)
            lk = ls - z_t
            if masked:
                lk = jnp.where(mask, lk, 0.0)
            jj = lax.broadcasted_iota(jnp.int32, (tk, tk), 0)
            mm = lax.broadcasted_iota(jnp.int32, (tk, tk), 1)
            upper = (mm > jj).astype(jnp.bfloat16)
            hi, mid = _split2(lk)
            suffix = _dot(jnp.concatenate([upper, upper], axis=1),
                          jnp.concatenate([hi, mid], axis=0))
            r_old = r_ref[...]
            w_t = jnp.exp(ls + (r_old + suffix))
            if masked:
                w_t = jnp.where(mask, w_t, 0.0)
            p_dst[0] = w_t.astype(jnp.bfloat16)
            r_ref[...] = r_old + suffix[0:1, :] + lk[0:1, :]
            return
        for c in range(n_comp):
            s_t = s_src[c]
            if kind == "fox":
                ck = ck_ref[key_rows(kb), :]
                s_t = s_t + cq_ref[...] - jnp.concatenate([ck] * (tq // LANES), axis=1)
            if masked:
                s_t = jnp.where(mask, s_t, NEG_BIG)
            m_old = m_ref[c]
            m_new = jnp.maximum(m_old, jnp.max(s_t, axis=0, keepdims=True))
            alpha = jnp.exp2(m_old - m_new)
            p_t = jnp.exp2(s_t - m_new)
            m_ref[c] = m_new
            a_ref[c] = alpha
            p_dst[c] = p_t.astype(jnp.bfloat16)

    def step(kb, parity, masked, next_kb, prev_kb):
        if next_kb is not None:
            scores_stage(next_kb, s_bufs[1 - parity])
        values_stage(prev_kb, p_bufs[1 - parity])
        weights_stage(kb, s_bufs[parity], p_bufs[parity], masked)

    if softmax and tq == 2 * tk:
        scores_stage(0, s_bufs[0])

        def body(t, carry):
            kb = 2 * t
            step(kb, 0, False, kb + 1, jnp.maximum(kb - 1, 0))
            step(kb + 1, 1, False, kb + 2, kb)
            return carry
        lax.fori_loop(0, qi, body, 0)
        step(last_kb - 1, 0, True, last_kb, jnp.maximum(last_kb - 2, 0))
        step(last_kb, 1, True, None, last_kb - 1)
        values_stage(last_kb, p_bufs[1])
    elif softmax:
        scores_stage(0, s_bufs[0])

        def body(t, carry):
            kb = 2 * t
            step(kb, 0, False, kb + 1, jnp.maximum(kb - 1, 0))
            step(kb + 1, 1, False, kb + 2, kb)
            return carry
        lax.fori_loop(0, qi // 2, body, 0)

        @pl.when(qi % 2 == 0)
        def _():
            step(qi, 0, True, None, jnp.maximum(qi - 1, 0))
            values_stage(qi, p_bufs[0])

        @pl.when(qi % 2 == 1)
        def _():
            step(qi - 1, 0, False, qi, jnp.maximum(qi - 2, 0))
            step(qi, 1, True, None, qi - 1)
            values_stage(qi, p_bufs[1])
    else:
        scores_stage(last_kb, s_bufs[0])
        step(last_kb, 0, True, last_kb - 1, last_kb)
        step(last_kb - 1, 1, True, jnp.maximum(last_kb - 2, 0), last_kb)

        def body(t, carry):
            kb = last_kb - 2 - 2 * t
            step(kb, 0, False, kb - 1, kb + 1)
            step(kb - 1, 1, False, jnp.maximum(kb - 2, 0), kb)
            return carry
        lax.fori_loop(0, qi, body, 0)
        values_stage(0, p_bufs[1])

    def normalised(c):
        return acc_ref[c, 0:HEAD_DIM, :] / acc_ref[c, HEAD_DIM:HEAD_DIM + 1, :]

    if kind == "sb":
        o_t = acc_ref[0]
    elif kind == "diff":
        o_t = normalised(0) - lam_ref[0] * normalised(1)
    else:
        o_t = normalised(0)
    o = o_t.T
    if kind == "diff":
        o = (o * lax.rsqrt(jnp.mean(o * o, axis=-1, keepdims=True) + RMS_EPS)
             * g_ref[...] * (1.0 - lam_init))
    o_ref[...] = o.astype(o_ref.dtype)


def _attention(kind, batch, seq, q_arr, q_col, k_arr, k_col, v_arr, v_col, dq,
               extras=(), lam=None, lam_init=0.0):
    t = q_arr.shape[0]
    tq = min(ATT_TQ, seq)
    tk = tq if kind in ("diff", "mla") else tq // 2
    nq = seq // tq
    q_spec = pl.BlockSpec((tq, dq), lambda b, h, i, *_: (b * nq + i, q_col + h))
    k_spec = pl.BlockSpec((seq, dq), lambda b, h, i, *_: (b, k_col + h))
    v_spec = pl.BlockSpec((seq, HEAD_DIM), lambda b, h, i, *_: (b, v_col + h))
    in_specs = [q_spec, k_spec, v_spec]
    args = [q_arr, k_arr, v_arr]
    n_comp = 2 if kind == "diff" else 1
    if kind == "diff":
        in_specs.append(_resident((1, HEAD_DIM), lambda b, h, i, *_: (0, 0)))
        args.append(extras[0])
    elif kind == "fox":
        cq_rows, ck_rep = extras
        in_specs.append(pl.BlockSpec((None, 1, tq), lambda b, h, i: (b * HEADS + h, 0, i)))
        in_specs.append(pl.BlockSpec((None, seq, LANES), lambda b, h, i: (b * HEADS + h, 0, 0)))
        args += [cq_rows, ck_rep]
    s_buf = pltpu.VMEM((n_comp, tk, tq), jnp.float32)
    p_buf = pltpu.VMEM((n_comp, tk, tq), jnp.bfloat16)
    row = pltpu.VMEM((n_comp, 1, tq), jnp.float32)
    nk = seq // tk
    if kind == "sb":
        scratch = [s_buf, s_buf, p_buf, p_buf,
                   pltpu.VMEM((nk, HEAD_DIM, tk), jnp.bfloat16),
                   pltpu.VMEM((1, tq), jnp.float32),
                   pltpu.VMEM((1, HEAD_DIM, tq), jnp.float32)]
    else:
        scratch = [s_buf, s_buf, p_buf, p_buf,
                   pltpu.VMEM((nk, HEAD_DIM + 16, tk), jnp.bfloat16), row, row,
                   pltpu.VMEM((n_comp, HEAD_DIM + 8, tq), jnp.float32)]
    body = functools.partial(_attn_body, kind=kind, tq=tq, tk=tk, lam_init=lam_init)
    grid = (batch, HEADS, nq)
    out_shape = jax.ShapeDtypeStruct((t, BRANCH_WIDTH), jnp.bfloat16)
    out_spec = pl.BlockSpec((tq, HEAD_DIM), lambda b, h, i, *_: (b * nq + i, h))
    cp = _cparams(("parallel", "parallel", "arbitrary"))
    if kind == "diff":
        return pl.pallas_call(
            body, out_shape=out_shape,
            grid_spec=pltpu.PrefetchScalarGridSpec(
                num_scalar_prefetch=1, grid=grid, in_specs=in_specs, out_specs=out_spec,
                scratch_shapes=scratch),
            compiler_params=cp, name="attn_" + kind)(lam, *args)
    return pl.pallas_call(
        body, out_shape=out_shape, grid=grid, in_specs=in_specs, out_specs=out_spec,
        scratch_shapes=scratch, compiler_params=cp, name="attn_" + kind)(*args)


def _layer_norm(y, g, b):
    mu = jnp.mean(y, axis=-1, keepdims=True)
    yc = y - mu
    var = jnp.mean(yc * yc, axis=-1, keepdims=True)
    return yc * lax.rsqrt(var + LN_EPS) * g + b


def _merge_body(g_ref, od_ref, of_ref, om_ref, os_ref, wb_ref, wo_ref, x_ref, lg_ref, lb_ref,
                wrh_ref, wrl_ref, h_o, hr_o, hb_o, lgt_o, *, alpha):
    d = x_ref.shape[1]
    merged = None
    for n, o_ref in enumerate((od_ref, of_ref, om_ref, os_ref)):
        gate = _sigmoid(g_ref[:, n * d:(n + 1) * d].astype(jnp.float32))
        term = gate * _dot(o_ref[...], wb_ref[n])
        merged = term if merged is None else merged + term
    mix = _dot(merged.astype(jnp.bfloat16), wo_ref[...])
    h = _layer_norm(alpha * x_ref[...] + mix, lg_ref[...], lb_ref[...])
    h_o[...] = h
    _to_chunked(hr_o, h)
    h_hi = h.astype(jnp.bfloat16)
    hb_o[...] = h_hi
    h_lo = (h - h_hi.astype(jnp.float32)).astype(jnp.bfloat16)
    lgt_o[...] = (_dot(h_hi, wrh_ref[...]) + _dot(h_lo, wrh_ref[...]) + _dot(h_hi, wrl_ref[...]))


def _merge(u, o_d, o_f, o_m, o_s, w_b, w_o, x, ln_g, ln_b, wr_hi, wr_lo, alpha):
    t, d = x.shape
    tm = min(256, t)
    row = lambda i: (i, 0)
    const2 = lambda i: (0, 0)
    o_spec = pl.BlockSpec((tm, BRANCH_WIDTH), row)
    return pl.pallas_call(
        functools.partial(_merge_body, alpha=alpha),
        out_shape=(jax.ShapeDtypeStruct((t, d), jnp.float32),
                   jax.ShapeDtypeStruct((t * _chunk_pitch(d), LANES), jnp.float32),
                   jax.ShapeDtypeStruct((t, d), jnp.bfloat16),
                   jax.ShapeDtypeStruct((t, LANES), jnp.float32)),
        grid=(t // tm,),
        in_specs=[pl.BlockSpec((tm, N_BRANCHES * d), row),
                  o_spec, o_spec, o_spec, o_spec,
                  _resident(w_b.shape, lambda i: (0, 0, 0)),
                  _resident(w_o.shape, const2),
                  pl.BlockSpec((tm, d), row),
                  _resident((1, d), const2), _resident((1, d), const2),
                  _resident(wr_hi.shape, const2), _resident(wr_lo.shape, const2)],
        out_specs=(pl.BlockSpec((tm, d), row),
                   pl.BlockSpec((tm * _chunk_pitch(d), LANES), row),
                   pl.BlockSpec((tm, d), row), pl.BlockSpec((tm, LANES), row)),
        compiler_params=_cparams(("parallel",)),
        name="merge_ln1",
    )(u, o_d, o_f, o_m, o_s, w_b, w_o, x, ln_g, ln_b, wr_hi, wr_lo)


CHUNK_PAD = 4


def _chunk_pitch(d):
    return d // LANES + CHUNK_PAD


def _to_chunked(ref, x):
    n, d = x.shape
    nc, pitch = d // LANES, _chunk_pitch(d)
    for c in range(nc):
        ref[pl.ds(c, n, stride=pitch), :] = x[:, c * LANES:(c + 1) * LANES]
    for c in range(nc, pitch):
        ref[pl.ds(c, n, stride=pitch), :] = jnp.zeros((n, LANES), x.dtype)


def _from_chunked(ref, n, d):
    pitch = _chunk_pitch(d)
    return jnp.concatenate([ref[pl.ds(c, n, stride=pitch), :] for c in range(d // LANES)],
                           axis=1)


def _gather_rows(idx_ref, base, src_hbm, dst, sem, n, d):
    nc, pitch = d // LANES, _chunk_pitch(d)
    for r in range(n):
        row = pl.multiple_of(idx_ref[base + r] * pitch, CHUNK_PAD)
        pltpu.make_async_copy(src_hbm.at[pl.ds(row, nc), :], dst.at[pl.ds(r * pitch, nc), :],
                              sem).start()


def _wait_rows(src_hbm, dst, sem, n, d):
    rows = n * (d // LANES)
    pltpu.make_async_copy(src_hbm.at[pl.ds(0, rows), :], dst.at[pl.ds(0, rows), :], sem).wait()


def _moe_body(sched_ref, n_used_ref, tok_ref, h_hbm, wg_hbm, wu_hbm, wd_hbm, y_ref,
              xbuf, sem, wg_f, wu_f, wd_f, wsem, wg_b, wu_b, wd_b, *, layer):
    i = pl.program_id(0)
    n_used = n_used_ref[0]
    d = wg_b.shape[0]
    bm = xbuf.shape[1] // _chunk_pitch(d)
    slot = i % 2

    def weight_copies(e, s):
        return (pltpu.make_async_copy(wg_hbm.at[layer, e], wg_f.at[s], wsem.at[s]),
                pltpu.make_async_copy(wu_hbm.at[layer, e], wu_f.at[s], wsem.at[s]),
                pltpu.make_async_copy(wd_hbm.at[layer, e], wd_f.at[s], wsem.at[s]))

    @pl.when((i == 0) & (n_used > 0))
    def _():
        for cp in weight_copies(sched_ref[0, 0], 0):
            cp.start()
        _gather_rows(tok_ref, 0, h_hbm, xbuf.at[0], sem.at[0], bm, d)

    @pl.when(i + 1 < n_used)
    def _():
        _gather_rows(tok_ref, (i + 1) * bm, h_hbm, xbuf.at[1 - slot], sem.at[1 - slot], bm, d)

    @pl.when(i < n_used)
    def _():
        @pl.when(sched_ref[1, i] == 1)
        def _():
            ws = sched_ref[2, i]
            for cp in weight_copies(sched_ref[0, i], ws):
                cp.wait()
            wg_b[...] = wg_f[ws].astype(jnp.bfloat16)
            wu_b[...] = wu_f[ws].astype(jnp.bfloat16)
            wd_b[...] = wd_f[ws].astype(jnp.bfloat16)

            @pl.when(sched_ref[3, i] >= 0)
            def _():
                for cp in weight_copies(sched_ref[3, i], 1 - ws):
                    cp.start()

        _wait_rows(h_hbm, xbuf.at[slot], sem.at[slot], bm, d)
        x = _from_chunked(xbuf.at[slot], bm, d).astype(jnp.bfloat16)
        a = _dot(x, wg_b[...])
        hid = (a * _sigmoid(a)) * _dot(x, wu_b[...])
        _to_chunked(y_ref, _dot(hid.astype(jnp.bfloat16), wd_b[...]))

    @pl.when(i >= n_used)
    def _():
        y_ref[...] = jnp.zeros_like(y_ref)


def _moe(sched, n_used, row_tok, h_rows, w_g, w_u, w_d, layer):
    n_rows = row_tok.shape[0]
    d, hid = w_g.shape[2], w_g.shape[3]
    pitch = _chunk_pitch(d)
    bm = MOE_ROWS
    any_spec = pl.BlockSpec(memory_space=pl.ANY)
    return pl.pallas_call(
        functools.partial(_moe_body, layer=layer),
        out_shape=jax.ShapeDtypeStruct((n_rows * pitch, LANES), jnp.float32),
        grid_spec=pltpu.PrefetchScalarGridSpec(
            num_scalar_prefetch=3, grid=(n_rows // bm,),
            in_specs=[any_spec, any_spec, any_spec, any_spec],
            out_specs=pl.BlockSpec((bm * pitch, LANES), lambda i, s, n, tok: (i, 0)),
            scratch_shapes=[pltpu.VMEM((2, bm * pitch, LANES), jnp.float32),
                            pltpu.SemaphoreType.DMA((2,)),
                            pltpu.VMEM((2, d, hid), jnp.float32),
                            pltpu.VMEM((2, d, hid), jnp.float32),
                            pltpu.VMEM((2, hid, d), jnp.float32),
                            pltpu.SemaphoreType.DMA((2,)),
                            pltpu.VMEM((d, hid), jnp.bfloat16),
                            pltpu.VMEM((d, hid), jnp.bfloat16),
                            pltpu.VMEM((hid, d), jnp.bfloat16)]),
        compiler_params=_cparams(("arbitrary",)),
        name="moe_experts",
    )(sched, n_used, row_tok, h_rows, w_g, w_u, w_d)


def _final_body(d0_ref, d1_ref, h_ref, hb_ref, y_hbm, gw_ref, p_ref, wg_ref, wp_ref, lg_ref,
                lb_ref, y_o, yb_o, gbuf, sem, *, alpha):
    i = pl.program_id(0)
    n = pl.num_programs(0)
    tm, d = h_ref.shape
    slot = i % 2

    def gather(step, s):
        _gather_rows(d0_ref, step * tm, y_hbm, gbuf.at[s, 0], sem.at[s], tm, d)
        _gather_rows(d1_ref, step * tm, y_hbm, gbuf.at[s, 1], sem.at[s], tm, d)

    @pl.when(i == 0)
    def _():
        gather(0, 0)

    @pl.when(i + 1 < n)
    def _():
        gather(i + 1, 1 - slot)

    gate = _sigmoid(_dot(hb_ref[...], wg_ref[...]))
    proj = _dot(p_ref[...].astype(jnp.bfloat16), wp_ref[...])
    _wait_rows(y_hbm, gbuf.at[slot, 0], sem.at[slot], tm, d)
    _wait_rows(y_hbm, gbuf.at[slot, 1], sem.at[slot], tm, d)
    gw = gw_ref[...]
    ffn = (_from_chunked(gbuf.at[slot, 0], tm, d) * gw[:, 0:1]
           + _from_chunked(gbuf.at[slot, 1], tm, d) * gw[:, 1:2])
    y = _layer_norm(alpha * h_ref[...] + ffn + gate * proj, lg_ref[...], lb_ref[...])
    y_o[...] = y
    yb_o[...] = y.astype(jnp.bfloat16)


def _final(dest0, dest1, h, hb, y_rows, gate_w, p, w_pg, w_pp, ln_g, ln_b, alpha):
    t, d = h.shape
    tm = min(256, t)
    row = lambda i, *_: (i, 0)
    const2 = lambda i, *_: (0, 0)
    return pl.pallas_call(
        functools.partial(_final_body, alpha=alpha),
        out_shape=(jax.ShapeDtypeStruct((t, d), jnp.float32),
                   jax.ShapeDtypeStruct((t, d), jnp.bfloat16)),
        grid_spec=pltpu.PrefetchScalarGridSpec(
            num_scalar_prefetch=2, grid=(t // tm,),
            in_specs=[pl.BlockSpec((tm, d), row), pl.BlockSpec((tm, d), row),
                      pl.BlockSpec(memory_space=pl.ANY),
                      pl.BlockSpec((tm, TOP_K), row), pl.BlockSpec((tm, p.shape[1]), row),
                      _resident(w_pg.shape, const2), _resident(w_pp.shape, const2),
                      _resident((1, d), const2), _resident((1, d), const2)],
            out_specs=(pl.BlockSpec((tm, d), row), pl.BlockSpec((tm, d), row)),
            scratch_shapes=[pltpu.VMEM((2, TOP_K, tm * _chunk_pitch(d), LANES), jnp.float32),
                            pltpu.SemaphoreType.DMA((2,))]),
        compiler_params=_cparams(("arbitrary",)),
        name="final_ln2",
    )(dest0, dest1, h, hb, y_rows, gate_w, p, w_pg, w_pp, ln_g, ln_b)


def _rope_tables(positions):
    pos = positions.reshape(-1).astype(jnp.float32)
    lane = jnp.arange(LANES)

    def tables(rot, period):
        half = rot // 2
        inv_freq = ROPE_THETA ** (-jnp.arange(half, dtype=jnp.float32) / half)
        ang = pos[:, None] * inv_freq
        cos, sin = jnp.cos(ang), jnp.sin(ang)
        lp = lane % period
        idx = lp % half
        in_rot = lp < rot
        c = jnp.where(in_rot[None, :], cos[:, idx], 1.0)
        s_up = jnp.where((lp < half)[None, :], -sin[:, idx], 0.0)
        s_dn = jnp.where(((lp >= half) & in_rot)[None, :], sin[:, idx], 0.0)
        return c, s_up, s_dn

    return tables(DIFF_ROT, DIFF_QK) + tables(MLA_ROPE, LANES)


def _w_in_segments(d):
    widths = [512, 512, 512, 512, 512, 512, HEADS, MLA_Q_LORA, MLA_KV_LORA, MLA_ROPE,
              512, 512, 512, N_BRANCHES * d]
    dsts = [COL_DQ, COL_DQ + 512, COL_DV, COL_FQ, COL_FK, COL_FV, COL_FF, COL_CQ, COL_CKV,
            COL_KR, COL_SQ, COL_SK, COL_SV, COL_GATES]
    segs, src = [], 0
    for wd, dst in zip(widths, dsts):
        segs.append((src, dst, wd))
        src += wd
    return segs


def _w_in_body(w_ref, o_ref, f_ref, *, segs):
    for src, dst, wd in segs:
        o_ref[:, dst:dst + wd] = w_ref[:, src:src + wd].astype(o_ref.dtype)
    rows = o_ref.shape[0]
    o_ref[:, COL_KR + MLA_ROPE:COL_KR + LANES] = jnp.zeros((rows, LANES - MLA_ROPE), o_ref.dtype)
    o_ref[:, COL_FF + HEADS:COL_FF + LANES] = jnp.zeros((rows, LANES - HEADS), o_ref.dtype)
    f_ref[...] = o_ref[:, COL_FF:COL_FF + LANES]


def _reorder_w_in(w_in, layer):
    _, d, n_in = w_in.shape
    tr = 128
    return pl.pallas_call(
        functools.partial(_w_in_body, segs=_w_in_segments(d)),
        out_shape=(jax.ShapeDtypeStruct((d, U_WIDTH), jnp.bfloat16),
                   jax.ShapeDtypeStruct((d, LANES), jnp.bfloat16)),
        grid=(d // tr,),
        in_specs=[pl.BlockSpec((None, tr, n_in), lambda r: (layer, r, 0))],
        out_specs=(pl.BlockSpec((tr, U_WIDTH), lambda r: (r, 0)),
                   pl.BlockSpec((tr, LANES), lambda r: (r, 0))),
        compiler_params=_cparams(("parallel",)),
        name="w_in_relayout",
    )(w_in)


def _route(logits, b_rg, b_re, t):
    g_logits = logits[:, :N_GROUPS] + b_rg
    grp = jnp.argmax(g_logits, axis=-1)
    p_grp = jnp.take_along_axis(jax.nn.softmax(g_logits, axis=-1), grp[:, None], axis=-1)
    e_logits = (logits[:, N_GROUPS:N_GROUPS + N_EXPERTS] + b_re).reshape(
        t, N_GROUPS, EXPERTS_PER_GROUP)
    e_in = jnp.take_along_axis(e_logits, grp[:, None, None], axis=1)[:, 0]
    top_p, top_i = lax.top_k(jax.nn.softmax(e_in, axis=-1), TOP_K)
    weights = p_grp * top_p / jnp.sum(top_p, axis=-1, keepdims=True)
    expert_idx = grp[:, None].astype(jnp.int32) * EXPERTS_PER_GROUP + top_i.astype(jnp.int32)
    return expert_idx, weights


def _dispatch_plan(expert_idx, t):
    a = t * TOP_K
    flat_e = expert_idx.reshape(a)
    onehot = (flat_e[:, None] == jnp.arange(N_EXPERTS)[None, :]).astype(jnp.int32)
    rank = jnp.take_along_axis(jnp.cumsum(onehot, axis=0), flat_e[:, None], axis=1)[:, 0] - 1
    sizes = jnp.sum(onehot, axis=0)
    padded = (sizes + MOE_ROWS - 1) // MOE_ROWS * MOE_ROWS
    pad_end = jnp.cumsum(padded)
    pad_start = pad_end - padded
    dest = (pad_start[flat_e] + rank).astype(jnp.int32)
    n_rows = a + N_EXPERTS * MOE_ROWS
    n_blk = n_rows // MOE_ROWS
    flat_tok = jnp.arange(a, dtype=jnp.int32) // TOP_K
    row_tok = jnp.zeros((n_rows,), jnp.int32).at[dest].set(flat_tok)
    blk_start = jnp.arange(n_blk, dtype=pad_end.dtype) * MOE_ROWS
    blk_e = jnp.minimum(jnp.sum(pad_end[None, :] <= blk_start[:, None], axis=1),
                        N_EXPERTS - 1).astype(jnp.int32)
    n_used = (pad_end[-1] // MOE_ROWS).astype(jnp.int32).reshape(1)
    first = jnp.concatenate([jnp.ones((1,), jnp.int32),
                             (blk_e[1:] != blk_e[:-1]).astype(jnp.int32)])
    parity = (jnp.cumsum(first) - 1) % 2
    experts = jnp.arange(N_EXPERTS, dtype=jnp.int32)
    used = jnp.where(padded > 0, experts, N_EXPERTS)
    later = jnp.flip(lax.cummin(jnp.flip(used)))
    next_used = jnp.concatenate([later[1:], jnp.full((1,), N_EXPERTS, jnp.int32)])
    next_used = jnp.where(next_used < N_EXPERTS, next_used, -1)
    sched = jnp.stack([blk_e, first, parity.astype(jnp.int32),
                       next_used[blk_e].astype(jnp.int32)])
    return dest.reshape(t, TOP_K), row_tok, sched, n_used


def kernel(x, p, positions, w_in, fox_f_bias, diff_lambda, diff_subln, mla_q_norm, mla_kv_norm,
           mla_w_uq, mla_w_ukv, w_branch, w_o, ln1_g, ln1_b, w_router_group, b_router_group,
           w_router_expert, b_router_expert, w_expert_gate, w_expert_up, w_expert_down,
           w_ple_gate, w_ple_proj, ln2_g, ln2_b):
    batch, seq, d = x.shape
    depth = w_in.shape[0]
    t = batch * seq
    alpha = (2 * depth) ** 0.25
    bf = jnp.bfloat16
    tabs = _rope_tables(positions)
    col_scale = jnp.ones((1, U_WIDTH), jnp.float32)
    col_scale = col_scale.at[0, COL_FQ:COL_FQ + BRANCH_WIDTH].set(HEAD_DIM ** -0.5 * LOG2E)
    col_scale = col_scale.at[0, COL_SQ:COL_SQ + BRANCH_WIDTH].set(HEAD_DIM ** -0.5)
    xf = x.reshape(t, d)
    xb = xf.astype(bf)

    for i in range(depth):
        w1, w_f = _reorder_w_in(w_in, i)
        u = _matmul(xb, w1, col_scale, bf, 1024, 1536, "in_proj")

        b_f = jnp.zeros((1, LANES), jnp.float32).at[0, :HEADS].set(fox_f_bias[i])
        cum = _fgate_cumsum(xb, w_f, b_f, batch, seq)[:, :HEADS] * LOG2E
        cum_bh = jnp.transpose(cum.reshape(batch, seq, HEADS), (0, 2, 1)).reshape(
            batch * HEADS, seq)
        cq_rows = cum_bh.reshape(batch * HEADS, 1, seq)
        ck_rep = jnp.broadcast_to(cum_bh[:, :, None], (batch * HEADS, seq, LANES))

        uq = mla_w_uq[i].reshape(MLA_Q_LORA, HEADS, MLA_NOPE + MLA_ROPE)
        uq = jnp.pad(uq, ((0, 0), (0, 0), (0, MLA_QK_PAD - MLA_NOPE - MLA_ROPE)))
        uq = uq.reshape(MLA_Q_LORA, HEADS * MLA_QK_PAD).astype(bf)
        ukv = mla_w_ukv[i].reshape(MLA_KV_LORA, HEADS, MLA_NOPE + HEAD_DIM)
        uk = ukv[:, :, :MLA_NOPE].reshape(MLA_KV_LORA, HEADS * MLA_NOPE).astype(bf)
        uv = ukv[:, :, MLA_NOPE:].reshape(MLA_KV_LORA, HEADS * HEAD_DIM).astype(bf)
        dqk, mq, mk, mv = _prep(u, tabs, mla_q_norm[i].reshape(1, -1),
                                mla_kv_norm[i].reshape(1, -1), uq, uk, uv)

        lam_init = 0.8 - 0.6 * math.exp(-0.3 * i)
        lp = diff_lambda[i].astype(jnp.float32)
        lam = (jnp.exp(jnp.sum(lp[0] * lp[1])) - jnp.exp(jnp.sum(lp[2] * lp[3]))
               + lam_init).reshape(1)
        blk = lambda col: col // HEAD_DIM
        o_d = _attention("diff", batch, seq, dqk, 0, dqk, HEADS, u, blk(COL_DV), HEAD_DIM,
                         extras=(diff_subln[i].reshape(1, HEAD_DIM),),
                         lam=lam, lam_init=lam_init)
        o_f = _attention("fox", batch, seq, u, blk(COL_FQ), u, blk(COL_FK), u, blk(COL_FV),
                         HEAD_DIM, extras=(cq_rows, ck_rep))
        o_m = _attention("mla", batch, seq, mq, 0, mk, 0, mv, 0, MLA_QK_PAD)
        o_s = _attention("sb", batch, seq, u, blk(COL_SQ), u, blk(COL_SK), u, blk(COL_SV),
                         HEAD_DIM)

        w_r = jnp.concatenate([w_router_group[i], w_router_expert[i]], axis=1)
        w_r = jnp.pad(w_r, ((0, 0), (0, LANES - w_r.shape[1])))
        wr_hi = w_r.astype(bf)
        wr_lo = (w_r - wr_hi.astype(jnp.float32)).astype(bf)
        h, h_rows, hb, logits = _merge(
            u, o_d, o_f, o_m, o_s, w_branch[i].astype(bf), w_o[i].astype(bf), xf,
            ln1_g[i].reshape(1, d), ln1_b[i].reshape(1, d), wr_hi, wr_lo, alpha)

        expert_idx, weights = _route(logits, b_router_group[i], b_router_expert[i], t)
        dest, row_tok, sched, n_used = _dispatch_plan(expert_idx, t)
        y_rows = _moe(sched, n_used, row_tok, h_rows, w_expert_gate, w_expert_up,
                      w_expert_down, i)
        xf, xb = _final(dest[:, 0], dest[:, 1], h, hb, y_rows, weights.astype(jnp.float32),
                        p[i].reshape(t, -1), w_ple_gate[i].astype(bf),
                        w_ple_proj[i].astype(bf), ln2_g[i].reshape(1, d),
                        ln2_b[i].reshape(1, d), alpha)
    return xf.reshape(batch, seq, d)
```

```python
import functools
import math

import jax
import jax.numpy as jnp
from jax import lax
from jax.experimental import pallas as pl
from jax.experimental.pallas import tpu as pltpu

HEAD_DIM = 128
HEADS = 4
N_BRANCHES = 4
BRANCH_WIDTH = HEADS * HEAD_DIM
DIFF_QK = 64
DIFF_ROT = DIFF_QK // 4
MLA_Q_LORA = 512
MLA_KV_LORA = 256
MLA_NOPE = 128
MLA_ROPE = 64
MLA_QK_PAD = 256
N_GROUPS = 4
EXPERTS_PER_GROUP = 8
N_EXPERTS = N_GROUPS * EXPERTS_PER_GROUP
TOP_K = 2
CHUNK = 64
ROPE_THETA = 500000.0
LN_EPS = 1e-5
RMS_EPS = 1e-6
NEG_BIG = -1e30
LOG2E = 1.4426950408889634
SB_EXP_UNDERFLOW = -104.0
LANES = 128
V7X_VMEM_LIMIT = 56 * 1024 * 1024

COL_GATES = 0
COL_DQ = 8192
COL_DV = COL_DQ + 1024
COL_FQ = COL_DV + 512
COL_FK = COL_FQ + 512
COL_FV = COL_FK + 512
COL_SQ = COL_FV + 512
COL_SK = COL_SQ + 512
COL_SV = COL_SK + 512
COL_CQ = COL_SV + 512
COL_CKV = COL_CQ + 512
COL_KR = COL_CKV + 256
COL_FF = COL_KR + 128
U_WIDTH = COL_FF + 128

ATT_TQ = 512
MOE_ROWS = 256


def _cparams(sem, vmem=V7X_VMEM_LIMIT):
    return pltpu.CompilerParams(dimension_semantics=sem, vmem_limit_bytes=vmem)


def _resident(shape, index_map):
    return pl.BlockSpec(shape, index_map, pipeline_mode=pl.Buffered(1))


def _matmul_body(x_ref, w_ref, c_ref, o_ref):
    acc = jnp.dot(x_ref[...], w_ref[...], preferred_element_type=jnp.float32)
    o_ref[...] = (acc * c_ref[...]).astype(o_ref.dtype)


def _matmul(x, w, col_scale, out_dtype, tm, tn, name):
    m, k = x.shape
    n = w.shape[1]
    tm, tn = min(tm, m), min(tn, n)
    return pl.pallas_call(
        _matmul_body,
        out_shape=jax.ShapeDtypeStruct((m, n), out_dtype),
        grid=(m // tm, n // tn),
        in_specs=[pl.BlockSpec((tm, k), lambda i, j: (i, 0)),
                  pl.BlockSpec((k, tn), lambda i, j: (0, j)),
                  pl.BlockSpec((1, tn), lambda i, j: (0, j))],
        out_specs=pl.BlockSpec((tm, tn), lambda i, j: (i, j)),
        compiler_params=_cparams(("parallel", "arbitrary")),
        name=name,
    )(x, w, col_scale)


def _split3(x):
    hi = x.astype(jnp.bfloat16)
    r1 = x - hi.astype(jnp.float32)
    mid = r1.astype(jnp.bfloat16)
    lo = (r1 - mid.astype(jnp.float32)).astype(jnp.bfloat16)
    return hi, mid, lo


def _split2(x):
    hi = x.astype(jnp.bfloat16)
    mid = (x - hi.astype(jnp.float32)).astype(jnp.bfloat16)
    return hi, mid


def _dot(a, b):
    return jnp.dot(a, b, preferred_element_type=jnp.float32)


def _dot_nt(a, b):
    return lax.dot_general(a, b, (((1,), (1,)), ((), ())), preferred_element_type=jnp.float32)


def _dot_tn(a, b):
    return lax.dot_general(a, b, (((0,), (0,)), ((), ())), preferred_element_type=jnp.float32)


def _log_sigmoid(z):
    neg_abs = pltpu.bitcast(pltpu.bitcast(z, jnp.uint32) | jnp.uint32(0x80000000), jnp.float32)
    return jnp.minimum(z, 0.0) - jnp.log(1.0 + jnp.exp(neg_abs))


def _sigmoid(z):
    return 1.0 / (1.0 + jnp.exp(-z))


def _fgate_body(x_ref, w_ref, b_ref, o_ref, carry_ref):
    @pl.when(pl.program_id(1) == 0)
    def _():
        carry_ref[...] = jnp.zeros_like(carry_ref)

    f = _dot(x_ref[...], w_ref[...]) + b_ref[...]
    log_f = _log_sigmoid(f)
    ts = log_f.shape[0]
    row = lax.broadcasted_iota(jnp.int32, (ts, ts), 0)
    col = lax.broadcasted_iota(jnp.int32, (ts, ts), 1)
    tri = (col <= row).astype(jnp.bfloat16)
    hi, mid, lo = _split3(log_f)
    cum = _dot(tri, hi) + _dot(tri, mid) + _dot(tri, lo) + carry_ref[...]
    o_ref[...] = cum
    carry_ref[...] = cum[ts - 1:ts, :]


def _fgate_cumsum(xb, w_f, b_f, batch, seq):
    t, d = xb.shape
    ts = min(512, seq)
    ns = seq // ts
    return pl.pallas_call(
        _fgate_body,
        out_shape=jax.ShapeDtypeStruct((t, LANES), jnp.float32),
        grid=(batch, ns),
        in_specs=[pl.BlockSpec((ts, d), lambda b, s: (b * ns + s, 0)),
                  _resident((d, LANES), lambda b, s: (0, 0)),
                  _resident((1, LANES), lambda b, s: (0, 0))],
        out_specs=pl.BlockSpec((ts, LANES), lambda b, s: (b * ns + s, 0)),
        scratch_shapes=[pltpu.VMEM((1, LANES), jnp.float32)],
        compiler_params=_cparams(("parallel", "arbitrary")),
        name="fgate_cumsum",
    )(xb, w_f, b_f)


def _rope_lanes(x, c, s_up, s_dn, half):
    return x * c + pltpu.roll(x, LANES - half, 1) * s_up + pltpu.roll(x, half, 1) * s_dn


def _prep_body(dqk_ref, cq_ref, ckv_ref, kr_ref, c16_ref, su16_ref, sd16_ref,
               c64_ref, su64_ref, sd64_ref, qg_ref, kvg_ref, wuq_ref, wuk_ref, wuv_ref,
               dqk_o, mq_o, mk_o, mv_o):
    c16, su16, sd16 = c16_ref[...], su16_ref[...], sd16_ref[...]
    c64, su64, sd64 = c64_ref[...], su64_ref[...], sd64_ref[...]
    for j in range(dqk_ref.shape[1] // LANES):
        sl = slice(j * LANES, (j + 1) * LANES)
        x = _rope_lanes(dqk_ref[:, sl].astype(jnp.float32), c16, su16, sd16, DIFF_ROT // 2)
        if j < HEADS:
            x = x * (DIFF_QK ** -0.5 * LOG2E)
        dqk_o[:, sl] = x.astype(dqk_o.dtype)

    cq = cq_ref[...].astype(jnp.float32)
    cqn = cq * lax.rsqrt(jnp.mean(cq * cq, axis=-1, keepdims=True) + RMS_EPS) * qg_ref[...]
    q = _dot(cqn.astype(jnp.bfloat16), wuq_ref[...])
    q = q * ((MLA_NOPE + MLA_ROPE) ** -0.5 * LOG2E)
    ckv = ckv_ref[...].astype(jnp.float32)
    ckvn = (ckv * lax.rsqrt(jnp.mean(ckv * ckv, axis=-1, keepdims=True) + RMS_EPS)
            * kvg_ref[...]).astype(jnp.bfloat16)
    kn = _dot(ckvn, wuk_ref[...])
    mv_o[...] = _dot(ckvn, wuv_ref[...]).astype(mv_o.dtype)
    kr = _rope_lanes(kr_ref[...].astype(jnp.float32), c64, su64, sd64, MLA_ROPE // 2)
    for h in range(HEADS):
        base = h * MLA_QK_PAD
        mq_o[:, base:base + LANES] = q[:, base:base + LANES].astype(mq_o.dtype)
        qr = _rope_lanes(q[:, base + LANES:base + 2 * LANES], c64, su64, sd64, MLA_ROPE // 2)
        mq_o[:, base + LANES:base + 2 * LANES] = qr.astype(mq_o.dtype)
        mk_o[:, base:base + LANES] = kn[:, h * LANES:(h + 1) * LANES].astype(mk_o.dtype)
        mk_o[:, base + LANES:base + 2 * LANES] = kr.astype(mk_o.dtype)


def _prep(u, tabs, q_norm, kv_norm, w_uq, w_uk, w_uv):
    t = u.shape[0]
    tm = min(512, t)
    row = lambda blk: (lambda i: (i, blk))
    const = lambda i: (0, 0)
    tab_spec = pl.BlockSpec((tm, LANES), lambda i: (i, 0))
    outs = pl.pallas_call(
        _prep_body,
        out_shape=(jax.ShapeDtypeStruct((t, 1024), jnp.bfloat16),
                   jax.ShapeDtypeStruct((t, HEADS * MLA_QK_PAD), jnp.bfloat16),
                   jax.ShapeDtypeStruct((t, HEADS * MLA_QK_PAD), jnp.bfloat16),
                   jax.ShapeDtypeStruct((t, BRANCH_WIDTH), jnp.bfloat16)),
        grid=(t // tm,),
        in_specs=[pl.BlockSpec((tm, 1024), row(COL_DQ // 1024)),
                  pl.BlockSpec((tm, MLA_Q_LORA), row(COL_CQ // MLA_Q_LORA)),
                  pl.BlockSpec((tm, MLA_KV_LORA), row(COL_CKV // MLA_KV_LORA)),
                  pl.BlockSpec((tm, LANES), row(COL_KR // LANES)),
                  tab_spec, tab_spec, tab_spec, tab_spec, tab_spec, tab_spec,
                  _resident((1, MLA_Q_LORA), const),
                  _resident((1, MLA_KV_LORA), const),
                  _resident(w_uq.shape, const),
                  _resident(w_uk.shape, const),
                  _resident(w_uv.shape, const)],
        out_specs=(pl.BlockSpec((tm, 1024), lambda i: (i, 0)),
                   pl.BlockSpec((tm, HEADS * MLA_QK_PAD), lambda i: (i, 0)),
                   pl.BlockSpec((tm, HEADS * MLA_QK_PAD), lambda i: (i, 0)),
                   pl.BlockSpec((tm, BRANCH_WIDTH), lambda i: (i, 0))),
        compiler_params=_cparams(("parallel",)),
        name="prep",
    )(u, u, u, u, *tabs, q_norm, kv_norm, w_uq, w_uk, w_uv)
    return outs


def _attn_body(*refs, kind, tq, tk, lam_init):
    softmax = kind != "sb"
    if kind == "diff":
        lam_ref, q_ref, k_ref, v_ref, g_ref, o_ref = refs[:6]
    elif kind == "fox":
        q_ref, k_ref, v_ref, cq_ref, ck_ref, o_ref = refs[:6]
    else:
        q_ref, k_ref, v_ref, o_ref = refs[:4]
    if softmax:
        s_bufs, p_bufs = refs[-8:-6], refs[-6:-4]
        vt_ref, m_ref, a_ref, acc_ref = refs[-4:]
    else:
        s_bufs, p_bufs = refs[-7:-5], refs[-5:-3]
        vt_ref, r_ref, acc_ref = refs[-3:]
    qi = pl.program_id(2)

    @pl.when(qi == 0)
    def _():
        for kb in range(vt_ref.shape[0]):
            vt_ref[kb, 0:HEAD_DIM, :] = v_ref[kb * tk:(kb + 1) * tk, :].T
            if softmax:
                vt_ref[kb, HEAD_DIM:, :] = jnp.ones((vt_ref.shape[1] - HEAD_DIM, tk),
                                                    vt_ref.dtype)

    q0 = qi * tq
    last_kb = 2 * qi + 1
    q = q_ref[...]
    n_comp = 2 if kind == "diff" else 1
    if kind == "diff":
        lane = lax.broadcasted_iota(jnp.int32, q.shape, 1)
        qs = [jnp.where(lane < DIFF_QK, q, jnp.zeros_like(q)),
              jnp.where(lane >= DIFF_QK, q, jnp.zeros_like(q))]
    else:
        qs = [q]

    acc_ref[...] = jnp.zeros_like(acc_ref)
    p_bufs[1][...] = jnp.zeros_like(p_bufs[1])
    if softmax:
        m_ref[...] = jnp.full_like(m_ref, NEG_BIG)
        a_ref[...] = jnp.zeros_like(a_ref)
    else:
        r_ref[...] = jnp.zeros_like(r_ref)

    def key_rows(kb):
        if isinstance(kb, int):
            return pl.ds(kb * tk, tk)
        return pl.ds(pl.multiple_of(kb * tk, tk), tk)

    def scores_stage(kb, s_dst):
        k_t = k_ref[key_rows(kb), :]
        for c in range(n_comp):
            s_dst[c] = _dot_nt(k_t, qs[c])

    def values_stage(kb, p_src):
        v_t = vt_ref[kb]
        for c in range(n_comp):
            pv = _dot(v_t, p_src[c])[:acc_ref.shape[1]]
            if softmax:
                acc_ref[c] = a_ref[c] * acc_ref[c] + pv
            else:
                acc_ref[c] = acc_ref[c] + pv

    def weights_stage(kb, s_src, p_dst, masked):
        if masked:
            j = lax.broadcasted_iota(jnp.int32, (tk, tq), 0)
            i = lax.broadcasted_iota(jnp.int32, (tk, tq), 1)
            off = q0 - kb * tk
            if kind in ("diff", "mla"):
                mask = (j // CHUNK - i // CHUNK) <= off // CHUNK
            elif kind == "fox":
                mask = (j - i) <= off
            else:
                mask = (j - i) < off
        if not softmax:
            z_t = s_src[0]
            ls = _log_sigmoid(z_t)
            lk = ls - z_t
            if masked:
                lk = jnp.where(mask, lk, 0.0)
            jj = lax.broadcasted_iota(jnp.int32, (tk, tk), 0)
            mm = lax.broadcasted_iota(jnp.int32, (tk, tk), 1)
            upper = (mm > jj).astype(jnp.bfloat16)
            hi, mid = _split2(lk)
            suffix = _dot(jnp.concatenate([upper, upper], axis=1),
                          jnp.concatenate([hi, mid], axis=0))
            r_old = r_ref[...]
            w_t = jnp.exp(ls + (r_old + suffix))
            if masked:
                w_t = jnp.where(mask, w_t, 0.0)
            p_dst[0] = w_t.astype(jnp.bfloat16)
            r_ref[...] = r_old + suffix[0:1, :] + lk[0:1, :]
            return
        for c in range(n_comp):
            s_t = s_src[c]
            if kind == "fox":
                ck = ck_ref[key_rows(kb), :]
                s_t = s_t + cq_ref[...] - jnp.concatenate([ck] * (tq // LANES), axis=1)
            if masked:
                s_t = jnp.where(mask, s_t, NEG_BIG)
            m_old = m_ref[c]
            m_new = jnp.maximum(m_old, jnp.max(s_t, axis=0, keepdims=True))
            alpha = jnp.exp2(m_old - m_new)
            p_t = jnp.exp2(s_t - m_new)
            m_ref[c] = m_new
            a_ref[c] = alpha
            p_dst[c] = p_t.astype(jnp.bfloat16)

    def step(kb, parity, masked, next_kb, prev_kb):
        if next_kb is not None:
            scores_stage(next_kb, s_bufs[1 - parity])
        values_stage(prev_kb, p_bufs[1 - parity])
        weights_stage(kb, s_bufs[parity], p_bufs[parity], masked)

    if softmax and tq == 2 * tk:
        scores_stage(0, s_bufs[0])

        def body(t, carry):
            kb = 2 * t
            step(kb, 0, False, kb + 1, jnp.maximum(kb - 1, 0))
            step(kb + 1, 1, False, kb + 2, kb)
            return carry
        lax.fori_loop(0, qi, body, 0)
        step(last_kb - 1, 0, True, last_kb, jnp.maximum(last_kb - 2, 0))
        step(last_kb, 1, True, None, last_kb - 1)
        values_stage(last_kb, p_bufs[1])
    elif softmax:
        scores_stage(0, s_bufs[0])

        def body(t, carry):
            kb = 2 * t
            step(kb, 0, False, kb + 1, jnp.maximum(kb - 1, 0))
            step(kb + 1, 1, False, kb + 2, kb)
            return carry
        lax.fori_loop(0, qi // 2, body, 0)

        @pl.when(qi % 2 == 0)
        def _():
            step(qi, 0, True, None, jnp.maximum(qi - 1, 0))
            values_stage(qi, p_bufs[0])

        @pl.when(qi % 2 == 1)
        def _():
            step(qi - 1, 0, False, qi, jnp.maximum(qi - 2, 0))
            step(qi, 1, True, None, qi - 1)
            values_stage(qi, p_bufs[1])
    else:
        scores_stage(last_kb, s_bufs[0])
        step(last_kb, 0, True, last_kb - 1, last_kb)
        step(last_kb - 1, 1, True, jnp.maximum(last_kb - 2, 0), last_kb)

        def live(carry):
            t, _, r_max = carry
            return (t < qi) & (r_max >= SB_EXP_UNDERFLOW)

        def body(carry):
            t = carry[0]
            kb = last_kb - 2 - 2 * t
            step(kb, 0, False, kb - 1, kb + 1)
            step(kb - 1, 1, False, jnp.maximum(kb - 2, 0), kb)
            return t + 1, kb - 1, jnp.max(r_ref[...])
        done = lax.while_loop(live, body, (0, last_kb - 1, jnp.max(r_ref[...])))
        values_stage(done[1], p_bufs[1])

    def normalised(c):
        return acc_ref[c, 0:HEAD_DIM, :] / acc_ref[c, HEAD_DIM:HEAD_DIM + 1, :]

    if kind == "sb":
        o_t = acc_ref[0]
    elif kind == "diff":
        o_t = normalised(0) - lam_ref[0] * normalised(1)
    else:
        o_t = normalised(0)
    o = o_t.T
    if kind == "diff":
        o = (o * lax.rsqrt(jnp.mean(o * o, axis=-1, keepdims=True) + RMS_EPS)
             * g_ref[...] * (1.0 - lam_init))
    o_ref[...] = o.astype(o_ref.dtype)


def _attention(kind, batch, seq, q_arr, q_col, k_arr, k_col, v_arr, v_col, dq,
               extras=(), lam=None, lam_init=0.0):
    t = q_arr.shape[0]
    tq = min(ATT_TQ, seq)
    tk = tq if kind in ("diff", "mla") else tq // 2
    nq = seq // tq
    q_spec = pl.BlockSpec((tq, dq), lambda b, h, i, *_: (b * nq + i, q_col + h))
    k_spec = pl.BlockSpec((seq, dq), lambda b, h, i, *_: (b, k_col + h))
    v_spec = pl.BlockSpec((seq, HEAD_DIM), lambda b, h, i, *_: (b, v_col + h))
    in_specs = [q_spec, k_spec, v_spec]
    args = [q_arr, k_arr, v_arr]
    n_comp = 2 if kind == "diff" else 1
    if kind == "diff":
        in_specs.append(_resident((1, HEAD_DIM), lambda b, h, i, *_: (0, 0)))
        args.append(extras[0])
    elif kind == "fox":
        cq_rows, ck_rep = extras
        in_specs.append(pl.BlockSpec((None, 1, tq), lambda b, h, i: (b * HEADS + h, 0, i)))
        in_specs.append(pl.BlockSpec((None, seq, LANES), lambda b, h, i: (b * HEADS + h, 0, 0)))
        args += [cq_rows, ck_rep]
    s_buf = pltpu.VMEM((n_comp, tk, tq), jnp.float32)
    p_buf = pltpu.VMEM((n_comp, tk, tq), jnp.bfloat16)
    row = pltpu.VMEM((n_comp, 1, tq), jnp.float32)
    nk = seq // tk
    if kind == "sb":
        scratch = [s_buf, s_buf, p_buf, p_buf,
                   pltpu.VMEM((nk, HEAD_DIM, tk), jnp.bfloat16),
                   pltpu.VMEM((1, tq), jnp.float32),
                   pltpu.VMEM((1, HEAD_DIM, tq), jnp.float32)]
    else:
        scratch = [s_buf, s_buf, p_buf, p_buf,
                   pltpu.VMEM((nk, HEAD_DIM + 16, tk), jnp.bfloat16), row, row,
                   pltpu.VMEM((n_comp, HEAD_DIM + 8, tq), jnp.float32)]
    body = functools.partial(_attn_body, kind=kind, tq=tq, tk=tk, lam_init=lam_init)
    grid = (batch, HEADS, nq)
    out_shape = jax.ShapeDtypeStruct((t, BRANCH_WIDTH), jnp.bfloat16)
    out_spec = pl.BlockSpec((tq, HEAD_DIM), lambda b, h, i, *_: (b * nq + i, h))
    cp = _cparams(("parallel", "parallel", "arbitrary"))
    if kind == "diff":
        return pl.pallas_call(
            body, out_shape=out_shape,
            grid_spec=pltpu.PrefetchScalarGridSpec(
                num_scalar_prefetch=1, grid=grid, in_specs=in_specs, out_specs=out_spec,
                scratch_shapes=scratch),
            compiler_params=cp, name="attn_" + kind)(lam, *args)
    return pl.pallas_call(
        body, out_shape=out_shape, grid=grid, in_specs=in_specs, out_specs=out_spec,
        scratch_shapes=scratch, compiler_params=cp, name="attn_" + kind)(*args)


def _layer_norm(y, g, b):
    mu = jnp.mean(y, axis=-1, keepdims=True)
    yc = y - mu
    var = jnp.mean(yc * yc, axis=-1, keepdims=True)
    return yc * lax.rsqrt(var + LN_EPS) * g + b


def _merge_body(g_ref, od_ref, of_ref, om_ref, os_ref, wb_ref, wo_ref, x_ref, lg_ref, lb_ref,
                wrh_ref, wrl_ref, h_o, hr_o, hb_o, lgt_o, *, alpha):
    d = x_ref.shape[1]
    merged = None
    for n, o_ref in enumerate((od_ref, of_ref, om_ref, os_ref)):
        gate = _sigmoid(g_ref[:, n * d:(n + 1) * d].astype(jnp.float32))
        term = gate * _dot(o_ref[...], wb_ref[n])
        merged = term if merged is None else merged + term
    mix = _dot(merged.astype(jnp.bfloat16), wo_ref[...])
    h = _layer_norm(alpha * x_ref[...] + mix, lg_ref[...], lb_ref[...])
    h_o[...] = h
    _to_chunked(hr_o, h)
    h_hi = h.astype(jnp.bfloat16)
    hb_o[...] = h_hi
    h_lo = (h - h_hi.astype(jnp.float32)).astype(jnp.bfloat16)
    lgt_o[...] = (_dot(h_hi, wrh_ref[...]) + _dot(h_lo, wrh_ref[...]) + _dot(h_hi, wrl_ref[...]))


def _merge(u, o_d, o_f, o_m, o_s, w_b, w_o, x, ln_g, ln_b, wr_hi, wr_lo, alpha):
    t, d = x.shape
    tm = min(256, t)
    row = lambda i: (i, 0)
    const2 = lambda i: (0, 0)
    o_spec = pl.BlockSpec((tm, BRANCH_WIDTH), row)
    return pl.pallas_call(
        functools.partial(_merge_body, alpha=alpha),
        out_shape=(jax.ShapeDtypeStruct((t, d), jnp.float32),
                   jax.ShapeDtypeStruct((t * _chunk_pitch(d), LANES), jnp.float32),
                   jax.ShapeDtypeStruct((t, d), jnp.bfloat16),
                   jax.ShapeDtypeStruct((t, LANES), jnp.float32)),
        grid=(t // tm,),
        in_specs=[pl.BlockSpec((tm, N_BRANCHES * d), row),
                  o_spec, o_spec, o_spec, o_spec,
                  _resident(w_b.shape, lambda i: (0, 0, 0)),
                  _resident(w_o.shape, const2),
                  pl.BlockSpec((tm, d), row),
                  _resident((1, d), const2), _resident((1, d), const2),
                  _resident(wr_hi.shape, const2), _resident(wr_lo.shape, const2)],
        out_specs=(pl.BlockSpec((tm, d), row),
                   pl.BlockSpec((tm * _chunk_pitch(d), LANES), row),
                   pl.BlockSpec((tm, d), row), pl.BlockSpec((tm, LANES), row)),
        compiler_params=_cparams(("parallel",)),
        name="merge_ln1",
    )(u, o_d, o_f, o_m, o_s, w_b, w_o, x, ln_g, ln_b, wr_hi, wr_lo)


CHUNK_PAD = 4


def _chunk_pitch(d):
    return d // LANES + CHUNK_PAD


def _to_chunked(ref, x):
    n, d = x.shape
    nc, pitch = d // LANES, _chunk_pitch(d)
    for c in range(nc):
        ref[pl.ds(c, n, stride=pitch), :] = x[:, c * LANES:(c + 1) * LANES]
    for c in range(nc, pitch):
        ref[pl.ds(c, n, stride=pitch), :] = jnp.zeros((n, LANES), x.dtype)


def _from_chunked(ref, n, d):
    pitch = _chunk_pitch(d)
    return jnp.concatenate([ref[pl.ds(c, n, stride=pitch), :] for c in range(d // LANES)],
                           axis=1)


def _gather_rows(idx_ref, base, src_hbm, dst, sem, n, d):
    nc, pitch = d // LANES, _chunk_pitch(d)
    for r in range(n):
        row = pl.multiple_of(idx_ref[base + r] * pitch, CHUNK_PAD)
        pltpu.make_async_copy(src_hbm.at[pl.ds(row, nc), :], dst.at[pl.ds(r * pitch, nc), :],
                              sem).start()


def _wait_rows(src_hbm, dst, sem, n, d):
    rows = n * (d // LANES)
    pltpu.make_async_copy(src_hbm.at[pl.ds(0, rows), :], dst.at[pl.ds(0, rows), :], sem).wait()


def _moe_body(sched_ref, n_used_ref, tok_ref, h_hbm, wg_hbm, wu_hbm, wd_hbm, y_ref,
              xbuf, sem, wg_f, wu_f, wd_f, wsem, wg_b, wu_b, wd_b, *, layer):
    i = pl.program_id(0)
    n_used = n_used_ref[0]
    d = wg_b.shape[0]
    bm = xbuf.shape[1] // _chunk_pitch(d)
    slot = i % 2

    def weight_copies(e, s):
        return (pltpu.make_async_copy(wg_hbm.at[layer, e], wg_f.at[s], wsem.at[s]),
                pltpu.make_async_copy(wu_hbm.at[layer, e], wu_f.at[s], wsem.at[s]),
                pltpu.make_async_copy(wd_hbm.at[layer, e], wd_f.at[s], wsem.at[s]))

    @pl.when((i == 0) & (n_used > 0))
    def _():
        for cp in weight_copies(sched_ref[0, 0], 0):
            cp.start(priority=1)
        _gather_rows(tok_ref, 0, h_hbm, xbuf.at[0], sem.at[0], bm, d)

    @pl.when(i + 1 < n_used)
    def _():
        _gather_rows(tok_ref, (i + 1) * bm, h_hbm, xbuf.at[1 - slot], sem.at[1 - slot], bm, d)

    @pl.when(i < n_used)
    def _():
        @pl.when(sched_ref[1, i] == 1)
        def _():
            ws = sched_ref[2, i]
            for cp in weight_copies(sched_ref[0, i], ws):
                cp.wait()
            wg_b[...] = wg_f[ws].astype(jnp.bfloat16)
            wu_b[...] = wu_f[ws].astype(jnp.bfloat16)
            wd_b[...] = wd_f[ws].astype(jnp.bfloat16)

            @pl.when(sched_ref[3, i] >= 0)
            def _():
                for cp in weight_copies(sched_ref[3, i], 1 - ws):
                    cp.start(priority=1)

        _wait_rows(h_hbm, xbuf.at[slot], sem.at[slot], bm, d)
        x = _from_chunked(xbuf.at[slot], bm, d).astype(jnp.bfloat16)
        a = _dot(x, wg_b[...])
        hid = (a * _sigmoid(a)) * _dot(x, wu_b[...])
        _to_chunked(y_ref, _dot(hid.astype(jnp.bfloat16), wd_b[...]))

    @pl.when(i >= n_used)
    def _():
        y_ref[...] = jnp.zeros_like(y_ref)


def _moe(sched, n_used, row_tok, h_rows, w_g, w_u, w_d, layer):
    n_rows = row_tok.shape[0]
    d, hid = w_g.shape[2], w_g.shape[3]
    pitch = _chunk_pitch(d)
    bm = MOE_ROWS
    any_spec = pl.BlockSpec(memory_space=pl.ANY)
    return pl.pallas_call(
        functools.partial(_moe_body, layer=layer),
        out_shape=jax.ShapeDtypeStruct((n_rows * pitch, LANES), jnp.float32),
        grid_spec=pltpu.PrefetchScalarGridSpec(
            num_scalar_prefetch=3, grid=(n_rows // bm,),
            in_specs=[any_spec, any_spec, any_spec, any_spec],
            out_specs=pl.BlockSpec((bm * pitch, LANES), lambda i, s, n, tok: (i, 0)),
            scratch_shapes=[pltpu.VMEM((2, bm * pitch, LANES), jnp.float32),
                            pltpu.SemaphoreType.DMA((2,)),
                            pltpu.VMEM((2, d, hid), jnp.float32),
                            pltpu.VMEM((2, d, hid), jnp.float32),
                            pltpu.VMEM((2, hid, d), jnp.float32),
                            pltpu.SemaphoreType.DMA((2,)),
                            pltpu.VMEM((d, hid), jnp.bfloat16),
                            pltpu.VMEM((d, hid), jnp.bfloat16),
                            pltpu.VMEM((hid, d), jnp.bfloat16)]),
        compiler_params=_cparams(("arbitrary",)),
        name="moe_experts",
    )(sched, n_used, row_tok, h_rows, w_g, w_u, w_d)


def _final_body(d0_ref, d1_ref, h_ref, hb_ref, y_hbm, gw_ref, p_ref, wg_ref, wp_ref, lg_ref,
                lb_ref, y_o, yb_o, gbuf, sem, *, alpha):
    i = pl.program_id(0)
    n = pl.num_programs(0)
    tm, d = h_ref.shape
    slot = i % 2

    def gather(step, s):
        _gather_rows(d0_ref, step * tm, y_hbm, gbuf.at[s, 0], sem.at[s], tm, d)
        _gather_rows(d1_ref, step * tm, y_hbm, gbuf.at[s, 1], sem.at[s], tm, d)

    @pl.when(i == 0)
    def _():
        gather(0, 0)

    @pl.when(i + 1 < n)
    def _():
        gather(i + 1, 1 - slot)

    gate = _sigmoid(_dot(hb_ref[...], wg_ref[...]))
    proj = _dot(p_ref[...].astype(jnp.bfloat16), wp_ref[...])
    _wait_rows(y_hbm, gbuf.at[slot, 0], sem.at[slot], tm, d)
    _wait_rows(y_hbm, gbuf.at[slot, 1], sem.at[slot], tm, d)
    gw = gw_ref[...]
    ffn = (_from_chunked(gbuf.at[slot, 0], tm, d) * gw[:, 0:1]
           + _from_chunked(gbuf.at[slot, 1], tm, d) * gw[:, 1:2])
    y = _layer_norm(alpha * h_ref[...] + ffn + gate * proj, lg_ref[...], lb_ref[...])
    y_o[...] = y
    yb_o[...] = y.astype(jnp.bfloat16)


def _final(dest0, dest1, h, hb, y_rows, gate_w, p, w_pg, w_pp, ln_g, ln_b, alpha):
    t, d = h.shape
    tm = min(256, t)
    row = lambda i, *_: (i, 0)
    const2 = lambda i, *_: (0, 0)
    return pl.pallas_call(
        functools.partial(_final_body, alpha=alpha),
        out_shape=(jax.ShapeDtypeStruct((t, d), jnp.float32),
                   jax.ShapeDtypeStruct((t, d), jnp.bfloat16)),
        grid_spec=pltpu.PrefetchScalarGridSpec(
            num_scalar_prefetch=2, grid=(t // tm,),
            in_specs=[pl.BlockSpec((tm, d), row), pl.BlockSpec((tm, d), row),
                      pl.BlockSpec(memory_space=pl.ANY),
                      pl.BlockSpec((tm, TOP_K), row), pl.BlockSpec((tm, p.shape[1]), row),
                      _resident(w_pg.shape, const2), _resident(w_pp.shape, const2),
                      _resident((1, d), const2), _resident((1, d), const2)],
            out_specs=(pl.BlockSpec((tm, d), row), pl.BlockSpec((tm, d), row)),
            scratch_shapes=[pltpu.VMEM((2, TOP_K, tm * _chunk_pitch(d), LANES), jnp.float32),
                            pltpu.SemaphoreType.DMA((2,))]),
        compiler_params=_cparams(("arbitrary",)),
        name="final_ln2",
    )(dest0, dest1, h, hb, y_rows, gate_w, p, w_pg, w_pp, ln_g, ln_b)


def _rope_tables(positions):
    pos = positions.reshape(-1).astype(jnp.float32)
    lane = jnp.arange(LANES)

    def tables(rot, period):
        half = rot // 2
        inv_freq = ROPE_THETA ** (-jnp.arange(half, dtype=jnp.float32) / half)
        ang = pos[:, None] * inv_freq
        cos, sin = jnp.cos(ang), jnp.sin(ang)
        lp = lane % period
        idx = lp % half
        in_rot = lp < rot
        c = jnp.where(in_rot[None, :], cos[:, idx], 1.0)
        s_up = jnp.where((lp < half)[None, :], -sin[:, idx], 0.0)
        s_dn = jnp.where(((lp >= half) & in_rot)[None, :], sin[:, idx], 0.0)
        return c, s_up, s_dn

    return tables(DIFF_ROT, DIFF_QK) + tables(MLA_ROPE, LANES)


def _w_in_segments(d):
    widths = [512, 512, 512, 512, 512, 512, HEADS, MLA_Q_LORA, MLA_KV_LORA, MLA_ROPE,
              512, 512, 512, N_BRANCHES * d]
    dsts = [COL_DQ, COL_DQ + 512, COL_DV, COL_FQ, COL_FK, COL_FV, COL_FF, COL_CQ, COL_CKV,
            COL_KR, COL_SQ, COL_SK, COL_SV, COL_GATES]
    segs, src = [], 0
    for wd, dst in zip(widths, dsts):
        segs.append((src, dst, wd))
        src += wd
    return segs


def _w_in_body(w_ref, o_ref, f_ref, *, segs):
    for src, dst, wd in segs:
        o_ref[:, dst:dst + wd] = w_ref[:, src:src + wd].astype(o_ref.dtype)
    rows = o_ref.shape[0]
    o_ref[:, COL_KR + MLA_ROPE:COL_KR + LANES] = jnp.zeros((rows, LANES - MLA_ROPE), o_ref.dtype)
    o_ref[:, COL_FF + HEADS:COL_FF + LANES] = jnp.zeros((rows, LANES - HEADS), o_ref.dtype)
    f_ref[...] = o_ref[:, COL_FF:COL_FF + LANES]


def _reorder_w_in(w_in, layer):
    _, d, n_in = w_in.shape
    tr = 128
    return pl.pallas_call(
        functools.partial(_w_in_body, segs=_w_in_segments(d)),
        out_shape=(jax.ShapeDtypeStruct((d, U_WIDTH), jnp.bfloat16),
                   jax.ShapeDtypeStruct((d, LANES), jnp.bfloat16)),
        grid=(d // tr,),
        in_specs=[pl.BlockSpec((None, tr, n_in), lambda r: (layer, r, 0))],
        out_specs=(pl.BlockSpec((tr, U_WIDTH), lambda r: (r, 0)),
                   pl.BlockSpec((tr, LANES), lambda r: (r, 0))),
        compiler_params=_cparams(("parallel",)),
        name="w_in_relayout",
    )(w_in)


def _route(logits, b_rg, b_re, t):
    def first_max(v, n):
        top = jnp.max(v, axis=-1, keepdims=True)
        ids = jnp.arange(n, dtype=jnp.int32)
        return top, jnp.min(jnp.where(v == top, ids, n), axis=-1, keepdims=True)

    g_logits = logits[:, :N_GROUPS] + b_rg
    g_max, grp = first_max(g_logits, N_GROUPS)
    p_grp = 1.0 / jnp.sum(jnp.exp(g_logits - g_max), axis=-1, keepdims=True)
    e_logits = (logits[:, N_GROUPS:N_GROUPS + N_EXPERTS] + b_re).reshape(
        t, N_GROUPS, EXPERTS_PER_GROUP)
    in_grp = jnp.arange(N_GROUPS, dtype=jnp.int32)[None, :, None] == grp[:, :, None]
    e_in = jnp.sum(jnp.where(in_grp, e_logits, 0.0), axis=1)
    probs = jax.nn.softmax(e_in, axis=-1)
    p1, i1 = first_max(probs, EXPERTS_PER_GROUP)
    rest = jnp.where(jnp.arange(EXPERTS_PER_GROUP, dtype=jnp.int32)[None, :] == i1, -1.0, probs)
    p2, i2 = first_max(rest, EXPERTS_PER_GROUP)
    top_p = jnp.concatenate([p1, p2], axis=-1)
    weights = p_grp * top_p / jnp.sum(top_p, axis=-1, keepdims=True)
    expert_idx = grp * EXPERTS_PER_GROUP + jnp.concatenate([i1, i2], axis=-1)
    return expert_idx.astype(jnp.int32), weights


def _dispatch_plan(expert_idx, t):
    a = t * TOP_K
    flat_e = expert_idx.reshape(a)
    onehot = (flat_e[:, None] == jnp.arange(N_EXPERTS)[None, :]).astype(jnp.int32)
    sizes = jnp.sum(onehot, axis=0)
    padded = (sizes + MOE_ROWS - 1) // MOE_ROWS * MOE_ROWS
    pad_end = jnp.cumsum(padded)
    pad_start = pad_end - padded
    dest = (jnp.sum(onehot * (jnp.cumsum(onehot, axis=0) + pad_start[None, :]), axis=1)
            - 1).astype(jnp.int32)
    n_rows = a + N_EXPERTS * MOE_ROWS
    n_blk = n_rows // MOE_ROWS
    flat_tok = jnp.arange(a, dtype=jnp.int32) // TOP_K
    row_tok = jnp.zeros((n_rows,), jnp.int32).at[dest].set(flat_tok)
    blk_start = jnp.arange(n_blk, dtype=pad_end.dtype) * MOE_ROWS
    blk_e = jnp.minimum(jnp.sum(pad_end[None, :] <= blk_start[:, None], axis=1),
                        N_EXPERTS - 1).astype(jnp.int32)
    n_used = (pad_end[-1] // MOE_ROWS).astype(jnp.int32).reshape(1)
    first = jnp.concatenate([jnp.ones((1,), jnp.int32),
                             (blk_e[1:] != blk_e[:-1]).astype(jnp.int32)])
    parity = (jnp.cumsum(first) - 1) % 2
    experts = jnp.arange(N_EXPERTS, dtype=jnp.int32)
    used = jnp.where(padded > 0, experts, N_EXPERTS)
    later = jnp.flip(lax.cummin(jnp.flip(used)))
    next_used = jnp.concatenate([later[1:], jnp.full((1,), N_EXPERTS, jnp.int32)])
    next_used = jnp.where(next_used < N_EXPERTS, next_used, -1)
    sched = jnp.stack([blk_e, first, parity.astype(jnp.int32),
                       next_used[blk_e].astype(jnp.int32)])
    return dest.reshape(t, TOP_K), row_tok, sched, n_used


def kernel(x, p, positions, w_in, fox_f_bias, diff_lambda, diff_subln, mla_q_norm, mla_kv_norm,
           mla_w_uq, mla_w_ukv, w_branch, w_o, ln1_g, ln1_b, w_router_group, b_router_group,
           w_router_expert, b_router_expert, w_expert_gate, w_expert_up, w_expert_down,
           w_ple_gate, w_ple_proj, ln2_g, ln2_b):
    batch, seq, d = x.shape
    depth = w_in.shape[0]
    t = batch * seq
    alpha = (2 * depth) ** 0.25
    bf = jnp.bfloat16
    tabs = _rope_tables(positions)
    col_scale = jnp.ones((1, U_WIDTH), jnp.float32)
    col_scale = col_scale.at[0, COL_FQ:COL_FQ + BRANCH_WIDTH].set(HEAD_DIM ** -0.5 * LOG2E)
    col_scale = col_scale.at[0, COL_SQ:COL_SQ + BRANCH_WIDTH].set(HEAD_DIM ** -0.5)
    xf = x.reshape(t, d)
    xb = xf.astype(bf)

    for i in range(depth):
        w1, w_f = _reorder_w_in(w_in, i)
        u = _matmul(xb, w1, col_scale, bf, 1024, 1536, "in_proj")

        b_f = jnp.zeros((1, LANES), jnp.float32).at[0, :HEADS].set(fox_f_bias[i])
        cum = _fgate_cumsum(xb, w_f, b_f, batch, seq)[:, :HEADS] * LOG2E
        cum_bh = jnp.transpose(cum.reshape(batch, seq, HEADS), (0, 2, 1)).reshape(
            batch * HEADS, seq)
        cq_rows = cum_bh.reshape(batch * HEADS, 1, seq)
        ck_rep = jnp.broadcast_to(cum_bh[:, :, None], (batch * HEADS, seq, LANES))

        uq = mla_w_uq[i].reshape(MLA_Q_LORA, HEADS, MLA_NOPE + MLA_ROPE)
        uq = jnp.pad(uq, ((0, 0), (0, 0), (0, MLA_QK_PAD - MLA_NOPE - MLA_ROPE)))
        uq = uq.reshape(MLA_Q_LORA, HEADS * MLA_QK_PAD).astype(bf)
        ukv = mla_w_ukv[i].reshape(MLA_KV_LORA, HEADS, MLA_NOPE + HEAD_DIM)
        uk = ukv[:, :, :MLA_NOPE].reshape(MLA_KV_LORA, HEADS * MLA_NOPE).astype(bf)
        uv = ukv[:, :, MLA_NOPE:].reshape(MLA_KV_LORA, HEADS * HEAD_DIM).astype(bf)
        dqk, mq, mk, mv = _prep(u, tabs, mla_q_norm[i].reshape(1, -1),
                                mla_kv_norm[i].reshape(1, -1), uq, uk, uv)

        lam_init = 0.8 - 0.6 * math.exp(-0.3 * i)
        lp = diff_lambda[i].astype(jnp.float32)
        lam = (jnp.exp(jnp.sum(lp[0] * lp[1])) - jnp.exp(jnp.sum(lp[2] * lp[3]))
               + lam_init).reshape(1)
        blk = lambda col: col // HEAD_DIM
        o_d = _attention("diff", batch, seq, dqk, 0, dqk, HEADS, u, blk(COL_DV), HEAD_DIM,
                         extras=(diff_subln[i].reshape(1, HEAD_DIM),),
                         lam=lam, lam_init=lam_init)
        o_f = _attention("fox", batch, seq, u, blk(COL_FQ), u, blk(COL_FK), u, blk(COL_FV),
                         HEAD_DIM, extras=(cq_rows, ck_rep))
        o_m = _attention("mla", batch, seq, mq, 0, mk, 0, mv, 0, MLA_QK_PAD)
        o_s = _attention("sb", batch, seq, u, blk(COL_SQ), u, blk(COL_SK), u, blk(COL_SV),
                         HEAD_DIM)

        w_r = jnp.concatenate([w_router_group[i], w_router_expert[i]], axis=1)
        w_r = jnp.pad(w_r, ((0, 0), (0, LANES - w_r.shape[1])))
        wr_hi = w_r.astype(bf)
        wr_lo = (w_r - wr_hi.astype(jnp.float32)).astype(bf)
        h, h_rows, hb, logits = _merge(
            u, o_d, o_f, o_m, o_s, w_branch[i].astype(bf), w_o[i].astype(bf), xf,
            ln1_g[i].reshape(1, d), ln1_b[i].reshape(1, d), wr_hi, wr_lo, alpha)

        expert_idx, weights = _route(logits, b_router_group[i], b_router_expert[i], t)
        dest, row_tok, sched, n_used = _dispatch_plan(expert_idx, t)
        y_rows = _moe(sched, n_used, row_tok, h_rows, w_expert_gate, w_expert_up,
                      w_expert_down, i)
        xf, xb = _final(dest[:, 0], dest[:, 1], h, hb, y_rows, weights.astype(jnp.float32),
                        p[i].reshape(t, -1), w_ple_gate[i].astype(bf),
                        w_ple_proj[i].astype(bf), ln2_g[i].reshape(1, d),
                        ln2_b[i].reshape(1, d), alpha)
    return xf.reshape(batch, seq, d)
```

```python
import functools
import math

import jax
import jax.numpy as jnp
from jax import lax
from jax.experimental import pallas as pl
from jax.experimental.pallas import tpu as pltpu

HEAD_DIM = 128
HEADS = 4
N_BRANCHES = 4
BRANCH_WIDTH = HEADS * HEAD_DIM
DIFF_QK = 64
DIFF_ROT = DIFF_QK // 4
MLA_Q_LORA = 512
MLA_KV_LORA = 256
MLA_NOPE = 128
MLA_ROPE = 64
MLA_QK_PAD = 256
N_GROUPS = 4
EXPERTS_PER_GROUP = 8
N_EXPERTS = N_GROUPS * EXPERTS_PER_GROUP
TOP_K = 2
CHUNK = 64
ROPE_THETA = 500000.0
LN_EPS = 1e-5
RMS_EPS = 1e-6
NEG_BIG = -1e30
LOG2E = 1.4426950408889634
SB_EXP_UNDERFLOW = -104.0
LANES = 128
V7X_VMEM_LIMIT = 56 * 1024 * 1024

COL_GATES = 0
COL_DQ = 8192
COL_DV = COL_DQ + 1024
COL_FQ = COL_DV + 512
COL_FK = COL_FQ + 512
COL_FV = COL_FK + 512
COL_SQ = COL_FV + 512
COL_SK = COL_SQ + 512
COL_SV = COL_SK + 512
COL_CQ = COL_SV + 512
COL_CKV = COL_CQ + 512
COL_KR = COL_CKV + 256
COL_FF = COL_KR + 128
U_WIDTH = COL_FF + 128

ATT_TQ = 512
MOE_ROWS = 256


def _cparams(sem, vmem=V7X_VMEM_LIMIT):
    return pltpu.CompilerParams(dimension_semantics=sem, vmem_limit_bytes=vmem)


def _resident(shape, index_map):
    return pl.BlockSpec(shape, index_map, pipeline_mode=pl.Buffered(1))


def _matmul_body(x_ref, w_ref, c_ref, o_ref):
    acc = jnp.dot(x_ref[...], w_ref[...], preferred_element_type=jnp.float32)
    o_ref[...] = (acc * c_ref[...]).astype(o_ref.dtype)


def _matmul(x, w, col_scale, out_dtype, tm, tn, name):
    m, k = x.shape
    n = w.shape[1]
    tm, tn = min(tm, m), min(tn, n)
    return pl.pallas_call(
        _matmul_body,
        out_shape=jax.ShapeDtypeStruct((m, n), out_dtype),
        grid=(m // tm, n // tn),
        in_specs=[pl.BlockSpec((tm, k), lambda i, j: (i, 0)),
                  pl.BlockSpec((k, tn), lambda i, j: (0, j)),
                  pl.BlockSpec((1, tn), lambda i, j: (0, j))],
        out_specs=pl.BlockSpec((tm, tn), lambda i, j: (i, j)),
        compiler_params=_cparams(("parallel", "arbitrary")),
        name=name,
    )(x, w, col_scale)


def _split3(x):
    hi = x.astype(jnp.bfloat16)
    r1 = x - hi.astype(jnp.float32)
    mid = r1.astype(jnp.bfloat16)
    lo = (r1 - mid.astype(jnp.float32)).astype(jnp.bfloat16)
    return hi, mid, lo


def _split2(x):
    hi = x.astype(jnp.bfloat16)
    mid = (x - hi.astype(jnp.float32)).astype(jnp.bfloat16)
    return hi, mid


def _dot(a, b):
    return jnp.dot(a, b, preferred_element_type=jnp.float32)


def _dot_nt(a, b):
    return lax.dot_general(a, b, (((1,), (1,)), ((), ())), preferred_element_type=jnp.float32)


def _dot_tn(a, b):
    return lax.dot_general(a, b, (((0,), (0,)), ((), ())), preferred_element_type=jnp.float32)


def _log_sigmoid(z):
    neg_abs = pltpu.bitcast(pltpu.bitcast(z, jnp.uint32) | jnp.uint32(0x80000000), jnp.float32)
    return jnp.minimum(z, 0.0) - jnp.log(1.0 + jnp.exp(neg_abs))


def _sigmoid(z):
    return 1.0 / (1.0 + jnp.exp(-z))


def _fgate_body(x_ref, w_ref, b_ref, o_ref, carry_ref):
    @pl.when(pl.program_id(1) == 0)
    def _():
        carry_ref[...] = jnp.zeros_like(carry_ref)

    f = _dot(x_ref[...], w_ref[...]) + b_ref[...]
    log_f = _log_sigmoid(f)
    ts = log_f.shape[0]
    row = lax.broadcasted_iota(jnp.int32, (ts, ts), 0)
    col = lax.broadcasted_iota(jnp.int32, (ts, ts), 1)
    tri = (col <= row).astype(jnp.bfloat16)
    hi, mid, lo = _split3(log_f)
    cum = _dot(tri, hi) + _dot(tri, mid) + _dot(tri, lo) + carry_ref[...]
    o_ref[...] = cum
    carry_ref[...] = cum[ts - 1:ts, :]


def _fgate_cumsum(xb, w_f, b_f, batch, seq):
    t, d = xb.shape
    ts = min(512, seq)
    ns = seq // ts
    return pl.pallas_call(
        _fgate_body,
        out_shape=jax.ShapeDtypeStruct((t, LANES), jnp.float32),
        grid=(batch, ns),
        in_specs=[pl.BlockSpec((ts, d), lambda b, s: (b * ns + s, 0)),
                  _resident((d, LANES), lambda b, s: (0, 0)),
                  _resident((1, LANES), lambda b, s: (0, 0))],
        out_specs=pl.BlockSpec((ts, LANES), lambda b, s: (b * ns + s, 0)),
        scratch_shapes=[pltpu.VMEM((1, LANES), jnp.float32)],
        compiler_params=_cparams(("parallel", "arbitrary")),
        name="fgate_cumsum",
    )(xb, w_f, b_f)


def _rope_lanes(x, c, s_up, s_dn, half):
    return x * c + pltpu.roll(x, LANES - half, 1) * s_up + pltpu.roll(x, half, 1) * s_dn


def _prep_body(dqk_ref, cq_ref, ckv_ref, kr_ref, c16_ref, su16_ref, sd16_ref,
               c64_ref, su64_ref, sd64_ref, qg_ref, kvg_ref, wuq_ref, wuk_ref, wuv_ref,
               dqk_o, mq_o, mk_o, mv_o):
    c16, su16, sd16 = c16_ref[...], su16_ref[...], sd16_ref[...]
    c64, su64, sd64 = c64_ref[...], su64_ref[...], sd64_ref[...]
    for j in range(dqk_ref.shape[1] // LANES):
        sl = slice(j * LANES, (j + 1) * LANES)
        x = _rope_lanes(dqk_ref[:, sl].astype(jnp.float32), c16, su16, sd16, DIFF_ROT // 2)
        if j < HEADS:
            x = x * (DIFF_QK ** -0.5 * LOG2E)
        dqk_o[:, sl] = x.astype(dqk_o.dtype)

    cq = cq_ref[...].astype(jnp.float32)
    cqn = cq * lax.rsqrt(jnp.mean(cq * cq, axis=-1, keepdims=True) + RMS_EPS) * qg_ref[...]
    q = _dot(cqn.astype(jnp.bfloat16), wuq_ref[...])
    q = q * ((MLA_NOPE + MLA_ROPE) ** -0.5 * LOG2E)
    ckv = ckv_ref[...].astype(jnp.float32)
    ckvn = (ckv * lax.rsqrt(jnp.mean(ckv * ckv, axis=-1, keepdims=True) + RMS_EPS)
            * kvg_ref[...]).astype(jnp.bfloat16)
    kn = _dot(ckvn, wuk_ref[...])
    mv_o[...] = _dot(ckvn, wuv_ref[...]).astype(mv_o.dtype)
    kr = _rope_lanes(kr_ref[...].astype(jnp.float32), c64, su64, sd64, MLA_ROPE // 2)
    for h in range(HEADS):
        base = h * MLA_QK_PAD
        mq_o[:, base:base + LANES] = q[:, base:base + LANES].astype(mq_o.dtype)
        qr = _rope_lanes(q[:, base + LANES:base + 2 * LANES], c64, su64, sd64, MLA_ROPE // 2)
        mq_o[:, base + LANES:base + 2 * LANES] = qr.astype(mq_o.dtype)
        mk_o[:, base:base + LANES] = kn[:, h * LANES:(h + 1) * LANES].astype(mk_o.dtype)
        mk_o[:, base + LANES:base + 2 * LANES] = kr.astype(mk_o.dtype)


def _prep(u, tabs, q_norm, kv_norm, w_uq, w_uk, w_uv):
    t = u.shape[0]
    tm = min(512, t)
    row = lambda blk: (lambda i: (i, blk))
    const = lambda i: (0, 0)
    tab_spec = pl.BlockSpec((tm, LANES), lambda i: (i, 0))
    outs = pl.pallas_call(
        _prep_body,
        out_shape=(jax.ShapeDtypeStruct((t, 1024), jnp.bfloat16),
                   jax.ShapeDtypeStruct((t, HEADS * MLA_QK_PAD), jnp.bfloat16),
                   jax.ShapeDtypeStruct((t, HEADS * MLA_QK_PAD), jnp.bfloat16),
                   jax.ShapeDtypeStruct((t, BRANCH_WIDTH), jnp.bfloat16)),
        grid=(t // tm,),
        in_specs=[pl.BlockSpec((tm, 1024), row(COL_DQ // 1024)),
                  pl.BlockSpec((tm, MLA_Q_LORA), row(COL_CQ // MLA_Q_LORA)),
                  pl.BlockSpec((tm, MLA_KV_LORA), row(COL_CKV // MLA_KV_LORA)),
                  pl.BlockSpec((tm, LANES), row(COL_KR // LANES)),
                  tab_spec, tab_spec, tab_spec, tab_spec, tab_spec, tab_spec,
                  _resident((1, MLA_Q_LORA), const),
                  _resident((1, MLA_KV_LORA), const),
                  _resident(w_uq.shape, const),
                  _resident(w_uk.shape, const),
                  _resident(w_uv.shape, const)],
        out_specs=(pl.BlockSpec((tm, 1024), lambda i: (i, 0)),
                   pl.BlockSpec((tm, HEADS * MLA_QK_PAD), lambda i: (i, 0)),
                   pl.BlockSpec((tm, HEADS * MLA_QK_PAD), lambda i: (i, 0)),
                   pl.BlockSpec((tm, BRANCH_WIDTH), lambda i: (i, 0))),
        compiler_params=_cparams(("parallel",)),
        name="prep",
    )(u, u, u, u, *tabs, q_norm, kv_norm, w_uq, w_uk, w_uv)
    return outs


def _attn_body(*refs, kind, tq, tk, lam_init):
    softmax = kind != "sb"
    if kind == "diff":
        lam_ref, q_ref, k_ref, v_ref, g_ref, o_ref = refs[:6]
    elif kind == "fox":
        q_ref, k_ref, v_ref, cq_ref, ck_ref, o_ref = refs[:6]
    else:
        q_ref, k_ref, v_ref, o_ref = refs[:4]
    if softmax:
        s_bufs, p_bufs = refs[-8:-6], refs[-6:-4]
        vt_ref, m_ref, a_ref, acc_ref = refs[-4:]
    else:
        s_bufs, p_bufs = refs[-7:-5], refs[-5:-3]
        vt_ref, r_ref, acc_ref = refs[-3:]
    qi = pl.program_id(2)

    @pl.when(qi == 0)
    def _():
        for kb in range(vt_ref.shape[0]):
            vt_ref[kb, 0:HEAD_DIM, :] = v_ref[kb * tk:(kb + 1) * tk, :].T
            if softmax:
                vt_ref[kb, HEAD_DIM:, :] = jnp.ones((vt_ref.shape[1] - HEAD_DIM, tk),
                                                    vt_ref.dtype)

    q0 = qi * tq
    last_kb = 2 * qi + 1
    q = q_ref[...]
    n_comp = 2 if kind == "diff" else 1
    if kind == "diff":
        lane = lax.broadcasted_iota(jnp.int32, q.shape, 1)
        qs = [jnp.where(lane < DIFF_QK, q, jnp.zeros_like(q)),
              jnp.where(lane >= DIFF_QK, q, jnp.zeros_like(q))]
    else:
        qs = [q]

    acc_ref[...] = jnp.zeros_like(acc_ref)
    p_bufs[1][...] = jnp.zeros_like(p_bufs[1])
    if softmax:
        m_ref[...] = jnp.full_like(m_ref, NEG_BIG)
        a_ref[...] = jnp.zeros_like(a_ref)
    else:
        r_ref[...] = jnp.zeros_like(r_ref)

    def key_rows(kb):
        if isinstance(kb, int):
            return pl.ds(kb * tk, tk)
        return pl.ds(pl.multiple_of(kb * tk, tk), tk)

    def scores_stage(kb, s_dst):
        k_t = k_ref[key_rows(kb), :]
        for c in range(n_comp):
            s_dst[c] = _dot_nt(k_t, qs[c])

    def values_stage(kb, p_src):
        v_t = vt_ref[kb]
        for c in range(n_comp):
            pv = _dot(v_t, p_src[c])[:acc_ref.shape[1]]
            if softmax:
                acc_ref[c] = a_ref[c] * acc_ref[c] + pv
            else:
                acc_ref[c] = acc_ref[c] + pv

    def weights_stage(kb, s_src, p_dst, masked):
        if masked:
            j = lax.broadcasted_iota(jnp.int32, (tk, tq), 0)
            i = lax.broadcasted_iota(jnp.int32, (tk, tq), 1)
            off = q0 - kb * tk
            if kind in ("diff", "mla"):
                mask = (j // CHUNK - i // CHUNK) <= off // CHUNK
            elif kind == "fox":
                mask = (j - i) <= off
            else:
                mask = (j - i) < off
        if not softmax:
            z_t = s_src[0]
            ls = _log_sigmoid(z_t)
            lk = ls - z_t
            if masked:
                lk = jnp.where(mask, lk, 0.0)
            jj = lax.broadcasted_iota(jnp.int32, (tk, tk), 0)
            mm = lax.broadcasted_iota(jnp.int32, (tk, tk), 1)
            upper = (mm > jj).astype(jnp.bfloat16)
            hi, mid = _split2(lk)
            suffix = _dot(jnp.concatenate([upper, upper], axis=1),
                          jnp.concatenate([hi, mid], axis=0))
            r_old = r_ref[...]
            w_t = jnp.exp(ls + (r_old + suffix))
            if masked:
                w_t = jnp.where(mask, w_t, 0.0)
            p_dst[0] = w_t.astype(jnp.bfloat16)
            r_ref[...] = r_old + suffix[0:1, :] + lk[0:1, :]
            return
        for c in range(n_comp):
            s_t = s_src[c]
            if kind == "fox":
                ck = ck_ref[key_rows(kb), :]
                s_t = s_t + cq_ref[...] - jnp.concatenate([ck] * (tq // LANES), axis=1)
            if masked:
                s_t = jnp.where(mask, s_t, NEG_BIG)
            m_old = m_ref[c]
            m_new = jnp.maximum(m_old, jnp.max(s_t, axis=0, keepdims=True))
            alpha = jnp.exp2(m_old - m_new)
            p_t = jnp.exp2(s_t - m_new)
            m_ref[c] = m_new
            a_ref[c] = alpha
            p_dst[c] = p_t.astype(jnp.bfloat16)

    def step(kb, parity, masked, next_kb, prev_kb):
        if next_kb is not None:
            scores_stage(next_kb, s_bufs[1 - parity])
        values_stage(prev_kb, p_bufs[1 - parity])
        weights_stage(kb, s_bufs[parity], p_bufs[parity], masked)

    if softmax and tq == 2 * tk:
        scores_stage(0, s_bufs[0])

        def body(t, carry):
            kb = 2 * t
            step(kb, 0, False, kb + 1, jnp.maximum(kb - 1, 0))
            step(kb + 1, 1, False, kb + 2, kb)
            return carry
        lax.fori_loop(0, qi, body, 0)
        step(last_kb - 1, 0, True, last_kb, jnp.maximum(last_kb - 2, 0))
        step(last_kb, 1, True, None, last_kb - 1)
        values_stage(last_kb, p_bufs[1])
    elif softmax:
        scores_stage(0, s_bufs[0])

        def body(t, carry):
            kb = 2 * t
            step(kb, 0, False, kb + 1, jnp.maximum(kb - 1, 0))
            step(kb + 1, 1, False, kb + 2, kb)
            return carry
        lax.fori_loop(0, qi // 2, body, 0)

        @pl.when(qi % 2 == 0)
        def _():
            step(qi, 0, True, None, jnp.maximum(qi - 1, 0))
            values_stage(qi, p_bufs[0])

        @pl.when(qi % 2 == 1)
        def _():
            step(qi - 1, 0, False, qi, jnp.maximum(qi - 2, 0))
            step(qi, 1, True, None, qi - 1)
            values_stage(qi, p_bufs[1])
    else:
        scores_stage(last_kb, s_bufs[0])
        step(last_kb, 0, True, last_kb - 1, last_kb)
        step(last_kb - 1, 1, True, jnp.maximum(last_kb - 2, 0), last_kb)

        def live(carry):
            t, _, r_max = carry
            return (t < qi) & (r_max >= SB_EXP_UNDERFLOW)

        def body(carry):
            t = carry[0]
            kb = last_kb - 2 - 2 * t
            step(kb, 0, False, kb - 1, kb + 1)
            step(kb - 1, 1, False, jnp.maximum(kb - 2, 0), kb)
            return t + 1, kb - 1, jnp.max(r_ref[...])
        done = lax.while_loop(live, body, (0, last_kb - 1, jnp.max(r_ref[...])))
        values_stage(done[1], p_bufs[1])

    def normalised(c):
        return acc_ref[c, 0:HEAD_DIM, :] / acc_ref[c, HEAD_DIM:HEAD_DIM + 1, :]

    if kind == "sb":
        o_t = acc_ref[0]
    elif kind == "diff":
        o_t = normalised(0) - lam_ref[0] * normalised(1)
    else:
        o_t = normalised(0)
    o = o_t.T
    if kind == "diff":
        o = (o * lax.rsqrt(jnp.mean(o * o, axis=-1, keepdims=True) + RMS_EPS)
             * g_ref[...] * (1.0 - lam_init))
    o_ref[...] = o.astype(o_ref.dtype)


def _attention(kind, batch, seq, q_arr, q_col, k_arr, k_col, v_arr, v_col, dq,
               extras=(), lam=None, lam_init=0.0):
    t = q_arr.shape[0]
    tq = min(ATT_TQ, seq)
    tk = tq if kind in ("diff", "mla") else tq // 2
    nq = seq // tq
    q_spec = pl.BlockSpec((tq, dq), lambda b, h, i, *_: (b * nq + i, q_col + h))
    k_spec = pl.BlockSpec((seq, dq), lambda b, h, i, *_: (b, k_col + h))
    v_spec = pl.BlockSpec((seq, HEAD_DIM), lambda b, h, i, *_: (b, v_col + h))
    in_specs = [q_spec, k_spec, v_spec]
    args = [q_arr, k_arr, v_arr]
    n_comp = 2 if kind == "diff" else 1
    if kind == "diff":
        in_specs.append(_resident((1, HEAD_DIM), lambda b, h, i, *_: (0, 0)))
        args.append(extras[0])
    elif kind == "fox":
        cq_rows, ck_rep = extras
        in_specs.append(pl.BlockSpec((None, 1, tq), lambda b, h, i: (b * HEADS + h, 0, i)))
        in_specs.append(pl.BlockSpec((None, seq, LANES), lambda b, h, i: (b * HEADS + h, 0, 0)))
        args += [cq_rows, ck_rep]
    s_buf = pltpu.VMEM((n_comp, tk, tq), jnp.float32)
    p_buf = pltpu.VMEM((n_comp, tk, tq), jnp.bfloat16)
    row = pltpu.VMEM((n_comp, 1, tq), jnp.float32)
    nk = seq // tk
    if kind == "sb":
        scratch = [s_buf, s_buf, p_buf, p_buf,
                   pltpu.VMEM((nk, HEAD_DIM, tk), jnp.bfloat16),
                   pltpu.VMEM((1, tq), jnp.float32),
                   pltpu.VMEM((1, HEAD_DIM, tq), jnp.float32)]
    else:
        scratch = [s_buf, s_buf, p_buf, p_buf,
                   pltpu.VMEM((nk, HEAD_DIM + 16, tk), jnp.bfloat16), row, row,
                   pltpu.VMEM((n_comp, HEAD_DIM + 8, tq), jnp.float32)]
    body = functools.partial(_attn_body, kind=kind, tq=tq, tk=tk, lam_init=lam_init)
    grid = (batch, HEADS, nq)
    out_shape = jax.ShapeDtypeStruct((t, BRANCH_WIDTH), jnp.bfloat16)
    out_spec = pl.BlockSpec((tq, HEAD_DIM), lambda b, h, i, *_: (b * nq + i, h))
    cp = _cparams(("parallel", "parallel", "arbitrary"))
    if kind == "diff":
        return pl.pallas_call(
            body, out_shape=out_shape,
            grid_spec=pltpu.PrefetchScalarGridSpec(
                num_scalar_prefetch=1, grid=grid, in_specs=in_specs, out_specs=out_spec,
                scratch_shapes=scratch),
            compiler_params=cp, name="attn_" + kind)(lam, *args)
    return pl.pallas_call(
        body, out_shape=out_shape, grid=grid, in_specs=in_specs, out_specs=out_spec,
        scratch_shapes=scratch, compiler_params=cp, name="attn_" + kind)(*args)


def _layer_norm(y, g, b):
    mu = jnp.mean(y, axis=-1, keepdims=True)
    yc = y - mu
    var = jnp.mean(yc * yc, axis=-1, keepdims=True)
    return yc * lax.rsqrt(var + LN_EPS) * g + b


def _merge_body(g_ref, od_ref, of_ref, om_ref, os_ref, wb_ref, wo_ref, x_ref, lg_ref, lb_ref,
                wrh_ref, wrl_ref, h_o, hr_o, hb_o, lgt_o, *, alpha):
    d = x_ref.shape[1]
    merged = None
    for n, o_ref in enumerate((od_ref, of_ref, om_ref, os_ref)):
        gate = _sigmoid(g_ref[:, n * d:(n + 1) * d].astype(jnp.float32))
        term = gate * _dot(o_ref[...], wb_ref[n])
        merged = term if merged is None else merged + term
    mix = _dot(merged.astype(jnp.bfloat16), wo_ref[...])
    h = _layer_norm(alpha * x_ref[...] + mix, lg_ref[...], lb_ref[...])
    h_o[...] = h
    _to_chunked(hr_o, h)
    h_hi = h.astype(jnp.bfloat16)
    hb_o[...] = h_hi
    h_lo = (h - h_hi.astype(jnp.float32)).astype(jnp.bfloat16)
    lgt_o[...] = (_dot(h_hi, wrh_ref[...]) + _dot(h_lo, wrh_ref[...]) + _dot(h_hi, wrl_ref[...]))


def _merge(u, o_d, o_f, o_m, o_s, w_b, w_o, x, ln_g, ln_b, wr_hi, wr_lo, alpha):
    t, d = x.shape
    tm = min(256, t)
    row = lambda i: (i, 0)
    const2 = lambda i: (0, 0)
    o_spec = pl.BlockSpec((tm, BRANCH_WIDTH), row)
    return pl.pallas_call(
        functools.partial(_merge_body, alpha=alpha),
        out_shape=(jax.ShapeDtypeStruct((t, d), jnp.float32),
                   jax.ShapeDtypeStruct((t * _chunk_pitch(d), LANES), jnp.float32),
                   jax.ShapeDtypeStruct((t, d), jnp.bfloat16),
                   jax.ShapeDtypeStruct((t, LANES), jnp.float32)),
        grid=(t // tm,),
        in_specs=[pl.BlockSpec((tm, N_BRANCHES * d), row),
                  o_spec, o_spec, o_spec, o_spec,
                  _resident(w_b.shape, lambda i: (0, 0, 0)),
                  _resident(w_o.shape, const2),
                  pl.BlockSpec((tm, d), row),
                  _resident((1, d), const2), _resident((1, d), const2),
                  _resident(wr_hi.shape, const2), _resident(wr_lo.shape, const2)],
        out_specs=(pl.BlockSpec((tm, d), row),
                   pl.BlockSpec((tm * _chunk_pitch(d), LANES), row),
                   pl.BlockSpec((tm, d), row), pl.BlockSpec((tm, LANES), row)),
        compiler_params=_cparams(("parallel",)),
        name="merge_ln1",
    )(u, o_d, o_f, o_m, o_s, w_b, w_o, x, ln_g, ln_b, wr_hi, wr_lo)


CHUNK_PAD = 2


def _chunk_pitch(d):
    return d // LANES + CHUNK_PAD


def _to_chunked(ref, x):
    n, d = x.shape
    nc, pitch = d // LANES, _chunk_pitch(d)
    for c in range(nc):
        ref[pl.ds(c, n, stride=pitch), :] = x[:, c * LANES:(c + 1) * LANES]
    for c in range(nc, pitch):
        ref[pl.ds(c, n, stride=pitch), :] = jnp.zeros((n, LANES), x.dtype)


def _from_chunked(ref, n, d):
    pitch = _chunk_pitch(d)
    return jnp.concatenate([ref[pl.ds(c, n, stride=pitch), :] for c in range(d // LANES)],
                           axis=1)


def _gather_rows(idx_ref, base, src_hbm, dst, sem, n, d):
    nc, pitch = d // LANES, _chunk_pitch(d)
    for r in range(n):
        row = pl.multiple_of(idx_ref[base + r] * pitch, CHUNK_PAD)
        pltpu.make_async_copy(src_hbm.at[pl.ds(row, nc), :], dst.at[pl.ds(r * pitch, nc), :],
                              sem).start(priority=r % 2)


def _wait_rows(src_hbm, dst, sem, n, d):
    rows = n * (d // LANES)
    pltpu.make_async_copy(src_hbm.at[pl.ds(0, rows), :], dst.at[pl.ds(0, rows), :], sem).wait()


def _moe_body(sched_ref, n_used_ref, tok_ref, h_hbm, wg_hbm, wu_hbm, wd_hbm, y_ref,
              xbuf, sem, wg_f, wu_f, wd_f, wsem, wg_b, wu_b, wd_b, *, layer):
    i = pl.program_id(0)
    n_used = n_used_ref[0]
    d = wg_b.shape[0]
    bm = xbuf.shape[1] // _chunk_pitch(d)
    slot = i % 2

    def weight_copies(e, s):
        return (pltpu.make_async_copy(wg_hbm.at[layer, e], wg_f.at[s], wsem.at[s]),
                pltpu.make_async_copy(wu_hbm.at[layer, e], wu_f.at[s], wsem.at[s]),
                pltpu.make_async_copy(wd_hbm.at[layer, e], wd_f.at[s], wsem.at[s]))

    @pl.when((i == 0) & (n_used > 0))
    def _():
        for cp in weight_copies(sched_ref[0, 0], 0):
            cp.start(priority=1)
        _gather_rows(tok_ref, 0, h_hbm, xbuf.at[0], sem.at[0], bm, d)

    @pl.when(i + 1 < n_used)
    def _():
        _gather_rows(tok_ref, (i + 1) * bm, h_hbm, xbuf.at[1 - slot], sem.at[1 - slot], bm, d)

    @pl.when(i < n_used)
    def _():
        @pl.when(sched_ref[1, i] == 1)
        def _():
            ws = sched_ref[2, i]
            for cp in weight_copies(sched_ref[0, i], ws):
                cp.wait()
            wg_b[...] = wg_f[ws].astype(jnp.bfloat16)
            wu_b[...] = wu_f[ws].astype(jnp.bfloat16)
            wd_b[...] = wd_f[ws].astype(jnp.bfloat16)

            @pl.when(sched_ref[3, i] >= 0)
            def _():
                for cp in weight_copies(sched_ref[3, i], 1 - ws):
                    cp.start(priority=1)

        _wait_rows(h_hbm, xbuf.at[slot], sem.at[slot], bm, d)
        x = _from_chunked(xbuf.at[slot], bm, d).astype(jnp.bfloat16)
        a = _dot(x, wg_b[...])
        hid = (a * _sigmoid(a)) * _dot(x, wu_b[...])
        _to_chunked(y_ref, _dot(hid.astype(jnp.bfloat16), wd_b[...]))

    @pl.when(i >= n_used)
    def _():
        y_ref[...] = jnp.zeros_like(y_ref)


def _moe(sched, n_used, row_tok, h_rows, w_g, w_u, w_d, layer):
    n_rows = row_tok.shape[0]
    d, hid = w_g.shape[2], w_g.shape[3]
    pitch = _chunk_pitch(d)
    bm = MOE_ROWS
    any_spec = pl.BlockSpec(memory_space=pl.ANY)
    return pl.pallas_call(
        functools.partial(_moe_body, layer=layer),
        out_shape=jax.ShapeDtypeStruct((n_rows * pitch, LANES), jnp.float32),
        grid_spec=pltpu.PrefetchScalarGridSpec(
            num_scalar_prefetch=3, grid=(n_rows // bm,),
            in_specs=[any_spec, any_spec, any_spec, any_spec],
            out_specs=pl.BlockSpec((bm * pitch, LANES), lambda i, s, n, tok: (i, 0)),
            scratch_shapes=[pltpu.VMEM((2, bm * pitch, LANES), jnp.float32),
                            pltpu.SemaphoreType.DMA((2,)),
                            pltpu.VMEM((2, d, hid), jnp.float32),
                            pltpu.VMEM((2, d, hid), jnp.float32),
                            pltpu.VMEM((2, hid, d), jnp.float32),
                            pltpu.SemaphoreType.DMA((2,)),
                            pltpu.VMEM((d, hid), jnp.bfloat16),
                            pltpu.VMEM((d, hid), jnp.bfloat16),
                            pltpu.VMEM((hid, d), jnp.bfloat16)]),
        compiler_params=_cparams(("arbitrary",)),
        name="moe_experts",
    )(sched, n_used, row_tok, h_rows, w_g, w_u, w_d)


def _final_body(d0_ref, d1_ref, h_ref, hb_ref, y_hbm, gw_ref, p_ref, wg_ref, wp_ref, lg_ref,
                lb_ref, y_o, yb_o, gbuf, sem, *, alpha):
    i = pl.program_id(0)
    n = pl.num_programs(0)
    tm, d = h_ref.shape
    slot = i % 2

    def gather(step, s):
        _gather_rows(d0_ref, step * tm, y_hbm, gbuf.at[s, 0], sem.at[s], tm, d)
        _gather_rows(d1_ref, step * tm, y_hbm, gbuf.at[s, 1], sem.at[s], tm, d)

    @pl.when(i == 0)
    def _():
        gather(0, 0)

    @pl.when(i + 1 < n)
    def _():
        gather(i + 1, 1 - slot)

    gate = _sigmoid(_dot(hb_ref[...], wg_ref[...]))
    proj = _dot(p_ref[...].astype(jnp.bfloat16), wp_ref[...])
    _wait_rows(y_hbm, gbuf.at[slot, 0], sem.at[slot], tm, d)
    _wait_rows(y_hbm, gbuf.at[slot, 1], sem.at[slot], tm, d)
    gw = gw_ref[...]
    ffn = (_from_chunked(gbuf.at[slot, 0], tm, d) * gw[:, 0:1]
           + _from_chunked(gbuf.at[slot, 1], tm, d) * gw[:, 1:2])
    y = _layer_norm(alpha * h_ref[...] + ffn + gate * proj, lg_ref[...], lb_ref[...])
    y_o[...] = y
    yb_o[...] = y.astype(jnp.bfloat16)


def _final(dest0, dest1, h, hb, y_rows, gate_w, p, w_pg, w_pp, ln_g, ln_b, alpha):
    t, d = h.shape
    tm = min(256, t)
    row = lambda i, *_: (i, 0)
    const2 = lambda i, *_: (0, 0)
    return pl.pallas_call(
        functools.partial(_final_body, alpha=alpha),
        out_shape=(jax.ShapeDtypeStruct((t, d), jnp.float32),
                   jax.ShapeDtypeStruct((t, d), jnp.bfloat16)),
        grid_spec=pltpu.PrefetchScalarGridSpec(
            num_scalar_prefetch=2, grid=(t // tm,),
            in_specs=[pl.BlockSpec((tm, d), row), pl.BlockSpec((tm, d), row),
                      pl.BlockSpec(memory_space=pl.ANY),
                      pl.BlockSpec((tm, TOP_K), row), pl.BlockSpec((tm, p.shape[1]), row),
                      _resident(w_pg.shape, const2), _resident(w_pp.shape, const2),
                      _resident((1, d), const2), _resident((1, d), const2)],
            out_specs=(pl.BlockSpec((tm, d), row), pl.BlockSpec((tm, d), row)),
            scratch_shapes=[pltpu.VMEM((2, TOP_K, tm * _chunk_pitch(d), LANES), jnp.float32),
                            pltpu.SemaphoreType.DMA((2,))]),
        compiler_params=_cparams(("arbitrary",)),
        name="final_ln2",
    )(dest0, dest1, h, hb, y_rows, gate_w, p, w_pg, w_pp, ln_g, ln_b)


def _rope_tables(positions):
    pos = positions.reshape(-1).astype(jnp.float32)
    lane = jnp.arange(LANES)

    def tables(rot, period):
        half = rot // 2
        inv_freq = ROPE_THETA ** (-jnp.arange(half, dtype=jnp.float32) / half)
        ang = pos[:, None] * inv_freq
        cos, sin = jnp.cos(ang), jnp.sin(ang)
        lp = lane % period
        idx = lp % half
        in_rot = lp < rot
        c = jnp.where(in_rot[None, :], cos[:, idx], 1.0)
        s_up = jnp.where((lp < half)[None, :], -sin[:, idx], 0.0)
        s_dn = jnp.where(((lp >= half) & in_rot)[None, :], sin[:, idx], 0.0)
        return c, s_up, s_dn

    return tables(DIFF_ROT, DIFF_QK) + tables(MLA_ROPE, LANES)


def _w_in_segments(d):
    widths = [512, 512, 512, 512, 512, 512, HEADS, MLA_Q_LORA, MLA_KV_LORA, MLA_ROPE,
              512, 512, 512, N_BRANCHES * d]
    dsts = [COL_DQ, COL_DQ + 512, COL_DV, COL_FQ, COL_FK, COL_FV, COL_FF, COL_CQ, COL_CKV,
            COL_KR, COL_SQ, COL_SK, COL_SV, COL_GATES]
    segs, src = [], 0
    for wd, dst in zip(widths, dsts):
        segs.append((src, dst, wd))
        src += wd
    return segs


def _w_in_body(w_ref, o_ref, f_ref, *, segs):
    for src, dst, wd in segs:
        o_ref[:, dst:dst + wd] = w_ref[:, src:src + wd].astype(o_ref.dtype)
    rows = o_ref.shape[0]
    o_ref[:, COL_KR + MLA_ROPE:COL_KR + LANES] = jnp.zeros((rows, LANES - MLA_ROPE), o_ref.dtype)
    o_ref[:, COL_FF + HEADS:COL_FF + LANES] = jnp.zeros((rows, LANES - HEADS), o_ref.dtype)
    f_ref[...] = o_ref[:, COL_FF:COL_FF + LANES]


def _reorder_w_in(w_in, layer):
    _, d, n_in = w_in.shape
    tr = 128
    return pl.pallas_call(
        functools.partial(_w_in_body, segs=_w_in_segments(d)),
        out_shape=(jax.ShapeDtypeStruct((d, U_WIDTH), jnp.bfloat16),
                   jax.ShapeDtypeStruct((d, LANES), jnp.bfloat16)),
        grid=(d // tr,),
        in_specs=[pl.BlockSpec((None, tr, n_in), lambda r: (layer, r, 0))],
        out_specs=(pl.BlockSpec((tr, U_WIDTH), lambda r: (r, 0)),
                   pl.BlockSpec((tr, LANES), lambda r: (r, 0))),
        compiler_params=_cparams(("parallel",)),
        name="w_in_relayout",
    )(w_in)


def _route(logits, b_rg, b_re, t):
    def first_max(v, n):
        top = jnp.max(v, axis=-1, keepdims=True)
        ids = jnp.arange(n, dtype=jnp.int32)
        return top, jnp.min(jnp.where(v == top, ids, n), axis=-1, keepdims=True)

    g_logits = logits[:, :N_GROUPS] + b_rg
    g_max, grp = first_max(g_logits, N_GROUPS)
    p_grp = 1.0 / jnp.sum(jnp.exp(g_logits - g_max), axis=-1, keepdims=True)
    e_logits = (logits[:, N_GROUPS:N_GROUPS + N_EXPERTS] + b_re).reshape(
        t, N_GROUPS, EXPERTS_PER_GROUP)
    in_grp = jnp.arange(N_GROUPS, dtype=jnp.int32)[None, :, None] == grp[:, :, None]
    e_in = jnp.sum(jnp.where(in_grp, e_logits, 0.0), axis=1)
    probs = jax.nn.softmax(e_in, axis=-1)
    p1, i1 = first_max(probs, EXPERTS_PER_GROUP)
    rest = jnp.where(jnp.arange(EXPERTS_PER_GROUP, dtype=jnp.int32)[None, :] == i1, -1.0, probs)
    p2, i2 = first_max(rest, EXPERTS_PER_GROUP)
    top_p = jnp.concatenate([p1, p2], axis=-1)
    weights = p_grp * top_p / jnp.sum(top_p, axis=-1, keepdims=True)
    expert_idx = grp * EXPERTS_PER_GROUP + jnp.concatenate([i1, i2], axis=-1)
    return expert_idx.astype(jnp.int32), weights


def _dispatch_plan(expert_idx, t):
    a = t * TOP_K
    flat_e = expert_idx.reshape(a)
    onehot = (flat_e[:, None] == jnp.arange(N_EXPERTS)[None, :]).astype(jnp.int32)
    sizes = jnp.sum(onehot, axis=0)
    padded = (sizes + MOE_ROWS - 1) // MOE_ROWS * MOE_ROWS
    pad_end = jnp.cumsum(padded)
    pad_start = pad_end - padded
    dest = (jnp.sum(onehot * (jnp.cumsum(onehot, axis=0) + pad_start[None, :]), axis=1)
            - 1).astype(jnp.int32)
    n_rows = a + N_EXPERTS * MOE_ROWS
    n_blk = n_rows // MOE_ROWS
    flat_tok = jnp.arange(a, dtype=jnp.int32) // TOP_K
    row_tok = jnp.zeros((n_rows,), jnp.int32).at[dest].set(flat_tok)
    blk_start = jnp.arange(n_blk, dtype=pad_end.dtype) * MOE_ROWS
    blk_e = jnp.minimum(jnp.sum(pad_end[None, :] <= blk_start[:, None], axis=1),
                        N_EXPERTS - 1).astype(jnp.int32)
    n_used = (pad_end[-1] // MOE_ROWS).astype(jnp.int32).reshape(1)
    first = jnp.concatenate([jnp.ones((1,), jnp.int32),
                             (blk_e[1:] != blk_e[:-1]).astype(jnp.int32)])
    parity = (jnp.cumsum(first) - 1) % 2
    experts = jnp.arange(N_EXPERTS, dtype=jnp.int32)
    used = jnp.where(padded > 0, experts, N_EXPERTS)
    later = jnp.flip(lax.cummin(jnp.flip(used)))
    next_used = jnp.concatenate([later[1:], jnp.full((1,), N_EXPERTS, jnp.int32)])
    next_used = jnp.where(next_used < N_EXPERTS, next_used, -1)
    sched = jnp.stack([blk_e, first, parity.astype(jnp.int32),
                       next_used[blk_e].astype(jnp.int32)])
    return dest.reshape(t, TOP_K), row_tok, sched, n_used


def kernel(x, p, positions, w_in, fox_f_bias, diff_lambda, diff_subln, mla_q_norm, mla_kv_norm,
           mla_w_uq, mla_w_ukv, w_branch, w_o, ln1_g, ln1_b, w_router_group, b_router_group,
           w_router_expert, b_router_expert, w_expert_gate, w_expert_up, w_expert_down,
           w_ple_gate, w_ple_proj, ln2_g, ln2_b):
    batch, seq, d = x.shape
    depth = w_in.shape[0]
    t = batch * seq
    alpha = (2 * depth) ** 0.25
    bf = jnp.bfloat16
    tabs = _rope_tables(positions)
    col_scale = jnp.ones((1, U_WIDTH), jnp.float32)
    col_scale = col_scale.at[0, COL_FQ:COL_FQ + BRANCH_WIDTH].set(HEAD_DIM ** -0.5 * LOG2E)
    col_scale = col_scale.at[0, COL_SQ:COL_SQ + BRANCH_WIDTH].set(HEAD_DIM ** -0.5)
    xf = x.reshape(t, d)
    xb = xf.astype(bf)

    for i in range(depth):
        w1, w_f = _reorder_w_in(w_in, i)
        u = _matmul(xb, w1, col_scale, bf, 1024, 1536, "in_proj")

        b_f = jnp.zeros((1, LANES), jnp.float32).at[0, :HEADS].set(fox_f_bias[i])
        cum = _fgate_cumsum(xb, w_f, b_f, batch, seq)[:, :HEADS] * LOG2E
        cum_bh = jnp.transpose(cum.reshape(batch, seq, HEADS), (0, 2, 1)).reshape(
            batch * HEADS, seq)
        cq_rows = cum_bh.reshape(batch * HEADS, 1, seq)
        ck_rep = jnp.broadcast_to(cum_bh[:, :, None], (batch * HEADS, seq, LANES))

        uq = mla_w_uq[i].reshape(MLA_Q_LORA, HEADS, MLA_NOPE + MLA_ROPE)
        uq = jnp.pad(uq, ((0, 0), (0, 0), (0, MLA_QK_PAD - MLA_NOPE - MLA_ROPE)))
        uq = uq.reshape(MLA_Q_LORA, HEADS * MLA_QK_PAD).astype(bf)
        ukv = mla_w_ukv[i].reshape(MLA_KV_LORA, HEADS, MLA_NOPE + HEAD_DIM)
        uk = ukv[:, :, :MLA_NOPE].reshape(MLA_KV_LORA, HEADS * MLA_NOPE).astype(bf)
        uv = ukv[:, :, MLA_NOPE:].reshape(MLA_KV_LORA, HEADS * HEAD_DIM).astype(bf)
        dqk, mq, mk, mv = _prep(u, tabs, mla_q_norm[i].reshape(1, -1),
                                mla_kv_norm[i].reshape(1, -1), uq, uk, uv)

        lam_init = 0.8 - 0.6 * math.exp(-0.3 * i)
        lp = diff_lambda[i].astype(jnp.float32)
        lam = (jnp.exp(jnp.sum(lp[0] * lp[1])) - jnp.exp(jnp.sum(lp[2] * lp[3]))
               + lam_init).reshape(1)
        blk = lambda col: col // HEAD_DIM
        o_d = _attention("diff", batch, seq, dqk, 0, dqk, HEADS, u, blk(COL_DV), HEAD_DIM,
                         extras=(diff_subln[i].reshape(1, HEAD_DIM),),
                         lam=lam, lam_init=lam_init)
        o_f = _attention("fox", batch, seq, u, blk(COL_FQ), u, blk(COL_FK), u, blk(COL_FV),
                         HEAD_DIM, extras=(cq_rows, ck_rep))
        o_m = _attention("mla", batch, seq, mq, 0, mk, 0, mv, 0, MLA_QK_PAD)
        o_s = _attention("sb", batch, seq, u, blk(COL_SQ), u, blk(COL_SK), u, blk(COL_SV),
                         HEAD_DIM)

        w_r = jnp.concatenate([w_router_group[i], w_router_expert[i]], axis=1)
        w_r = jnp.pad(w_r, ((0, 0), (0, LANES - w_r.shape[1])))
        wr_hi = w_r.astype(bf)
        wr_lo = (w_r - wr_hi.astype(jnp.float32)).astype(bf)
        h, h_rows, hb, logits = _merge(
            u, o_d, o_f, o_m, o_s, w_branch[i].astype(bf), w_o[i].astype(bf), xf,
            ln1_g[i].reshape(1, d), ln1_b[i].reshape(1, d), wr_hi, wr_lo, alpha)

        expert_idx, weights = _route(logits, b_router_group[i], b_router_expert[i], t)
        dest, row_tok, sched, n_used = _dispatch_plan(expert_idx, t)
        y_rows = _moe(sched, n_used, row_tok, h_rows, w_expert_gate, w_expert_up,
                      w_expert_down, i)
        xf, xb = _final(dest[:, 0], dest[:, 1], h, hb, y_rows, weights.astype(jnp.float32),
                        p[i].reshape(t, -1), w_ple_gate[i].astype(bf),
                        w_ple_proj[i].astype(bf), ln2_g[i].reshape(1, d),
                        ln2_b[i].reshape(1, d), alpha)
    return xf.reshape(batch, seq, d)
```

```python
import functools
import math

import jax
import jax.numpy as jnp
from jax import lax
from jax.experimental import pallas as pl
from jax.experimental.pallas import tpu as pltpu

HEAD_DIM = 128
HEADS = 4
N_BRANCHES = 4
BRANCH_WIDTH = HEADS * HEAD_DIM
DIFF_QK = 64
DIFF_ROT = DIFF_QK // 4
MLA_Q_LORA = 512
MLA_KV_LORA = 256
MLA_NOPE = 128
MLA_ROPE = 64
MLA_QK_PAD = 256
N_GROUPS = 4
EXPERTS_PER_GROUP = 8
N_EXPERTS = N_GROUPS * EXPERTS_PER_GROUP
TOP_K = 2
CHUNK = 64
ROPE_THETA = 500000.0
LN_EPS = 1e-5
RMS_EPS = 1e-6
NEG_BIG = -1e30
LOG2E = 1.4426950408889634
SB_EXP_UNDERFLOW = -104.0
LANES = 128
V7X_VMEM_LIMIT = 56 * 1024 * 1024

COL_GATES = 0
COL_DQ = 8192
COL_DV = COL_DQ + 1024
COL_FQ = COL_DV + 512
COL_FK = COL_FQ + 512
COL_FV = COL_FK + 512
COL_SQ = COL_FV + 512
COL_SK = COL_SQ + 512
COL_SV = COL_SK + 512
COL_CQ = COL_SV + 512
COL_CKV = COL_CQ + 512
COL_KR = COL_CKV + 256
COL_FF = COL_KR + 128
U_WIDTH = COL_FF + 128

ATT_TQ = 512
MOE_ROWS = 256


def _cparams(sem, vmem=V7X_VMEM_LIMIT):
    return pltpu.CompilerParams(dimension_semantics=sem, vmem_limit_bytes=vmem)


def _resident(shape, index_map):
    return pl.BlockSpec(shape, index_map, pipeline_mode=pl.Buffered(1))


def _matmul_body(x_ref, w_ref, c_ref, o_ref):
    acc = jnp.dot(x_ref[...], w_ref[...], preferred_element_type=jnp.float32)
    o_ref[...] = (acc * c_ref[...]).astype(o_ref.dtype)


def _matmul(x, w, col_scale, out_dtype, tm, tn, name):
    m, k = x.shape
    n = w.shape[1]
    tm, tn = min(tm, m), min(tn, n)
    return pl.pallas_call(
        _matmul_body,
        out_shape=jax.ShapeDtypeStruct((m, n), out_dtype),
        grid=(m // tm, n // tn),
        in_specs=[pl.BlockSpec((tm, k), lambda i, j: (i, 0)),
                  pl.BlockSpec((k, tn), lambda i, j: (0, j)),
                  pl.BlockSpec((1, tn), lambda i, j: (0, j))],
        out_specs=pl.BlockSpec((tm, tn), lambda i, j: (i, j)),
        compiler_params=_cparams(("parallel", "arbitrary")),
        name=name,
    )(x, w, col_scale)


def _split3(x):
    hi = x.astype(jnp.bfloat16)
    r1 = x - hi.astype(jnp.float32)
    mid = r1.astype(jnp.bfloat16)
    lo = (r1 - mid.astype(jnp.float32)).astype(jnp.bfloat16)
    return hi, mid, lo


def _split2(x):
    hi = x.astype(jnp.bfloat16)
    mid = (x - hi.astype(jnp.float32)).astype(jnp.bfloat16)
    return hi, mid


def _dot(a, b):
    return jnp.dot(a, b, preferred_element_type=jnp.float32)


def _dot_nt(a, b):
    return lax.dot_general(a, b, (((1,), (1,)), ((), ())), preferred_element_type=jnp.float32)


def _dot_tn(a, b):
    return lax.dot_general(a, b, (((0,), (0,)), ((), ())), preferred_element_type=jnp.float32)


def _log_sigmoid(z):
    neg_abs = pltpu.bitcast(pltpu.bitcast(z, jnp.uint32) | jnp.uint32(0x80000000), jnp.float32)
    return jnp.minimum(z, 0.0) - jnp.log(1.0 + jnp.exp(neg_abs))


def _sigmoid(z):
    return 1.0 / (1.0 + jnp.exp(-z))


def _fgate_body(x_ref, w_ref, b_ref, o_ref, carry_ref):
    @pl.when(pl.program_id(1) == 0)
    def _():
        carry_ref[...] = jnp.zeros_like(carry_ref)

    f = _dot(x_ref[...], w_ref[...]) + b_ref[...]
    log_f = _log_sigmoid(f)
    ts = log_f.shape[0]
    row = lax.broadcasted_iota(jnp.int32, (ts, ts), 0)
    col = lax.broadcasted_iota(jnp.int32, (ts, ts), 1)
    tri = (col <= row).astype(jnp.bfloat16)
    hi, mid, lo = _split3(log_f)
    cum = _dot(tri, hi) + _dot(tri, mid) + _dot(tri, lo) + carry_ref[...]
    o_ref[...] = cum
    carry_ref[...] = cum[ts - 1:ts, :]


def _fgate_cumsum(xb, w_f, b_f, batch, seq):
    t, d = xb.shape
    ts = min(512, seq)
    ns = seq // ts
    return pl.pallas_call(
        _fgate_body,
        out_shape=jax.ShapeDtypeStruct((t, LANES), jnp.float32),
        grid=(batch, ns),
        in_specs=[pl.BlockSpec((ts, d), lambda b, s: (b * ns + s, 0)),
                  _resident((d, LANES), lambda b, s: (0, 0)),
                  _resident((1, LANES), lambda b, s: (0, 0))],
        out_specs=pl.BlockSpec((ts, LANES), lambda b, s: (b * ns + s, 0)),
        scratch_shapes=[pltpu.VMEM((1, LANES), jnp.float32)],
        compiler_params=_cparams(("parallel", "arbitrary")),
        name="fgate_cumsum",
    )(xb, w_f, b_f)


def _rope_lanes(x, c, s_up, s_dn, half):
    return x * c + pltpu.roll(x, LANES - half, 1) * s_up + pltpu.roll(x, half, 1) * s_dn


def _prep_body(dqk_ref, cq_ref, ckv_ref, kr_ref, c16_ref, su16_ref, sd16_ref,
               c64_ref, su64_ref, sd64_ref, qg_ref, kvg_ref, wuq_ref, wuk_ref, wuv_ref,
               dqk_o, mq_o, mk_o, mv_o):
    c16, su16, sd16 = c16_ref[...], su16_ref[...], sd16_ref[...]
    c64, su64, sd64 = c64_ref[...], su64_ref[...], sd64_ref[...]
    for j in range(dqk_ref.shape[1] // LANES):
        sl = slice(j * LANES, (j + 1) * LANES)
        x = _rope_lanes(dqk_ref[:, sl].astype(jnp.float32), c16, su16, sd16, DIFF_ROT // 2)
        if j < HEADS:
            x = x * (DIFF_QK ** -0.5 * LOG2E)
        dqk_o[:, sl] = x.astype(dqk_o.dtype)

    cq = cq_ref[...].astype(jnp.float32)
    cqn = cq * lax.rsqrt(jnp.mean(cq * cq, axis=-1, keepdims=True) + RMS_EPS) * qg_ref[...]
    q = _dot(cqn.astype(jnp.bfloat16), wuq_ref[...])
    q = q * ((MLA_NOPE + MLA_ROPE) ** -0.5 * LOG2E)
    ckv = ckv_ref[...].astype(jnp.float32)
    ckvn = (ckv * lax.rsqrt(jnp.mean(ckv * ckv, axis=-1, keepdims=True) + RMS_EPS)
            * kvg_ref[...]).astype(jnp.bfloat16)
    kn = _dot(ckvn, wuk_ref[...])
    mv_o[...] = _dot(ckvn, wuv_ref[...]).astype(mv_o.dtype)
    kr = _rope_lanes(kr_ref[...].astype(jnp.float32), c64, su64, sd64, MLA_ROPE // 2)
    for h in range(HEADS):
        base = h * MLA_QK_PAD
        mq_o[:, base:base + LANES] = q[:, base:base + LANES].astype(mq_o.dtype)
        qr = _rope_lanes(q[:, base + LANES:base + 2 * LANES], c64, su64, sd64, MLA_ROPE // 2)
        mq_o[:, base + LANES:base + 2 * LANES] = qr.astype(mq_o.dtype)
        mk_o[:, base:base + LANES] = kn[:, h * LANES:(h + 1) * LANES].astype(mk_o.dtype)
        mk_o[:, base + LANES:base + 2 * LANES] = kr.astype(mk_o.dtype)


def _prep(u, tabs, q_norm, kv_norm, w_uq, w_uk, w_uv):
    t = u.shape[0]
    tm = min(512, t)
    row = lambda blk: (lambda i: (i, blk))
    const = lambda i: (0, 0)
    tab_spec = pl.BlockSpec((tm, LANES), lambda i: (i, 0))
    outs = pl.pallas_call(
        _prep_body,
        out_shape=(jax.ShapeDtypeStruct((t, 1024), jnp.bfloat16),
                   jax.ShapeDtypeStruct((t, HEADS * MLA_QK_PAD), jnp.bfloat16),
                   jax.ShapeDtypeStruct((t, HEADS * MLA_QK_PAD), jnp.bfloat16),
                   jax.ShapeDtypeStruct((t, BRANCH_WIDTH), jnp.bfloat16)),
        grid=(t // tm,),
        in_specs=[pl.BlockSpec((tm, 1024), row(COL_DQ // 1024)),
                  pl.BlockSpec((tm, MLA_Q_LORA), row(COL_CQ // MLA_Q_LORA)),
                  pl.BlockSpec((tm, MLA_KV_LORA), row(COL_CKV // MLA_KV_LORA)),
                  pl.BlockSpec((tm, LANES), row(COL_KR // LANES)),
                  tab_spec, tab_spec, tab_spec, tab_spec, tab_spec, tab_spec,
                  _resident((1, MLA_Q_LORA), const),
                  _resident((1, MLA_KV_LORA), const),
                  _resident(w_uq.shape, const),
                  _resident(w_uk.shape, const),
                  _resident(w_uv.shape, const)],
        out_specs=(pl.BlockSpec((tm, 1024), lambda i: (i, 0)),
                   pl.BlockSpec((tm, HEADS * MLA_QK_PAD), lambda i: (i, 0)),
                   pl.BlockSpec((tm, HEADS * MLA_QK_PAD), lambda i: (i, 0)),
                   pl.BlockSpec((tm, BRANCH_WIDTH), lambda i: (i, 0))),
        compiler_params=_cparams(("parallel",)),
        name="prep",
    )(u, u, u, u, *tabs, q_norm, kv_norm, w_uq, w_uk, w_uv)
    return outs


def _attn_body(*refs, kind, tq, tk, lam_init):
    softmax = kind != "sb"
    if kind == "diff":
        lam_ref, q_ref, k_ref, v_ref, g_ref, o_ref = refs[:6]
    elif kind == "fox":
        q_ref, k_ref, v_ref, cq_ref, ck_ref, o_ref = refs[:6]
    else:
        q_ref, k_ref, v_ref, o_ref = refs[:4]
    if softmax:
        s_bufs, p_bufs = refs[-8:-6], refs[-6:-4]
        vt_ref, m_ref, a_ref, acc_ref = refs[-4:]
    else:
        s_bufs, p_bufs = refs[-7:-5], refs[-5:-3]
        vt_ref, r_ref, acc_ref = refs[-3:]
    qi = pl.program_id(2)

    @pl.when(qi == 0)
    def _():
        for kb in range(vt_ref.shape[0]):
            vt_ref[kb, 0:HEAD_DIM, :] = v_ref[kb * tk:(kb + 1) * tk, :].T
            if softmax:
                vt_ref[kb, HEAD_DIM:, :] = jnp.ones((vt_ref.shape[1] - HEAD_DIM, tk),
                                                    vt_ref.dtype)

    q0 = qi * tq
    last_kb = 2 * qi + 1
    q = q_ref[...]
    n_comp = 2 if kind == "diff" else 1
    if kind == "diff":
        lane = lax.broadcasted_iota(jnp.int32, q.shape, 1)
        qs = [jnp.where(lane < DIFF_QK, q, jnp.zeros_like(q)),
              jnp.where(lane >= DIFF_QK, q, jnp.zeros_like(q))]
    else:
        qs = [q]

    acc_ref[...] = jnp.zeros_like(acc_ref)
    p_bufs[1][...] = jnp.zeros_like(p_bufs[1])
    if softmax:
        m_ref[...] = jnp.full_like(m_ref, NEG_BIG)
        a_ref[...] = jnp.zeros_like(a_ref)
    else:
        r_ref[...] = jnp.zeros_like(r_ref)

    def key_rows(kb):
        if isinstance(kb, int):
            return pl.ds(kb * tk, tk)
        return pl.ds(pl.multiple_of(kb * tk, tk), tk)

    def scores_stage(kb, s_dst):
        k_t = k_ref[key_rows(kb), :]
        for c in range(n_comp):
            s_dst[c] = _dot_nt(k_t, qs[c])

    def values_stage(kb, p_src):
        v_t = vt_ref[kb]
        for c in range(n_comp):
            pv = _dot(v_t, p_src[c])[:acc_ref.shape[1]]
            if softmax:
                acc_ref[c] = a_ref[c] * acc_ref[c] + pv
            else:
                acc_ref[c] = acc_ref[c] + pv

    def weights_stage(kb, s_src, p_dst, masked):
        if masked:
            j = lax.broadcasted_iota(jnp.int32, (tk, tq), 0)
            i = lax.broadcasted_iota(jnp.int32, (tk, tq), 1)
            off = q0 - kb * tk
            if kind in ("diff", "mla"):
                mask = (j // CHUNK - i // CHUNK) <= off // CHUNK
            elif kind == "fox":
                mask = (j - i) <= off
            else:
                mask = (j - i) < off
        if not softmax:
            z_t = s_src[0]
            ls = _log_sigmoid(z_t)
            lk = ls - z_t
            if masked:
                lk = jnp.where(mask, lk, 0.0)
            jj = lax.broadcasted_iota(jnp.int32, (tk, tk), 0)
            mm = lax.broadcasted_iota(jnp.int32, (tk, tk), 1)
            upper = (mm > jj).astype(jnp.bfloat16)
            hi, mid = _split2(lk)
            suffix = _dot(jnp.concatenate([upper, upper], axis=1),
                          jnp.concatenate([hi, mid], axis=0))
            r_old = r_ref[...]
            w_t = jnp.exp(ls + (r_old + suffix))
            if masked:
                w_t = jnp.where(mask, w_t, 0.0)
            p_dst[0] = w_t.astype(jnp.bfloat16)
            r_ref[...] = r_old + suffix[0:1, :] + lk[0:1, :]
            return
        for c in range(n_comp):
            s_t = s_src[c]
            if kind == "fox":
                ck = ck_ref[key_rows(kb), :]
                s_t = s_t + cq_ref[...] - jnp.concatenate([ck] * (tq // LANES), axis=1)
            if masked:
                s_t = jnp.where(mask, s_t, NEG_BIG)
            m_old = m_ref[c]
            m_new = jnp.maximum(m_old, jnp.max(s_t, axis=0, keepdims=True))
            alpha = jnp.exp2(m_old - m_new)
            p_t = jnp.exp2(s_t - m_new)
            m_ref[c] = m_new
            a_ref[c] = alpha
            p_dst[c] = p_t.astype(jnp.bfloat16)

    def step(kb, parity, masked, next_kb, prev_kb):
        if next_kb is not None:
            scores_stage(next_kb, s_bufs[1 - parity])
        values_stage(prev_kb, p_bufs[1 - parity])
        weights_stage(kb, s_bufs[parity], p_bufs[parity], masked)

    if softmax and tq == 2 * tk:
        scores_stage(0, s_bufs[0])

        def body(t, carry):
            kb = 2 * t
            step(kb, 0, False, kb + 1, jnp.maximum(kb - 1, 0))
            step(kb + 1, 1, False, kb + 2, kb)
            return carry
        lax.fori_loop(0, qi, body, 0)
        step(last_kb - 1, 0, True, last_kb, jnp.maximum(last_kb - 2, 0))
        step(last_kb, 1, True, None, last_kb - 1)
        values_stage(last_kb, p_bufs[1])
    elif softmax:
        scores_stage(0, s_bufs[0])

        def body(t, carry):
            kb = 2 * t
            step(kb, 0, False, kb + 1, jnp.maximum(kb - 1, 0))
            step(kb + 1, 1, False, kb + 2, kb)
            return carry
        lax.fori_loop(0, qi // 2, body, 0)

        @pl.when(qi % 2 == 0)
        def _():
            step(qi, 0, True, None, jnp.maximum(qi - 1, 0))
            values_stage(qi, p_bufs[0])

        @pl.when(qi % 2 == 1)
        def _():
            step(qi - 1, 0, False, qi, jnp.maximum(qi - 2, 0))
            step(qi, 1, True, None, qi - 1)
            values_stage(qi, p_bufs[1])
    else:
        scores_stage(last_kb, s_bufs[0])
        step(last_kb, 0, True, last_kb - 1, last_kb)
        step(last_kb - 1, 1, True, jnp.maximum(last_kb - 2, 0), last_kb)

        def live(carry):
            t, _, r_max = carry
            return (t < qi) & (r_max >= SB_EXP_UNDERFLOW)

        def body(carry):
            t = carry[0]
            kb = last_kb - 2 - 2 * t
            step(kb, 0, False, kb - 1, kb + 1)
            step(kb - 1, 1, False, jnp.maximum(kb - 2, 0), kb)
            return t + 1, kb - 1, jnp.max(r_ref[...])
        done = lax.while_loop(live, body, (0, last_kb - 1, jnp.max(r_ref[...])))
        values_stage(done[1], p_bufs[1])

    def normalised(c):
        return acc_ref[c, 0:HEAD_DIM, :] / acc_ref[c, HEAD_DIM:HEAD_DIM + 1, :]

    if kind == "sb":
        o_t = acc_ref[0]
    elif kind == "diff":
        o_t = normalised(0) - lam_ref[0] * normalised(1)
    else:
        o_t = normalised(0)
    o = o_t.T
    if kind == "diff":
        o = (o * lax.rsqrt(jnp.mean(o * o, axis=-1, keepdims=True) + RMS_EPS)
             * g_ref[...] * (1.0 - lam_init))
    o_ref[...] = o.astype(o_ref.dtype)


def _attention(kind, batch, seq, q_arr, q_col, k_arr, k_col, v_arr, v_col, dq,
               extras=(), lam=None, lam_init=0.0):
    t = q_arr.shape[0]
    tq = min(ATT_TQ, seq)
    tk = tq if kind in ("diff", "mla") else tq // 2
    nq = seq // tq
    q_spec = pl.BlockSpec((tq, dq), lambda b, h, i, *_: (b * nq + i, q_col + h))
    k_spec = pl.BlockSpec((seq, dq), lambda b, h, i, *_: (b, k_col + h))
    v_spec = pl.BlockSpec((seq, HEAD_DIM), lambda b, h, i, *_: (b, v_col + h))
    in_specs = [q_spec, k_spec, v_spec]
    args = [q_arr, k_arr, v_arr]
    n_comp = 2 if kind == "diff" else 1
    if kind == "diff":
        in_specs.append(_resident((1, HEAD_DIM), lambda b, h, i, *_: (0, 0)))
        args.append(extras[0])
    elif kind == "fox":
        cq_rows, ck_rep = extras
        in_specs.append(pl.BlockSpec((None, 1, tq), lambda b, h, i: (b * HEADS + h, 0, i)))
        in_specs.append(pl.BlockSpec((None, seq, LANES), lambda b, h, i: (b * HEADS + h, 0, 0)))
        args += [cq_rows, ck_rep]
    s_buf = pltpu.VMEM((n_comp, tk, tq), jnp.float32)
    p_buf = pltpu.VMEM((n_comp, tk, tq), jnp.bfloat16)
    row = pltpu.VMEM((n_comp, 1, tq), jnp.float32)
    nk = seq // tk
    if kind == "sb":
        scratch = [s_buf, s_buf, p_buf, p_buf,
                   pltpu.VMEM((nk, HEAD_DIM, tk), jnp.bfloat16),
                   pltpu.VMEM((1, tq), jnp.float32),
                   pltpu.VMEM((1, HEAD_DIM, tq), jnp.float32)]
    else:
        scratch = [s_buf, s_buf, p_buf, p_buf,
                   pltpu.VMEM((nk, HEAD_DIM + 16, tk), jnp.bfloat16), row, row,
                   pltpu.VMEM((n_comp, HEAD_DIM + 8, tq), jnp.float32)]
    body = functools.partial(_attn_body, kind=kind, tq=tq, tk=tk, lam_init=lam_init)
    grid = (batch, HEADS, nq)
    out_shape = jax.ShapeDtypeStruct((t, BRANCH_WIDTH), jnp.bfloat16)
    out_spec = pl.BlockSpec((tq, HEAD_DIM), lambda b, h, i, *_: (b * nq + i, h))
    cp = _cparams(("parallel", "parallel", "arbitrary"))
    if kind == "diff":
        return pl.pallas_call(
            body, out_shape=out_shape,
            grid_spec=pltpu.PrefetchScalarGridSpec(
                num_scalar_prefetch=1, grid=grid, in_specs=in_specs, out_specs=out_spec,
                scratch_shapes=scratch),
            compiler_params=cp, name="attn_" + kind)(lam, *args)
    return pl.pallas_call(
        body, out_shape=out_shape, grid=grid, in_specs=in_specs, out_specs=out_spec,
        scratch_shapes=scratch, compiler_params=cp, name="attn_" + kind)(*args)


def _layer_norm(y, g, b):
    mu = jnp.mean(y, axis=-1, keepdims=True)
    yc = y - mu
    var = jnp.mean(yc * yc, axis=-1, keepdims=True)
    return yc * lax.rsqrt(var + LN_EPS) * g + b


def _merge_body(g_ref, od_ref, of_ref, om_ref, os_ref, wb_ref, wo_ref, x_ref, lg_ref, lb_ref,
                wrh_ref, wrl_ref, h_o, hr_o, hb_o, lgt_o, *, alpha):
    d = x_ref.shape[1]
    merged = None
    for n, o_ref in enumerate((od_ref, of_ref, om_ref, os_ref)):
        gate = _sigmoid(g_ref[:, n * d:(n + 1) * d].astype(jnp.float32))
        term = gate * _dot(o_ref[...], wb_ref[n])
        merged = term if merged is None else merged + term
    mix = _dot(merged.astype(jnp.bfloat16), wo_ref[...])
    h = _layer_norm(alpha * x_ref[...] + mix, lg_ref[...], lb_ref[...])
    h_o[...] = h
    h_hi = h.astype(jnp.bfloat16)
    hb_o[...] = h_hi
    _to_chunked(hr_o, _pack_bf16_halves(h_hi))
    h_lo = (h - h_hi.astype(jnp.float32)).astype(jnp.bfloat16)
    lgt_o[...] = (_dot(h_hi, wrh_ref[...]) + _dot(h_lo, wrh_ref[...]) + _dot(h_hi, wrl_ref[...]))


def _merge(u, o_d, o_f, o_m, o_s, w_b, w_o, x, ln_g, ln_b, wr_hi, wr_lo, alpha):
    t, d = x.shape
    tm = min(256, t)
    row = lambda i: (i, 0)
    const2 = lambda i: (0, 0)
    o_spec = pl.BlockSpec((tm, BRANCH_WIDTH), row)
    return pl.pallas_call(
        functools.partial(_merge_body, alpha=alpha),
        out_shape=(jax.ShapeDtypeStruct((t, d), jnp.float32),
                   jax.ShapeDtypeStruct((t * _chunk_pitch(d // 2), LANES), jnp.uint32),
                   jax.ShapeDtypeStruct((t, d), jnp.bfloat16),
                   jax.ShapeDtypeStruct((t, LANES), jnp.float32)),
        grid=(t // tm,),
        in_specs=[pl.BlockSpec((tm, N_BRANCHES * d), row),
                  o_spec, o_spec, o_spec, o_spec,
                  _resident(w_b.shape, lambda i: (0, 0, 0)),
                  _resident(w_o.shape, const2),
                  pl.BlockSpec((tm, d), row),
                  _resident((1, d), const2), _resident((1, d), const2),
                  _resident(wr_hi.shape, const2), _resident(wr_lo.shape, const2)],
        out_specs=(pl.BlockSpec((tm, d), row),
                   pl.BlockSpec((tm * _chunk_pitch(d // 2), LANES), row),
                   pl.BlockSpec((tm, d), row), pl.BlockSpec((tm, LANES), row)),
        compiler_params=_cparams(("parallel",)),
        name="merge_ln1",
    )(u, o_d, o_f, o_m, o_s, w_b, w_o, x, ln_g, ln_b, wr_hi, wr_lo)


CHUNK_PAD = 2


def _chunk_pitch(d):
    return d // LANES + CHUNK_PAD


def _to_chunked(ref, x):
    n, d = x.shape
    nc, pitch = d // LANES, _chunk_pitch(d)
    for c in range(nc):
        ref[pl.ds(c, n, stride=pitch), :] = x[:, c * LANES:(c + 1) * LANES]
    for c in range(nc, pitch):
        ref[pl.ds(c, n, stride=pitch), :] = jnp.zeros((n, LANES), x.dtype)


def _from_chunked(ref, n, d):
    pitch = _chunk_pitch(d)
    return jnp.concatenate([ref[pl.ds(c, n, stride=pitch), :] for c in range(d // LANES)],
                           axis=1)


def _gather_rows(idx_ref, base, src_hbm, dst, sem, n, d):
    nc, pitch = d // LANES, _chunk_pitch(d)
    for r in range(n):
        row = pl.multiple_of(idx_ref[base + r] * pitch, CHUNK_PAD)
        pltpu.make_async_copy(src_hbm.at[pl.ds(row, nc), :], dst.at[pl.ds(r * pitch, nc), :],
                              sem).start(priority=r % 2)


def _wait_rows(src_hbm, dst, sem, n, d):
    rows = n * (d // LANES)
    pltpu.make_async_copy(src_hbm.at[pl.ds(0, rows), :], dst.at[pl.ds(0, rows), :], sem).wait()


def _pack_bf16_halves(x):
    bits = pltpu.bitcast(x.astype(jnp.float32), jnp.uint32)
    half = x.shape[1] // 2
    return bits[:, :half] | (bits[:, half:] >> 16)


def _unpack_bf16_halves(w):
    left = pltpu.bitcast(w & jnp.uint32(0xFFFF0000), jnp.float32)
    right = pltpu.bitcast(w << 16, jnp.float32)
    return jnp.concatenate([left, right], axis=1).astype(jnp.bfloat16)


def _moe_body(sched_ref, n_used_ref, tok_ref, h_hbm, wg_hbm, wu_hbm, wd_hbm, y_ref,
              xbuf, sem, wg_f, wu_f, wd_f, wsem, wg_b, wu_b, wd_b, *, layer):
    i = pl.program_id(0)
    n_used = n_used_ref[0]
    d = wg_b.shape[0] // 2
    bm = xbuf.shape[1] // _chunk_pitch(d)
    slot = i % 2

    def weight_copies(e, s):
        return (pltpu.make_async_copy(wg_hbm.at[layer, e], wg_f.at[s], wsem.at[s]),
                pltpu.make_async_copy(wu_hbm.at[layer, e], wu_f.at[s], wsem.at[s]),
                pltpu.make_async_copy(wd_hbm.at[layer, e], wd_f.at[s], wsem.at[s]))

    @pl.when((i == 0) & (n_used > 0))
    def _():
        for cp in weight_copies(sched_ref[0, 0], 0):
            cp.start(priority=1)
        _gather_rows(tok_ref, 0, h_hbm, xbuf.at[0], sem.at[0], bm, d)

    @pl.when(i + 1 < n_used)
    def _():
        _gather_rows(tok_ref, (i + 1) * bm, h_hbm, xbuf.at[1 - slot], sem.at[1 - slot], bm, d)

    @pl.when(i < n_used)
    def _():
        @pl.when(sched_ref[1, i] == 1)
        def _():
            ws = sched_ref[2, i]
            for cp in weight_copies(sched_ref[0, i], ws):
                cp.wait()
            wg_b[...] = wg_f[ws].astype(jnp.bfloat16)
            wu_b[...] = wu_f[ws].astype(jnp.bfloat16)
            wd_b[...] = wd_f[ws].astype(jnp.bfloat16)

            @pl.when(sched_ref[3, i] >= 0)
            def _():
                for cp in weight_copies(sched_ref[3, i], 1 - ws):
                    cp.start(priority=1)

        _wait_rows(h_hbm, xbuf.at[slot], sem.at[slot], bm, d)
        x = _unpack_bf16_halves(_from_chunked(xbuf.at[slot], bm, d))
        a = _dot(x, wg_b[...])
        hid = (a * _sigmoid(a)) * _dot(x, wu_b[...])
        _to_chunked(y_ref, _dot(hid.astype(jnp.bfloat16), wd_b[...]))

    @pl.when(i >= n_used)
    def _():
        y_ref[...] = jnp.zeros_like(y_ref)


def _moe(sched, n_used, row_tok, h_rows, w_g, w_u, w_d, layer):
    n_rows = row_tok.shape[0]
    d, hid = w_g.shape[2], w_g.shape[3]
    pitch = _chunk_pitch(d)
    bm = MOE_ROWS
    any_spec = pl.BlockSpec(memory_space=pl.ANY)
    return pl.pallas_call(
        functools.partial(_moe_body, layer=layer),
        out_shape=jax.ShapeDtypeStruct((n_rows * pitch, LANES), jnp.float32),
        grid_spec=pltpu.PrefetchScalarGridSpec(
            num_scalar_prefetch=3, grid=(n_rows // bm,),
            in_specs=[any_spec, any_spec, any_spec, any_spec],
            out_specs=pl.BlockSpec((bm * pitch, LANES), lambda i, s, n, tok: (i, 0)),
            scratch_shapes=[pltpu.VMEM((2, bm * _chunk_pitch(d // 2), LANES), jnp.uint32),
                            pltpu.SemaphoreType.DMA((2,)),
                            pltpu.VMEM((2, d, hid), jnp.float32),
                            pltpu.VMEM((2, d, hid), jnp.float32),
                            pltpu.VMEM((2, hid, d), jnp.float32),
                            pltpu.SemaphoreType.DMA((2,)),
                            pltpu.VMEM((d, hid), jnp.bfloat16),
                            pltpu.VMEM((d, hid), jnp.bfloat16),
                            pltpu.VMEM((hid, d), jnp.bfloat16)]),
        compiler_params=_cparams(("arbitrary",)),
        name="moe_experts",
    )(sched, n_used, row_tok, h_rows, w_g, w_u, w_d)


def _final_body(d0_ref, d1_ref, h_ref, hb_ref, y_hbm, gw_ref, p_ref, wg_ref, wp_ref, lg_ref,
                lb_ref, y_o, yb_o, gbuf, sem, *, alpha):
    i = pl.program_id(0)
    n = pl.num_programs(0)
    tm, d = h_ref.shape
    slot = i % 2

    def gather(step, s):
        _gather_rows(d0_ref, step * tm, y_hbm, gbuf.at[s, 0], sem.at[s], tm, d)
        _gather_rows(d1_ref, step * tm, y_hbm, gbuf.at[s, 1], sem.at[s], tm, d)

    @pl.when(i == 0)
    def _():
        gather(0, 0)

    @pl.when(i + 1 < n)
    def _():
        gather(i + 1, 1 - slot)

    gate = _sigmoid(_dot(hb_ref[...], wg_ref[...]))
    proj = _dot(p_ref[...].astype(jnp.bfloat16), wp_ref[...])
    _wait_rows(y_hbm, gbuf.at[slot, 0], sem.at[slot], tm, d)
    _wait_rows(y_hbm, gbuf.at[slot, 1], sem.at[slot], tm, d)
    gw = gw_ref[...]
    ffn = (_from_chunked(gbuf.at[slot, 0], tm, d) * gw[:, 0:1]
           + _from_chunked(gbuf.at[slot, 1], tm, d) * gw[:, 1:2])
    y = _layer_norm(alpha * h_ref[...] + ffn + gate * proj, lg_ref[...], lb_ref[...])
    y_o[...] = y
    yb_o[...] = y.astype(jnp.bfloat16)


def _final(dest0, dest1, h, hb, y_rows, gate_w, p, w_pg, w_pp, ln_g, ln_b, alpha):
    t, d = h.shape
    tm = min(256, t)
    row = lambda i, *_: (i, 0)
    const2 = lambda i, *_: (0, 0)
    return pl.pallas_call(
        functools.partial(_final_body, alpha=alpha),
        out_shape=(jax.ShapeDtypeStruct((t, d), jnp.float32),
                   jax.ShapeDtypeStruct((t, d), jnp.bfloat16)),
        grid_spec=pltpu.PrefetchScalarGridSpec(
            num_scalar_prefetch=2, grid=(t // tm,),
            in_specs=[pl.BlockSpec((tm, d), row), pl.BlockSpec((tm, d), row),
                      pl.BlockSpec(memory_space=pl.ANY),
                      pl.BlockSpec((tm, TOP_K), row), pl.BlockSpec((tm, p.shape[1]), row),
                      _resident(w_pg.shape, const2), _resident(w_pp.shape, const2),
                      _resident((1, d), const2), _resident((1, d), const2)],
            out_specs=(pl.BlockSpec((tm, d), row), pl.BlockSpec((tm, d), row)),
            scratch_shapes=[pltpu.VMEM((2, TOP_K, tm * _chunk_pitch(d), LANES), jnp.float32),
                            pltpu.SemaphoreType.DMA((2,))]),
        compiler_params=_cparams(("arbitrary",)),
        name="final_ln2",
    )(dest0, dest1, h, hb, y_rows, gate_w, p, w_pg, w_pp, ln_g, ln_b)


def _rope_tables(positions):
    pos = positions.reshape(-1).astype(jnp.float32)
    lane = jnp.arange(LANES)

    def tables(rot, period):
        half = rot // 2
        inv_freq = ROPE_THETA ** (-jnp.arange(half, dtype=jnp.float32) / half)
        ang = pos[:, None] * inv_freq
        cos, sin = jnp.cos(ang), jnp.sin(ang)
        lp = lane % period
        idx = lp % half
        in_rot = lp < rot
        c = jnp.where(in_rot[None, :], cos[:, idx], 1.0)
        s_up = jnp.where((lp < half)[None, :], -sin[:, idx], 0.0)
        s_dn = jnp.where(((lp >= half) & in_rot)[None, :], sin[:, idx], 0.0)
        return c, s_up, s_dn

    return tables(DIFF_ROT, DIFF_QK) + tables(MLA_ROPE, LANES)


def _w_in_segments(d):
    widths = [512, 512, 512, 512, 512, 512, HEADS, MLA_Q_LORA, MLA_KV_LORA, MLA_ROPE,
              512, 512, 512, N_BRANCHES * d]
    dsts = [COL_DQ, COL_DQ + 512, COL_DV, COL_FQ, COL_FK, COL_FV, COL_FF, COL_CQ, COL_CKV,
            COL_KR, COL_SQ, COL_SK, COL_SV, COL_GATES]
    segs, src = [], 0
    for wd, dst in zip(widths, dsts):
        segs.append((src, dst, wd))
        src += wd
    return segs


def _w_in_body(w_ref, o_ref, f_ref, *, segs):
    for src, dst, wd in segs:
        o_ref[:, dst:dst + wd] = w_ref[:, src:src + wd].astype(o_ref.dtype)
    rows = o_ref.shape[0]
    o_ref[:, COL_KR + MLA_ROPE:COL_KR + LANES] = jnp.zeros((rows, LANES - MLA_ROPE), o_ref.dtype)
    o_ref[:, COL_FF + HEADS:COL_FF + LANES] = jnp.zeros((rows, LANES - HEADS), o_ref.dtype)
    f_ref[...] = o_ref[:, COL_FF:COL_FF + LANES]


def _reorder_w_in(w_in, layer):
    _, d, n_in = w_in.shape
    tr = 128
    return pl.pallas_call(
        functools.partial(_w_in_body, segs=_w_in_segments(d)),
        out_shape=(jax.ShapeDtypeStruct((d, U_WIDTH), jnp.bfloat16),
                   jax.ShapeDtypeStruct((d, LANES), jnp.bfloat16)),
        grid=(d // tr,),
        in_specs=[pl.BlockSpec((None, tr, n_in), lambda r: (layer, r, 0))],
        out_specs=(pl.BlockSpec((tr, U_WIDTH), lambda r: (r, 0)),
                   pl.BlockSpec((tr, LANES), lambda r: (r, 0))),
        compiler_params=_cparams(("parallel",)),
        name="w_in_relayout",
    )(w_in)


def _route(logits, b_rg, b_re, t):
    def first_max(v, n):
        top = jnp.max(v, axis=-1, keepdims=True)
        ids = jnp.arange(n, dtype=jnp.int32)
        return top, jnp.min(jnp.where(v == top, ids, n), axis=-1, keepdims=True)

    g_logits = logits[:, :N_GROUPS] + b_rg
    g_max, grp = first_max(g_logits, N_GROUPS)
    p_grp = 1.0 / jnp.sum(jnp.exp(g_logits - g_max), axis=-1, keepdims=True)
    e_logits = (logits[:, N_GROUPS:N_GROUPS + N_EXPERTS] + b_re).reshape(
        t, N_GROUPS, EXPERTS_PER_GROUP)
    in_grp = jnp.arange(N_GROUPS, dtype=jnp.int32)[None, :, None] == grp[:, :, None]
    e_in = jnp.sum(jnp.where(in_grp, e_logits, 0.0), axis=1)
    probs = jax.nn.softmax(e_in, axis=-1)
    p1, i1 = first_max(probs, EXPERTS_PER_GROUP)
    rest = jnp.where(jnp.arange(EXPERTS_PER_GROUP, dtype=jnp.int32)[None, :] == i1, -1.0, probs)
    p2, i2 = first_max(rest, EXPERTS_PER_GROUP)
    top_p = jnp.concatenate([p1, p2], axis=-1)
    weights = p_grp * top_p / jnp.sum(top_p, axis=-1, keepdims=True)
    expert_idx = grp * EXPERTS_PER_GROUP + jnp.concatenate([i1, i2], axis=-1)
    return expert_idx.astype(jnp.int32), weights


def _dispatch_plan(expert_idx, t):
    a = t * TOP_K
    flat_e = expert_idx.reshape(a)
    onehot = (flat_e[:, None] == jnp.arange(N_EXPERTS)[None, :]).astype(jnp.int32)
    sizes = jnp.sum(onehot, axis=0)
    padded = (sizes + MOE_ROWS - 1) // MOE_ROWS * MOE_ROWS
    pad_end = jnp.cumsum(padded)
    pad_start = pad_end - padded
    dest = (jnp.sum(onehot * (jnp.cumsum(onehot, axis=0) + pad_start[None, :]), axis=1)
            - 1).astype(jnp.int32)
    n_rows = a + N_EXPERTS * MOE_ROWS
    n_blk = n_rows // MOE_ROWS
    flat_tok = jnp.arange(a, dtype=jnp.int32) // TOP_K
    row_tok = jnp.zeros((n_rows,), jnp.int32).at[dest].set(flat_tok)
    blk_start = jnp.arange(n_blk, dtype=pad_end.dtype) * MOE_ROWS
    blk_e = jnp.minimum(jnp.sum(pad_end[None, :] <= blk_start[:, None], axis=1),
                        N_EXPERTS - 1).astype(jnp.int32)
    n_used = (pad_end[-1] // MOE_ROWS).astype(jnp.int32).reshape(1)
    first = jnp.concatenate([jnp.ones((1,), jnp.int32),
                             (blk_e[1:] != blk_e[:-1]).astype(jnp.int32)])
    parity = (jnp.cumsum(first) - 1) % 2
    experts = jnp.arange(N_EXPERTS, dtype=jnp.int32)
    used = jnp.where(padded > 0, experts, N_EXPERTS)
    later = jnp.flip(lax.cummin(jnp.flip(used)))
    next_used = jnp.concatenate([later[1:], jnp.full((1,), N_EXPERTS, jnp.int32)])
    next_used = jnp.where(next_used < N_EXPERTS, next_used, -1)
    sched = jnp.stack([blk_e, first, parity.astype(jnp.int32),
                       next_used[blk_e].astype(jnp.int32)])
    return dest.reshape(t, TOP_K), row_tok, sched, n_used


def kernel(x, p, positions, w_in, fox_f_bias, diff_lambda, diff_subln, mla_q_norm, mla_kv_norm,
           mla_w_uq, mla_w_ukv, w_branch, w_o, ln1_g, ln1_b, w_router_group, b_router_group,
           w_router_expert, b_router_expert, w_expert_gate, w_expert_up, w_expert_down,
           w_ple_gate, w_ple_proj, ln2_g, ln2_b):
    batch, seq, d = x.shape
    depth = w_in.shape[0]
    t = batch * seq
    alpha = (2 * depth) ** 0.25
    bf = jnp.bfloat16
    tabs = _rope_tables(positions)
    col_scale = jnp.ones((1, U_WIDTH), jnp.float32)
    col_scale = col_scale.at[0, COL_FQ:COL_FQ + BRANCH_WIDTH].set(HEAD_DIM ** -0.5 * LOG2E)
    col_scale = col_scale.at[0, COL_SQ:COL_SQ + BRANCH_WIDTH].set(HEAD_DIM ** -0.5)
    xf = x.reshape(t, d)
    xb = xf.astype(bf)

    for i in range(depth):
        w1, w_f = _reorder_w_in(w_in, i)
        u = _matmul(xb, w1, col_scale, bf, 1024, 1536, "in_proj")

        b_f = jnp.zeros((1, LANES), jnp.float32).at[0, :HEADS].set(fox_f_bias[i])
        cum = _fgate_cumsum(xb, w_f, b_f, batch, seq)[:, :HEADS] * LOG2E
        cum_bh = jnp.transpose(cum.reshape(batch, seq, HEADS), (0, 2, 1)).reshape(
            batch * HEADS, seq)
        cq_rows = cum_bh.reshape(batch * HEADS, 1, seq)
        ck_rep = jnp.broadcast_to(cum_bh[:, :, None], (batch * HEADS, seq, LANES))

        uq = mla_w_uq[i].reshape(MLA_Q_LORA, HEADS, MLA_NOPE + MLA_ROPE)
        uq = jnp.pad(uq, ((0, 0), (0, 0), (0, MLA_QK_PAD - MLA_NOPE - MLA_ROPE)))
        uq = uq.reshape(MLA_Q_LORA, HEADS * MLA_QK_PAD).astype(bf)
        ukv = mla_w_ukv[i].reshape(MLA_KV_LORA, HEADS, MLA_NOPE + HEAD_DIM)
        uk = ukv[:, :, :MLA_NOPE].reshape(MLA_KV_LORA, HEADS * MLA_NOPE).astype(bf)
        uv = ukv[:, :, MLA_NOPE:].reshape(MLA_KV_LORA, HEADS * HEAD_DIM).astype(bf)
        dqk, mq, mk, mv = _prep(u, tabs, mla_q_norm[i].reshape(1, -1),
                                mla_kv_norm[i].reshape(1, -1), uq, uk, uv)

        lam_init = 0.8 - 0.6 * math.exp(-0.3 * i)
        lp = diff_lambda[i].astype(jnp.float32)
        lam = (jnp.exp(jnp.sum(lp[0] * lp[1])) - jnp.exp(jnp.sum(lp[2] * lp[3]))
               + lam_init).reshape(1)
        blk = lambda col: col // HEAD_DIM
        o_d = _attention("diff", batch, seq, dqk, 0, dqk, HEADS, u, blk(COL_DV), HEAD_DIM,
                         extras=(diff_subln[i].reshape(1, HEAD_DIM),),
                         lam=lam, lam_init=lam_init)
        o_f = _attention("fox", batch, seq, u, blk(COL_FQ), u, blk(COL_FK), u, blk(COL_FV),
                         HEAD_DIM, extras=(cq_rows, ck_rep))
        o_m = _attention("mla", batch, seq, mq, 0, mk, 0, mv, 0, MLA_QK_PAD)
        o_s = _attention("sb", batch, seq, u, blk(COL_SQ), u, blk(COL_SK), u, blk(COL_SV),
                         HEAD_DIM)

        w_r = jnp.concatenate([w_router_group[i], w_router_expert[i]], axis=1)
        w_r = jnp.pad(w_r, ((0, 0), (0, LANES - w_r.shape[1])))
        wr_hi = w_r.astype(bf)
        wr_lo = (w_r - wr_hi.astype(jnp.float32)).astype(bf)
        h, h_rows, hb, logits = _merge(
            u, o_d, o_f, o_m, o_s, w_branch[i].astype(bf), w_o[i].astype(bf), xf,
            ln1_g[i].reshape(1, d), ln1_b[i].reshape(1, d), wr_hi, wr_lo, alpha)

        expert_idx, weights = _route(logits, b_router_group[i], b_router_expert[i], t)
        dest, row_tok, sched, n_used = _dispatch_plan(expert_idx, t)
        y_rows = _moe(sched, n_used, row_tok, h_rows, w_expert_gate, w_expert_up,
                      w_expert_down, i)
        xf, xb = _final(dest[:, 0], dest[:, 1], h, hb, y_rows, weights.astype(jnp.float32),
                        p[i].reshape(t, -1), w_ple_gate[i].astype(bf),
                        w_ple_proj[i].astype(bf), ln2_g[i].reshape(1, d),
                        ln2_b[i].reshape(1, d), alpha)
    return xf.reshape(batch, seq, d)
```

```python
import functools
import math

import jax
import jax.numpy as jnp
from jax import lax
from jax.experimental import pallas as pl
from jax.experimental.pallas import tpu as pltpu

HEAD_DIM = 128
HEADS = 4
N_BRANCHES = 4
BRANCH_WIDTH = HEADS * HEAD_DIM
DIFF_QK = 64
DIFF_ROT = DIFF_QK // 4
MLA_Q_LORA = 512
MLA_KV_LORA = 256
MLA_NOPE = 128
MLA_ROPE = 64
MLA_QK_PAD = 256
N_GROUPS = 4
EXPERTS_PER_GROUP = 8
N_EXPERTS = N_GROUPS * EXPERTS_PER_GROUP
TOP_K = 2
CHUNK = 64
ROPE_THETA = 500000.0
LN_EPS = 1e-5
RMS_EPS = 1e-6
NEG_BIG = -1e30
LOG2E = 1.4426950408889634
SB_EXP_UNDERFLOW = -104.0
LANES = 128
V7X_VMEM_LIMIT = 56 * 1024 * 1024

COL_GATES = 0
COL_DQ = 8192
COL_DV = COL_DQ + 1024
COL_FQ = COL_DV + 512
COL_FK = COL_FQ + 512
COL_FV = COL_FK + 512
COL_SQ = COL_FV + 512
COL_SK = COL_SQ + 512
COL_SV = COL_SK + 512
COL_CQ = COL_SV + 512
COL_CKV = COL_CQ + 512
COL_KR = COL_CKV + 256
COL_FF = COL_KR + 128
U_WIDTH = COL_FF + 128

ATT_TQ = 512
MOE_ROWS = 256


def _cparams(sem, vmem=V7X_VMEM_LIMIT):
    return pltpu.CompilerParams(dimension_semantics=sem, vmem_limit_bytes=vmem)


def _resident(shape, index_map):
    return pl.BlockSpec(shape, index_map, pipeline_mode=pl.Buffered(1))


def _matmul_body(x_ref, w_ref, c_ref, o_ref):
    acc = lax.dot_general(x_ref[...], w_ref[...], (((1,), (1,)), ((), ())),
                          preferred_element_type=jnp.float32)
    o_ref[...] = (acc * c_ref[...]).astype(o_ref.dtype)


def _matmul(x, w_t, col_scale, out_dtype, tm, tn, name):
    m, k = x.shape
    n = w_t.shape[0]
    tm, tn = min(tm, m), min(tn, n)
    return pl.pallas_call(
        _matmul_body,
        out_shape=jax.ShapeDtypeStruct((m, n), out_dtype),
        grid=(m // tm, n // tn),
        in_specs=[pl.BlockSpec((tm, k), lambda i, j: (i, 0)),
                  pl.BlockSpec((tn, k), lambda i, j: (j, 0)),
                  pl.BlockSpec((1, tn), lambda i, j: (0, j))],
        out_specs=pl.BlockSpec((tm, tn), lambda i, j: (i, j)),
        compiler_params=_cparams(("parallel", "arbitrary")),
        name=name,
    )(x, w_t, col_scale)


def _split3(x):
    hi = x.astype(jnp.bfloat16)
    r1 = x - hi.astype(jnp.float32)
    mid = r1.astype(jnp.bfloat16)
    lo = (r1 - mid.astype(jnp.float32)).astype(jnp.bfloat16)
    return hi, mid, lo


def _split2(x):
    hi = x.astype(jnp.bfloat16)
    mid = (x - hi.astype(jnp.float32)).astype(jnp.bfloat16)
    return hi, mid


def _dot(a, b):
    return jnp.dot(a, b, preferred_element_type=jnp.float32)


def _dot_nt(a, b):
    return lax.dot_general(a, b, (((1,), (1,)), ((), ())), preferred_element_type=jnp.float32)


def _dot_tn(a, b):
    return lax.dot_general(a, b, (((0,), (0,)), ((), ())), preferred_element_type=jnp.float32)


def _log_sigmoid(z):
    neg_abs = pltpu.bitcast(pltpu.bitcast(z, jnp.uint32) | jnp.uint32(0x80000000), jnp.float32)
    return jnp.minimum(z, 0.0) - jnp.log(1.0 + jnp.exp(neg_abs))


def _sigmoid(z):
    return 1.0 / (1.0 + jnp.exp(-z))


def _fgate_body(x_ref, w_ref, b_ref, o_ref, carry_ref):
    @pl.when(pl.program_id(1) == 0)
    def _():
        carry_ref[...] = jnp.zeros_like(carry_ref)

    f = _dot_nt(x_ref[...], w_ref[...]) + b_ref[...]
    log_f = _log_sigmoid(f)
    ts = log_f.shape[0]
    row = lax.broadcasted_iota(jnp.int32, (ts, ts), 0)
    col = lax.broadcasted_iota(jnp.int32, (ts, ts), 1)
    tri = (col <= row).astype(jnp.bfloat16)
    hi, mid, lo = _split3(log_f)
    cum = _dot(tri, hi) + _dot(tri, mid) + _dot(tri, lo) + carry_ref[...]
    o_ref[...] = cum
    carry_ref[...] = cum[ts - 1:ts, :]


def _fgate_cumsum(xb, w1_t, b_f, batch, seq):
    t, d = xb.shape
    ts = min(512, seq)
    ns = seq // ts
    return pl.pallas_call(
        _fgate_body,
        out_shape=jax.ShapeDtypeStruct((t, LANES), jnp.float32),
        grid=(batch, ns),
        in_specs=[pl.BlockSpec((ts, d), lambda b, s: (b * ns + s, 0)),
                  _resident((LANES, d), lambda b, s: (COL_FF // LANES, 0)),
                  _resident((1, LANES), lambda b, s: (0, 0))],
        out_specs=pl.BlockSpec((ts, LANES), lambda b, s: (b * ns + s, 0)),
        scratch_shapes=[pltpu.VMEM((1, LANES), jnp.float32)],
        compiler_params=_cparams(("parallel", "arbitrary")),
        name="fgate_cumsum",
    )(xb, w1_t, b_f)


def _rope_lanes(x, c, s_up, s_dn, half):
    return x * c + pltpu.roll(x, LANES - half, 1) * s_up + pltpu.roll(x, half, 1) * s_dn


def _prep_body(dqk_ref, cq_ref, ckv_ref, kr_ref, c16_ref, su16_ref, sd16_ref,
               c64_ref, su64_ref, sd64_ref, qg_ref, kvg_ref, wuq_ref, wuk_ref, wuv_ref,
               dqk_o, mq_o, mk_o, mv_o):
    c16, su16, sd16 = c16_ref[...], su16_ref[...], sd16_ref[...]
    c64, su64, sd64 = c64_ref[...], su64_ref[...], sd64_ref[...]
    for j in range(dqk_ref.shape[1] // LANES):
        sl = slice(j * LANES, (j + 1) * LANES)
        x = _rope_lanes(dqk_ref[:, sl].astype(jnp.float32), c16, su16, sd16, DIFF_ROT // 2)
        if j < HEADS:
            x = x * (DIFF_QK ** -0.5 * LOG2E)
        dqk_o[:, sl] = x.astype(dqk_o.dtype)

    cq = cq_ref[...].astype(jnp.float32)
    cqn = cq * lax.rsqrt(jnp.mean(cq * cq, axis=-1, keepdims=True) + RMS_EPS) * qg_ref[...]
    q = _dot(cqn.astype(jnp.bfloat16), wuq_ref[...])
    q = q * ((MLA_NOPE + MLA_ROPE) ** -0.5 * LOG2E)
    ckv = ckv_ref[...].astype(jnp.float32)
    ckvn = (ckv * lax.rsqrt(jnp.mean(ckv * ckv, axis=-1, keepdims=True) + RMS_EPS)
            * kvg_ref[...]).astype(jnp.bfloat16)
    kn = _dot(ckvn, wuk_ref[...])
    mv_o[...] = _dot(ckvn, wuv_ref[...]).astype(mv_o.dtype)
    kr = _rope_lanes(kr_ref[...].astype(jnp.float32), c64, su64, sd64, MLA_ROPE // 2)
    for h in range(HEADS):
        base = h * MLA_QK_PAD
        mq_o[:, base:base + LANES] = q[:, base:base + LANES].astype(mq_o.dtype)
        qr = _rope_lanes(q[:, base + LANES:base + 2 * LANES], c64, su64, sd64, MLA_ROPE // 2)
        mq_o[:, base + LANES:base + 2 * LANES] = qr.astype(mq_o.dtype)
        mk_o[:, base:base + LANES] = kn[:, h * LANES:(h + 1) * LANES].astype(mk_o.dtype)
        mk_o[:, base + LANES:base + 2 * LANES] = kr.astype(mk_o.dtype)


def _prep(u, tabs, q_norm, kv_norm, w_uq, w_uk, w_uv):
    t = u.shape[0]
    tm = min(512, t)
    row = lambda blk: (lambda i: (i, blk))
    const = lambda i: (0, 0)
    tab_spec = pl.BlockSpec((tm, LANES), lambda i: (i, 0))
    outs = pl.pallas_call(
        _prep_body,
        out_shape=(jax.ShapeDtypeStruct((t, 1024), jnp.bfloat16),
                   jax.ShapeDtypeStruct((t, HEADS * MLA_QK_PAD), jnp.bfloat16),
                   jax.ShapeDtypeStruct((t, HEADS * MLA_QK_PAD), jnp.bfloat16),
                   jax.ShapeDtypeStruct((t, BRANCH_WIDTH), jnp.bfloat16)),
        grid=(t // tm,),
        in_specs=[pl.BlockSpec((tm, 1024), row(COL_DQ // 1024)),
                  pl.BlockSpec((tm, MLA_Q_LORA), row(COL_CQ // MLA_Q_LORA)),
                  pl.BlockSpec((tm, MLA_KV_LORA), row(COL_CKV // MLA_KV_LORA)),
                  pl.BlockSpec((tm, LANES), row(COL_KR // LANES)),
                  tab_spec, tab_spec, tab_spec, tab_spec, tab_spec, tab_spec,
                  _resident((1, MLA_Q_LORA), const),
                  _resident((1, MLA_KV_LORA), const),
                  _resident(w_uq.shape, const),
                  _resident(w_uk.shape, const),
                  _resident(w_uv.shape, const)],
        out_specs=(pl.BlockSpec((tm, 1024), lambda i: (i, 0)),
                   pl.BlockSpec((tm, HEADS * MLA_QK_PAD), lambda i: (i, 0)),
                   pl.BlockSpec((tm, HEADS * MLA_QK_PAD), lambda i: (i, 0)),
                   pl.BlockSpec((tm, BRANCH_WIDTH), lambda i: (i, 0))),
        compiler_params=_cparams(("parallel",)),
        name="prep",
    )(u, u, u, u, *tabs, q_norm, kv_norm, w_uq, w_uk, w_uv)
    return outs


def _attn_body(*refs, kind, tq, tk, lam_init):
    softmax = kind != "sb"
    if kind == "diff":
        lam_ref, q_ref, k_ref, v_ref, g_ref, o_ref = refs[:6]
    elif kind == "fox":
        q_ref, k_ref, v_ref, cq_ref, ck_ref, o_ref = refs[:6]
    else:
        q_ref, k_ref, v_ref, o_ref = refs[:4]
    if softmax:
        s_bufs, p_bufs = refs[-8:-6], refs[-6:-4]
        vt_ref, m_ref, a_ref, acc_ref = refs[-4:]
    else:
        s_bufs, p_bufs = refs[-7:-5], refs[-5:-3]
        vt_ref, r_ref, acc_ref = refs[-3:]
    qi = pl.program_id(2)

    @pl.when(qi == 0)
    def _():
        for kb in range(vt_ref.shape[0]):
            vt_ref[kb, 0:HEAD_DIM, :] = v_ref[kb * tk:(kb + 1) * tk, :].T
            if softmax:
                vt_ref[kb, HEAD_DIM:, :] = jnp.ones((vt_ref.shape[1] - HEAD_DIM, tk),
                                                    vt_ref.dtype)

    q0 = qi * tq
    last_kb = 2 * qi + 1
    q = q_ref[...]
    n_comp = 2 if kind == "diff" else 1
    if kind == "diff":
        lane = lax.broadcasted_iota(jnp.int32, q.shape, 1)
        qs = [jnp.where(lane < DIFF_QK, q, jnp.zeros_like(q)),
              jnp.where(lane >= DIFF_QK, q, jnp.zeros_like(q))]
    else:
        qs = [q]

    acc_ref[...] = jnp.zeros_like(acc_ref)
    p_bufs[1][...] = jnp.zeros_like(p_bufs[1])
    if softmax:
        m_ref[...] = jnp.full_like(m_ref, NEG_BIG)
        a_ref[...] = jnp.zeros_like(a_ref)
    else:
        r_ref[...] = jnp.zeros_like(r_ref)

    def key_rows(kb):
        if isinstance(kb, int):
            return pl.ds(kb * tk, tk)
        return pl.ds(pl.multiple_of(kb * tk, tk), tk)

    def scores_stage(kb, s_dst):
        k_t = k_ref[key_rows(kb), :]
        for c in range(n_comp):
            s_dst[c] = _dot_nt(k_t, qs[c])

    def values_stage(kb, p_src):
        v_t = vt_ref[kb]
        for c in range(n_comp):
            pv = _dot(v_t, p_src[c])[:acc_ref.shape[1]]
            if softmax:
                acc_ref[c] = a_ref[c] * acc_ref[c] + pv
            else:
                acc_ref[c] = acc_ref[c] + pv

    def weights_stage(kb, s_src, p_dst, masked):
        if masked:
            j = lax.broadcasted_iota(jnp.int32, (tk, tq), 0)
            i = lax.broadcasted_iota(jnp.int32, (tk, tq), 1)
            off = q0 - kb * tk
            if kind in ("diff", "mla"):
                mask = (j // CHUNK - i // CHUNK) <= off // CHUNK
            elif kind == "fox":
                mask = (j - i) <= off
            else:
                mask = (j - i) < off
        if not softmax:
            z_t = s_src[0]
            ls = _log_sigmoid(z_t)
            lk = ls - z_t
            if masked:
                lk = jnp.where(mask, lk, 0.0)
            jj = lax.broadcasted_iota(jnp.int32, (tk, tk), 0)
            mm = lax.broadcasted_iota(jnp.int32, (tk, tk), 1)
            upper = (mm > jj).astype(jnp.bfloat16)
            hi, mid = _split2(lk)
            suffix = _dot(jnp.concatenate([upper, upper], axis=1),
                          jnp.concatenate([hi, mid], axis=0))
            r_old = r_ref[...]
            w_t = jnp.exp(ls + (r_old + suffix))
            if masked:
                w_t = jnp.where(mask, w_t, 0.0)
            p_dst[0] = w_t.astype(jnp.bfloat16)
            r_ref[...] = r_old + suffix[0:1, :] + lk[0:1, :]
            return
        for c in range(n_comp):
            s_t = s_src[c]
            if kind == "fox":
                ck = ck_ref[key_rows(kb), :]
                s_t = s_t + cq_ref[...] - jnp.concatenate([ck] * (tq // LANES), axis=1)
            if masked:
                s_t = jnp.where(mask, s_t, NEG_BIG)
            m_old = m_ref[c]
            m_new = jnp.maximum(m_old, jnp.max(s_t, axis=0, keepdims=True))
            alpha = jnp.exp2(m_old - m_new)
            p_t = jnp.exp2(s_t - m_new)
            m_ref[c] = m_new
            a_ref[c] = alpha
            p_dst[c] = p_t.astype(jnp.bfloat16)

    def step(kb, parity, masked, next_kb, prev_kb):
        if next_kb is not None:
            scores_stage(next_kb, s_bufs[1 - parity])
        values_stage(prev_kb, p_bufs[1 - parity])
        weights_stage(kb, s_bufs[parity], p_bufs[parity], masked)

    if softmax and tq == 2 * tk:
        scores_stage(0, s_bufs[0])

        def body(t, carry):
            kb = 2 * t
            step(kb, 0, False, kb + 1, jnp.maximum(kb - 1, 0))
            step(kb + 1, 1, False, kb + 2, kb)
            return carry
        lax.fori_loop(0, qi, body, 0)
        step(last_kb - 1, 0, True, last_kb, jnp.maximum(last_kb - 2, 0))
        step(last_kb, 1, True, None, last_kb - 1)
        values_stage(last_kb, p_bufs[1])
    elif softmax:
        scores_stage(0, s_bufs[0])

        def body(t, carry):
            kb = 2 * t
            step(kb, 0, False, kb + 1, jnp.maximum(kb - 1, 0))
            step(kb + 1, 1, False, kb + 2, kb)
            return carry
        lax.fori_loop(0, qi // 2, body, 0)

        @pl.when(qi % 2 == 0)
        def _():
            step(qi, 0, True, None, jnp.maximum(qi - 1, 0))
            values_stage(qi, p_bufs[0])

        @pl.when(qi % 2 == 1)
        def _():
            step(qi - 1, 0, False, qi, jnp.maximum(qi - 2, 0))
            step(qi, 1, True, None, qi - 1)
            values_stage(qi, p_bufs[1])
    else:
        scores_stage(last_kb, s_bufs[0])
        step(last_kb, 0, True, last_kb - 1, last_kb)
        step(last_kb - 1, 1, True, jnp.maximum(last_kb - 2, 0), last_kb)

        def live(carry):
            t, _, r_max = carry
            return (t < qi) & (r_max >= SB_EXP_UNDERFLOW)

        def body(carry):
            t = carry[0]
            kb = last_kb - 2 - 2 * t
            step(kb, 0, False, kb - 1, kb + 1)
            step(kb - 1, 1, False, jnp.maximum(kb - 2, 0), kb)
            return t + 1, kb - 1, jnp.max(r_ref[...])
        done = lax.while_loop(live, body, (0, last_kb - 1, jnp.max(r_ref[...])))
        values_stage(done[1], p_bufs[1])

    def normalised(c):
        return acc_ref[c, 0:HEAD_DIM, :] / acc_ref[c, HEAD_DIM:HEAD_DIM + 1, :]

    if kind == "sb":
        o_t = acc_ref[0]
    elif kind == "diff":
        o_t = normalised(0) - lam_ref[0] * normalised(1)
    else:
        o_t = normalised(0)
    o = o_t.T
    if kind == "diff":
        o = (o * lax.rsqrt(jnp.mean(o * o, axis=-1, keepdims=True) + RMS_EPS)
             * g_ref[...] * (1.0 - lam_init))
    o_ref[...] = o.astype(o_ref.dtype)


def _attention(kind, batch, seq, q_arr, q_col, k_arr, k_col, v_arr, v_col, dq,
               extras=(), lam=None, lam_init=0.0):
    t = q_arr.shape[0]
    tq = min(ATT_TQ, seq)
    tk = tq if kind in ("diff", "mla") else tq // 2
    nq = seq // tq
    q_spec = pl.BlockSpec((tq, dq), lambda b, h, i, *_: (b * nq + i, q_col + h))
    k_spec = pl.BlockSpec((seq, dq), lambda b, h, i, *_: (b, k_col + h))
    v_spec = pl.BlockSpec((seq, HEAD_DIM), lambda b, h, i, *_: (b, v_col + h))
    in_specs = [q_spec, k_spec, v_spec]
    args = [q_arr, k_arr, v_arr]
    n_comp = 2 if kind == "diff" else 1
    if kind == "diff":
        in_specs.append(_resident((1, HEAD_DIM), lambda b, h, i, *_: (0, 0)))
        args.append(extras[0])
    elif kind == "fox":
        cq_rows, ck_rep = extras
        in_specs.append(pl.BlockSpec((None, 1, tq), lambda b, h, i: (b * HEADS + h, 0, i)))
        in_specs.append(pl.BlockSpec((None, seq, LANES), lambda b, h, i: (b * HEADS + h, 0, 0)))
        args += [cq_rows, ck_rep]
    s_buf = pltpu.VMEM((n_comp, tk, tq), jnp.float32)
    p_buf = pltpu.VMEM((n_comp, tk, tq), jnp.bfloat16)
    row = pltpu.VMEM((n_comp, 1, tq), jnp.float32)
    nk = seq // tk
    if kind == "sb":
        scratch = [s_buf, s_buf, p_buf, p_buf,
                   pltpu.VMEM((nk, HEAD_DIM, tk), jnp.bfloat16),
                   pltpu.VMEM((1, tq), jnp.float32),
                   pltpu.VMEM((1, HEAD_DIM, tq), jnp.float32)]
    else:
        scratch = [s_buf, s_buf, p_buf, p_buf,
                   pltpu.VMEM((nk, HEAD_DIM + 16, tk), jnp.bfloat16), row, row,
                   pltpu.VMEM((n_comp, HEAD_DIM + 8, tq), jnp.float32)]
    body = functools.partial(_attn_body, kind=kind, tq=tq, tk=tk, lam_init=lam_init)
    grid = (batch, HEADS, nq)
    out_shape = jax.ShapeDtypeStruct((t, BRANCH_WIDTH), jnp.bfloat16)
    out_spec = pl.BlockSpec((tq, HEAD_DIM), lambda b, h, i, *_: (b * nq + i, h))
    cp = _cparams(("parallel", "parallel", "arbitrary"))
    if kind == "diff":
        return pl.pallas_call(
            body, out_shape=out_shape,
            grid_spec=pltpu.PrefetchScalarGridSpec(
                num_scalar_prefetch=1, grid=grid, in_specs=in_specs, out_specs=out_spec,
                scratch_shapes=scratch),
            compiler_params=cp, name="attn_" + kind)(lam, *args)
    return pl.pallas_call(
        body, out_shape=out_shape, grid=grid, in_specs=in_specs, out_specs=out_spec,
        scratch_shapes=scratch, compiler_params=cp, name="attn_" + kind)(*args)


def _layer_norm(y, g, b):
    mu = jnp.mean(y, axis=-1, keepdims=True)
    yc = y - mu
    var = jnp.mean(yc * yc, axis=-1, keepdims=True)
    return yc * lax.rsqrt(var + LN_EPS) * g + b


def _merge_body(g_ref, od_ref, of_ref, om_ref, os_ref, wb_ref, wo_ref, x_ref, lg_ref, lb_ref,
                wrh_ref, wrl_ref, h_o, hr_o, hb_o, lgt_o, *, alpha):
    d = x_ref.shape[1]
    merged = None
    for n, o_ref in enumerate((od_ref, of_ref, om_ref, os_ref)):
        gate = _sigmoid(g_ref[:, n * d:(n + 1) * d].astype(jnp.float32))
        term = gate * _dot(o_ref[...], wb_ref[n])
        merged = term if merged is None else merged + term
    mix = _dot(merged.astype(jnp.bfloat16), wo_ref[...])
    h = _layer_norm(alpha * x_ref[...] + mix, lg_ref[...], lb_ref[...])
    h_o[...] = h
    h_hi = h.astype(jnp.bfloat16)
    hb_o[...] = h_hi
    _to_chunked(hr_o, _pack_bf16_halves(h_hi))
    h_lo = (h - h_hi.astype(jnp.float32)).astype(jnp.bfloat16)
    lgt_o[...] = (_dot(h_hi, wrh_ref[...]) + _dot(h_lo, wrh_ref[...]) + _dot(h_hi, wrl_ref[...]))


def _merge(u, o_d, o_f, o_m, o_s, w_b, w_o, x, ln_g, ln_b, wr_hi, wr_lo, alpha):
    t, d = x.shape
    tm = min(256, t)
    row = lambda i: (i, 0)
    const2 = lambda i: (0, 0)
    o_spec = pl.BlockSpec((tm, BRANCH_WIDTH), row)
    return pl.pallas_call(
        functools.partial(_merge_body, alpha=alpha),
        out_shape=(jax.ShapeDtypeStruct((t, d), jnp.float32),
                   jax.ShapeDtypeStruct((t * _chunk_pitch(d // 2), LANES), jnp.uint32),
                   jax.ShapeDtypeStruct((t, d), jnp.bfloat16),
                   jax.ShapeDtypeStruct((t, LANES), jnp.float32)),
        grid=(t // tm,),
        in_specs=[pl.BlockSpec((tm, N_BRANCHES * d), row),
                  o_spec, o_spec, o_spec, o_spec,
                  _resident(w_b.shape, lambda i: (0, 0, 0)),
                  _resident(w_o.shape, const2),
                  pl.BlockSpec((tm, d), row),
                  _resident((1, d), const2), _resident((1, d), const2),
                  _resident(wr_hi.shape, const2), _resident(wr_lo.shape, const2)],
        out_specs=(pl.BlockSpec((tm, d), row),
                   pl.BlockSpec((tm * _chunk_pitch(d // 2), LANES), row),
                   pl.BlockSpec((tm, d), row), pl.BlockSpec((tm, LANES), row)),
        compiler_params=_cparams(("parallel",)),
        name="merge_ln1",
    )(u, o_d, o_f, o_m, o_s, w_b, w_o, x, ln_g, ln_b, wr_hi, wr_lo)


CHUNK_PAD = 2


def _chunk_pitch(d):
    return d // LANES + CHUNK_PAD


def _to_chunked(ref, x):
    n, d = x.shape
    nc, pitch = d // LANES, _chunk_pitch(d)
    for c in range(nc):
        ref[pl.ds(c, n, stride=pitch), :] = x[:, c * LANES:(c + 1) * LANES]
    for c in range(nc, pitch):
        ref[pl.ds(c, n, stride=pitch), :] = jnp.zeros((n, LANES), x.dtype)


def _from_chunked(ref, n, d):
    pitch = _chunk_pitch(d)
    return jnp.concatenate([ref[pl.ds(c, n, stride=pitch), :] for c in range(d // LANES)],
                           axis=1)


def _gather_rows(idx_ref, base, src_hbm, dst, sem, n, d):
    nc, pitch = d // LANES, _chunk_pitch(d)
    for r in range(n):
        row = pl.multiple_of(idx_ref[base + r] * pitch, CHUNK_PAD)
        pltpu.make_async_copy(src_hbm.at[pl.ds(row, nc), :], dst.at[pl.ds(r * pitch, nc), :],
                              sem).start(priority=r % 2)


def _wait_rows(src_hbm, dst, sem, n, d):
    rows = n * (d // LANES)
    pltpu.make_async_copy(src_hbm.at[pl.ds(0, rows), :], dst.at[pl.ds(0, rows), :], sem).wait()


def _pack_bf16_halves(x):
    bits = pltpu.bitcast(x.astype(jnp.float32), jnp.uint32)
    half = x.shape[1] // 2
    return bits[:, :half] | (bits[:, half:] >> 16)


def _unpack_bf16_halves(w):
    left = pltpu.bitcast(w & jnp.uint32(0xFFFF0000), jnp.float32)
    right = pltpu.bitcast(w << 16, jnp.float32)
    return jnp.concatenate([left, right], axis=1).astype(jnp.bfloat16)


def _moe_body(sched_ref, n_used_ref, tok_ref, h_hbm, wg_hbm, wu_hbm, wd_hbm, y_ref,
              xbuf, sem, wg_f, wu_f, wd_f, wsem, wg_b, wu_b, wd_b, *, layer):
    i = pl.program_id(0)
    n_used = n_used_ref[0]
    d = wg_b.shape[0] // 2
    bm = xbuf.shape[1] // _chunk_pitch(d)
    slot = i % 2

    def weight_copies(e, s):
        return (pltpu.make_async_copy(wg_hbm.at[layer, e], wg_f.at[s], wsem.at[s]),
                pltpu.make_async_copy(wu_hbm.at[layer, e], wu_f.at[s], wsem.at[s]),
                pltpu.make_async_copy(wd_hbm.at[layer, e], wd_f.at[s], wsem.at[s]))

    @pl.when((i == 0) & (n_used > 0))
    def _():
        for cp in weight_copies(sched_ref[0, 0], 0):
            cp.start(priority=1)
        _gather_rows(tok_ref, 0, h_hbm, xbuf.at[0], sem.at[0], bm, d)

    @pl.when(i + 1 < n_used)
    def _():
        _gather_rows(tok_ref, (i + 1) * bm, h_hbm, xbuf.at[1 - slot], sem.at[1 - slot], bm, d)

    @pl.when(i < n_used)
    def _():
        @pl.when(sched_ref[1, i] == 1)
        def _():
            ws = sched_ref[2, i]
            for cp in weight_copies(sched_ref[0, i], ws):
                cp.wait()
            wg_b[...] = wg_f[ws].astype(jnp.bfloat16)
            wu_b[...] = wu_f[ws].astype(jnp.bfloat16)
            wd_b[...] = wd_f[ws].astype(jnp.bfloat16)

            @pl.when(sched_ref[3, i] >= 0)
            def _():
                for cp in weight_copies(sched_ref[3, i], 1 - ws):
                    cp.start(priority=1)

        _wait_rows(h_hbm, xbuf.at[slot], sem.at[slot], bm, d)
        x = _unpack_bf16_halves(_from_chunked(xbuf.at[slot], bm, d))
        a = _dot(x, wg_b[...])
        hid = (a * _sigmoid(a)) * _dot(x, wu_b[...])
        _to_chunked(y_ref, _dot(hid.astype(jnp.bfloat16), wd_b[...]))

    @pl.when(i >= n_used)
    def _():
        y_ref[...] = jnp.zeros_like(y_ref)


def _moe(sched, n_used, row_tok, h_rows, w_g, w_u, w_d, layer):
    n_rows = row_tok.shape[0]
    d, hid = w_g.shape[2], w_g.shape[3]
    pitch = _chunk_pitch(d)
    bm = MOE_ROWS
    any_spec = pl.BlockSpec(memory_space=pl.ANY)
    return pl.pallas_call(
        functools.partial(_moe_body, layer=layer),
        out_shape=jax.ShapeDtypeStruct((n_rows * pitch, LANES), jnp.float32),
        grid_spec=pltpu.PrefetchScalarGridSpec(
            num_scalar_prefetch=3, grid=(n_rows // bm,),
            in_specs=[any_spec, any_spec, any_spec, any_spec],
            out_specs=pl.BlockSpec((bm * pitch, LANES), lambda i, s, n, tok: (i, 0)),
            scratch_shapes=[pltpu.VMEM((2, bm * _chunk_pitch(d // 2), LANES), jnp.uint32),
                            pltpu.SemaphoreType.DMA((2,)),
                            pltpu.VMEM((2, d, hid), jnp.float32),
                            pltpu.VMEM((2, d, hid), jnp.float32),
                            pltpu.VMEM((2, hid, d), jnp.float32),
                            pltpu.SemaphoreType.DMA((2,)),
                            pltpu.VMEM((d, hid), jnp.bfloat16),
                            pltpu.VMEM((d, hid), jnp.bfloat16),
                            pltpu.VMEM((hid, d), jnp.bfloat16)]),
        compiler_params=_cparams(("arbitrary",)),
        name="moe_experts",
    )(sched, n_used, row_tok, h_rows, w_g, w_u, w_d)


def _final_body(d0_ref, d1_ref, h_ref, hb_ref, y_hbm, gw_ref, p_ref, wg_ref, wp_ref, lg_ref,
                lb_ref, y_o, yb_o, gbuf, sem, *, alpha):
    i = pl.program_id(0)
    n = pl.num_programs(0)
    tm, d = h_ref.shape
    slot = i % 2

    def gather(step, s):
        _gather_rows(d0_ref, step * tm, y_hbm, gbuf.at[s, 0], sem.at[s], tm, d)
        _gather_rows(d1_ref, step * tm, y_hbm, gbuf.at[s, 1], sem.at[s], tm, d)

    @pl.when(i == 0)
    def _():
        gather(0, 0)

    @pl.when(i + 1 < n)
    def _():
        gather(i + 1, 1 - slot)

    gate = _sigmoid(_dot(hb_ref[...], wg_ref[...]))
    proj = _dot(p_ref[...].astype(jnp.bfloat16), wp_ref[...])
    _wait_rows(y_hbm, gbuf.at[slot, 0], sem.at[slot], tm, d)
    _wait_rows(y_hbm, gbuf.at[slot, 1], sem.at[slot], tm, d)
    gw = gw_ref[...]
    ffn = (_from_chunked(gbuf.at[slot, 0], tm, d) * gw[:, 0:1]
           + _from_chunked(gbuf.at[slot, 1], tm, d) * gw[:, 1:2])
    y = _layer_norm(alpha * h_ref[...] + ffn + gate * proj, lg_ref[...], lb_ref[...])
    y_o[...] = y
    yb_o[...] = y.astype(jnp.bfloat16)


def _final(dest0, dest1, h, hb, y_rows, gate_w, p, w_pg, w_pp, ln_g, ln_b, alpha):
    t, d = h.shape
    tm = min(256, t)
    row = lambda i, *_: (i, 0)
    const2 = lambda i, *_: (0, 0)
    return pl.pallas_call(
        functools.partial(_final_body, alpha=alpha),
        out_shape=(jax.ShapeDtypeStruct((t, d), jnp.float32),
                   jax.ShapeDtypeStruct((t, d), jnp.bfloat16)),
        grid_spec=pltpu.PrefetchScalarGridSpec(
            num_scalar_prefetch=2, grid=(t // tm,),
            in_specs=[pl.BlockSpec((tm, d), row), pl.BlockSpec((tm, d), row),
                      pl.BlockSpec(memory_space=pl.ANY),
                      pl.BlockSpec((tm, TOP_K), row), pl.BlockSpec((tm, p.shape[1]), row),
                      _resident(w_pg.shape, const2), _resident(w_pp.shape, const2),
                      _resident((1, d), const2), _resident((1, d), const2)],
            out_specs=(pl.BlockSpec((tm, d), row), pl.BlockSpec((tm, d), row)),
            scratch_shapes=[pltpu.VMEM((2, TOP_K, tm * _chunk_pitch(d), LANES), jnp.float32),
                            pltpu.SemaphoreType.DMA((2,))]),
        compiler_params=_cparams(("arbitrary",)),
        name="final_ln2",
    )(dest0, dest1, h, hb, y_rows, gate_w, p, w_pg, w_pp, ln_g, ln_b)


def _rope_tables(positions):
    pos = positions.reshape(-1).astype(jnp.float32)
    lane = jnp.arange(LANES)

    def tables(rot, period):
        half = rot // 2
        inv_freq = ROPE_THETA ** (-jnp.arange(half, dtype=jnp.float32) / half)
        ang = pos[:, None] * inv_freq
        cos, sin = jnp.cos(ang), jnp.sin(ang)
        lp = lane % period
        idx = lp % half
        in_rot = lp < rot
        c = jnp.where(in_rot[None, :], cos[:, idx], 1.0)
        s_up = jnp.where((lp < half)[None, :], -sin[:, idx], 0.0)
        s_dn = jnp.where(((lp >= half) & in_rot)[None, :], sin[:, idx], 0.0)
        return c, s_up, s_dn

    return tables(DIFF_ROT, DIFF_QK) + tables(MLA_ROPE, LANES)


def _w_in_segments(d):
    widths = [512, 512, 512, 512, 512, 512, HEADS, MLA_Q_LORA, MLA_KV_LORA, MLA_ROPE,
              512, 512, 512, N_BRANCHES * d]
    dsts = [COL_DQ, COL_DQ + 512, COL_DV, COL_FQ, COL_FK, COL_FV, COL_FF, COL_CQ, COL_CKV,
            COL_KR, COL_SQ, COL_SK, COL_SV, COL_GATES]
    segs, src = [], 0
    for wd, dst in zip(widths, dsts):
        segs.append((src, dst, wd))
        src += wd
    return segs


def _w_in_body(src_ref, valid_ref, wt_hbm, o_ref, buf, sem, *, layer):
    b = pl.program_id(0)
    slot = b % 2
    rows = buf.shape[1]

    def copy(blk, s):
        return pltpu.make_async_copy(wt_hbm.at[pl.ds(src_ref[blk], rows), layer, :],
                                     buf.at[s], sem.at[s])

    @pl.when(b == 0)
    def _():
        copy(0, 0).start()

    @pl.when(b + 1 < pl.num_programs(0))
    def _():
        copy(b + 1, 1 - slot).start()

    copy(b, slot).wait()
    r = lax.broadcasted_iota(jnp.int32, buf.shape[1:], 0)
    o_ref[...] = jnp.where(r < valid_ref[b], buf[slot], 0.0).astype(o_ref.dtype)


def _reorder_w_in(w_t, layer):
    n_in, _, d = w_t.shape
    src, valid = [0] * (U_WIDTH // LANES), [0] * (U_WIDTH // LANES)
    for s0, dst, wd in _w_in_segments(d):
        for off in range(0, wd, LANES):
            blk = (dst + off) // LANES
            src[blk], valid[blk] = s0 + off, min(LANES, wd - off)
    assert max(s + LANES for s in src) <= n_in
    return pl.pallas_call(
        functools.partial(_w_in_body, layer=layer),
        out_shape=jax.ShapeDtypeStruct((U_WIDTH, d), jnp.bfloat16),
        grid_spec=pltpu.PrefetchScalarGridSpec(
            num_scalar_prefetch=2, grid=(U_WIDTH // LANES,),
            in_specs=[pl.BlockSpec(memory_space=pl.ANY)],
            out_specs=pl.BlockSpec((LANES, d), lambda b, s, v: (b, 0)),
            scratch_shapes=[pltpu.VMEM((2, LANES, d), jnp.float32),
                            pltpu.SemaphoreType.DMA((2,))]),
        compiler_params=_cparams(("arbitrary",)),
        name="w_in_relayout",
    )(jnp.asarray(src, jnp.int32), jnp.asarray(valid, jnp.int32), w_t)


def _route(logits, b_rg, b_re, t):
    def first_max(v, n):
        top = jnp.max(v, axis=-1, keepdims=True)
        ids = jnp.arange(n, dtype=jnp.int32)
        return top, jnp.min(jnp.where(v == top, ids, n), axis=-1, keepdims=True)

    g_logits = logits[:, :N_GROUPS] + b_rg
    g_max, grp = first_max(g_logits, N_GROUPS)
    p_grp = 1.0 / jnp.sum(jnp.exp(g_logits - g_max), axis=-1, keepdims=True)
    e_logits = (logits[:, N_GROUPS:N_GROUPS + N_EXPERTS] + b_re).reshape(
        t, N_GROUPS, EXPERTS_PER_GROUP)
    in_grp = jnp.arange(N_GROUPS, dtype=jnp.int32)[None, :, None] == grp[:, :, None]
    e_in = jnp.sum(jnp.where(in_grp, e_logits, 0.0), axis=1)
    probs = jax.nn.softmax(e_in, axis=-1)
    p1, i1 = first_max(probs, EXPERTS_PER_GROUP)
    rest = jnp.where(jnp.arange(EXPERTS_PER_GROUP, dtype=jnp.int32)[None, :] == i1, -1.0, probs)
    p2, i2 = first_max(rest, EXPERTS_PER_GROUP)
    top_p = jnp.concatenate([p1, p2], axis=-1)
    weights = p_grp * top_p / jnp.sum(top_p, axis=-1, keepdims=True)
    expert_idx = grp * EXPERTS_PER_GROUP + jnp.concatenate([i1, i2], axis=-1)
    return expert_idx.astype(jnp.int32), weights


def _dispatch_plan(expert_idx, t):
    a = t * TOP_K
    flat_e = expert_idx.reshape(a)
    onehot = (flat_e[:, None] == jnp.arange(N_EXPERTS)[None, :]).astype(jnp.int32)
    sizes = jnp.sum(onehot, axis=0)
    padded = (sizes + MOE_ROWS - 1) // MOE_ROWS * MOE_ROWS
    pad_end = jnp.cumsum(padded)
    pad_start = pad_end - padded
    dest = (jnp.sum(onehot * (jnp.cumsum(onehot, axis=0) + pad_start[None, :]), axis=1)
            - 1).astype(jnp.int32)
    n_rows = a + N_EXPERTS * MOE_ROWS
    n_blk = n_rows // MOE_ROWS
    flat_tok = jnp.arange(a, dtype=jnp.int32) // TOP_K
    row_tok = jnp.zeros((n_rows,), jnp.int32).at[dest].set(flat_tok)
    blk_start = jnp.arange(n_blk, dtype=pad_end.dtype) * MOE_ROWS
    blk_e = jnp.minimum(jnp.sum(pad_end[None, :] <= blk_start[:, None], axis=1),
                        N_EXPERTS - 1).astype(jnp.int32)
    n_used = (pad_end[-1] // MOE_ROWS).astype(jnp.int32).reshape(1)
    first = jnp.concatenate([jnp.ones((1,), jnp.int32),
                             (blk_e[1:] != blk_e[:-1]).astype(jnp.int32)])
    parity = (jnp.cumsum(first) - 1) % 2
    experts = jnp.arange(N_EXPERTS, dtype=jnp.int32)
    used = jnp.where(padded > 0, experts, N_EXPERTS)
    later = jnp.flip(lax.cummin(jnp.flip(used)))
    next_used = jnp.concatenate([later[1:], jnp.full((1,), N_EXPERTS, jnp.int32)])
    next_used = jnp.where(next_used < N_EXPERTS, next_used, -1)
    sched = jnp.stack([blk_e, first, parity.astype(jnp.int32),
                       next_used[blk_e].astype(jnp.int32)])
    return dest.reshape(t, TOP_K), row_tok, sched, n_used


def kernel(x, p, positions, w_in, fox_f_bias, diff_lambda, diff_subln, mla_q_norm, mla_kv_norm,
           mla_w_uq, mla_w_ukv, w_branch, w_o, ln1_g, ln1_b, w_router_group, b_router_group,
           w_router_expert, b_router_expert, w_expert_gate, w_expert_up, w_expert_down,
           w_ple_gate, w_ple_proj, ln2_g, ln2_b):
    batch, seq, d = x.shape
    depth = w_in.shape[0]
    t = batch * seq
    alpha = (2 * depth) ** 0.25
    bf = jnp.bfloat16
    tabs = _rope_tables(positions)
    w_t = jnp.transpose(w_in, (2, 0, 1))
    col_scale = jnp.ones((1, U_WIDTH), jnp.float32)
    col_scale = col_scale.at[0, COL_FQ:COL_FQ + BRANCH_WIDTH].set(HEAD_DIM ** -0.5 * LOG2E)
    col_scale = col_scale.at[0, COL_SQ:COL_SQ + BRANCH_WIDTH].set(HEAD_DIM ** -0.5)
    xf = x.reshape(t, d)
    xb = xf.astype(bf)

    for i in range(depth):
        w1 = _reorder_w_in(w_t, i)
        u = _matmul(xb, w1, col_scale, bf, 1024, 1536, "in_proj")

        b_f = jnp.zeros((1, LANES), jnp.float32).at[0, :HEADS].set(fox_f_bias[i])
        cum = _fgate_cumsum(xb, w1, b_f, batch, seq)[:, :HEADS] * LOG2E
        cum_bh = jnp.transpose(cum.reshape(batch, seq, HEADS), (0, 2, 1)).reshape(
            batch * HEADS, seq)
        cq_rows = cum_bh.reshape(batch * HEADS, 1, seq)
        ck_rep = jnp.broadcast_to(cum_bh[:, :, None], (batch * HEADS, seq, LANES))

        uq = mla_w_uq[i].reshape(MLA_Q_LORA, HEADS, MLA_NOPE + MLA_ROPE)
        uq = jnp.pad(uq, ((0, 0), (0, 0), (0, MLA_QK_PAD - MLA_NOPE - MLA_ROPE)))
        uq = uq.reshape(MLA_Q_LORA, HEADS * MLA_QK_PAD).astype(bf)
        ukv = mla_w_ukv[i].reshape(MLA_KV_LORA, HEADS, MLA_NOPE + HEAD_DIM)
        uk = ukv[:, :, :MLA_NOPE].reshape(MLA_KV_LORA, HEADS * MLA_NOPE).astype(bf)
        uv = ukv[:, :, MLA_NOPE:].reshape(MLA_KV_LORA, HEADS * HEAD_DIM).astype(bf)
        dqk, mq, mk, mv = _prep(u, tabs, mla_q_norm[i].reshape(1, -1),
                                mla_kv_norm[i].reshape(1, -1), uq, uk, uv)

        lam_init = 0.8 - 0.6 * math.exp(-0.3 * i)
        lp = diff_lambda[i].astype(jnp.float32)
        lam = (jnp.exp(jnp.sum(lp[0] * lp[1])) - jnp.exp(jnp.sum(lp[2] * lp[3]))
               + lam_init).reshape(1)
        blk = lambda col: col // HEAD_DIM
        o_d = _attention("diff", batch, seq, dqk, 0, dqk, HEADS, u, blk(COL_DV), HEAD_DIM,
                         extras=(diff_subln[i].reshape(1, HEAD_DIM),),
                         lam=lam, lam_init=lam_init)
        o_f = _attention("fox", batch, seq, u, blk(COL_FQ), u, blk(COL_FK), u, blk(COL_FV),
                         HEAD_DIM, extras=(cq_rows, ck_rep))
        o_m = _attention("mla", batch, seq, mq, 0, mk, 0, mv, 0, MLA_QK_PAD)
        o_s = _attention("sb", batch, seq, u, blk(COL_SQ), u, blk(COL_SK), u, blk(COL_SV),
                         HEAD_DIM)

        w_r = jnp.concatenate([w_router_group[i], w_router_expert[i]], axis=1)
        w_r = jnp.pad(w_r, ((0, 0), (0, LANES - w_r.shape[1])))
        wr_hi = w_r.astype(bf)
        wr_lo = (w_r - wr_hi.astype(jnp.float32)).astype(bf)
        h, h_rows, hb, logits = _merge(
            u, o_d, o_f, o_m, o_s, w_branch[i].astype(bf), w_o[i].astype(bf), xf,
            ln1_g[i].reshape(1, d), ln1_b[i].reshape(1, d), wr_hi, wr_lo, alpha)

        expert_idx, weights = _route(logits, b_router_group[i], b_router_expert[i], t)
        dest, row_tok, sched, n_used = _dispatch_plan(expert_idx, t)
        y_rows = _moe(sched, n_used, row_tok, h_rows, w_expert_gate, w_expert_up,
                      w_expert_down, i)
        xf, xb = _final(dest[:, 0], dest[:, 1], h, hb, y_rows, weights.astype(jnp.float32),
                        p[i].reshape(t, -1), w_ple_gate[i].astype(bf),
                        w_ple_proj[i].astype(bf), ln2_g[i].reshape(1, d),
                        ln2_b[i].reshape(1, d), alpha)
    return xf.reshape(batch, seq, d)
```

```python
import functools
import math

import jax
import jax.numpy as jnp
from jax import lax
from jax.experimental import pallas as pl
from jax.experimental.pallas import tpu as pltpu

HEAD_DIM = 128
HEADS = 4
N_BRANCHES = 4
BRANCH_WIDTH = HEADS * HEAD_DIM
DIFF_QK = 64
DIFF_ROT = DIFF_QK // 4
MLA_Q_LORA = 512
MLA_KV_LORA = 256
MLA_NOPE = 128
MLA_ROPE = 64
MLA_QK_PAD = 256
N_GROUPS = 4
EXPERTS_PER_GROUP = 8
N_EXPERTS = N_GROUPS * EXPERTS_PER_GROUP
TOP_K = 2
CHUNK = 64
ROPE_THETA = 500000.0
LN_EPS = 1e-5
RMS_EPS = 1e-6
NEG_BIG = -1e30
LOG2E = 1.4426950408889634
SB_EXP_UNDERFLOW = -104.0
LANES = 128
V7X_VMEM_LIMIT = 56 * 1024 * 1024

COL_GATES = 0
COL_DQ = 8192
COL_DV = COL_DQ + 1024
COL_FQ = COL_DV + 512
COL_FK = COL_FQ + 512
COL_FV = COL_FK + 512
COL_SQ = COL_FV + 512
COL_SK = COL_SQ + 512
COL_SV = COL_SK + 512
COL_CQ = COL_SV + 512
COL_CKV = COL_CQ + 512
COL_KR = COL_CKV + 256
COL_FF = COL_KR + 128
U_WIDTH = COL_FF + 128

ATT_TQ = 512
ATT_TQ_WIDE = 1024
MOE_ROWS = 256


def _cparams(sem, vmem=V7X_VMEM_LIMIT):
    return pltpu.CompilerParams(dimension_semantics=sem, vmem_limit_bytes=vmem)


def _resident(shape, index_map):
    return pl.BlockSpec(shape, index_map, pipeline_mode=pl.Buffered(1))


def _matmul_body(x_ref, w_ref, c_ref, o_ref):
    acc = lax.dot_general(x_ref[...], w_ref[...], (((1,), (1,)), ((), ())),
                          preferred_element_type=jnp.float32)
    o_ref[...] = (acc * c_ref[...]).astype(o_ref.dtype)


def _matmul(x, w_t, col_scale, out_dtype, tm, tn, name):
    m, k = x.shape
    n = w_t.shape[0]
    tm, tn = min(tm, m), min(tn, n)
    return pl.pallas_call(
        _matmul_body,
        out_shape=jax.ShapeDtypeStruct((m, n), out_dtype),
        grid=(m // tm, n // tn),
        in_specs=[pl.BlockSpec((tm, k), lambda i, j: (i, 0)),
                  pl.BlockSpec((tn, k), lambda i, j: (j, 0)),
                  pl.BlockSpec((1, tn), lambda i, j: (0, j))],
        out_specs=pl.BlockSpec((tm, tn), lambda i, j: (i, j)),
        compiler_params=_cparams(("parallel", "arbitrary")),
        name=name,
    )(x, w_t, col_scale)


def _split3(x):
    hi = x.astype(jnp.bfloat16)
    r1 = x - hi.astype(jnp.float32)
    mid = r1.astype(jnp.bfloat16)
    lo = (r1 - mid.astype(jnp.float32)).astype(jnp.bfloat16)
    return hi, mid, lo


def _split2(x):
    hi = x.astype(jnp.bfloat16)
    mid = (x - hi.astype(jnp.float32)).astype(jnp.bfloat16)
    return hi, mid


def _dot(a, b):
    return jnp.dot(a, b, preferred_element_type=jnp.float32)


def _dot_nt(a, b):
    return lax.dot_general(a, b, (((1,), (1,)), ((), ())), preferred_element_type=jnp.float32)


def _dot_tn(a, b):
    return lax.dot_general(a, b, (((0,), (0,)), ((), ())), preferred_element_type=jnp.float32)


def _log_sigmoid(z):
    neg_abs = pltpu.bitcast(pltpu.bitcast(z, jnp.uint32) | jnp.uint32(0x80000000), jnp.float32)
    return jnp.minimum(z, 0.0) - jnp.log(1.0 + jnp.exp(neg_abs))


def _sigmoid(z):
    return 1.0 / (1.0 + jnp.exp(-z))


def _fgate_body(x_ref, w_ref, b_ref, o_ref, carry_ref):
    @pl.when(pl.program_id(1) == 0)
    def _():
        carry_ref[...] = jnp.zeros_like(carry_ref)

    f = _dot_nt(x_ref[...], w_ref[...]) + b_ref[...]
    log_f = _log_sigmoid(f)
    ts = log_f.shape[0]
    row = lax.broadcasted_iota(jnp.int32, (ts, ts), 0)
    col = lax.broadcasted_iota(jnp.int32, (ts, ts), 1)
    tri = (col <= row).astype(jnp.bfloat16)
    hi, mid, lo = _split3(log_f)
    cum = _dot(tri, hi) + _dot(tri, mid) + _dot(tri, lo) + carry_ref[...]
    o_ref[...] = cum
    carry_ref[...] = cum[ts - 1:ts, :]


def _fgate_cumsum(xb, w1_t, b_f, batch, seq):
    t, d = xb.shape
    ts = min(512, seq)
    ns = seq // ts
    return pl.pallas_call(
        _fgate_body,
        out_shape=jax.ShapeDtypeStruct((t, LANES), jnp.float32),
        grid=(batch, ns),
        in_specs=[pl.BlockSpec((ts, d), lambda b, s: (b * ns + s, 0)),
                  _resident((LANES, d), lambda b, s: (COL_FF // LANES, 0)),
                  _resident((1, LANES), lambda b, s: (0, 0))],
        out_specs=pl.BlockSpec((ts, LANES), lambda b, s: (b * ns + s, 0)),
        scratch_shapes=[pltpu.VMEM((1, LANES), jnp.float32)],
        compiler_params=_cparams(("parallel", "arbitrary")),
        name="fgate_cumsum",
    )(xb, w1_t, b_f)


def _rope_lanes(x, c, s_up, s_dn, half):
    return x * c + pltpu.roll(x, LANES - half, 1) * s_up + pltpu.roll(x, half, 1) * s_dn


def _prep_body(dqk_ref, cq_ref, ckv_ref, kr_ref, c16_ref, su16_ref, sd16_ref,
               c64_ref, su64_ref, sd64_ref, qg_ref, kvg_ref, wuq_ref, wuk_ref, wuv_ref,
               dqk_o, mq_o, mk_o, mv_o):
    c16, su16, sd16 = c16_ref[...], su16_ref[...], sd16_ref[...]
    c64, su64, sd64 = c64_ref[...], su64_ref[...], sd64_ref[...]
    for j in range(dqk_ref.shape[1] // LANES):
        sl = slice(j * LANES, (j + 1) * LANES)
        x = _rope_lanes(dqk_ref[:, sl].astype(jnp.float32), c16, su16, sd16, DIFF_ROT // 2)
        if j < HEADS:
            x = x * (DIFF_QK ** -0.5 * LOG2E)
        dqk_o[:, sl] = x.astype(dqk_o.dtype)

    cq = cq_ref[...].astype(jnp.float32)
    cqn = cq * lax.rsqrt(jnp.mean(cq * cq, axis=-1, keepdims=True) + RMS_EPS) * qg_ref[...]
    q = _dot(cqn.astype(jnp.bfloat16), wuq_ref[...])
    q = q * ((MLA_NOPE + MLA_ROPE) ** -0.5 * LOG2E)
    ckv = ckv_ref[...].astype(jnp.float32)
    ckvn = (ckv * lax.rsqrt(jnp.mean(ckv * ckv, axis=-1, keepdims=True) + RMS_EPS)
            * kvg_ref[...]).astype(jnp.bfloat16)
    kn = _dot(ckvn, wuk_ref[...])
    mv_o[...] = _dot(ckvn, wuv_ref[...]).astype(mv_o.dtype)
    kr = _rope_lanes(kr_ref[...].astype(jnp.float32), c64, su64, sd64, MLA_ROPE // 2)
    for h in range(HEADS):
        base = h * MLA_QK_PAD
        mq_o[:, base:base + LANES] = q[:, base:base + LANES].astype(mq_o.dtype)
        qr = _rope_lanes(q[:, base + LANES:base + 2 * LANES], c64, su64, sd64, MLA_ROPE // 2)
        mq_o[:, base + LANES:base + 2 * LANES] = qr.astype(mq_o.dtype)
        mk_o[:, base:base + LANES] = kn[:, h * LANES:(h + 1) * LANES].astype(mk_o.dtype)
        mk_o[:, base + LANES:base + 2 * LANES] = kr.astype(mk_o.dtype)


def _prep(u, tabs, q_norm, kv_norm, w_uq, w_uk, w_uv):
    t = u.shape[0]
    tm = min(512, t)
    row = lambda blk: (lambda i: (i, blk))
    const = lambda i: (0, 0)
    tab_spec = pl.BlockSpec((tm, LANES), lambda i: (i, 0))
    outs = pl.pallas_call(
        _prep_body,
        out_shape=(jax.ShapeDtypeStruct((t, 1024), jnp.bfloat16),
                   jax.ShapeDtypeStruct((t, HEADS * MLA_QK_PAD), jnp.bfloat16),
                   jax.ShapeDtypeStruct((t, HEADS * MLA_QK_PAD), jnp.bfloat16),
                   jax.ShapeDtypeStruct((t, BRANCH_WIDTH), jnp.bfloat16)),
        grid=(t // tm,),
        in_specs=[pl.BlockSpec((tm, 1024), row(COL_DQ // 1024)),
                  pl.BlockSpec((tm, MLA_Q_LORA), row(COL_CQ // MLA_Q_LORA)),
                  pl.BlockSpec((tm, MLA_KV_LORA), row(COL_CKV // MLA_KV_LORA)),
                  pl.BlockSpec((tm, LANES), row(COL_KR // LANES)),
                  tab_spec, tab_spec, tab_spec, tab_spec, tab_spec, tab_spec,
                  _resident((1, MLA_Q_LORA), const),
                  _resident((1, MLA_KV_LORA), const),
                  _resident(w_uq.shape, const),
                  _resident(w_uk.shape, const),
                  _resident(w_uv.shape, const)],
        out_specs=(pl.BlockSpec((tm, 1024), lambda i: (i, 0)),
                   pl.BlockSpec((tm, HEADS * MLA_QK_PAD), lambda i: (i, 0)),
                   pl.BlockSpec((tm, HEADS * MLA_QK_PAD), lambda i: (i, 0)),
                   pl.BlockSpec((tm, BRANCH_WIDTH), lambda i: (i, 0))),
        compiler_params=_cparams(("parallel",)),
        name="prep",
    )(u, u, u, u, *tabs, q_norm, kv_norm, w_uq, w_uk, w_uv)
    return outs


def _attn_body(*refs, kind, tq, tk, lam_init):
    softmax = kind != "sb"
    if kind == "diff":
        lam_ref, q_ref, k_ref, v_ref, g_ref, o_ref = refs[:6]
    elif kind == "fox":
        q_ref, k_ref, v_ref, cq_ref, ck_ref, o_ref = refs[:6]
    else:
        q_ref, k_ref, v_ref, o_ref = refs[:4]
    if softmax:
        s_bufs, p_bufs = refs[-8:-6], refs[-6:-4]
        vt_ref, m_ref, a_ref, acc_ref = refs[-4:]
    else:
        s_bufs, p_bufs = refs[-7:-5], refs[-5:-3]
        vt_ref, r_ref, acc_ref = refs[-3:]
    qi = pl.program_id(2)

    @pl.when(qi == 0)
    def _():
        for kb in range(vt_ref.shape[0]):
            vt_ref[kb, 0:HEAD_DIM, :] = v_ref[kb * tk:(kb + 1) * tk, :].T
            if softmax:
                vt_ref[kb, HEAD_DIM:, :] = jnp.ones((vt_ref.shape[1] - HEAD_DIM, tk),
                                                    vt_ref.dtype)

    q0 = qi * tq
    last_kb = 2 * qi + 1
    q = q_ref[...]
    n_comp = 2 if kind == "diff" else 1
    if kind == "diff":
        lane = lax.broadcasted_iota(jnp.int32, q.shape, 1)
        qs = [jnp.where(lane < DIFF_QK, q, jnp.zeros_like(q)),
              jnp.where(lane >= DIFF_QK, q, jnp.zeros_like(q))]
    else:
        qs = [q]

    acc_ref[...] = jnp.zeros_like(acc_ref)
    p_bufs[1][...] = jnp.zeros_like(p_bufs[1])
    if softmax:
        m_ref[...] = jnp.full_like(m_ref, NEG_BIG)
        a_ref[...] = jnp.zeros_like(a_ref)
    else:
        r_ref[...] = jnp.zeros_like(r_ref)

    def key_rows(kb):
        if isinstance(kb, int):
            return pl.ds(kb * tk, tk)
        return pl.ds(pl.multiple_of(kb * tk, tk), tk)

    def scores_stage(kb, s_dst):
        k_t = k_ref[key_rows(kb), :]
        for c in range(n_comp):
            s_dst[c] = _dot_nt(k_t, qs[c])

    def values_stage(kb, p_src):
        v_t = vt_ref[kb]
        for c in range(n_comp):
            pv = _dot(v_t, p_src[c])[:acc_ref.shape[1]]
            if softmax:
                acc_ref[c] = a_ref[c] * acc_ref[c] + pv
            else:
                acc_ref[c] = acc_ref[c] + pv

    def weights_stage(kb, s_src, p_dst, masked):
        if masked:
            j = lax.broadcasted_iota(jnp.int32, (tk, tq), 0)
            i = lax.broadcasted_iota(jnp.int32, (tk, tq), 1)
            off = q0 - kb * tk
            if kind in ("diff", "mla"):
                mask = (j // CHUNK - i // CHUNK) <= off // CHUNK
            elif kind == "fox":
                mask = (j - i) <= off
            else:
                mask = (j - i) < off
        if not softmax:
            z_t = s_src[0]
            ls = _log_sigmoid(z_t)
            lk = ls - z_t
            if masked:
                lk = jnp.where(mask, lk, 0.0)
            jj = lax.broadcasted_iota(jnp.int32, (tk, tk), 0)
            mm = lax.broadcasted_iota(jnp.int32, (tk, tk), 1)
            upper = (mm > jj).astype(jnp.bfloat16)
            hi, mid = _split2(lk)
            suffix = _dot(jnp.concatenate([upper, upper], axis=1),
                          jnp.concatenate([hi, mid], axis=0))
            r_old = r_ref[...]
            w_t = jnp.exp(ls + (r_old + suffix))
            if masked:
                w_t = jnp.where(mask, w_t, 0.0)
            p_dst[0] = w_t.astype(jnp.bfloat16)
            r_ref[...] = r_old + suffix[0:1, :] + lk[0:1, :]
            return
        for c in range(n_comp):
            s_t = s_src[c]
            if kind == "fox":
                ck = ck_ref[key_rows(kb), :]
                s_t = s_t + cq_ref[...] - jnp.concatenate([ck] * (tq // LANES), axis=1)
            if masked:
                s_t = jnp.where(mask, s_t, NEG_BIG)
            m_old = m_ref[c]
            m_new = jnp.maximum(m_old, jnp.max(s_t, axis=0, keepdims=True))
            alpha = jnp.exp2(m_old - m_new)
            p_t = jnp.exp2(s_t - m_new)
            m_ref[c] = m_new
            a_ref[c] = alpha
            p_dst[c] = p_t.astype(jnp.bfloat16)

    def step(kb, parity, masked, next_kb, prev_kb):
        if next_kb is not None:
            scores_stage(next_kb, s_bufs[1 - parity])
        values_stage(prev_kb, p_bufs[1 - parity])
        weights_stage(kb, s_bufs[parity], p_bufs[parity], masked)

    if softmax and tq == 2 * tk:
        scores_stage(0, s_bufs[0])

        def body(t, carry):
            kb = 2 * t
            step(kb, 0, False, kb + 1, jnp.maximum(kb - 1, 0))
            step(kb + 1, 1, False, kb + 2, kb)
            return carry
        lax.fori_loop(0, qi, body, 0)
        step(last_kb - 1, 0, True, last_kb, jnp.maximum(last_kb - 2, 0))
        step(last_kb, 1, True, None, last_kb - 1)
        values_stage(last_kb, p_bufs[1])
    elif softmax:
        scores_stage(0, s_bufs[0])

        def body(t, carry):
            kb = 2 * t
            step(kb, 0, False, kb + 1, jnp.maximum(kb - 1, 0))
            step(kb + 1, 1, False, kb + 2, kb)
            return carry
        lax.fori_loop(0, qi // 2, body, 0)

        @pl.when(qi % 2 == 0)
        def _():
            step(qi, 0, True, None, jnp.maximum(qi - 1, 0))
            values_stage(qi, p_bufs[0])

        @pl.when(qi % 2 == 1)
        def _():
            step(qi - 1, 0, False, qi, jnp.maximum(qi - 2, 0))
            step(qi, 1, True, None, qi - 1)
            values_stage(qi, p_bufs[1])
    else:
        scores_stage(last_kb, s_bufs[0])
        step(last_kb, 0, True, last_kb - 1, last_kb)
        step(last_kb - 1, 1, True, jnp.maximum(last_kb - 2, 0), last_kb)

        def live(carry):
            t, _, r_max = carry
            return (t < qi) & (r_max >= SB_EXP_UNDERFLOW)

        def body(carry):
            t = carry[0]
            kb = last_kb - 2 - 2 * t
            step(kb, 0, False, kb - 1, kb + 1)
            step(kb - 1, 1, False, jnp.maximum(kb - 2, 0), kb)
            return t + 1, kb - 1, jnp.max(r_ref[...])
        done = lax.while_loop(live, body, (0, last_kb - 1, jnp.max(r_ref[...])))
        values_stage(done[1], p_bufs[1])

    def normalised(c):
        return acc_ref[c, 0:HEAD_DIM, :] / acc_ref[c, HEAD_DIM:HEAD_DIM + 1, :]

    if kind == "sb":
        o_t = acc_ref[0]
    elif kind == "diff":
        o_t = normalised(0) - lam_ref[0] * normalised(1)
    else:
        o_t = normalised(0)
    o = o_t.T
    if kind == "diff":
        o = (o * lax.rsqrt(jnp.mean(o * o, axis=-1, keepdims=True) + RMS_EPS)
             * g_ref[...] * (1.0 - lam_init))
    o_ref[...] = o.astype(o_ref.dtype)


def _attention(kind, batch, seq, q_arr, q_col, k_arr, k_col, v_arr, v_col, dq,
               extras=(), lam=None, lam_init=0.0):
    t = q_arr.shape[0]
    tq = min(ATT_TQ_WIDE if kind in ("diff", "mla") else ATT_TQ, seq)
    tk = tq // 2
    nq = seq // tq
    q_spec = pl.BlockSpec((tq, dq), lambda b, h, i, *_: (b * nq + i, q_col + h))
    k_spec = pl.BlockSpec((seq, dq), lambda b, h, i, *_: (b, k_col + h))
    v_spec = pl.BlockSpec((seq, HEAD_DIM), lambda b, h, i, *_: (b, v_col + h))
    in_specs = [q_spec, k_spec, v_spec]
    args = [q_arr, k_arr, v_arr]
    n_comp = 2 if kind == "diff" else 1
    if kind == "diff":
        in_specs.append(_resident((1, HEAD_DIM), lambda b, h, i, *_: (0, 0)))
        args.append(extras[0])
    elif kind == "fox":
        cq_rows, ck_rep = extras
        in_specs.append(pl.BlockSpec((None, 1, tq), lambda b, h, i: (b * HEADS + h, 0, i)))
        in_specs.append(pl.BlockSpec((None, seq, LANES), lambda b, h, i: (b * HEADS + h, 0, 0)))
        args += [cq_rows, ck_rep]
    s_buf = pltpu.VMEM((n_comp, tk, tq), jnp.float32)
    p_buf = pltpu.VMEM((n_comp, tk, tq), jnp.bfloat16)
    row = pltpu.VMEM((n_comp, 1, tq), jnp.float32)
    nk = seq // tk
    if kind == "sb":
        scratch = [s_buf, s_buf, p_buf, p_buf,
                   pltpu.VMEM((nk, HEAD_DIM, tk), jnp.bfloat16),
                   pltpu.VMEM((1, tq), jnp.float32),
                   pltpu.VMEM((1, HEAD_DIM, tq), jnp.float32)]
    else:
        scratch = [s_buf, s_buf, p_buf, p_buf,
                   pltpu.VMEM((nk, HEAD_DIM + 16, tk), jnp.bfloat16), row, row,
                   pltpu.VMEM((n_comp, HEAD_DIM + 8, tq), jnp.float32)]
    body = functools.partial(_attn_body, kind=kind, tq=tq, tk=tk, lam_init=lam_init)
    grid = (batch, HEADS, nq)
    out_shape = jax.ShapeDtypeStruct((t, BRANCH_WIDTH), jnp.bfloat16)
    out_spec = pl.BlockSpec((tq, HEAD_DIM), lambda b, h, i, *_: (b * nq + i, h))
    cp = _cparams(("parallel", "parallel", "arbitrary"))
    if kind == "diff":
        return pl.pallas_call(
            body, out_shape=out_shape,
            grid_spec=pltpu.PrefetchScalarGridSpec(
                num_scalar_prefetch=1, grid=grid, in_specs=in_specs, out_specs=out_spec,
                scratch_shapes=scratch),
            compiler_params=cp, name="attn_" + kind)(lam, *args)
    return pl.pallas_call(
        body, out_shape=out_shape, grid=grid, in_specs=in_specs, out_specs=out_spec,
        scratch_shapes=scratch, compiler_params=cp, name="attn_" + kind)(*args)


def _layer_norm(y, g, b):
    mu = jnp.mean(y, axis=-1, keepdims=True)
    yc = y - mu
    var = jnp.mean(yc * yc, axis=-1, keepdims=True)
    return yc * lax.rsqrt(var + LN_EPS) * g + b


def _merge_body(g_ref, od_ref, of_ref, om_ref, os_ref, wb_ref, wo_ref, x_ref, lg_ref, lb_ref,
                wrh_ref, wrl_ref, h_o, hr_o, hb_o, lgt_o, *, alpha):
    d = x_ref.shape[1]
    merged = None
    for n, o_ref in enumerate((od_ref, of_ref, om_ref, os_ref)):
        gate = _sigmoid(g_ref[:, n * d:(n + 1) * d].astype(jnp.float32))
        term = gate * _dot(o_ref[...], wb_ref[n])
        merged = term if merged is None else merged + term
    mix = _dot(merged.astype(jnp.bfloat16), wo_ref[...])
    h = _layer_norm(alpha * x_ref[...] + mix, lg_ref[...], lb_ref[...])
    h_o[...] = h
    h_hi = h.astype(jnp.bfloat16)
    hb_o[...] = h_hi
    _to_chunked(hr_o, _pack_bf16_halves(h_hi))
    h_lo = (h - h_hi.astype(jnp.float32)).astype(jnp.bfloat16)
    lgt_o[...] = (_dot(h_hi, wrh_ref[...]) + _dot(h_lo, wrh_ref[...]) + _dot(h_hi, wrl_ref[...]))


def _merge(u, o_d, o_f, o_m, o_s, w_b, w_o, x, ln_g, ln_b, wr_hi, wr_lo, alpha):
    t, d = x.shape
    tm = min(256, t)
    row = lambda i: (i, 0)
    const2 = lambda i: (0, 0)
    o_spec = pl.BlockSpec((tm, BRANCH_WIDTH), row)
    return pl.pallas_call(
        functools.partial(_merge_body, alpha=alpha),
        out_shape=(jax.ShapeDtypeStruct((t, d), jnp.float32),
                   jax.ShapeDtypeStruct((t * _chunk_pitch(d // 2), LANES), jnp.uint32),
                   jax.ShapeDtypeStruct((t, d), jnp.bfloat16),
                   jax.ShapeDtypeStruct((t, LANES), jnp.float32)),
        grid=(t // tm,),
        in_specs=[pl.BlockSpec((tm, N_BRANCHES * d), row),
                  o_spec, o_spec, o_spec, o_spec,
                  _resident(w_b.shape, lambda i: (0, 0, 0)),
                  _resident(w_o.shape, const2),
                  pl.BlockSpec((tm, d), row),
                  _resident((1, d), const2), _resident((1, d), const2),
                  _resident(wr_hi.shape, const2), _resident(wr_lo.shape, const2)],
        out_specs=(pl.BlockSpec((tm, d), row),
                   pl.BlockSpec((tm * _chunk_pitch(d // 2), LANES), row),
                   pl.BlockSpec((tm, d), row), pl.BlockSpec((tm, LANES), row)),
        compiler_params=_cparams(("parallel",)),
        name="merge_ln1",
    )(u, o_d, o_f, o_m, o_s, w_b, w_o, x, ln_g, ln_b, wr_hi, wr_lo)


CHUNK_PAD = 2


def _chunk_pitch(d):
    return d // LANES + CHUNK_PAD


def _to_chunked(ref, x):
    n, d = x.shape
    nc, pitch = d // LANES, _chunk_pitch(d)
    for c in range(nc):
        ref[pl.ds(c, n, stride=pitch), :] = x[:, c * LANES:(c + 1) * LANES]
    for c in range(nc, pitch):
        ref[pl.ds(c, n, stride=pitch), :] = jnp.zeros((n, LANES), x.dtype)


def _from_chunked(ref, n, d):
    pitch = _chunk_pitch(d)
    return jnp.concatenate([ref[pl.ds(c, n, stride=pitch), :] for c in range(d // LANES)],
                           axis=1)


def _gather_rows(idx_ref, base, src_hbm, dst, sem, n, d):
    nc, pitch = d // LANES, _chunk_pitch(d)
    for r in range(n):
        row = pl.multiple_of(idx_ref[base + r] * pitch, CHUNK_PAD)
        pltpu.make_async_copy(src_hbm.at[pl.ds(row, nc), :], dst.at[pl.ds(r * pitch, nc), :],
                              sem).start(priority=r % 2)


def _wait_rows(src_hbm, dst, sem, n, d):
    rows = n * (d // LANES)
    pltpu.make_async_copy(src_hbm.at[pl.ds(0, rows), :], dst.at[pl.ds(0, rows), :], sem).wait()


def _pack_bf16_halves(x):
    bits = pltpu.bitcast(x.astype(jnp.float32), jnp.uint32)
    half = x.shape[1] // 2
    return bits[:, :half] | (bits[:, half:] >> 16)


def _unpack_bf16_halves(w):
    left = pltpu.bitcast(w & jnp.uint32(0xFFFF0000), jnp.float32)
    right = pltpu.bitcast(w << 16, jnp.float32)
    return jnp.concatenate([left, right], axis=1).astype(jnp.bfloat16)


def _moe_body(sched_ref, n_used_ref, tok_ref, h_hbm, wg_hbm, wu_hbm, wd_hbm, y_ref,
              xbuf, sem, wg_f, wu_f, wd_f, wsem, wg_b, wu_b, wd_b, *, layer):
    i = pl.program_id(0)
    n_used = n_used_ref[0]
    d = wg_b.shape[0] // 2
    bm = xbuf.shape[1] // _chunk_pitch(d)
    slot = i % 2

    def weight_copies(e, s):
        return (pltpu.make_async_copy(wg_hbm.at[layer, e], wg_f.at[s], wsem.at[s]),
                pltpu.make_async_copy(wu_hbm.at[layer, e], wu_f.at[s], wsem.at[s]),
                pltpu.make_async_copy(wd_hbm.at[layer, e], wd_f.at[s], wsem.at[s]))

    @pl.when((i == 0) & (n_used > 0))
    def _():
        for cp in weight_copies(sched_ref[0, 0], 0):
            cp.start(priority=1)
        _gather_rows(tok_ref, 0, h_hbm, xbuf.at[0], sem.at[0], bm, d)

    @pl.when(i + 1 < n_used)
    def _():
        _gather_rows(tok_ref, (i + 1) * bm, h_hbm, xbuf.at[1 - slot], sem.at[1 - slot], bm, d)

    @pl.when(i < n_used)
    def _():
        @pl.when(sched_ref[1, i] == 1)
        def _():
            ws = sched_ref[2, i]
            for cp in weight_copies(sched_ref[0, i], ws):
                cp.wait()
            wg_b[...] = wg_f[ws].astype(jnp.bfloat16)
            wu_b[...] = wu_f[ws].astype(jnp.bfloat16)
            wd_b[...] = wd_f[ws].astype(jnp.bfloat16)

            @pl.when(sched_ref[3, i] >= 0)
            def _():
                for cp in weight_copies(sched_ref[3, i], 1 - ws):
                    cp.start(priority=1)

        _wait_rows(h_hbm, xbuf.at[slot], sem.at[slot], bm, d)
        x = _unpack_bf16_halves(_from_chunked(xbuf.at[slot], bm, d))
        a = _dot(x, wg_b[...])
        hid = (a * _sigmoid(a)) * _dot(x, wu_b[...])
        _to_chunked(y_ref, _dot(hid.astype(jnp.bfloat16), wd_b[...]))

    @pl.when(i >= n_used)
    def _():
        y_ref[...] = jnp.zeros_like(y_ref)


def _moe(sched, n_used, row_tok, h_rows, w_g, w_u, w_d, layer):
    n_rows = row_tok.shape[0]
    d, hid = w_g.shape[2], w_g.shape[3]
    pitch = _chunk_pitch(d)
    bm = MOE_ROWS
    any_spec = pl.BlockSpec(memory_space=pl.ANY)
    return pl.pallas_call(
        functools.partial(_moe_body, layer=layer),
        out_shape=jax.ShapeDtypeStruct((n_rows * pitch, LANES), jnp.float32),
        grid_spec=pltpu.PrefetchScalarGridSpec(
            num_scalar_prefetch=3, grid=(n_rows // bm,),
            in_specs=[any_spec, any_spec, any_spec, any_spec],
            out_specs=pl.BlockSpec((bm * pitch, LANES), lambda i, s, n, tok: (i, 0)),
            scratch_shapes=[pltpu.VMEM((2, bm * _chunk_pitch(d // 2), LANES), jnp.uint32),
                            pltpu.SemaphoreType.DMA((2,)),
                            pltpu.VMEM((2, d, hid), jnp.float32),
                            pltpu.VMEM((2, d, hid), jnp.float32),
                            pltpu.VMEM((2, hid, d), jnp.float32),
                            pltpu.SemaphoreType.DMA((2,)),
                            pltpu.VMEM((d, hid), jnp.bfloat16),
                            pltpu.VMEM((d, hid), jnp.bfloat16),
                            pltpu.VMEM((hid, d), jnp.bfloat16)]),
        compiler_params=_cparams(("arbitrary",)),
        name="moe_experts",
    )(sched, n_used, row_tok, h_rows, w_g, w_u, w_d)


def _final_body(d0_ref, d1_ref, h_ref, hb_ref, y_hbm, gw_ref, p_ref, wg_ref, wp_ref, lg_ref,
                lb_ref, y_o, yb_o, gbuf, sem, *, alpha):
    i = pl.program_id(0)
    n = pl.num_programs(0)
    tm, d = h_ref.shape
    slot = i % 2

    def gather(step, s):
        _gather_rows(d0_ref, step * tm, y_hbm, gbuf.at[s, 0], sem.at[s], tm, d)
        _gather_rows(d1_ref, step * tm, y_hbm, gbuf.at[s, 1], sem.at[s], tm, d)

    @pl.when(i == 0)
    def _():
        gather(0, 0)

    @pl.when(i + 1 < n)
    def _():
        gather(i + 1, 1 - slot)

    gate = _sigmoid(_dot(hb_ref[...], wg_ref[...]))
    proj = _dot(p_ref[...].astype(jnp.bfloat16), wp_ref[...])
    _wait_rows(y_hbm, gbuf.at[slot, 0], sem.at[slot], tm, d)
    _wait_rows(y_hbm, gbuf.at[slot, 1], sem.at[slot], tm, d)
    gw = gw_ref[...]
    ffn = (_from_chunked(gbuf.at[slot, 0], tm, d) * gw[:, 0:1]
           + _from_chunked(gbuf.at[slot, 1], tm, d) * gw[:, 1:2])
    y = _layer_norm(alpha * h_ref[...] + ffn + gate * proj, lg_ref[...], lb_ref[...])
    y_o[...] = y
    yb_o[...] = y.astype(jnp.bfloat16)


def _final(dest0, dest1, h, hb, y_rows, gate_w, p, w_pg, w_pp, ln_g, ln_b, alpha):
    t, d = h.shape
    tm = min(256, t)
    row = lambda i, *_: (i, 0)
    const2 = lambda i, *_: (0, 0)
    return pl.pallas_call(
        functools.partial(_final_body, alpha=alpha),
        out_shape=(jax.ShapeDtypeStruct((t, d), jnp.float32),
                   jax.ShapeDtypeStruct((t, d), jnp.bfloat16)),
        grid_spec=pltpu.PrefetchScalarGridSpec(
            num_scalar_prefetch=2, grid=(t // tm,),
            in_specs=[pl.BlockSpec((tm, d), row), pl.BlockSpec((tm, d), row),
                      pl.BlockSpec(memory_space=pl.ANY),
                      pl.BlockSpec((tm, TOP_K), row), pl.BlockSpec((tm, p.shape[1]), row),
                      _resident(w_pg.shape, const2), _resident(w_pp.shape, const2),
                      _resident((1, d), const2), _resident((1, d), const2)],
            out_specs=(pl.BlockSpec((tm, d), row), pl.BlockSpec((tm, d), row)),
            scratch_shapes=[pltpu.VMEM((2, TOP_K, tm * _chunk_pitch(d), LANES), jnp.float32),
                            pltpu.SemaphoreType.DMA((2,))]),
        compiler_params=_cparams(("arbitrary",)),
        name="final_ln2",
    )(dest0, dest1, h, hb, y_rows, gate_w, p, w_pg, w_pp, ln_g, ln_b)


def _rope_tables(positions):
    pos = positions.reshape(-1).astype(jnp.float32)
    lane = jnp.arange(LANES)

    def tables(rot, period):
        half = rot // 2
        inv_freq = ROPE_THETA ** (-jnp.arange(half, dtype=jnp.float32) / half)
        ang = pos[:, None] * inv_freq
        cos, sin = jnp.cos(ang), jnp.sin(ang)
        lp = lane % period
        idx = lp % half
        in_rot = lp < rot
        c = jnp.where(in_rot[None, :], cos[:, idx], 1.0)
        s_up = jnp.where((lp < half)[None, :], -sin[:, idx], 0.0)
        s_dn = jnp.where(((lp >= half) & in_rot)[None, :], sin[:, idx], 0.0)
        return c, s_up, s_dn

    return tables(DIFF_ROT, DIFF_QK) + tables(MLA_ROPE, LANES)


def _w_in_segments(d):
    widths = [512, 512, 512, 512, 512, 512, HEADS, MLA_Q_LORA, MLA_KV_LORA, MLA_ROPE,
              512, 512, 512, N_BRANCHES * d]
    dsts = [COL_DQ, COL_DQ + 512, COL_DV, COL_FQ, COL_FK, COL_FV, COL_FF, COL_CQ, COL_CKV,
            COL_KR, COL_SQ, COL_SK, COL_SV, COL_GATES]
    segs, src = [], 0
    for wd, dst in zip(widths, dsts):
        segs.append((src, dst, wd))
        src += wd
    return segs


def _w_in_body(src_ref, valid_ref, wt_hbm, o_ref, buf, sem, *, layer):
    b = pl.program_id(0)
    slot = b % 2
    rows = buf.shape[1]

    def copy(blk, s):
        return pltpu.make_async_copy(wt_hbm.at[pl.ds(src_ref[blk], rows), layer, :],
                                     buf.at[s], sem.at[s])

    @pl.when(b == 0)
    def _():
        copy(0, 0).start()

    @pl.when(b + 1 < pl.num_programs(0))
    def _():
        copy(b + 1, 1 - slot).start()

    copy(b, slot).wait()
    r = lax.broadcasted_iota(jnp.int32, buf.shape[1:], 0)
    o_ref[...] = jnp.where(r < valid_ref[b], buf[slot], 0.0).astype(o_ref.dtype)


def _reorder_w_in(w_t, layer):
    n_in, _, d = w_t.shape
    src, valid = [0] * (U_WIDTH // LANES), [0] * (U_WIDTH // LANES)
    for s0, dst, wd in _w_in_segments(d):
        for off in range(0, wd, LANES):
            blk = (dst + off) // LANES
            src[blk], valid[blk] = s0 + off, min(LANES, wd - off)
    assert max(s + LANES for s in src) <= n_in
    return pl.pallas_call(
        functools.partial(_w_in_body, layer=layer),
        out_shape=jax.ShapeDtypeStruct((U_WIDTH, d), jnp.bfloat16),
        grid_spec=pltpu.PrefetchScalarGridSpec(
            num_scalar_prefetch=2, grid=(U_WIDTH // LANES,),
            in_specs=[pl.BlockSpec(memory_space=pl.ANY)],
            out_specs=pl.BlockSpec((LANES, d), lambda b, s, v: (b, 0)),
            scratch_shapes=[pltpu.VMEM((2, LANES, d), jnp.float32),
                            pltpu.SemaphoreType.DMA((2,))]),
        compiler_params=_cparams(("arbitrary",)),
        name="w_in_relayout",
    )(jnp.asarray(src, jnp.int32), jnp.asarray(valid, jnp.int32), w_t)


def _route(logits, b_rg, b_re, t):
    def first_max(v, n):
        top = jnp.max(v, axis=-1, keepdims=True)
        ids = jnp.arange(n, dtype=jnp.int32)
        return top, jnp.min(jnp.where(v == top, ids, n), axis=-1, keepdims=True)

    g_logits = logits[:, :N_GROUPS] + b_rg
    g_max, grp = first_max(g_logits, N_GROUPS)
    p_grp = 1.0 / jnp.sum(jnp.exp(g_logits - g_max), axis=-1, keepdims=True)
    e_logits = (logits[:, N_GROUPS:N_GROUPS + N_EXPERTS] + b_re).reshape(
        t, N_GROUPS, EXPERTS_PER_GROUP)
    in_grp = jnp.arange(N_GROUPS, dtype=jnp.int32)[None, :, None] == grp[:, :, None]
    e_in = jnp.sum(jnp.where(in_grp, e_logits, 0.0), axis=1)
    probs = jax.nn.softmax(e_in, axis=-1)
    p1, i1 = first_max(probs, EXPERTS_PER_GROUP)
    rest = jnp.where(jnp.arange(EXPERTS_PER_GROUP, dtype=jnp.int32)[None, :] == i1, -1.0, probs)
    p2, i2 = first_max(rest, EXPERTS_PER_GROUP)
    top_p = jnp.concatenate([p1, p2], axis=-1)
    weights = p_grp * top_p / jnp.sum(top_p, axis=-1, keepdims=True)
    expert_idx = grp * EXPERTS_PER_GROUP + jnp.concatenate([i1, i2], axis=-1)
    return expert_idx.astype(jnp.int32), weights


def _dispatch_plan(expert_idx, t):
    a = t * TOP_K
    flat_e = expert_idx.reshape(a)
    onehot = (flat_e[:, None] == jnp.arange(N_EXPERTS)[None, :]).astype(jnp.int32)
    sizes = jnp.sum(onehot, axis=0)
    padded = (sizes + MOE_ROWS - 1) // MOE_ROWS * MOE_ROWS
    pad_end = jnp.cumsum(padded)
    pad_start = pad_end - padded
    dest = (jnp.sum(onehot * (jnp.cumsum(onehot, axis=0) + pad_start[None, :]), axis=1)
            - 1).astype(jnp.int32)
    n_rows = a + N_EXPERTS * MOE_ROWS
    n_blk = n_rows // MOE_ROWS
    flat_tok = jnp.arange(a, dtype=jnp.int32) // TOP_K
    row_tok = jnp.zeros((n_rows,), jnp.int32).at[dest].set(flat_tok)
    blk_start = jnp.arange(n_blk, dtype=pad_end.dtype) * MOE_ROWS
    blk_e = jnp.minimum(jnp.sum(pad_end[None, :] <= blk_start[:, None], axis=1),
                        N_EXPERTS - 1).astype(jnp.int32)
    n_used = (pad_end[-1] // MOE_ROWS).astype(jnp.int32).reshape(1)
    first = jnp.concatenate([jnp.ones((1,), jnp.int32),
                             (blk_e[1:] != blk_e[:-1]).astype(jnp.int32)])
    parity = (jnp.cumsum(first) - 1) % 2
    experts = jnp.arange(N_EXPERTS, dtype=jnp.int32)
    used = jnp.where(padded > 0, experts, N_EXPERTS)
    later = jnp.flip(lax.cummin(jnp.flip(used)))
    next_used = jnp.concatenate([later[1:], jnp.full((1,), N_EXPERTS, jnp.int32)])
    next_used = jnp.where(next_used < N_EXPERTS, next_used, -1)
    sched = jnp.stack([blk_e, first, parity.astype(jnp.int32),
                       next_used[blk_e].astype(jnp.int32)])
    return dest.reshape(t, TOP_K), row_tok, sched, n_used


def kernel(x, p, positions, w_in, fox_f_bias, diff_lambda, diff_subln, mla_q_norm, mla_kv_norm,
           mla_w_uq, mla_w_ukv, w_branch, w_o, ln1_g, ln1_b, w_router_group, b_router_group,
           w_router_expert, b_router_expert, w_expert_gate, w_expert_up, w_expert_down,
           w_ple_gate, w_ple_proj, ln2_g, ln2_b):
    batch, seq, d = x.shape
    depth = w_in.shape[0]
    t = batch * seq
    alpha = (2 * depth) ** 0.25
    bf = jnp.bfloat16
    tabs = _rope_tables(positions)
    w_t = jnp.transpose(w_in, (2, 0, 1))
    col_scale = jnp.ones((1, U_WIDTH), jnp.float32)
    col_scale = col_scale.at[0, COL_FQ:COL_FQ + BRANCH_WIDTH].set(HEAD_DIM ** -0.5 * LOG2E)
    col_scale = col_scale.at[0, COL_SQ:COL_SQ + BRANCH_WIDTH].set(HEAD_DIM ** -0.5)
    xf = x.reshape(t, d)
    xb = xf.astype(bf)

    for i in range(depth):
        w1 = _reorder_w_in(w_t, i)
        u = _matmul(xb, w1, col_scale, bf, 1024, 1536, "in_proj")

        b_f = jnp.zeros((1, LANES), jnp.float32).at[0, :HEADS].set(fox_f_bias[i])
        cum = _fgate_cumsum(xb, w1, b_f, batch, seq)[:, :HEADS] * LOG2E
        cum_bh = jnp.transpose(cum.reshape(batch, seq, HEADS), (0, 2, 1)).reshape(
            batch * HEADS, seq)
        cq_rows = cum_bh.reshape(batch * HEADS, 1, seq)
        ck_rep = jnp.broadcast_to(cum_bh[:, :, None], (batch * HEADS, seq, LANES))

        uq = mla_w_uq[i].reshape(MLA_Q_LORA, HEADS, MLA_NOPE + MLA_ROPE)
        uq = jnp.pad(uq, ((0, 0), (0, 0), (0, MLA_QK_PAD - MLA_NOPE - MLA_ROPE)))
        uq = uq.reshape(MLA_Q_LORA, HEADS * MLA_QK_PAD).astype(bf)
        ukv = mla_w_ukv[i].reshape(MLA_KV_LORA, HEADS, MLA_NOPE + HEAD_DIM)
        uk = ukv[:, :, :MLA_NOPE].reshape(MLA_KV_LORA, HEADS * MLA_NOPE).astype(bf)
        uv = ukv[:, :, MLA_NOPE:].reshape(MLA_KV_LORA, HEADS * HEAD_DIM).astype(bf)
        dqk, mq, mk, mv = _prep(u, tabs, mla_q_norm[i].reshape(1, -1),
                                mla_kv_norm[i].reshape(1, -1), uq, uk, uv)

        lam_init = 0.8 - 0.6 * math.exp(-0.3 * i)
        lp = diff_lambda[i].astype(jnp.float32)
        lam = (jnp.exp(jnp.sum(lp[0] * lp[1])) - jnp.exp(jnp.sum(lp[2] * lp[3]))
               + lam_init).reshape(1)
        blk = lambda col: col // HEAD_DIM
        o_d = _attention("diff", batch, seq, dqk, 0, dqk, HEADS, u, blk(COL_DV), HEAD_DIM,
                         extras=(diff_subln[i].reshape(1, HEAD_DIM),),
                         lam=lam, lam_init=lam_init)
        o_f = _attention("fox", batch, seq, u, blk(COL_FQ), u, blk(COL_FK), u, blk(COL_FV),
                         HEAD_DIM, extras=(cq_rows, ck_rep))
        o_m = _attention("mla", batch, seq, mq, 0, mk, 0, mv, 0, MLA_QK_PAD)
        o_s = _attention("sb", batch, seq, u, blk(COL_SQ), u, blk(COL_SK), u, blk(COL_SV),
                         HEAD_DIM)

        w_r = jnp.concatenate([w_router_group[i], w_router_expert[i]], axis=1)
        w_r = jnp.pad(w_r, ((0, 0), (0, LANES - w_r.shape[1])))
        wr_hi = w_r.astype(bf)
        wr_lo = (w_r - wr_hi.astype(jnp.float32)).astype(bf)
        h, h_rows, hb, logits = _merge(
            u, o_d, o_f, o_m, o_s, w_branch[i].astype(bf), w_o[i].astype(bf), xf,
            ln1_g[i].reshape(1, d), ln1_b[i].reshape(1, d), wr_hi, wr_lo, alpha)

        expert_idx, weights = _route(logits, b_router_group[i], b_router_expert[i], t)
        dest, row_tok, sched, n_used = _dispatch_plan(expert_idx, t)
        y_rows = _moe(sched, n_used, row_tok, h_rows, w_expert_gate, w_expert_up,
                      w_expert_down, i)
        xf, xb = _final(dest[:, 0], dest[:, 1], h, hb, y_rows, weights.astype(jnp.float32),
                        p[i].reshape(t, -1), w_ple_gate[i].astype(bf),
                        w_ple_proj[i].astype(bf), ln2_g[i].reshape(1, d),
                        ln2_b[i].reshape(1, d), alpha)
    return xf.reshape(batch, seq, d)
```

```python
import functools
import math

import jax
import jax.numpy as jnp
from jax import lax
from jax.experimental import pallas as pl
from jax.experimental.pallas import tpu as pltpu

HEAD_DIM = 128
HEADS = 4
N_BRANCHES = 4
BRANCH_WIDTH = HEADS * HEAD_DIM
DIFF_QK = 64
DIFF_ROT = DIFF_QK // 4
MLA_Q_LORA = 512
MLA_KV_LORA = 256
MLA_NOPE = 128
MLA_ROPE = 64
MLA_QK_PAD = 256
N_GROUPS = 4
EXPERTS_PER_GROUP = 8
N_EXPERTS = N_GROUPS * EXPERTS_PER_GROUP
TOP_K = 2
CHUNK = 64
ROPE_THETA = 500000.0
LN_EPS = 1e-5
RMS_EPS = 1e-6
NEG_BIG = -1e30
LOG2E = 1.4426950408889634
SB_EXP_UNDERFLOW = -104.0
LANES = 128
V7X_VMEM_LIMIT = 56 * 1024 * 1024

COL_GATES = 0
COL_DQ = 8192
COL_DV = COL_DQ + 1024
COL_FQ = COL_DV + 512
COL_FK = COL_FQ + 512
COL_FV = COL_FK + 512
COL_SQ = COL_FV + 512
COL_SK = COL_SQ + 512
COL_SV = COL_SK + 512
COL_CQ = COL_SV + 512
COL_CKV = COL_CQ + 512
COL_KR = COL_CKV + 256
COL_FF = COL_KR + 128
U_WIDTH = COL_FF + 128

ATT_TQ = 512
MOE_ROWS = 256


def _cparams(sem, vmem=V7X_VMEM_LIMIT):
    return pltpu.CompilerParams(dimension_semantics=sem, vmem_limit_bytes=vmem)


def _resident(shape, index_map):
    return pl.BlockSpec(shape, index_map, pipeline_mode=pl.Buffered(1))


def _matmul_body(x_ref, w_ref, c_ref, o_ref, f_ref, *, f_block, f_off):
    acc = lax.dot_general(x_ref[...], w_ref[...], (((1,), (1,)), ((), ())),
                          preferred_element_type=jnp.float32)
    o_ref[...] = (acc * c_ref[...]).astype(o_ref.dtype)

    @pl.when(pl.program_id(1) == f_block)
    def _():
        f_ref[...] = acc[:, f_off:f_off + LANES]


def _matmul(x, w_t, col_scale, out_dtype, tm, tn, f32_col, name):
    m, k = x.shape
    n = w_t.shape[0]
    tm, tn = min(tm, m), min(tn, n)
    body = functools.partial(_matmul_body, f_block=f32_col // tn, f_off=f32_col % tn)
    return pl.pallas_call(
        body,
        out_shape=(jax.ShapeDtypeStruct((m, n), out_dtype),
                   jax.ShapeDtypeStruct((m, LANES), jnp.float32)),
        grid=(m // tm, n // tn),
        in_specs=[pl.BlockSpec((tm, k), lambda i, j: (i, 0)),
                  pl.BlockSpec((tn, k), lambda i, j: (j, 0)),
                  pl.BlockSpec((1, tn), lambda i, j: (0, j))],
        out_specs=(pl.BlockSpec((tm, tn), lambda i, j: (i, j)),
                   pl.BlockSpec((tm, LANES), lambda i, j: (i, 0))),
        compiler_params=_cparams(("parallel", "arbitrary")),
        name=name,
    )(x, w_t, col_scale)


def _split3(x):
    hi = x.astype(jnp.bfloat16)
    r1 = x - hi.astype(jnp.float32)
    mid = r1.astype(jnp.bfloat16)
    lo = (r1 - mid.astype(jnp.float32)).astype(jnp.bfloat16)
    return hi, mid, lo


def _split2(x):
    hi = x.astype(jnp.bfloat16)
    mid = (x - hi.astype(jnp.float32)).astype(jnp.bfloat16)
    return hi, mid


def _dot(a, b):
    return jnp.dot(a, b, preferred_element_type=jnp.float32)


def _dot_nt(a, b):
    return lax.dot_general(a, b, (((1,), (1,)), ((), ())), preferred_element_type=jnp.float32)


def _dot_tn(a, b):
    return lax.dot_general(a, b, (((0,), (0,)), ((), ())), preferred_element_type=jnp.float32)


def _log_sigmoid(z):
    neg_abs = pltpu.bitcast(pltpu.bitcast(z, jnp.uint32) | jnp.uint32(0x80000000), jnp.float32)
    return jnp.minimum(z, 0.0) - jnp.log(1.0 + jnp.exp(neg_abs))


def _sigmoid(z):
    return 1.0 / (1.0 + jnp.exp(-z))


def _fgate_body(f_ref, b_ref, o_ref, carry_ref):
    @pl.when(pl.program_id(1) == 0)
    def _():
        carry_ref[...] = jnp.zeros_like(carry_ref)

    log_f = _log_sigmoid(f_ref[...] + b_ref[...])
    ts = log_f.shape[0]
    row = lax.broadcasted_iota(jnp.int32, (ts, ts), 0)
    col = lax.broadcasted_iota(jnp.int32, (ts, ts), 1)
    tri = (col <= row).astype(jnp.bfloat16)
    hi, mid, lo = _split3(log_f)
    cum = _dot(tri, hi) + _dot(tri, mid) + _dot(tri, lo) + carry_ref[...]
    o_ref[...] = cum
    carry_ref[...] = cum[ts - 1:ts, :]


def _fgate_cumsum(f_logits, b_f, batch, seq):
    t = f_logits.shape[0]
    ts = min(512, seq)
    ns = seq // ts
    return pl.pallas_call(
        _fgate_body,
        out_shape=jax.ShapeDtypeStruct((t, LANES), jnp.float32),
        grid=(batch, ns),
        in_specs=[pl.BlockSpec((ts, LANES), lambda b, s: (b * ns + s, 0)),
                  _resident((1, LANES), lambda b, s: (0, 0))],
        out_specs=pl.BlockSpec((ts, LANES), lambda b, s: (b * ns + s, 0)),
        scratch_shapes=[pltpu.VMEM((1, LANES), jnp.float32)],
        compiler_params=_cparams(("parallel", "arbitrary")),
        name="fgate_cumsum",
    )(f_logits, b_f)


def _rope_lanes(x, c, s_up, s_dn, half):
    return x * c + pltpu.roll(x, LANES - half, 1) * s_up + pltpu.roll(x, half, 1) * s_dn


def _prep_body(dqk_ref, cq_ref, ckv_ref, kr_ref, c16_ref, su16_ref, sd16_ref,
               c64_ref, su64_ref, sd64_ref, qg_ref, kvg_ref, wuq_ref, wuk_ref, wuv_ref,
               dqk_o, mq_o, mk_o, mv_o):
    c16, su16, sd16 = c16_ref[...], su16_ref[...], sd16_ref[...]
    c64, su64, sd64 = c64_ref[...], su64_ref[...], sd64_ref[...]
    for j in range(dqk_ref.shape[1] // LANES):
        sl = slice(j * LANES, (j + 1) * LANES)
        x = _rope_lanes(dqk_ref[:, sl].astype(jnp.float32), c16, su16, sd16, DIFF_ROT // 2)
        if j < HEADS:
            x = x * (DIFF_QK ** -0.5 * LOG2E)
        dqk_o[:, sl] = x.astype(dqk_o.dtype)

    cq = cq_ref[...].astype(jnp.float32)
    cqn = cq * lax.rsqrt(jnp.mean(cq * cq, axis=-1, keepdims=True) + RMS_EPS) * qg_ref[...]
    q = _dot(cqn.astype(jnp.bfloat16), wuq_ref[...])
    q = q * ((MLA_NOPE + MLA_ROPE) ** -0.5 * LOG2E)
    ckv = ckv_ref[...].astype(jnp.float32)
    ckvn = (ckv * lax.rsqrt(jnp.mean(ckv * ckv, axis=-1, keepdims=True) + RMS_EPS)
            * kvg_ref[...]).astype(jnp.bfloat16)
    kn = _dot(ckvn, wuk_ref[...])
    mv_o[...] = _dot(ckvn, wuv_ref[...]).astype(mv_o.dtype)
    kr = _rope_lanes(kr_ref[...].astype(jnp.float32), c64, su64, sd64, MLA_ROPE // 2)
    for h in range(HEADS):
        base = h * MLA_QK_PAD
        mq_o[:, base:base + LANES] = q[:, base:base + LANES].astype(mq_o.dtype)
        qr = _rope_lanes(q[:, base + LANES:base + 2 * LANES], c64, su64, sd64, MLA_ROPE // 2)
        mq_o[:, base + LANES:base + 2 * LANES] = qr.astype(mq_o.dtype)
        mk_o[:, base:base + LANES] = kn[:, h * LANES:(h + 1) * LANES].astype(mk_o.dtype)
        mk_o[:, base + LANES:base + 2 * LANES] = kr.astype(mk_o.dtype)


def _prep(u, tabs, q_norm, kv_norm, w_uq, w_uk, w_uv):
    t = u.shape[0]
    tm = min(512, t)
    row = lambda blk: (lambda i: (i, blk))
    const = lambda i: (0, 0)
    tab_spec = pl.BlockSpec((tm, LANES), lambda i: (i, 0))
    outs = pl.pallas_call(
        _prep_body,
        out_shape=(jax.ShapeDtypeStruct((t, 1024), jnp.bfloat16),
                   jax.ShapeDtypeStruct((t, HEADS * MLA_QK_PAD), jnp.bfloat16),
                   jax.ShapeDtypeStruct((t, HEADS * MLA_QK_PAD), jnp.bfloat16),
                   jax.ShapeDtypeStruct((t, BRANCH_WIDTH), jnp.bfloat16)),
        grid=(t // tm,),
        in_specs=[pl.BlockSpec((tm, 1024), row(COL_DQ // 1024)),
                  pl.BlockSpec((tm, MLA_Q_LORA), row(COL_CQ // MLA_Q_LORA)),
                  pl.BlockSpec((tm, MLA_KV_LORA), row(COL_CKV // MLA_KV_LORA)),
                  pl.BlockSpec((tm, LANES), row(COL_KR // LANES)),
                  tab_spec, tab_spec, tab_spec, tab_spec, tab_spec, tab_spec,
                  _resident((1, MLA_Q_LORA), const),
                  _resident((1, MLA_KV_LORA), const),
                  _resident(w_uq.shape, const),
                  _resident(w_uk.shape, const),
                  _resident(w_uv.shape, const)],
        out_specs=(pl.BlockSpec((tm, 1024), lambda i: (i, 0)),
                   pl.BlockSpec((tm, HEADS * MLA_QK_PAD), lambda i: (i, 0)),
                   pl.BlockSpec((tm, HEADS * MLA_QK_PAD), lambda i: (i, 0)),
                   pl.BlockSpec((tm, BRANCH_WIDTH), lambda i: (i, 0))),
        compiler_params=_cparams(("parallel",)),
        name="prep",
    )(u, u, u, u, *tabs, q_norm, kv_norm, w_uq, w_uk, w_uv)
    return outs


def _attn_body(*refs, kind, tq, tk, lam_init):
    softmax = kind != "sb"
    if kind == "diff":
        lam_ref, q_ref, k_ref, v_ref, g_ref, o_ref = refs[:6]
    elif kind == "fox":
        q_ref, k_ref, v_ref, cq_ref, ck_ref, o_ref = refs[:6]
    else:
        q_ref, k_ref, v_ref, o_ref = refs[:4]
    if softmax:
        s_bufs, p_bufs = refs[-8:-6], refs[-6:-4]
        vt_ref, m_ref, a_ref, acc_ref = refs[-4:]
    else:
        s_bufs, p_bufs = refs[-7:-5], refs[-5:-3]
        vt_ref, r_ref, acc_ref = refs[-3:]
    qi = pl.program_id(2)

    @pl.when(qi == 0)
    def _():
        for kb in range(vt_ref.shape[0]):
            vt_ref[kb, 0:HEAD_DIM, :] = v_ref[kb * tk:(kb + 1) * tk, :].T
            if softmax:
                vt_ref[kb, HEAD_DIM:, :] = jnp.ones((vt_ref.shape[1] - HEAD_DIM, tk),
                                                    vt_ref.dtype)

    q0 = qi * tq
    last_kb = 2 * qi + 1
    q = q_ref[...]
    n_comp = 2 if kind == "diff" else 1
    if kind == "diff":
        lane = lax.broadcasted_iota(jnp.int32, q.shape, 1)
        qs = [jnp.where(lane < DIFF_QK, q, jnp.zeros_like(q)),
              jnp.where(lane >= DIFF_QK, q, jnp.zeros_like(q))]
    else:
        qs = [q]

    acc_ref[...] = jnp.zeros_like(acc_ref)
    p_bufs[1][...] = jnp.zeros_like(p_bufs[1])
    if softmax:
        m_ref[...] = jnp.full_like(m_ref, NEG_BIG)
        a_ref[...] = jnp.zeros_like(a_ref)
    else:
        r_ref[...] = jnp.zeros_like(r_ref)

    def key_rows(kb):
        if isinstance(kb, int):
            return pl.ds(kb * tk, tk)
        return pl.ds(pl.multiple_of(kb * tk, tk), tk)

    def scores_stage(kb, s_dst):
        k_t = k_ref[key_rows(kb), :]
        for c in range(n_comp):
            s_dst[c] = _dot_nt(k_t, qs[c])

    def values_stage(kb, p_src):
        v_t = vt_ref[kb]
        for c in range(n_comp):
            pv = _dot(v_t, p_src[c])[:acc_ref.shape[1]]
            if softmax:
                acc_ref[c] = a_ref[c] * acc_ref[c] + pv
            else:
                acc_ref[c] = acc_ref[c] + pv

    def weights_stage(kb, s_src, p_dst, masked):
        if masked:
            j = lax.broadcasted_iota(jnp.int32, (tk, tq), 0)
            i = lax.broadcasted_iota(jnp.int32, (tk, tq), 1)
            off = q0 - kb * tk
            if kind in ("diff", "mla"):
                mask = (j // CHUNK - i // CHUNK) <= off // CHUNK
            elif kind == "fox":
                mask = (j - i) <= off
            else:
                mask = (j - i) < off
        if not softmax:
            z_t = s_src[0]
            ls = _log_sigmoid(z_t)
            lk = ls - z_t
            if masked:
                lk = jnp.where(mask, lk, 0.0)
            jj = lax.broadcasted_iota(jnp.int32, (tk, tk), 0)
            mm = lax.broadcasted_iota(jnp.int32, (tk, tk), 1)
            upper = (mm > jj).astype(jnp.bfloat16)
            hi, mid = _split2(lk)
            suffix = _dot(jnp.concatenate([upper, upper], axis=1),
                          jnp.concatenate([hi, mid], axis=0))
            r_old = r_ref[...]
            w_t = jnp.exp(ls + (r_old + suffix))
            if masked:
                w_t = jnp.where(mask, w_t, 0.0)
            p_dst[0] = w_t.astype(jnp.bfloat16)
            r_ref[...] = r_old + suffix[0:1, :] + lk[0:1, :]
            return
        for c in range(n_comp):
            s_t = s_src[c]
            if kind == "fox":
                ck = ck_ref[key_rows(kb), :]
                s_t = s_t + cq_ref[...] - jnp.concatenate([ck] * (tq // LANES), axis=1)
            if masked:
                s_t = jnp.where(mask, s_t, NEG_BIG)
            m_old = m_ref[c]
            m_new = jnp.maximum(m_old, jnp.max(s_t, axis=0, keepdims=True))
            alpha = jnp.exp2(m_old - m_new)
            p_t = jnp.exp2(s_t - m_new)
            m_ref[c] = m_new
            a_ref[c] = alpha
            p_dst[c] = p_t.astype(jnp.bfloat16)

    def step(kb, parity, masked, next_kb, prev_kb):
        if next_kb is not None:
            scores_stage(next_kb, s_bufs[1 - parity])
        values_stage(prev_kb, p_bufs[1 - parity])
        weights_stage(kb, s_bufs[parity], p_bufs[parity], masked)

    if softmax and tq == 2 * tk:
        scores_stage(0, s_bufs[0])

        def body(t, carry):
            kb = 2 * t
            step(kb, 0, False, kb + 1, jnp.maximum(kb - 1, 0))
            step(kb + 1, 1, False, kb + 2, kb)
            return carry
        lax.fori_loop(0, qi, body, 0)
        step(last_kb - 1, 0, True, last_kb, jnp.maximum(last_kb - 2, 0))
        step(last_kb, 1, True, None, last_kb - 1)
        values_stage(last_kb, p_bufs[1])
    elif softmax:
        scores_stage(0, s_bufs[0])

        def body(t, carry):
            kb = 2 * t
            step(kb, 0, False, kb + 1, jnp.maximum(kb - 1, 0))
            step(kb + 1, 1, False, kb + 2, kb)
            return carry
        lax.fori_loop(0, qi // 2, body, 0)

        @pl.when(qi % 2 == 0)
        def _():
            step(qi, 0, True, None, jnp.maximum(qi - 1, 0))
            values_stage(qi, p_bufs[0])

        @pl.when(qi % 2 == 1)
        def _():
            step(qi - 1, 0, False, qi, jnp.maximum(qi - 2, 0))
            step(qi, 1, True, None, qi - 1)
            values_stage(qi, p_bufs[1])
    else:
        scores_stage(last_kb, s_bufs[0])
        step(last_kb, 0, True, last_kb - 1, last_kb)
        step(last_kb - 1, 1, True, jnp.maximum(last_kb - 2, 0), last_kb)

        def live(carry):
            t, _, r_max = carry
            return (t < qi) & (r_max >= SB_EXP_UNDERFLOW)

        def body(carry):
            t = carry[0]
            kb = last_kb - 2 - 2 * t
            step(kb, 0, False, kb - 1, kb + 1)
            step(kb - 1, 1, False, jnp.maximum(kb - 2, 0), kb)
            return t + 1, kb - 1, jnp.max(r_ref[...])
        done = lax.while_loop(live, body, (0, last_kb - 1, jnp.max(r_ref[...])))
        values_stage(done[1], p_bufs[1])

    def normalised(c):
        return acc_ref[c, 0:HEAD_DIM, :] / acc_ref[c, HEAD_DIM:HEAD_DIM + 1, :]

    if kind == "sb":
        o_t = acc_ref[0]
    elif kind == "diff":
        o_t = normalised(0) - lam_ref[0] * normalised(1)
    else:
        o_t = normalised(0)
    o = o_t.T
    if kind == "diff":
        o = (o * lax.rsqrt(jnp.mean(o * o, axis=-1, keepdims=True) + RMS_EPS)
             * g_ref[...] * (1.0 - lam_init))
    o_ref[...] = o.astype(o_ref.dtype)


def _attention(kind, batch, seq, q_arr, q_col, k_arr, k_col, v_arr, v_col, dq,
               extras=(), lam=None, lam_init=0.0):
    t = q_arr.shape[0]
    tq = min(ATT_TQ, seq)
    tk = tq if kind in ("diff", "mla") else tq // 2
    nq = seq // tq
    q_spec = pl.BlockSpec((tq, dq), lambda b, h, i, *_: (b * nq + i, q_col + h))
    k_spec = pl.BlockSpec((seq, dq), lambda b, h, i, *_: (b, k_col + h))
    v_spec = pl.BlockSpec((seq, HEAD_DIM), lambda b, h, i, *_: (b, v_col + h))
    in_specs = [q_spec, k_spec, v_spec]
    args = [q_arr, k_arr, v_arr]
    n_comp = 2 if kind == "diff" else 1
    if kind == "diff":
        in_specs.append(_resident((1, HEAD_DIM), lambda b, h, i, *_: (0, 0)))
        args.append(extras[0])
    elif kind == "fox":
        cq_rows, ck_rep = extras
        in_specs.append(pl.BlockSpec((None, 1, tq), lambda b, h, i: (b * HEADS + h, 0, i)))
        in_specs.append(pl.BlockSpec((None, seq, LANES), lambda b, h, i: (b * HEADS + h, 0, 0)))
        args += [cq_rows, ck_rep]
    s_buf = pltpu.VMEM((n_comp, tk, tq), jnp.float32)
    p_buf = pltpu.VMEM((n_comp, tk, tq), jnp.bfloat16)
    row = pltpu.VMEM((n_comp, 1, tq), jnp.float32)
    nk = seq // tk
    if kind == "sb":
        scratch = [s_buf, s_buf, p_buf, p_buf,
                   pltpu.VMEM((nk, HEAD_DIM, tk), jnp.bfloat16),
                   pltpu.VMEM((1, tq), jnp.float32),
                   pltpu.VMEM((1, HEAD_DIM, tq), jnp.float32)]
    else:
        scratch = [s_buf, s_buf, p_buf, p_buf,
                   pltpu.VMEM((nk, HEAD_DIM + 16, tk), jnp.bfloat16), row, row,
                   pltpu.VMEM((n_comp, HEAD_DIM + 8, tq), jnp.float32)]
    body = functools.partial(_attn_body, kind=kind, tq=tq, tk=tk, lam_init=lam_init)
    grid = (batch, HEADS, nq)
    out_shape = jax.ShapeDtypeStruct((t, BRANCH_WIDTH), jnp.bfloat16)
    out_spec = pl.BlockSpec((tq, HEAD_DIM), lambda b, h, i, *_: (b * nq + i, h))
    cp = _cparams(("parallel", "parallel", "arbitrary"))
    if kind == "diff":
        return pl.pallas_call(
            body, out_shape=out_shape,
            grid_spec=pltpu.PrefetchScalarGridSpec(
                num_scalar_prefetch=1, grid=grid, in_specs=in_specs, out_specs=out_spec,
                scratch_shapes=scratch),
            compiler_params=cp, name="attn_" + kind)(lam, *args)
    return pl.pallas_call(
        body, out_shape=out_shape, grid=grid, in_specs=in_specs, out_specs=out_spec,
        scratch_shapes=scratch, compiler_params=cp, name="attn_" + kind)(*args)


def _layer_norm(y, g, b):
    mu = jnp.mean(y, axis=-1, keepdims=True)
    yc = y - mu
    var = jnp.mean(yc * yc, axis=-1, keepdims=True)
    return yc * lax.rsqrt(var + LN_EPS) * g + b


def _merge_body(g_ref, od_ref, of_ref, om_ref, os_ref, wb_ref, wo_ref, x_ref, lg_ref, lb_ref,
                wrh_ref, wrl_ref, h_o, hr_o, hb_o, lgt_o, *, alpha):
    d = x_ref.shape[1]
    merged = None
    for n, o_ref in enumerate((od_ref, of_ref, om_ref, os_ref)):
        gate = _sigmoid(g_ref[:, n * d:(n + 1) * d].astype(jnp.float32))
        term = gate * _dot(o_ref[...], wb_ref[n])
        merged = term if merged is None else merged + term
    mix = _dot(merged.astype(jnp.bfloat16), wo_ref[...])
    h = _layer_norm(alpha * x_ref[...] + mix, lg_ref[...], lb_ref[...])
    h_o[...] = h
    h_hi = h.astype(jnp.bfloat16)
    hb_o[...] = h_hi
    _to_chunked(hr_o, _pack_bf16_halves(h_hi))
    h_lo = (h - h_hi.astype(jnp.float32)).astype(jnp.bfloat16)
    lgt_o[...] = (_dot(h_hi, wrh_ref[...]) + _dot(h_lo, wrh_ref[...]) + _dot(h_hi, wrl_ref[...]))


def _merge(u, o_d, o_f, o_m, o_s, w_b, w_o, x, ln_g, ln_b, wr_hi, wr_lo, alpha):
    t, d = x.shape
    tm = min(256, t)
    row = lambda i: (i, 0)
    const2 = lambda i: (0, 0)
    o_spec = pl.BlockSpec((tm, BRANCH_WIDTH), row)
    return pl.pallas_call(
        functools.partial(_merge_body, alpha=alpha),
        out_shape=(jax.ShapeDtypeStruct((t, d), jnp.float32),
                   jax.ShapeDtypeStruct((t * _chunk_pitch(d // 2), LANES), jnp.uint32),
                   jax.ShapeDtypeStruct((t, d), jnp.bfloat16),
                   jax.ShapeDtypeStruct((t, LANES), jnp.float32)),
        grid=(t // tm,),
        in_specs=[pl.BlockSpec((tm, N_BRANCHES * d), row),
                  o_spec, o_spec, o_spec, o_spec,
                  _resident(w_b.shape, lambda i: (0, 0, 0)),
                  _resident(w_o.shape, const2),
                  pl.BlockSpec((tm, d), row),
                  _resident((1, d), const2), _resident((1, d), const2),
                  _resident(wr_hi.shape, const2), _resident(wr_lo.shape, const2)],
        out_specs=(pl.BlockSpec((tm, d), row),
                   pl.BlockSpec((tm * _chunk_pitch(d // 2), LANES), row),
                   pl.BlockSpec((tm, d), row), pl.BlockSpec((tm, LANES), row)),
        compiler_params=_cparams(("parallel",)),
        name="merge_ln1",
    )(u, o_d, o_f, o_m, o_s, w_b, w_o, x, ln_g, ln_b, wr_hi, wr_lo)


CHUNK_PAD = 2


def _chunk_pitch(d):
    return d // LANES + CHUNK_PAD


def _to_chunked(ref, x):
    n, d = x.shape
    nc, pitch = d // LANES, _chunk_pitch(d)
    for c in range(nc):
        ref[pl.ds(c, n, stride=pitch), :] = x[:, c * LANES:(c + 1) * LANES]
    for c in range(nc, pitch):
        ref[pl.ds(c, n, stride=pitch), :] = jnp.zeros((n, LANES), x.dtype)


def _from_chunked(ref, n, d):
    pitch = _chunk_pitch(d)
    return jnp.concatenate([ref[pl.ds(c, n, stride=pitch), :] for c in range(d // LANES)],
                           axis=1)


def _gather_rows(idx_ref, base, src_hbm, dst, sem, n, d, first=0):
    nc, pitch = d // LANES, _chunk_pitch(d)
    for r in range(first, n):
        row = pl.multiple_of(idx_ref[base + r] * pitch, CHUNK_PAD)
        pltpu.make_async_copy(src_hbm.at[pl.ds(row, nc), :], dst.at[pl.ds(r * pitch, nc), :],
                              sem).start(priority=r % 2)


def _wait_rows(src_hbm, dst, sem, n, d):
    rows = n * (d // LANES)
    pltpu.make_async_copy(src_hbm.at[pl.ds(0, rows), :], dst.at[pl.ds(0, rows), :], sem).wait()


def _pack_bf16_halves(x):
    bits = pltpu.bitcast(x.astype(jnp.float32), jnp.uint32)
    half = x.shape[1] // 2
    return bits[:, :half] | (bits[:, half:] >> 16)


def _unpack_bf16_halves(w):
    left = pltpu.bitcast(w & jnp.uint32(0xFFFF0000), jnp.float32)
    right = pltpu.bitcast(w << 16, jnp.float32)
    return jnp.concatenate([left, right], axis=1).astype(jnp.bfloat16)


def _moe_body(sched_ref, n_used_ref, tok_ref, h_hbm, wg_hbm, wu_hbm, wd_hbm, y_ref,
              xbuf, sem, wg_f, wu_f, wd_f, wsem, wg_b, wu_b, wd_b, *, layer):
    i = pl.program_id(0)
    n_used = n_used_ref[0]
    d = wg_b.shape[0] // 2
    bm = xbuf.shape[1] // _chunk_pitch(d)
    slot = i % 2

    def weight_copies(e, s):
        return (pltpu.make_async_copy(wg_hbm.at[layer, e], wg_f.at[s], wsem.at[s]),
                pltpu.make_async_copy(wu_hbm.at[layer, e], wu_f.at[s], wsem.at[s]),
                pltpu.make_async_copy(wd_hbm.at[layer, e], wd_f.at[s], wsem.at[s]))

    @pl.when((i == 0) & (n_used > 0))
    def _():
        for cp in weight_copies(sched_ref[0, 0], 0):
            cp.start(priority=1)
        _gather_rows(tok_ref, 0, h_hbm, xbuf.at[0], sem.at[0], bm, d)

    def prefetch_rows(first, last):
        @pl.when(i + 1 < n_used)
        def _():
            _gather_rows(tok_ref, (i + 1) * bm, h_hbm, xbuf.at[1 - slot], sem.at[1 - slot],
                         last, d, first=first)

    prefetch_rows(0, bm // 2)

    @pl.when(i < n_used)
    def _():
        @pl.when(sched_ref[1, i] == 1)
        def _():
            ws = sched_ref[2, i]
            for cp in weight_copies(sched_ref[0, i], ws):
                cp.wait()
            wg_b[...] = wg_f[ws].astype(jnp.bfloat16)
            wu_b[...] = wu_f[ws].astype(jnp.bfloat16)
            wd_b[...] = wd_f[ws].astype(jnp.bfloat16)

            @pl.when(sched_ref[3, i] >= 0)
            def _():
                for cp in weight_copies(sched_ref[3, i], 1 - ws):
                    cp.start(priority=1)

        _wait_rows(h_hbm, xbuf.at[slot], sem.at[slot], bm, d)
        x = _unpack_bf16_halves(_from_chunked(xbuf.at[slot], bm, d))
        a = _dot(x, wg_b[...])
        hid = (a * _sigmoid(a)) * _dot(x, wu_b[...])
        _to_chunked(y_ref, _dot(hid.astype(jnp.bfloat16), wd_b[...]))

    prefetch_rows(bm // 2, bm)

    @pl.when(i >= n_used)
    def _():
        y_ref[...] = jnp.zeros_like(y_ref)


def _moe(sched, n_used, row_tok, h_rows, w_g, w_u, w_d, layer):
    n_rows = row_tok.shape[0]
    d, hid = w_g.shape[2], w_g.shape[3]
    pitch = _chunk_pitch(d)
    bm = MOE_ROWS
    any_spec = pl.BlockSpec(memory_space=pl.ANY)
    return pl.pallas_call(
        functools.partial(_moe_body, layer=layer),
        out_shape=jax.ShapeDtypeStruct((n_rows * pitch, LANES), jnp.float32),
        grid_spec=pltpu.PrefetchScalarGridSpec(
            num_scalar_prefetch=3, grid=(n_rows // bm,),
            in_specs=[any_spec, any_spec, any_spec, any_spec],
            out_specs=pl.BlockSpec((bm * pitch, LANES), lambda i, s, n, tok: (i, 0)),
            scratch_shapes=[pltpu.VMEM((2, bm * _chunk_pitch(d // 2), LANES), jnp.uint32),
                            pltpu.SemaphoreType.DMA((2,)),
                            pltpu.VMEM((2, d, hid), jnp.float32),
                            pltpu.VMEM((2, d, hid), jnp.float32),
                            pltpu.VMEM((2, hid, d), jnp.float32),
                            pltpu.SemaphoreType.DMA((2,)),
                            pltpu.VMEM((d, hid), jnp.bfloat16),
                            pltpu.VMEM((d, hid), jnp.bfloat16),
                            pltpu.VMEM((hid, d), jnp.bfloat16)]),
        compiler_params=_cparams(("arbitrary",)),
        name="moe_experts",
    )(sched, n_used, row_tok, h_rows, w_g, w_u, w_d)


def _final_body(d0_ref, d1_ref, h_ref, hb_ref, y_hbm, gw_ref, p_ref, wg_ref, wp_ref, lg_ref,
                lb_ref, y_o, yb_o, gbuf, sem, *, alpha):
    i = pl.program_id(0)
    n = pl.num_programs(0)
    tm, d = h_ref.shape
    slot = i % 2

    def gather(step, s):
        _gather_rows(d0_ref, step * tm, y_hbm, gbuf.at[s, 0], sem.at[s], tm, d)
        _gather_rows(d1_ref, step * tm, y_hbm, gbuf.at[s, 1], sem.at[s], tm, d)

    @pl.when(i == 0)
    def _():
        gather(0, 0)

    @pl.when(i + 1 < n)
    def _():
        gather(i + 1, 1 - slot)

    gate = _sigmoid(_dot(hb_ref[...], wg_ref[...]))
    proj = _dot(p_ref[...].astype(jnp.bfloat16), wp_ref[...])
    _wait_rows(y_hbm, gbuf.at[slot, 0], sem.at[slot], tm, d)
    _wait_rows(y_hbm, gbuf.at[slot, 1], sem.at[slot], tm, d)
    gw = gw_ref[...]
    ffn = (_from_chunked(gbuf.at[slot, 0], tm, d) * gw[:, 0:1]
           + _from_chunked(gbuf.at[slot, 1], tm, d) * gw[:, 1:2])
    y = _layer_norm(alpha * h_ref[...] + ffn + gate * proj, lg_ref[...], lb_ref[...])
    y_o[...] = y
    yb_o[...] = y.astype(jnp.bfloat16)


def _final(dest0, dest1, h, hb, y_rows, gate_w, p, w_pg, w_pp, ln_g, ln_b, alpha):
    t, d = h.shape
    tm = min(256, t)
    row = lambda i, *_: (i, 0)
    const2 = lambda i, *_: (0, 0)
    return pl.pallas_call(
        functools.partial(_final_body, alpha=alpha),
        out_shape=(jax.ShapeDtypeStruct((t, d), jnp.float32),
                   jax.ShapeDtypeStruct((t, d), jnp.bfloat16)),
        grid_spec=pltpu.PrefetchScalarGridSpec(
            num_scalar_prefetch=2, grid=(t // tm,),
            in_specs=[pl.BlockSpec((tm, d), row), pl.BlockSpec((tm, d), row),
                      pl.BlockSpec(memory_space=pl.ANY),
                      pl.BlockSpec((tm, TOP_K), row), pl.BlockSpec((tm, p.shape[1]), row),
                      _resident(w_pg.shape, const2), _resident(w_pp.shape, const2),
                      _resident((1, d), const2), _resident((1, d), const2)],
            out_specs=(pl.BlockSpec((tm, d), row), pl.BlockSpec((tm, d), row)),
            scratch_shapes=[pltpu.VMEM((2, TOP_K, tm * _chunk_pitch(d), LANES), jnp.float32),
                            pltpu.SemaphoreType.DMA((2,))]),
        compiler_params=_cparams(("arbitrary",)),
        name="final_ln2",
    )(dest0, dest1, h, hb, y_rows, gate_w, p, w_pg, w_pp, ln_g, ln_b)


def _rope_tables(positions):
    pos = positions.reshape(-1).astype(jnp.float32)
    lane = jnp.arange(LANES)

    def tables(rot, period):
        half = rot // 2
        inv_freq = ROPE_THETA ** (-jnp.arange(half, dtype=jnp.float32) / half)
        ang = pos[:, None] * inv_freq
        cos, sin = jnp.cos(ang), jnp.sin(ang)
        lp = lane % period
        idx = lp % half
        in_rot = lp < rot
        c = jnp.where(in_rot[None, :], cos[:, idx], 1.0)
        s_up = jnp.where((lp < half)[None, :], -sin[:, idx], 0.0)
        s_dn = jnp.where(((lp >= half) & in_rot)[None, :], sin[:, idx], 0.0)
        return c, s_up, s_dn

    return tables(DIFF_ROT, DIFF_QK) + tables(MLA_ROPE, LANES)


def _w_in_segments(d):
    widths = [512, 512, 512, 512, 512, 512, HEADS, MLA_Q_LORA, MLA_KV_LORA, MLA_ROPE,
              512, 512, 512, N_BRANCHES * d]
    dsts = [COL_DQ, COL_DQ + 512, COL_DV, COL_FQ, COL_FK, COL_FV, COL_FF, COL_CQ, COL_CKV,
            COL_KR, COL_SQ, COL_SK, COL_SV, COL_GATES]
    segs, src = [], 0
    for wd, dst in zip(widths, dsts):
        segs.append((src, dst, wd))
        src += wd
    return segs


def _w_in_body(src_ref, valid_ref, wt_hbm, o_ref, buf, sem, *, layer):
    b = pl.program_id(0)
    slot = b % 2
    rows = buf.shape[1]

    def copy(blk, s):
        return pltpu.make_async_copy(wt_hbm.at[pl.ds(src_ref[blk], rows), layer, :],
                                     buf.at[s], sem.at[s])

    @pl.when(b == 0)
    def _():
        copy(0, 0).start()

    @pl.when(b + 1 < pl.num_programs(0))
    def _():
        copy(b + 1, 1 - slot).start()

    copy(b, slot).wait()
    r = lax.broadcasted_iota(jnp.int32, buf.shape[1:], 0)
    o_ref[...] = jnp.where(r < valid_ref[b], buf[slot], 0.0).astype(o_ref.dtype)


def _reorder_w_in(w_t, layer):
    n_in, _, d = w_t.shape
    src, valid = [0] * (U_WIDTH // LANES), [0] * (U_WIDTH // LANES)
    for s0, dst, wd in _w_in_segments(d):
        for off in range(0, wd, LANES):
            blk = (dst + off) // LANES
            src[blk], valid[blk] = s0 + off, min(LANES, wd - off)
    assert max(s + LANES for s in src) <= n_in
    return pl.pallas_call(
        functools.partial(_w_in_body, layer=layer),
        out_shape=jax.ShapeDtypeStruct((U_WIDTH, d), jnp.bfloat16),
        grid_spec=pltpu.PrefetchScalarGridSpec(
            num_scalar_prefetch=2, grid=(U_WIDTH // LANES,),
            in_specs=[pl.BlockSpec(memory_space=pl.ANY)],
            out_specs=pl.BlockSpec((LANES, d), lambda b, s, v: (b, 0)),
            scratch_shapes=[pltpu.VMEM((2, LANES, d), jnp.float32),
                            pltpu.SemaphoreType.DMA((2,))]),
        compiler_params=_cparams(("arbitrary",)),
        name="w_in_relayout",
    )(jnp.asarray(src, jnp.int32), jnp.asarray(valid, jnp.int32), w_t)


def _route(logits, b_rg, b_re, t):
    def first_max(v, n):
        top = jnp.max(v, axis=-1, keepdims=True)
        ids = jnp.arange(n, dtype=jnp.int32)
        return top, jnp.min(jnp.where(v == top, ids, n), axis=-1, keepdims=True)

    g_logits = logits[:, :N_GROUPS] + b_rg
    g_max, grp = first_max(g_logits, N_GROUPS)
    p_grp = 1.0 / jnp.sum(jnp.exp(g_logits - g_max), axis=-1, keepdims=True)
    e_logits = (logits[:, N_GROUPS:N_GROUPS + N_EXPERTS] + b_re).reshape(
        t, N_GROUPS, EXPERTS_PER_GROUP)
    in_grp = jnp.arange(N_GROUPS, dtype=jnp.int32)[None, :, None] == grp[:, :, None]
    e_in = jnp.sum(jnp.where(in_grp, e_logits, 0.0), axis=1)
    probs = jax.nn.softmax(e_in, axis=-1)
    p1, i1 = first_max(probs, EXPERTS_PER_GROUP)
    rest = jnp.where(jnp.arange(EXPERTS_PER_GROUP, dtype=jnp.int32)[None, :] == i1, -1.0, probs)
    p2, i2 = first_max(rest, EXPERTS_PER_GROUP)
    top_p = jnp.concatenate([p1, p2], axis=-1)
    weights = p_grp * top_p / jnp.sum(top_p, axis=-1, keepdims=True)
    expert_idx = grp * EXPERTS_PER_GROUP + jnp.concatenate([i1, i2], axis=-1)
    return expert_idx.astype(jnp.int32), weights


def _dispatch_plan(expert_idx, t):
    a = t * TOP_K
    flat_e = expert_idx.reshape(a)
    onehot = (flat_e[:, None] == jnp.arange(N_EXPERTS)[None, :]).astype(jnp.int32)
    sizes = jnp.sum(onehot, axis=0)
    padded = (sizes + MOE_ROWS - 1) // MOE_ROWS * MOE_ROWS
    pad_end = jnp.cumsum(padded)
    pad_start = pad_end - padded
    dest = (jnp.sum(onehot * (jnp.cumsum(onehot, axis=0) + pad_start[None, :]), axis=1)
            - 1).astype(jnp.int32)
    n_rows = a + N_EXPERTS * MOE_ROWS
    n_blk = n_rows // MOE_ROWS
    flat_tok = jnp.arange(a, dtype=jnp.int32) // TOP_K
    row_tok = jnp.zeros((n_rows,), jnp.int32).at[dest].set(flat_tok)
    blk_start = jnp.arange(n_blk, dtype=pad_end.dtype) * MOE_ROWS
    blk_e = jnp.minimum(jnp.sum(pad_end[None, :] <= blk_start[:, None], axis=1),
                        N_EXPERTS - 1).astype(jnp.int32)
    n_used = (pad_end[-1] // MOE_ROWS).astype(jnp.int32).reshape(1)
    first = jnp.concatenate([jnp.ones((1,), jnp.int32),
                             (blk_e[1:] != blk_e[:-1]).astype(jnp.int32)])
    parity = (jnp.cumsum(first) - 1) % 2
    experts = jnp.arange(N_EXPERTS, dtype=jnp.int32)
    used = jnp.where(padded > 0, experts, N_EXPERTS)
    later = jnp.flip(lax.cummin(jnp.flip(used)))
    next_used = jnp.concatenate([later[1:], jnp.full((1,), N_EXPERTS, jnp.int32)])
    next_used = jnp.where(next_used < N_EXPERTS, next_used, -1)
    sched = jnp.stack([blk_e, first, parity.astype(jnp.int32),
                       next_used[blk_e].astype(jnp.int32)])
    return dest.reshape(t, TOP_K), row_tok, sched, n_used


def kernel(x, p, positions, w_in, fox_f_bias, diff_lambda, diff_subln, mla_q_norm, mla_kv_norm,
           mla_w_uq, mla_w_ukv, w_branch, w_o, ln1_g, ln1_b, w_router_group, b_router_group,
           w_router_expert, b_router_expert, w_expert_gate, w_expert_up, w_expert_down,
           w_ple_gate, w_ple_proj, ln2_g, ln2_b):
    batch, seq, d = x.shape
    depth = w_in.shape[0]
    t = batch * seq
    alpha = (2 * depth) ** 0.25
    bf = jnp.bfloat16
    tabs = _rope_tables(positions)
    w_t = jnp.transpose(w_in, (2, 0, 1))
    col_scale = jnp.ones((1, U_WIDTH), jnp.float32)
    col_scale = col_scale.at[0, COL_FQ:COL_FQ + BRANCH_WIDTH].set(HEAD_DIM ** -0.5 * LOG2E)
    col_scale = col_scale.at[0, COL_SQ:COL_SQ + BRANCH_WIDTH].set(HEAD_DIM ** -0.5)
    xf = x.reshape(t, d)
    xb = xf.astype(bf)

    for i in range(depth):
        w1 = _reorder_w_in(w_t, i)
        u, f_logits = _matmul(xb, w1, col_scale, bf, 1024, 1536, COL_FF, "in_proj")

        b_f = jnp.zeros((1, LANES), jnp.float32).at[0, :HEADS].set(fox_f_bias[i])
        cum = _fgate_cumsum(f_logits, b_f, batch, seq)[:, :HEADS] * LOG2E
        cum_bh = jnp.transpose(cum.reshape(batch, seq, HEADS), (0, 2, 1)).reshape(
            batch * HEADS, seq)
        cq_rows = cum_bh.reshape(batch * HEADS, 1, seq)
        ck_rep = jnp.broadcast_to(cum_bh[:, :, None], (batch * HEADS, seq, LANES))

        uq = mla_w_uq[i].reshape(MLA_Q_LORA, HEADS, MLA_NOPE + MLA_ROPE)
        uq = jnp.pad(uq, ((0, 0), (0, 0), (0, MLA_QK_PAD - MLA_NOPE - MLA_ROPE)))
        uq = uq.reshape(MLA_Q_LORA, HEADS * MLA_QK_PAD).astype(bf)
        ukv = mla_w_ukv[i].reshape(MLA_KV_LORA, HEADS, MLA_NOPE + HEAD_DIM)
        uk = ukv[:, :, :MLA_NOPE].reshape(MLA_KV_LORA, HEADS * MLA_NOPE).astype(bf)
        uv = ukv[:, :, MLA_NOPE:].reshape(MLA_KV_LORA, HEADS * HEAD_DIM).astype(bf)
        dqk, mq, mk, mv = _prep(u, tabs, mla_q_norm[i].reshape(1, -1),
                                mla_kv_norm[i].reshape(1, -1), uq, uk, uv)

        lam_init = 0.8 - 0.6 * math.exp(-0.3 * i)
        lp = diff_lambda[i].astype(jnp.float32)
        lam = (jnp.exp(jnp.sum(lp[0] * lp[1])) - jnp.exp(jnp.sum(lp[2] * lp[3]))
               + lam_init).reshape(1)
        blk = lambda col: col // HEAD_DIM
        o_d = _attention("diff", batch, seq, dqk, 0, dqk, HEADS, u, blk(COL_DV), HEAD_DIM,
                         extras=(diff_subln[i].reshape(1, HEAD_DIM),),
                         lam=lam, lam_init=lam_init)
        o_f = _attention("fox", batch, seq, u, blk(COL_FQ), u, blk(COL_FK), u, blk(COL_FV),
                         HEAD_DIM, extras=(cq_rows, ck_rep))
        o_m = _attention("mla", batch, seq, mq, 0, mk, 0, mv, 0, MLA_QK_PAD)
        o_s = _attention("sb", batch, seq, u, blk(COL_SQ), u, blk(COL_SK), u, blk(COL_SV),
                         HEAD_DIM)

        w_r = jnp.concatenate([w_router_group[i], w_router_expert[i]], axis=1)
        w_r = jnp.pad(w_r, ((0, 0), (0, LANES - w_r.shape[1])))
        wr_hi = w_r.astype(bf)
        wr_lo = (w_r - wr_hi.astype(jnp.float32)).astype(bf)
        h, h_rows, hb, logits = _merge(
            u, o_d, o_f, o_m, o_s, w_branch[i].astype(bf), w_o[i].astype(bf), xf,
            ln1_g[i].reshape(1, d), ln1_b[i].reshape(1, d), wr_hi, wr_lo, alpha)

        expert_idx, weights = _route(logits, b_router_group[i], b_router_expert[i], t)
        dest, row_tok, sched, n_used = _dispatch_plan(expert_idx, t)
        y_rows = _moe(sched, n_used, row_tok, h_rows, w_expert_gate, w_expert_up,
                      w_expert_down, i)
        xf, xb = _final(dest[:, 0], dest[:, 1], h, hb, y_rows, weights.astype(jnp.float32),
                        p[i].reshape(t, -1), w_ple_gate[i].astype(bf),
                        w_ple_proj[i].astype(bf), ln2_g[i].reshape(1, d),
                        ln2_b[i].reshape(1, d), alpha)
    return xf.reshape(batch, seq, d)
```

```python
import functools
import math

import jax
import jax.numpy as jnp
from jax import lax
from jax.experimental import pallas as pl
from jax.experimental.pallas import tpu as pltpu

HEAD_DIM = 128
HEADS = 4
N_BRANCHES = 4
BRANCH_WIDTH = HEADS * HEAD_DIM
DIFF_QK = 64
DIFF_ROT = DIFF_QK // 4
MLA_Q_LORA = 512
MLA_KV_LORA = 256
MLA_NOPE = 128
MLA_ROPE = 64
MLA_QK_PAD = 256
N_GROUPS = 4
EXPERTS_PER_GROUP = 8
N_EXPERTS = N_GROUPS * EXPERTS_PER_GROUP
TOP_K = 2
CHUNK = 64
ROPE_THETA = 500000.0
LN_EPS = 1e-5
RMS_EPS = 1e-6
NEG_BIG = -1e30
LOG2E = 1.4426950408889634
SB_EXP_UNDERFLOW = -104.0
LANES = 128
V7X_VMEM_LIMIT = 56 * 1024 * 1024

COL_GATES = 0
COL_DQ = 8192
COL_DV = COL_DQ + 1024
COL_FQ = COL_DV + 512
COL_FK = COL_FQ + 512
COL_FV = COL_FK + 512
COL_SQ = COL_FV + 512
COL_SK = COL_SQ + 512
COL_SV = COL_SK + 512
COL_CQ = COL_SV + 512
COL_CKV = COL_CQ + 512
COL_KR = COL_CKV + 256
COL_FF = COL_KR + 128
U_WIDTH = COL_FF + 128

ATT_TQ = 512
MOE_ROWS = 256


def _cparams(sem, vmem=V7X_VMEM_LIMIT):
    return pltpu.CompilerParams(dimension_semantics=sem, vmem_limit_bytes=vmem)


def _resident(shape, index_map):
    return pl.BlockSpec(shape, index_map, pipeline_mode=pl.Buffered(1))


def _matmul_body(x_ref, w_ref, c_ref, o_ref, f_ref, *, f_block, f_off):
    acc = lax.dot_general(x_ref[...], w_ref[...], (((1,), (1,)), ((), ())),
                          preferred_element_type=jnp.float32)
    o_ref[...] = (acc * c_ref[...]).astype(o_ref.dtype)

    @pl.when(pl.program_id(1) == f_block)
    def _():
        f_ref[...] = acc[:, f_off:f_off + LANES]


def _matmul(x, w_t, col_scale, out_dtype, tm, tn, f32_col, name):
    m, k = x.shape
    n = w_t.shape[0]
    tm, tn = min(tm, m), min(tn, n)
    body = functools.partial(_matmul_body, f_block=f32_col // tn, f_off=f32_col % tn)
    return pl.pallas_call(
        body,
        out_shape=(jax.ShapeDtypeStruct((m, n), out_dtype),
                   jax.ShapeDtypeStruct((m, LANES), jnp.float32)),
        grid=(m // tm, n // tn),
        in_specs=[pl.BlockSpec((tm, k), lambda i, j: (i, 0)),
                  pl.BlockSpec((tn, k), lambda i, j: (j, 0)),
                  pl.BlockSpec((1, tn), lambda i, j: (0, j))],
        out_specs=(pl.BlockSpec((tm, tn), lambda i, j: (i, j)),
                   pl.BlockSpec((tm, LANES), lambda i, j: (i, 0))),
        compiler_params=_cparams(("parallel", "arbitrary")),
        name=name,
    )(x, w_t, col_scale)


def _split3(x):
    hi = x.astype(jnp.bfloat16)
    r1 = x - hi.astype(jnp.float32)
    mid = r1.astype(jnp.bfloat16)
    lo = (r1 - mid.astype(jnp.float32)).astype(jnp.bfloat16)
    return hi, mid, lo


def _split2(x):
    hi = x.astype(jnp.bfloat16)
    mid = (x - hi.astype(jnp.float32)).astype(jnp.bfloat16)
    return hi, mid


def _dot(a, b):
    return jnp.dot(a, b, preferred_element_type=jnp.float32)


def _dot_nt(a, b):
    return lax.dot_general(a, b, (((1,), (1,)), ((), ())), preferred_element_type=jnp.float32)


def _dot_tn(a, b):
    return lax.dot_general(a, b, (((0,), (0,)), ((), ())), preferred_element_type=jnp.float32)


def _log_sigmoid(z):
    neg_abs = pltpu.bitcast(pltpu.bitcast(z, jnp.uint32) | jnp.uint32(0x80000000), jnp.float32)
    return jnp.minimum(z, 0.0) - jnp.log(1.0 + jnp.exp(neg_abs))


def _sigmoid(z):
    return 1.0 / (1.0 + jnp.exp(-z))


def _fgate_body(f_ref, b_ref, o_ref, carry_ref):
    @pl.when(pl.program_id(1) == 0)
    def _():
        carry_ref[...] = jnp.zeros_like(carry_ref)

    log_f = _log_sigmoid(f_ref[...] + b_ref[...])
    ts = log_f.shape[0]
    row = lax.broadcasted_iota(jnp.int32, (ts, ts), 0)
    col = lax.broadcasted_iota(jnp.int32, (ts, ts), 1)
    tri = (col <= row).astype(jnp.bfloat16)
    hi, mid, lo = _split3(log_f)
    cum = _dot(tri, hi) + _dot(tri, mid) + _dot(tri, lo) + carry_ref[...]
    o_ref[...] = cum
    carry_ref[...] = cum[ts - 1:ts, :]


def _fgate_cumsum(f_logits, b_f, batch, seq):
    t = f_logits.shape[0]
    ts = min(512, seq)
    ns = seq // ts
    return pl.pallas_call(
        _fgate_body,
        out_shape=jax.ShapeDtypeStruct((t, LANES), jnp.float32),
        grid=(batch, ns),
        in_specs=[pl.BlockSpec((ts, LANES), lambda b, s: (b * ns + s, 0)),
                  _resident((1, LANES), lambda b, s: (0, 0))],
        out_specs=pl.BlockSpec((ts, LANES), lambda b, s: (b * ns + s, 0)),
        scratch_shapes=[pltpu.VMEM((1, LANES), jnp.float32)],
        compiler_params=_cparams(("parallel", "arbitrary")),
        name="fgate_cumsum",
    )(f_logits, b_f)


def _rope_lanes(x, c, s_up, s_dn, half):
    return x * c + pltpu.roll(x, LANES - half, 1) * s_up + pltpu.roll(x, half, 1) * s_dn


def _prep_body(dqk_ref, cq_ref, ckv_ref, kr_ref, c16_ref, su16_ref, sd16_ref,
               c64_ref, su64_ref, sd64_ref, qg_ref, kvg_ref, wuq_ref, wuk_ref, wuv_ref,
               dqk_o, mq_o, mk_o, mv_o):
    c16, su16, sd16 = c16_ref[...], su16_ref[...], sd16_ref[...]
    c64, su64, sd64 = c64_ref[...], su64_ref[...], sd64_ref[...]
    for j in range(dqk_ref.shape[1] // LANES):
        sl = slice(j * LANES, (j + 1) * LANES)
        x = _rope_lanes(dqk_ref[:, sl].astype(jnp.float32), c16, su16, sd16, DIFF_ROT // 2)
        if j < HEADS:
            x = x * (DIFF_QK ** -0.5 * LOG2E)
        dqk_o[:, sl] = x.astype(dqk_o.dtype)

    cq = cq_ref[...].astype(jnp.float32)
    cqn = cq * lax.rsqrt(jnp.mean(cq * cq, axis=-1, keepdims=True) + RMS_EPS) * qg_ref[...]
    q = _dot(cqn.astype(jnp.bfloat16), wuq_ref[...])
    q = q * ((MLA_NOPE + MLA_ROPE) ** -0.5 * LOG2E)
    ckv = ckv_ref[...].astype(jnp.float32)
    ckvn = (ckv * lax.rsqrt(jnp.mean(ckv * ckv, axis=-1, keepdims=True) + RMS_EPS)
            * kvg_ref[...]).astype(jnp.bfloat16)
    kn = _dot(ckvn, wuk_ref[...])
    mv_o[...] = _dot(ckvn, wuv_ref[...]).astype(mv_o.dtype)
    kr = _rope_lanes(kr_ref[...].astype(jnp.float32), c64, su64, sd64, MLA_ROPE // 2)
    for h in range(HEADS):
        base = h * MLA_QK_PAD
        mq_o[:, base:base + LANES] = q[:, base:base + LANES].astype(mq_o.dtype)
        qr = _rope_lanes(q[:, base + LANES:base + 2 * LANES], c64, su64, sd64, MLA_ROPE // 2)
        mq_o[:, base + LANES:base + 2 * LANES] = qr.astype(mq_o.dtype)
        mk_o[:, base:base + LANES] = kn[:, h * LANES:(h + 1) * LANES].astype(mk_o.dtype)
        mk_o[:, base + LANES:base + 2 * LANES] = kr.astype(mk_o.dtype)


def _prep(u, tabs, q_norm, kv_norm, w_uq, w_uk, w_uv):
    t = u.shape[0]
    tm = min(512, t)
    row = lambda blk: (lambda i: (i, blk))
    const = lambda i: (0, 0)
    tab_spec = pl.BlockSpec((tm, LANES), lambda i: (i, 0))
    outs = pl.pallas_call(
        _prep_body,
        out_shape=(jax.ShapeDtypeStruct((t, 1024), jnp.bfloat16),
                   jax.ShapeDtypeStruct((t, HEADS * MLA_QK_PAD), jnp.bfloat16),
                   jax.ShapeDtypeStruct((t, HEADS * MLA_QK_PAD), jnp.bfloat16),
                   jax.ShapeDtypeStruct((t, BRANCH_WIDTH), jnp.bfloat16)),
        grid=(t // tm,),
        in_specs=[pl.BlockSpec((tm, 1024), row(COL_DQ // 1024)),
                  pl.BlockSpec((tm, MLA_Q_LORA), row(COL_CQ // MLA_Q_LORA)),
                  pl.BlockSpec((tm, MLA_KV_LORA), row(COL_CKV // MLA_KV_LORA)),
                  pl.BlockSpec((tm, LANES), row(COL_KR // LANES)),
                  tab_spec, tab_spec, tab_spec, tab_spec, tab_spec, tab_spec,
                  _resident((1, MLA_Q_LORA), const),
                  _resident((1, MLA_KV_LORA), const),
                  _resident(w_uq.shape, const),
                  _resident(w_uk.shape, const),
                  _resident(w_uv.shape, const)],
        out_specs=(pl.BlockSpec((tm, 1024), lambda i: (i, 0)),
                   pl.BlockSpec((tm, HEADS * MLA_QK_PAD), lambda i: (i, 0)),
                   pl.BlockSpec((tm, HEADS * MLA_QK_PAD), lambda i: (i, 0)),
                   pl.BlockSpec((tm, BRANCH_WIDTH), lambda i: (i, 0))),
        compiler_params=_cparams(("parallel",)),
        name="prep",
    )(u, u, u, u, *tabs, q_norm, kv_norm, w_uq, w_uk, w_uv)
    return outs


def _attn_body(*refs, kind, tq, tk, lam_init):
    softmax = kind != "sb"
    if kind == "diff":
        lam_ref, q_ref, k_ref, v_ref, g_ref, o_ref = refs[:6]
    elif kind == "fox":
        q_ref, k_ref, v_ref, cq_ref, ck_ref, o_ref = refs[:6]
    else:
        q_ref, k_ref, v_ref, o_ref = refs[:4]
    if softmax:
        s_bufs, p_bufs = refs[-8:-6], refs[-6:-4]
        vt_ref, m_ref, a_ref, acc_ref = refs[-4:]
    else:
        s_bufs, p_bufs = refs[-7:-5], refs[-5:-3]
        vt_ref, r_ref, acc_ref = refs[-3:]
    qi = pl.program_id(2)

    @pl.when(qi == 0)
    def _():
        for kb in range(vt_ref.shape[0]):
            vt_ref[kb, 0:HEAD_DIM, :] = v_ref[kb * tk:(kb + 1) * tk, :].T
            if softmax:
                vt_ref[kb, HEAD_DIM:, :] = jnp.ones((vt_ref.shape[1] - HEAD_DIM, tk),
                                                    vt_ref.dtype)

    q0 = qi * tq
    last_kb = 2 * qi + 1
    q = q_ref[...]
    n_comp = 2 if kind == "diff" else 1
    if kind == "diff":
        lane = lax.broadcasted_iota(jnp.int32, q.shape, 1)
        qs = [jnp.where(lane < DIFF_QK, q, jnp.zeros_like(q)),
              jnp.where(lane >= DIFF_QK, q, jnp.zeros_like(q))]
    else:
        qs = [q]

    acc_ref[...] = jnp.zeros_like(acc_ref)
    p_bufs[1][...] = jnp.zeros_like(p_bufs[1])
    if softmax:
        m_ref[...] = jnp.full_like(m_ref, NEG_BIG)
        a_ref[...] = jnp.zeros_like(a_ref)
    else:
        r_ref[...] = jnp.zeros_like(r_ref)

    def key_rows(kb):
        if isinstance(kb, int):
            return pl.ds(kb * tk, tk)
        return pl.ds(pl.multiple_of(kb * tk, tk), tk)

    def scores_stage(kb, s_dst):
        k_t = k_ref[key_rows(kb), :]
        for c in range(n_comp):
            s_dst[c] = _dot_nt(k_t, qs[c])

    def values_stage(kb, p_src):
        v_t = vt_ref[kb]
        for c in range(n_comp):
            pv = _dot(v_t, p_src[c])[:acc_ref.shape[1]]
            if softmax:
                acc_ref[c] = a_ref[c] * acc_ref[c] + pv
            else:
                acc_ref[c] = acc_ref[c] + pv

    def weights_stage(kb, s_src, p_dst, masked):
        if masked:
            j = lax.broadcasted_iota(jnp.int32, (tk, tq), 0)
            i = lax.broadcasted_iota(jnp.int32, (tk, tq), 1)
            off = q0 - kb * tk
            if kind in ("diff", "mla"):
                mask = (j // CHUNK - i // CHUNK) <= off // CHUNK
            elif kind == "fox":
                mask = (j - i) <= off
            else:
                mask = (j - i) < off
        if not softmax:
            z_t = s_src[0]
            ls = _log_sigmoid(z_t)
            lk = ls - z_t
            if masked:
                lk = jnp.where(mask, lk, 0.0)
            jj = lax.broadcasted_iota(jnp.int32, (tk, tk), 0)
            mm = lax.broadcasted_iota(jnp.int32, (tk, tk), 1)
            upper = (mm > jj).astype(jnp.bfloat16)
            hi, mid = _split2(lk)
            suffix = _dot(jnp.concatenate([upper, upper], axis=1),
                          jnp.concatenate([hi, mid], axis=0))
            r_old = r_ref[...]
            w_t = jnp.exp(ls + (r_old + suffix))
            if masked:
                w_t = jnp.where(mask, w_t, 0.0)
            p_dst[0] = w_t.astype(jnp.bfloat16)
            r_ref[...] = r_old + suffix[0:1, :] + lk[0:1, :]
            return
        for c in range(n_comp):
            s_t = s_src[c]
            if kind == "fox":
                ck = ck_ref[key_rows(kb), :]
                s_t = s_t + cq_ref[...] - jnp.concatenate([ck] * (tq // LANES), axis=1)
            if masked:
                s_t = jnp.where(mask, s_t, NEG_BIG)
            m_old = m_ref[c]
            m_new = jnp.maximum(m_old, jnp.max(s_t, axis=0, keepdims=True))
            alpha = jnp.exp2(m_old - m_new)
            p_t = jnp.exp2(s_t - m_new)
            m_ref[c] = m_new
            a_ref[c] = alpha
            p_dst[c] = p_t.astype(jnp.bfloat16)

    def step(kb, parity, masked, next_kb, prev_kb):
        if next_kb is not None:
            scores_stage(next_kb, s_bufs[1 - parity])
        values_stage(prev_kb, p_bufs[1 - parity])
        weights_stage(kb, s_bufs[parity], p_bufs[parity], masked)

    if softmax and tq == 2 * tk:
        scores_stage(0, s_bufs[0])

        def body(t, carry):
            kb = 2 * t
            step(kb, 0, False, kb + 1, jnp.maximum(kb - 1, 0))
            step(kb + 1, 1, False, kb + 2, kb)
            return carry
        lax.fori_loop(0, qi, body, 0)
        step(last_kb - 1, 0, True, last_kb, jnp.maximum(last_kb - 2, 0))
        step(last_kb, 1, True, None, last_kb - 1)
        values_stage(last_kb, p_bufs[1])
    elif softmax:
        scores_stage(0, s_bufs[0])

        def body(t, carry):
            kb = 2 * t
            step(kb, 0, False, kb + 1, jnp.maximum(kb - 1, 0))
            step(kb + 1, 1, False, kb + 2, kb)
            return carry
        lax.fori_loop(0, qi // 2, body, 0)

        @pl.when(qi % 2 == 0)
        def _():
            step(qi, 0, True, None, jnp.maximum(qi - 1, 0))
            values_stage(qi, p_bufs[0])

        @pl.when(qi % 2 == 1)
        def _():
            step(qi - 1, 0, False, qi, jnp.maximum(qi - 2, 0))
            step(qi, 1, True, None, qi - 1)
            values_stage(qi, p_bufs[1])
    else:
        scores_stage(last_kb, s_bufs[0])
        step(last_kb, 0, True, last_kb - 1, last_kb)
        step(last_kb - 1, 1, True, jnp.maximum(last_kb - 2, 0), last_kb)

        def live(carry):
            t, _, r_max = carry
            return (t < qi) & (r_max >= SB_EXP_UNDERFLOW)

        def body(carry):
            t = carry[0]
            kb = last_kb - 2 - 2 * t
            step(kb, 0, False, kb - 1, kb + 1)
            step(kb - 1, 1, False, jnp.maximum(kb - 2, 0), kb)
            return t + 1, kb - 1, jnp.max(r_ref[...])
        done = lax.while_loop(live, body, (0, last_kb - 1, jnp.max(r_ref[...])))
        values_stage(done[1], p_bufs[1])

    def normalised(c):
        return acc_ref[c, 0:HEAD_DIM, :] / acc_ref[c, HEAD_DIM:HEAD_DIM + 1, :]

    if kind == "sb":
        o_t = acc_ref[0]
    elif kind == "diff":
        o_t = normalised(0) - lam_ref[0] * normalised(1)
    else:
        o_t = normalised(0)
    o = o_t.T
    if kind == "diff":
        o = (o * lax.rsqrt(jnp.mean(o * o, axis=-1, keepdims=True) + RMS_EPS)
             * g_ref[...] * (1.0 - lam_init))
    o_ref[...] = o.astype(o_ref.dtype)


def _attention(kind, batch, seq, q_arr, q_col, k_arr, k_col, v_arr, v_col, dq,
               extras=(), lam=None, lam_init=0.0):
    t = q_arr.shape[0]
    tq = min(ATT_TQ, seq)
    tk = tq if kind in ("diff", "mla") else tq // 2
    nq = seq // tq
    q_spec = pl.BlockSpec((tq, dq), lambda b, h, i, *_: (b * nq + i, q_col + h))
    k_spec = pl.BlockSpec((seq, dq), lambda b, h, i, *_: (b, k_col + h))
    v_spec = pl.BlockSpec((seq, HEAD_DIM), lambda b, h, i, *_: (b, v_col + h))
    in_specs = [q_spec, k_spec, v_spec]
    args = [q_arr, k_arr, v_arr]
    n_comp = 2 if kind == "diff" else 1
    if kind == "diff":
        in_specs.append(_resident((1, HEAD_DIM), lambda b, h, i, *_: (0, 0)))
        args.append(extras[0])
    elif kind == "fox":
        cq_rows, ck_rep = extras
        in_specs.append(pl.BlockSpec((None, 1, tq), lambda b, h, i: (b * HEADS + h, 0, i)))
        in_specs.append(pl.BlockSpec((None, seq, LANES), lambda b, h, i: (b * HEADS + h, 0, 0)))
        args += [cq_rows, ck_rep]
    s_buf = pltpu.VMEM((n_comp, tk, tq), jnp.float32)
    p_buf = pltpu.VMEM((n_comp, tk, tq), jnp.bfloat16)
    row = pltpu.VMEM((n_comp, 1, tq), jnp.float32)
    nk = seq // tk
    if kind == "sb":
        scratch = [s_buf, s_buf, p_buf, p_buf,
                   pltpu.VMEM((nk, HEAD_DIM, tk), jnp.bfloat16),
                   pltpu.VMEM((1, tq), jnp.float32),
                   pltpu.VMEM((1, HEAD_DIM, tq), jnp.float32)]
    else:
        scratch = [s_buf, s_buf, p_buf, p_buf,
                   pltpu.VMEM((nk, HEAD_DIM + 16, tk), jnp.bfloat16), row, row,
                   pltpu.VMEM((n_comp, HEAD_DIM + 8, tq), jnp.float32)]
    body = functools.partial(_attn_body, kind=kind, tq=tq, tk=tk, lam_init=lam_init)
    grid = (batch, HEADS, nq)
    out_shape = jax.ShapeDtypeStruct((t, BRANCH_WIDTH), jnp.bfloat16)
    out_spec = pl.BlockSpec((tq, HEAD_DIM), lambda b, h, i, *_: (b * nq + i, h))
    cp = _cparams(("parallel", "parallel", "arbitrary"))
    if kind == "diff":
        return pl.pallas_call(
            body, out_shape=out_shape,
            grid_spec=pltpu.PrefetchScalarGridSpec(
                num_scalar_prefetch=1, grid=grid, in_specs=in_specs, out_specs=out_spec,
                scratch_shapes=scratch),
            compiler_params=cp, name="attn_" + kind)(lam, *args)
    return pl.pallas_call(
        body, out_shape=out_shape, grid=grid, in_specs=in_specs, out_specs=out_spec,
        scratch_shapes=scratch, compiler_params=cp, name="attn_" + kind)(*args)


def _layer_norm(y, g, b):
    mu = jnp.mean(y, axis=-1, keepdims=True)
    yc = y - mu
    var = jnp.mean(yc * yc, axis=-1, keepdims=True)
    return yc * lax.rsqrt(var + LN_EPS) * g + b


def _merge_body(g_ref, od_ref, of_ref, om_ref, os_ref, wb_ref, wo_ref, x_ref, lg_ref, lb_ref,
                wrh_ref, wrl_ref, h_o, hr_o, hb_o, lgt_o, *, alpha):
    d = x_ref.shape[1]
    merged = None
    for n, o_ref in enumerate((od_ref, of_ref, om_ref, os_ref)):
        gate = _sigmoid(g_ref[:, n * d:(n + 1) * d].astype(jnp.float32))
        term = gate * _dot(o_ref[...], wb_ref[n])
        merged = term if merged is None else merged + term
    mix = _dot(merged.astype(jnp.bfloat16), wo_ref[...])
    h = _layer_norm(alpha * x_ref[...] + mix, lg_ref[...], lb_ref[...])
    h_o[...] = h
    h_hi = h.astype(jnp.bfloat16)
    hb_o[...] = h_hi
    _to_chunked(hr_o, _pack_bf16_halves(h_hi))
    h_lo = (h - h_hi.astype(jnp.float32)).astype(jnp.bfloat16)
    lgt_o[...] = (_dot(h_hi, wrh_ref[...]) + _dot(h_lo, wrh_ref[...]) + _dot(h_hi, wrl_ref[...]))


def _merge(u, o_d, o_f, o_m, o_s, w_b, w_o, x, ln_g, ln_b, wr_hi, wr_lo, alpha):
    t, d = x.shape
    tm = min(256, t)
    row = lambda i: (i, 0)
    const2 = lambda i: (0, 0)
    o_spec = pl.BlockSpec((tm, BRANCH_WIDTH), row)
    return pl.pallas_call(
        functools.partial(_merge_body, alpha=alpha),
        out_shape=(jax.ShapeDtypeStruct((t, d), jnp.float32),
                   jax.ShapeDtypeStruct((t * _chunk_pitch(d // 2), LANES), jnp.uint32),
                   jax.ShapeDtypeStruct((t, d), jnp.bfloat16),
                   jax.ShapeDtypeStruct((t, LANES), jnp.float32)),
        grid=(t // tm,),
        in_specs=[pl.BlockSpec((tm, N_BRANCHES * d), row),
                  o_spec, o_spec, o_spec, o_spec,
                  _resident(w_b.shape, lambda i: (0, 0, 0)),
                  _resident(w_o.shape, const2),
                  pl.BlockSpec((tm, d), row),
                  _resident((1, d), const2), _resident((1, d), const2),
                  _resident(wr_hi.shape, const2), _resident(wr_lo.shape, const2)],
        out_specs=(pl.BlockSpec((tm, d), row),
                   pl.BlockSpec((tm * _chunk_pitch(d // 2), LANES), row),
                   pl.BlockSpec((tm, d), row), pl.BlockSpec((tm, LANES), row)),
        compiler_params=_cparams(("parallel",)),
        name="merge_ln1",
    )(u, o_d, o_f, o_m, o_s, w_b, w_o, x, ln_g, ln_b, wr_hi, wr_lo)


CHUNK_PAD = 2


def _chunk_pitch(d):
    return d // LANES + CHUNK_PAD


def _to_chunked(ref, x):
    n, d = x.shape
    nc, pitch = d // LANES, _chunk_pitch(d)
    for c in range(nc):
        ref[pl.ds(c, n, stride=pitch), :] = x[:, c * LANES:(c + 1) * LANES]
    for c in range(nc, pitch):
        ref[pl.ds(c, n, stride=pitch), :] = jnp.zeros((n, LANES), x.dtype)


def _from_chunked(ref, n, d):
    pitch = _chunk_pitch(d)
    return jnp.concatenate([ref[pl.ds(c, n, stride=pitch), :] for c in range(d // LANES)],
                           axis=1)


def _gather_rows(idx_ref, base, src_hbm, dst, sem, n, d):
    nc, pitch = d // LANES, _chunk_pitch(d)
    for r in range(n):
        row = pl.multiple_of(idx_ref[base + r] * pitch, CHUNK_PAD)
        pltpu.make_async_copy(src_hbm.at[pl.ds(row, nc), :], dst.at[pl.ds(r * pitch, nc), :],
                              sem).start(priority=r % 2)


def _wait_rows(src_hbm, dst, sem, n, d):
    rows = n * (d // LANES)
    pltpu.make_async_copy(src_hbm.at[pl.ds(0, rows), :], dst.at[pl.ds(0, rows), :], sem).wait()


def _pack_bf16_halves(x):
    bits = pltpu.bitcast(x.astype(jnp.float32), jnp.uint32)
    half = x.shape[1] // 2
    return bits[:, :half] | (bits[:, half:] >> 16)


def _unpack_bf16_halves(w):
    left = pltpu.bitcast(w & jnp.uint32(0xFFFF0000), jnp.float32)
    right = pltpu.bitcast(w << 16, jnp.float32)
    return jnp.concatenate([left, right], axis=1).astype(jnp.bfloat16)


def _moe_body(sched_ref, n_used_ref, tok_ref, h_hbm, wg_hbm, wu_hbm, wd_hbm, y_ref,
              xbuf, sem, wg_f, wu_f, wd_f, wsem, wg_b, wu_b, wd_b, *, layer):
    i = pl.program_id(0)
    n_used = n_used_ref[0]
    d = wg_b.shape[0] // 2
    bm = xbuf.shape[1] // _chunk_pitch(d)
    slot = i % 2

    def weight_copies(e, s):
        return (pltpu.make_async_copy(wg_hbm.at[layer, e], wg_f.at[s], wsem.at[s]),
                pltpu.make_async_copy(wu_hbm.at[layer, e], wu_f.at[s], wsem.at[s]),
                pltpu.make_async_copy(wd_hbm.at[layer, e], wd_f.at[s], wsem.at[s]))

    @pl.when((i == 0) & (n_used > 0))
    def _():
        for cp in weight_copies(sched_ref[0, 0], 0):
            cp.start(priority=1)
        _gather_rows(tok_ref, 0, h_hbm, xbuf.at[0], sem.at[0], bm, d)

    @pl.when(i + 1 < n_used)
    def _():
        _gather_rows(tok_ref, (i + 1) * bm, h_hbm, xbuf.at[1 - slot], sem.at[1 - slot], bm, d)

    @pl.when(i < n_used)
    def _():
        @pl.when(sched_ref[1, i] == 1)
        def _():
            ws = sched_ref[2, i]
            for cp in weight_copies(sched_ref[0, i], ws):
                cp.wait()
            wg_b[...] = wg_f[ws].astype(jnp.bfloat16)
            wu_b[...] = wu_f[ws].astype(jnp.bfloat16)
            wd_b[...] = wd_f[ws].astype(jnp.bfloat16)

            @pl.when(sched_ref[3, i] >= 0)
            def _():
                for cp in weight_copies(sched_ref[3, i], 1 - ws):
                    cp.start(priority=1)

        _wait_rows(h_hbm, xbuf.at[slot], sem.at[slot], bm, d)
        x = _unpack_bf16_halves(_from_chunked(xbuf.at[slot], bm, d))
        a = _dot(x, wg_b[...])
        hid = (a * _sigmoid(a)) * _dot(x, wu_b[...])
        _to_chunked(y_ref, _dot(hid.astype(jnp.bfloat16), wd_b[...]))

    @pl.when(i >= n_used)
    def _():
        y_ref[...] = jnp.zeros_like(y_ref)


def _moe(sched, n_used, row_tok, h_rows, w_g, w_u, w_d, layer):
    n_rows = row_tok.shape[0]
    d, hid = w_g.shape[2], w_g.shape[3]
    pitch = _chunk_pitch(d)
    bm = MOE_ROWS
    any_spec = pl.BlockSpec(memory_space=pl.ANY)
    return pl.pallas_call(
        functools.partial(_moe_body, layer=layer),
        out_shape=jax.ShapeDtypeStruct((n_rows * pitch, LANES), jnp.float32),
        grid_spec=pltpu.PrefetchScalarGridSpec(
            num_scalar_prefetch=3, grid=(n_rows // bm,),
            in_specs=[any_spec, any_spec, any_spec, any_spec],
            out_specs=pl.BlockSpec((bm * pitch, LANES), lambda i, s, n, tok: (i, 0)),
            scratch_shapes=[pltpu.VMEM((2, bm * _chunk_pitch(d // 2), LANES), jnp.uint32),
                            pltpu.SemaphoreType.DMA((2,)),
                            pltpu.VMEM((2, d, hid), jnp.float32),
                            pltpu.VMEM((2, d, hid), jnp.float32),
                            pltpu.VMEM((2, hid, d), jnp.float32),
                            pltpu.SemaphoreType.DMA((2,)),
                            pltpu.VMEM((d, hid), jnp.bfloat16),
                            pltpu.VMEM((d, hid), jnp.bfloat16),
                            pltpu.VMEM((hid, d), jnp.bfloat16)]),
        compiler_params=_cparams(("arbitrary",)),
        name="moe_experts",
    )(sched, n_used, row_tok, h_rows, w_g, w_u, w_d)


def _final_body(d0_ref, d1_ref, h_ref, hb_ref, y_hbm, gw_ref, p_ref, wg_ref, wp_ref, lg_ref,
                lb_ref, y_o, yb_o, gbuf, sem, *, alpha):
    i = pl.program_id(0)
    n = pl.num_programs(0)
    tm, d = h_ref.shape
    slot = i % 2

    def gather(step, s):
        _gather_rows(d0_ref, step * tm, y_hbm, gbuf.at[s, 0], sem.at[s], tm, d)
        _gather_rows(d1_ref, step * tm, y_hbm, gbuf.at[s, 1], sem.at[s], tm, d)

    @pl.when(i == 0)
    def _():
        gather(0, 0)

    @pl.when(i + 1 < n)
    def _():
        gather(i + 1, 1 - slot)

    gate = _sigmoid(_dot(hb_ref[...], wg_ref[...]))
    proj = _dot(p_ref[...].astype(jnp.bfloat16), wp_ref[...])
    _wait_rows(y_hbm, gbuf.at[slot, 0], sem.at[slot], tm, d)
    _wait_rows(y_hbm, gbuf.at[slot, 1], sem.at[slot], tm, d)
    gw = gw_ref[...]
    ffn = (_from_chunked(gbuf.at[slot, 0], tm, d) * gw[:, 0:1]
           + _from_chunked(gbuf.at[slot, 1], tm, d) * gw[:, 1:2])
    y = _layer_norm(alpha * h_ref[...] + ffn + gate * proj, lg_ref[...], lb_ref[...])
    y_o[...] = y
    yb_o[...] = y.astype(jnp.bfloat16)


def _final(dest0, dest1, h, hb, y_rows, gate_w, p, w_pg, w_pp, ln_g, ln_b, alpha):
    t, d = h.shape
    tm = min(256, t)
    row = lambda i, *_: (i, 0)
    const2 = lambda i, *_: (0, 0)
    return pl.pallas_call(
        functools.partial(_final_body, alpha=alpha),
        out_shape=(jax.ShapeDtypeStruct((t, d), jnp.float32),
                   jax.ShapeDtypeStruct((t, d), jnp.bfloat16)),
        grid_spec=pltpu.PrefetchScalarGridSpec(
            num_scalar_prefetch=2, grid=(t // tm,),
            in_specs=[pl.BlockSpec((tm, d), row), pl.BlockSpec((tm, d), row),
                      pl.BlockSpec(memory_space=pl.ANY),
                      pl.BlockSpec((tm, TOP_K), row), pl.BlockSpec((tm, p.shape[1]), row),
                      _resident(w_pg.shape, const2), _resident(w_pp.shape, const2),
                      _resident((1, d), const2), _resident((1, d), const2)],
            out_specs=(pl.BlockSpec((tm, d), row), pl.BlockSpec((tm, d), row)),
            scratch_shapes=[pltpu.VMEM((2, TOP_K, tm * _chunk_pitch(d), LANES), jnp.float32),
                            pltpu.SemaphoreType.DMA((2,))]),
        compiler_params=_cparams(("arbitrary",)),
        name="final_ln2",
    )(dest0, dest1, h, hb, y_rows, gate_w, p, w_pg, w_pp, ln_g, ln_b)


def _rope_tables(positions):
    pos = positions.reshape(-1).astype(jnp.float32)
    lane = jnp.arange(LANES)

    def tables(rot, period):
        half = rot // 2
        inv_freq = ROPE_THETA ** (-jnp.arange(half, dtype=jnp.float32) / half)
        ang = pos[:, None] * inv_freq
        cos, sin = jnp.cos(ang), jnp.sin(ang)
        lp = lane % period
        idx = lp % half
        in_rot = lp < rot
        c = jnp.where(in_rot[None, :], cos[:, idx], 1.0)
        s_up = jnp.where((lp < half)[None, :], -sin[:, idx], 0.0)
        s_dn = jnp.where(((lp >= half) & in_rot)[None, :], sin[:, idx], 0.0)
        return c, s_up, s_dn

    return tables(DIFF_ROT, DIFF_QK) + tables(MLA_ROPE, LANES)


def _w_in_segments(d):
    widths = [512, 512, 512, 512, 512, 512, HEADS, MLA_Q_LORA, MLA_KV_LORA, MLA_ROPE,
              512, 512, 512, N_BRANCHES * d]
    dsts = [COL_DQ, COL_DQ + 512, COL_DV, COL_FQ, COL_FK, COL_FV, COL_FF, COL_CQ, COL_CKV,
            COL_KR, COL_SQ, COL_SK, COL_SV, COL_GATES]
    segs, src = [], 0
    for wd, dst in zip(widths, dsts):
        segs.append((src, dst, wd))
        src += wd
    return segs


def _w_in_body(src_ref, valid_ref, wt_hbm, o_ref, buf, sem, *, layer):
    b = pl.program_id(0)
    slot = b % 2
    rows = buf.shape[1]

    def copy(blk, s):
        return pltpu.make_async_copy(wt_hbm.at[pl.ds(src_ref[blk], rows), layer, :],
                                     buf.at[s], sem.at[s])

    @pl.when(b == 0)
    def _():
        copy(0, 0).start()

    @pl.when(b + 1 < pl.num_programs(0))
    def _():
        copy(b + 1, 1 - slot).start()

    copy(b, slot).wait()
    r = lax.broadcasted_iota(jnp.int32, buf.shape[1:], 0)
    o_ref[...] = jnp.where(r < valid_ref[b], buf[slot], 0.0).astype(o_ref.dtype)


def _reorder_w_in(w_t, layer):
    n_in, _, d = w_t.shape
    src, valid = [0] * (U_WIDTH // LANES), [0] * (U_WIDTH // LANES)
    for s0, dst, wd in _w_in_segments(d):
        for off in range(0, wd, LANES):
            blk = (dst + off) // LANES
            src[blk], valid[blk] = s0 + off, min(LANES, wd - off)
    assert max(s + LANES for s in src) <= n_in
    return pl.pallas_call(
        functools.partial(_w_in_body, layer=layer),
        out_shape=jax.ShapeDtypeStruct((U_WIDTH, d), jnp.bfloat16),
        grid_spec=pltpu.PrefetchScalarGridSpec(
            num_scalar_prefetch=2, grid=(U_WIDTH // LANES,),
            in_specs=[pl.BlockSpec(memory_space=pl.ANY)],
            out_specs=pl.BlockSpec((LANES, d), lambda b, s, v: (b, 0)),
            scratch_shapes=[pltpu.VMEM((2, LANES, d), jnp.float32),
                            pltpu.SemaphoreType.DMA((2,))]),
        compiler_params=_cparams(("arbitrary",)),
        name="w_in_relayout",
    )(jnp.asarray(src, jnp.int32), jnp.asarray(valid, jnp.int32), w_t)


def _route(logits, b_rg, b_re, t):
    def first_max(v, n):
        top = jnp.max(v, axis=-1, keepdims=True)
        ids = jnp.arange(n, dtype=jnp.int32)
        return top, jnp.min(jnp.where(v == top, ids, n), axis=-1, keepdims=True)

    g_logits = logits[:, :N_GROUPS] + b_rg
    g_max, grp = first_max(g_logits, N_GROUPS)
    p_grp = 1.0 / jnp.sum(jnp.exp(g_logits - g_max), axis=-1, keepdims=True)
    e_logits = (logits[:, N_GROUPS:N_GROUPS + N_EXPERTS] + b_re).reshape(
        t, N_GROUPS, EXPERTS_PER_GROUP)
    in_grp = jnp.arange(N_GROUPS, dtype=jnp.int32)[None, :, None] == grp[:, :, None]
    e_in = jnp.sum(jnp.where(in_grp, e_logits, 0.0), axis=1)
    probs = jax.nn.softmax(e_in, axis=-1)
    p1, i1 = first_max(probs, EXPERTS_PER_GROUP)
    rest = jnp.where(jnp.arange(EXPERTS_PER_GROUP, dtype=jnp.int32)[None, :] == i1, -1.0, probs)
    p2, i2 = first_max(rest, EXPERTS_PER_GROUP)
    top_p = jnp.concatenate([p1, p2], axis=-1)
    weights = p_grp * top_p / jnp.sum(top_p, axis=-1, keepdims=True)
    expert_idx = grp * EXPERTS_PER_GROUP + jnp.concatenate([i1, i2], axis=-1)
    return expert_idx.astype(jnp.int32), weights


def _dispatch_plan(expert_idx, t):
    a = t * TOP_K
    flat_e = expert_idx.reshape(a)
    onehot = (flat_e[:, None] == jnp.arange(N_EXPERTS)[None, :]).astype(jnp.int32)
    sizes = jnp.sum(onehot, axis=0)
    padded = (sizes + MOE_ROWS - 1) // MOE_ROWS * MOE_ROWS
    pad_end = jnp.cumsum(padded)
    pad_start = pad_end - padded
    dest = (jnp.sum(onehot * (jnp.cumsum(onehot, axis=0) + pad_start[None, :]), axis=1)
            - 1).astype(jnp.int32)
    n_rows = a + N_EXPERTS * MOE_ROWS
    n_blk = n_rows // MOE_ROWS
    flat_tok = jnp.arange(a, dtype=jnp.int32) // TOP_K
    row_tok = jnp.zeros((n_rows,), jnp.int32).at[dest].set(flat_tok)
    blk_start = jnp.arange(n_blk, dtype=pad_end.dtype) * MOE_ROWS
    blk_e = jnp.minimum(jnp.sum(pad_end[None, :] <= blk_start[:, None], axis=1),
                        N_EXPERTS - 1).astype(jnp.int32)
    n_used = (pad_end[-1] // MOE_ROWS).astype(jnp.int32).reshape(1)
    first = jnp.concatenate([jnp.ones((1,), jnp.int32),
                             (blk_e[1:] != blk_e[:-1]).astype(jnp.int32)])
    parity = (jnp.cumsum(first) - 1) % 2
    experts = jnp.arange(N_EXPERTS, dtype=jnp.int32)
    used = jnp.where(padded > 0, experts, N_EXPERTS)
    later = jnp.flip(lax.cummin(jnp.flip(used)))
    next_used = jnp.concatenate([later[1:], jnp.full((1,), N_EXPERTS, jnp.int32)])
    next_used = jnp.where(next_used < N_EXPERTS, next_used, -1)
    sched = jnp.stack([blk_e, first, parity.astype(jnp.int32),
                       next_used[blk_e].astype(jnp.int32)])
    return dest.reshape(t, TOP_K), row_tok, sched, n_used


def kernel(x, p, positions, w_in, fox_f_bias, diff_lambda, diff_subln, mla_q_norm, mla_kv_norm,
           mla_w_uq, mla_w_ukv, w_branch, w_o, ln1_g, ln1_b, w_router_group, b_router_group,
           w_router_expert, b_router_expert, w_expert_gate, w_expert_up, w_expert_down,
           w_ple_gate, w_ple_proj, ln2_g, ln2_b):
    batch, seq, d = x.shape
    depth = w_in.shape[0]
    t = batch * seq
    alpha = (2 * depth) ** 0.25
    bf = jnp.bfloat16
    tabs = _rope_tables(positions)
    w_t = jnp.transpose(w_in, (2, 0, 1))
    col_scale = jnp.ones((1, U_WIDTH), jnp.float32)
    col_scale = col_scale.at[0, COL_FQ:COL_FQ + BRANCH_WIDTH].set(HEAD_DIM ** -0.5 * LOG2E)
    col_scale = col_scale.at[0, COL_SQ:COL_SQ + BRANCH_WIDTH].set(HEAD_DIM ** -0.5)
    xf = x.reshape(t, d)
    xb = xf.astype(bf)

    for i in range(depth):
        w1 = _reorder_w_in(w_t, i)
        u, f_logits = _matmul(xb, w1, col_scale, bf, 1024, 1536, COL_FF, "in_proj")

        b_f = jnp.zeros((1, LANES), jnp.float32).at[0, :HEADS].set(fox_f_bias[i])
        cum = _fgate_cumsum(f_logits, b_f, batch, seq)[:, :HEADS] * LOG2E
        cum_bh = jnp.transpose(cum.reshape(batch, seq, HEADS), (0, 2, 1)).reshape(
            batch * HEADS, seq)
        cq_rows = cum_bh.reshape(batch * HEADS, 1, seq)
        ck_rep = jnp.broadcast_to(cum_bh[:, :, None], (batch * HEADS, seq, LANES))

        uq = mla_w_uq[i].reshape(MLA_Q_LORA, HEADS, MLA_NOPE + MLA_ROPE)
        uq = jnp.pad(uq, ((0, 0), (0, 0), (0, MLA_QK_PAD - MLA_NOPE - MLA_ROPE)))
        uq = uq.reshape(MLA_Q_LORA, HEADS * MLA_QK_PAD).astype(bf)
        ukv = mla_w_ukv[i].reshape(MLA_KV_LORA, HEADS, MLA_NOPE + HEAD_DIM)
        uk = ukv[:, :, :MLA_NOPE].reshape(MLA_KV_LORA, HEADS * MLA_NOPE).astype(bf)
        uv = ukv[:, :, MLA_NOPE:].reshape(MLA_KV_LORA, HEADS * HEAD_DIM).astype(bf)
        dqk, mq, mk, mv = _prep(u, tabs, mla_q_norm[i].reshape(1, -1),
                                mla_kv_norm[i].reshape(1, -1), uq, uk, uv)

        lam_init = 0.8 - 0.6 * math.exp(-0.3 * i)
        lp = diff_lambda[i].astype(jnp.float32)
        lam = (jnp.exp(jnp.sum(lp[0] * lp[1])) - jnp.exp(jnp.sum(lp[2] * lp[3]))
               + lam_init).reshape(1)
        blk = lambda col: col // HEAD_DIM
        o_d = _attention("diff", batch, seq, dqk, 0, dqk, HEADS, u, blk(COL_DV), HEAD_DIM,
                         extras=(diff_subln[i].reshape(1, HEAD_DIM),),
                         lam=lam, lam_init=lam_init)
        o_f = _attention("fox", batch, seq, u, blk(COL_FQ), u, blk(COL_FK), u, blk(COL_FV),
                         HEAD_DIM, extras=(cq_rows, ck_rep))
        o_m = _attention("mla", batch, seq, mq, 0, mk, 0, mv, 0, MLA_QK_PAD)
        o_s = _attention("sb", batch, seq, u, blk(COL_SQ), u, blk(COL_SK), u, blk(COL_SV),
                         HEAD_DIM)

        w_r = jnp.concatenate([w_router_group[i], w_router_expert[i]], axis=1)
        w_r = jnp.pad(w_r, ((0, 0), (0, LANES - w_r.shape[1])))
        wr_hi = w_r.astype(bf)
        wr_lo = (w_r - wr_hi.astype(jnp.float32)).astype(bf)
        h, h_rows, hb, logits = _merge(
            u, o_d, o_f, o_m, o_s, w_branch[i].astype(bf), w_o[i].astype(bf), xf,
            ln1_g[i].reshape(1, d), ln1_b[i].reshape(1, d), wr_hi, wr_lo, alpha)

        expert_idx, weights = _route(logits, b_router_group[i], b_router_expert[i], t)
        dest, row_tok, sched, n_used = _dispatch_plan(expert_idx, t)
        y_rows = _moe(sched, n_used, row_tok, h_rows, w_expert_gate, w_expert_up,
                      w_expert_down, i)
        xf, xb = _final(dest[:, 0], dest[:, 1], h, hb, y_rows, weights.astype(jnp.float32),
                        p[i].reshape(t, -1), w_ple_gate[i].astype(bf),
                        w_ple_proj[i].astype(bf), ln2_g[i].reshape(1, d),
                        ln2_b[i].reshape(1, d), alpha)
    return xf.reshape(batch, seq, d)
```

```python
import functools
import math

import jax
import jax.numpy as jnp
from jax import lax
from jax.experimental import pallas as pl
from jax.experimental.pallas import tpu as pltpu

HEAD_DIM = 128
HEADS = 4
N_BRANCHES = 4
BRANCH_WIDTH = HEADS * HEAD_DIM
DIFF_QK = 64
DIFF_ROT = DIFF_QK // 4
MLA_Q_LORA = 512
MLA_KV_LORA = 256
MLA_NOPE = 128
MLA_ROPE = 64
MLA_QK_PAD = 256
N_GROUPS = 4
EXPERTS_PER_GROUP = 8
N_EXPERTS = N_GROUPS * EXPERTS_PER_GROUP
TOP_K = 2
CHUNK = 64
ROPE_THETA = 500000.0
LN_EPS = 1e-5
RMS_EPS = 1e-6
NEG_BIG = -1e30
LOG2E = 1.4426950408889634
SB_EXP_UNDERFLOW = -104.0
LANES = 128
V7X_VMEM_LIMIT = 56 * 1024 * 1024

COL_GATES = 0
COL_DQ = 8192
COL_DV = COL_DQ + 1024
COL_FQ = COL_DV + 512
COL_FK = COL_FQ + 512
COL_FV = COL_FK + 512
COL_SQ = COL_FV + 512
COL_SK = COL_SQ + 512
COL_SV = COL_SK + 512
COL_CQ = COL_SV + 512
COL_CKV = COL_CQ + 512
COL_KR = COL_CKV + 256
COL_FF = COL_KR + 128
U_WIDTH = COL_FF + 128

ATT_TQ = 512
MOE_ROWS = 256


def _cparams(sem, vmem=V7X_VMEM_LIMIT):
    return pltpu.CompilerParams(dimension_semantics=sem, vmem_limit_bytes=vmem)


def _resident(shape, index_map):
    return pl.BlockSpec(shape, index_map, pipeline_mode=pl.Buffered(1))


def _matmul_body(x_ref, w_ref, c_ref, o_ref, f_ref, *, f_block, f_off):
    acc = lax.dot_general(x_ref[...], w_ref[...], (((1,), (1,)), ((), ())),
                          preferred_element_type=jnp.float32)
    o_ref[...] = (acc * c_ref[...]).astype(o_ref.dtype)

    @pl.when(pl.program_id(1) == f_block)
    def _():
        f_ref[...] = acc[:, f_off:f_off + LANES]


def _matmul(x, w_t, col_scale, out_dtype, tm, tn, f32_col, name):
    m, k = x.shape
    n = w_t.shape[0]
    tm, tn = min(tm, m), min(tn, n)
    body = functools.partial(_matmul_body, f_block=f32_col // tn, f_off=f32_col % tn)
    return pl.pallas_call(
        body,
        out_shape=(jax.ShapeDtypeStruct((m, n), out_dtype),
                   jax.ShapeDtypeStruct((m, LANES), jnp.float32)),
        grid=(m // tm, n // tn),
        in_specs=[pl.BlockSpec((tm, k), lambda i, j: (i, 0)),
                  pl.BlockSpec((tn, k), lambda i, j: (j, 0)),
                  pl.BlockSpec((1, tn), lambda i, j: (0, j))],
        out_specs=(pl.BlockSpec((tm, tn), lambda i, j: (i, j)),
                   pl.BlockSpec((tm, LANES), lambda i, j: (i, 0))),
        compiler_params=_cparams(("parallel", "arbitrary")),
        name=name,
    )(x, w_t, col_scale)


def _split3(x):
    hi = x.astype(jnp.bfloat16)
    r1 = x - hi.astype(jnp.float32)
    mid = r1.astype(jnp.bfloat16)
    lo = (r1 - mid.astype(jnp.float32)).astype(jnp.bfloat16)
    return hi, mid, lo


def _split2(x):
    hi = x.astype(jnp.bfloat16)
    mid = (x - hi.astype(jnp.float32)).astype(jnp.bfloat16)
    return hi, mid


def _dot(a, b):
    return jnp.dot(a, b, preferred_element_type=jnp.float32)


def _dot_nt(a, b):
    return lax.dot_general(a, b, (((1,), (1,)), ((), ())), preferred_element_type=jnp.float32)


def _log_sigmoid(z):
    neg_abs = pltpu.bitcast(pltpu.bitcast(z, jnp.uint32) | jnp.uint32(0x80000000), jnp.float32)
    return jnp.minimum(z, 0.0) - jnp.log(1.0 + jnp.exp(neg_abs))


def _sigmoid(z):
    return 1.0 / (1.0 + jnp.exp(-z))


def _fgate_body(f_ref, b_ref, o_ref, carry_ref):
    @pl.when(pl.program_id(1) == 0)
    def _():
        carry_ref[...] = jnp.zeros_like(carry_ref)

    log_f = _log_sigmoid(f_ref[...] + b_ref[...])
    ts = log_f.shape[0]
    row = lax.broadcasted_iota(jnp.int32, (ts, ts), 0)
    col = lax.broadcasted_iota(jnp.int32, (ts, ts), 1)
    tri = (col <= row).astype(jnp.bfloat16)
    hi, mid, lo = _split3(log_f)
    cum = _dot(tri, hi) + _dot(tri, mid) + _dot(tri, lo) + carry_ref[...]
    o_ref[...] = cum
    carry_ref[...] = cum[ts - 1:ts, :]


def _fgate_cumsum(f_logits, b_f, batch, seq):
    t = f_logits.shape[0]
    ts = min(512, seq)
    ns = seq // ts
    return pl.pallas_call(
        _fgate_body,
        out_shape=jax.ShapeDtypeStruct((t, LANES), jnp.float32),
        grid=(batch, ns),
        in_specs=[pl.BlockSpec((ts, LANES), lambda b, s: (b * ns + s, 0)),
                  _resident((1, LANES), lambda b, s: (0, 0))],
        out_specs=pl.BlockSpec((ts, LANES), lambda b, s: (b * ns + s, 0)),
        scratch_shapes=[pltpu.VMEM((1, LANES), jnp.float32)],
        compiler_params=_cparams(("parallel", "arbitrary")),
        name="fgate_cumsum",
    )(f_logits, b_f)


def _rope_lanes(x, c, s_up, s_dn, half):
    return x * c + pltpu.roll(x, LANES - half, 1) * s_up + pltpu.roll(x, half, 1) * s_dn


def _prep_body(dqk_ref, cq_ref, ckv_ref, kr_ref, c16_ref, su16_ref, sd16_ref,
               c64_ref, su64_ref, sd64_ref, qg_ref, kvg_ref, wuq_ref, wuk_ref, wuv_ref,
               dqk_o, mq_o, mk_o, mv_o):
    c16, su16, sd16 = c16_ref[...], su16_ref[...], sd16_ref[...]
    c64, su64, sd64 = c64_ref[...], su64_ref[...], sd64_ref[...]
    for j in range(dqk_ref.shape[1] // LANES):
        sl = slice(j * LANES, (j + 1) * LANES)
        x = _rope_lanes(dqk_ref[:, sl].astype(jnp.float32), c16, su16, sd16, DIFF_ROT // 2)
        if j < HEADS:
            x = x * (DIFF_QK ** -0.5 * LOG2E)
        dqk_o[:, sl] = x.astype(dqk_o.dtype)

    cq = cq_ref[...].astype(jnp.float32)
    cqn = cq * lax.rsqrt(jnp.mean(cq * cq, axis=-1, keepdims=True) + RMS_EPS) * qg_ref[...]
    q = _dot(cqn.astype(jnp.bfloat16), wuq_ref[...])
    q = q * ((MLA_NOPE + MLA_ROPE) ** -0.5 * LOG2E)
    ckv = ckv_ref[...].astype(jnp.float32)
    ckvn = (ckv * lax.rsqrt(jnp.mean(ckv * ckv, axis=-1, keepdims=True) + RMS_EPS)
            * kvg_ref[...]).astype(jnp.bfloat16)
    kn = _dot(ckvn, wuk_ref[...])
    mv_o[...] = _dot(ckvn, wuv_ref[...]).astype(mv_o.dtype)
    kr = _rope_lanes(kr_ref[...].astype(jnp.float32), c64, su64, sd64, MLA_ROPE // 2)
    for h in range(HEADS):
        base = h * MLA_QK_PAD
        mq_o[:, base:base + LANES] = q[:, base:base + LANES].astype(mq_o.dtype)
        qr = _rope_lanes(q[:, base + LANES:base + 2 * LANES], c64, su64, sd64, MLA_ROPE // 2)
        mq_o[:, base + LANES:base + 2 * LANES] = qr.astype(mq_o.dtype)
        mk_o[:, base:base + LANES] = kn[:, h * LANES:(h + 1) * LANES].astype(mk_o.dtype)
        mk_o[:, base + LANES:base + 2 * LANES] = kr.astype(mk_o.dtype)


def _prep(u, tabs, q_norm, kv_norm, w_uq, w_uk, w_uv):
    t = u.shape[0]
    tm = min(512, t)
    row = lambda blk: (lambda i: (i, blk))
    const = lambda i: (0, 0)
    tab_spec = pl.BlockSpec((tm, LANES), lambda i: (i, 0))
    outs = pl.pallas_call(
        _prep_body,
        out_shape=(jax.ShapeDtypeStruct((t, 1024), jnp.bfloat16),
                   jax.ShapeDtypeStruct((t, HEADS * MLA_QK_PAD), jnp.bfloat16),
                   jax.ShapeDtypeStruct((t, HEADS * MLA_QK_PAD), jnp.bfloat16),
                   jax.ShapeDtypeStruct((t, BRANCH_WIDTH), jnp.bfloat16)),
        grid=(t // tm,),
        in_specs=[pl.BlockSpec((tm, 1024), row(COL_DQ // 1024)),
                  pl.BlockSpec((tm, MLA_Q_LORA), row(COL_CQ // MLA_Q_LORA)),
                  pl.BlockSpec((tm, MLA_KV_LORA), row(COL_CKV // MLA_KV_LORA)),
                  pl.BlockSpec((tm, LANES), row(COL_KR // LANES)),
                  tab_spec, tab_spec, tab_spec, tab_spec, tab_spec, tab_spec,
                  _resident((1, MLA_Q_LORA), const),
                  _resident((1, MLA_KV_LORA), const),
                  _resident(w_uq.shape, const),
                  _resident(w_uk.shape, const),
                  _resident(w_uv.shape, const)],
        out_specs=(pl.BlockSpec((tm, 1024), lambda i: (i, 0)),
                   pl.BlockSpec((tm, HEADS * MLA_QK_PAD), lambda i: (i, 0)),
                   pl.BlockSpec((tm, HEADS * MLA_QK_PAD), lambda i: (i, 0)),
                   pl.BlockSpec((tm, BRANCH_WIDTH), lambda i: (i, 0))),
        compiler_params=_cparams(("parallel",)),
        name="prep",
    )(u, u, u, u, *tabs, q_norm, kv_norm, w_uq, w_uk, w_uv)
    return outs


def _attn_body(*refs, kind, tq, tk, lam_init):
    softmax = kind != "sb"
    if kind == "diff":
        lam_ref, q_ref, k_ref, v_ref, g_ref, o_ref = refs[:6]
    elif kind == "fox":
        q_ref, k_ref, v_ref, cq_ref, ck_ref, o_ref = refs[:6]
    else:
        q_ref, k_ref, v_ref, o_ref = refs[:4]
    if softmax:
        s_bufs, p_bufs = refs[-8:-6], refs[-6:-4]
        vt_ref, m_ref, a_ref, acc_ref = refs[-4:]
    else:
        s_bufs, p_bufs = refs[-7:-5], refs[-5:-3]
        vt_ref, r_ref, acc_ref = refs[-3:]
    qi = pl.program_id(2)

    @pl.when(qi == 0)
    def _():
        for kb in range(vt_ref.shape[0]):
            vt_ref[kb, 0:HEAD_DIM, :] = v_ref[kb * tk:(kb + 1) * tk, :].T
            if softmax:
                vt_ref[kb, HEAD_DIM:, :] = jnp.ones((vt_ref.shape[1] - HEAD_DIM, tk),
                                                    vt_ref.dtype)

    q0 = qi * tq
    last_kb = 2 * qi + 1
    q = q_ref[...]
    n_comp = 2 if kind == "diff" else 1
    if kind == "diff":
        lane = lax.broadcasted_iota(jnp.int32, q.shape, 1)
        qs = [jnp.where(lane < DIFF_QK, q, jnp.zeros_like(q)),
              jnp.where(lane >= DIFF_QK, q, jnp.zeros_like(q))]
    else:
        qs = [q]

    acc_ref[...] = jnp.zeros_like(acc_ref)
    p_bufs[1][...] = jnp.zeros_like(p_bufs[1])
    if softmax:
        m_ref[...] = jnp.full_like(m_ref, NEG_BIG)
        a_ref[...] = jnp.zeros_like(a_ref)
    else:
        r_ref[...] = jnp.zeros_like(r_ref)

    def key_rows(kb):
        if isinstance(kb, int):
            return pl.ds(kb * tk, tk)
        return pl.ds(pl.multiple_of(kb * tk, tk), tk)

    def scores_stage(kb, s_dst):
        k_t = k_ref[key_rows(kb), :]
        for c in range(n_comp):
            s_dst[c] = _dot_nt(k_t, qs[c])

    def values_stage(kb, p_src):
        v_t = vt_ref[kb]
        for c in range(n_comp):
            pv = _dot(v_t, p_src[c])[:acc_ref.shape[1]]
            if softmax:
                acc_ref[c] = a_ref[c] * acc_ref[c] + pv
            else:
                acc_ref[c] = acc_ref[c] + pv

    def weights_stage(kb, s_src, p_dst, masked):
        if masked:
            j = lax.broadcasted_iota(jnp.int32, (tk, tq), 0)
            i = lax.broadcasted_iota(jnp.int32, (tk, tq), 1)
            off = q0 - kb * tk
            if kind in ("diff", "mla"):
                mask = (j // CHUNK - i // CHUNK) <= off // CHUNK
            elif kind == "fox":
                mask = (j - i) <= off
            else:
                mask = (j - i) < off
        if not softmax:
            z_t = s_src[0]
            ls = _log_sigmoid(z_t)
            lk = ls - z_t
            if masked:
                lk = jnp.where(mask, lk, 0.0)
            jj = lax.broadcasted_iota(jnp.int32, (tk, tk), 0)
            mm = lax.broadcasted_iota(jnp.int32, (tk, tk), 1)
            upper = (mm > jj).astype(jnp.bfloat16)
            hi, mid = _split2(lk)
            suffix = _dot(jnp.concatenate([upper, upper], axis=1),
                          jnp.concatenate([hi, mid], axis=0))
            r_old = r_ref[...]
            w_t = jnp.exp(ls + (r_old + suffix))
            if masked:
                w_t = jnp.where(mask, w_t, 0.0)
            p_dst[0] = w_t.astype(jnp.bfloat16)
            r_ref[...] = r_old + suffix[0:1, :] + lk[0:1, :]
            return
        for c in range(n_comp):
            s_t = s_src[c]
            if kind == "fox":
                ck = ck_ref[key_rows(kb), :]
                s_t = s_t + cq_ref[...] - jnp.concatenate([ck] * (tq // LANES), axis=1)
            if masked:
                s_t = jnp.where(mask, s_t, NEG_BIG)
            m_old = m_ref[c]
            m_new = jnp.maximum(m_old, jnp.max(s_t, axis=0, keepdims=True))
            alpha = jnp.exp2(m_old - m_new)
            p_t = jnp.exp2(s_t - m_new)
            m_ref[c] = m_new
            a_ref[c] = alpha
            p_dst[c] = p_t.astype(jnp.bfloat16)

    def step(kb, parity, masked, next_kb, prev_kb):
        if next_kb is not None:
            scores_stage(next_kb, s_bufs[1 - parity])
        values_stage(prev_kb, p_bufs[1 - parity])
        weights_stage(kb, s_bufs[parity], p_bufs[parity], masked)

    if softmax and tq == 2 * tk:
        scores_stage(0, s_bufs[0])

        def body(t, carry):
            kb = 2 * t
            step(kb, 0, False, kb + 1, jnp.maximum(kb - 1, 0))
            step(kb + 1, 1, False, kb + 2, kb)
            return carry
        lax.fori_loop(0, qi, body, 0)
        step(last_kb - 1, 0, True, last_kb, jnp.maximum(last_kb - 2, 0))
        step(last_kb, 1, True, None, last_kb - 1)
        values_stage(last_kb, p_bufs[1])
    elif softmax:
        scores_stage(0, s_bufs[0])

        def body(t, carry):
            kb = 2 * t
            step(kb, 0, False, kb + 1, jnp.maximum(kb - 1, 0))
            step(kb + 1, 1, False, kb + 2, kb)
            return carry
        lax.fori_loop(0, qi // 2, body, 0)

        @pl.when(qi % 2 == 0)
        def _():
            step(qi, 0, True, None, jnp.maximum(qi - 1, 0))
            values_stage(qi, p_bufs[0])

        @pl.when(qi % 2 == 1)
        def _():
            step(qi - 1, 0, False, qi, jnp.maximum(qi - 2, 0))
            step(qi, 1, True, None, qi - 1)
            values_stage(qi, p_bufs[1])
    else:
        scores_stage(last_kb, s_bufs[0])
        step(last_kb, 0, True, last_kb - 1, last_kb)
        step(last_kb - 1, 1, True, jnp.maximum(last_kb - 2, 0), last_kb)

        def live(carry):
            t, _, r_max = carry
            return (t < qi) & (r_max >= SB_EXP_UNDERFLOW)

        def body(carry):
            t = carry[0]
            kb = last_kb - 2 - 2 * t
            step(kb, 0, False, kb - 1, kb + 1)
            step(kb - 1, 1, False, jnp.maximum(kb - 2, 0), kb)
            return t + 1, kb - 1, jnp.max(r_ref[...])
        done = lax.while_loop(live, body, (0, last_kb - 1, jnp.max(r_ref[...])))
        values_stage(done[1], p_bufs[1])

    def normalised(c):
        return acc_ref[c, 0:HEAD_DIM, :] / acc_ref[c, HEAD_DIM:HEAD_DIM + 1, :]

    if kind == "sb":
        o_t = acc_ref[0]
    elif kind == "diff":
        o_t = normalised(0) - lam_ref[0] * normalised(1)
    else:
        o_t = normalised(0)
    o = o_t.T
    if kind == "diff":
        o = (o * lax.rsqrt(jnp.mean(o * o, axis=-1, keepdims=True) + RMS_EPS)
             * g_ref[...] * (1.0 - lam_init))
    o_ref[...] = o.astype(o_ref.dtype)


def _attention(kind, batch, seq, q_arr, q_col, k_arr, k_col, v_arr, v_col, dq,
               extras=(), lam=None, lam_init=0.0):
    t = q_arr.shape[0]
    tq = min(ATT_TQ, seq)
    tk = tq if kind in ("diff", "mla") else tq // 2
    nq = seq // tq
    q_spec = pl.BlockSpec((tq, dq), lambda b, h, i, *_: (b * nq + i, q_col + h))
    k_spec = pl.BlockSpec((seq, dq), lambda b, h, i, *_: (b, k_col + h))
    v_spec = pl.BlockSpec((seq, HEAD_DIM), lambda b, h, i, *_: (b, v_col + h))
    in_specs = [q_spec, k_spec, v_spec]
    args = [q_arr, k_arr, v_arr]
    n_comp = 2 if kind == "diff" else 1
    if kind == "diff":
        in_specs.append(_resident((1, HEAD_DIM), lambda b, h, i, *_: (0, 0)))
        args.append(extras[0])
    elif kind == "fox":
        cq_rows, ck_rep = extras
        in_specs.append(pl.BlockSpec((None, 1, tq), lambda b, h, i: (b * HEADS + h, 0, i)))
        in_specs.append(pl.BlockSpec((None, seq, LANES), lambda b, h, i: (b * HEADS + h, 0, 0)))
        args += [cq_rows, ck_rep]
    s_buf = pltpu.VMEM((n_comp, tk, tq), jnp.float32)
    p_buf = pltpu.VMEM((n_comp, tk, tq), jnp.bfloat16)
    row = pltpu.VMEM((n_comp, 1, tq), jnp.float32)
    nk = seq // tk
    if kind == "sb":
        scratch = [s_buf, s_buf, p_buf, p_buf,
                   pltpu.VMEM((nk, HEAD_DIM, tk), jnp.bfloat16),
                   pltpu.VMEM((1, tq), jnp.float32),
                   pltpu.VMEM((1, HEAD_DIM, tq), jnp.float32)]
    else:
        scratch = [s_buf, s_buf, p_buf, p_buf,
                   pltpu.VMEM((nk, HEAD_DIM + 16, tk), jnp.bfloat16), row, row,
                   pltpu.VMEM((n_comp, HEAD_DIM + 8, tq), jnp.float32)]
    body = functools.partial(_attn_body, kind=kind, tq=tq, tk=tk, lam_init=lam_init)
    grid = (batch, HEADS, nq)
    out_shape = jax.ShapeDtypeStruct((t, BRANCH_WIDTH), jnp.bfloat16)
    out_spec = pl.BlockSpec((tq, HEAD_DIM), lambda b, h, i, *_: (b * nq + i, h))
    cp = _cparams(("parallel", "parallel", "arbitrary"))
    if kind == "diff":
        return pl.pallas_call(
            body, out_shape=out_shape,
            grid_spec=pltpu.PrefetchScalarGridSpec(
                num_scalar_prefetch=1, grid=grid, in_specs=in_specs, out_specs=out_spec,
                scratch_shapes=scratch),
            compiler_params=cp, name="attn_" + kind)(lam, *args)
    return pl.pallas_call(
        body, out_shape=out_shape, grid=grid, in_specs=in_specs, out_specs=out_spec,
        scratch_shapes=scratch, compiler_params=cp, name="attn_" + kind)(*args)


def _layer_norm(y, g, b):
    mu = jnp.mean(y, axis=-1, keepdims=True)
    yc = y - mu
    var = jnp.mean(yc * yc, axis=-1, keepdims=True)
    return yc * lax.rsqrt(var + LN_EPS) * g + b


def _merge_body(g_ref, od_ref, of_ref, om_ref, os_ref, wb_ref, wo_ref, x_ref, lg_ref, lb_ref,
                wrh_ref, wrl_ref, h_o, hr_o, hb_o, lgt_o, *, alpha):
    d = x_ref.shape[1]
    merged = None
    for n, o_ref in enumerate((od_ref, of_ref, om_ref, os_ref)):
        gate = _sigmoid(g_ref[:, n * d:(n + 1) * d].astype(jnp.float32))
        term = gate * _dot(o_ref[...], wb_ref[n])
        merged = term if merged is None else merged + term
    mix = _dot(merged.astype(jnp.bfloat16), wo_ref[...])
    h = _layer_norm(alpha * x_ref[...] + mix, lg_ref[...], lb_ref[...])
    h_o[...] = h
    h_hi = h.astype(jnp.bfloat16)
    hb_o[...] = h_hi
    _to_chunked(hr_o, _pack_bf16_halves(h_hi))
    h_lo = (h - h_hi.astype(jnp.float32)).astype(jnp.bfloat16)
    lgt_o[...] = (_dot(h_hi, wrh_ref[...]) + _dot(h_lo, wrh_ref[...]) + _dot(h_hi, wrl_ref[...]))


def _merge(u, o_d, o_f, o_m, o_s, w_b, w_o, x, ln_g, ln_b, wr_hi, wr_lo, alpha):
    t, d = x.shape
    tm = min(256, t)
    row = lambda i: (i, 0)
    const2 = lambda i: (0, 0)
    o_spec = pl.BlockSpec((tm, BRANCH_WIDTH), row)
    return pl.pallas_call(
        functools.partial(_merge_body, alpha=alpha),
        out_shape=(jax.ShapeDtypeStruct((t, d), jnp.float32),
                   jax.ShapeDtypeStruct((t * _chunk_pitch(d // 2), LANES), jnp.uint32),
                   jax.ShapeDtypeStruct((t, d), jnp.bfloat16),
                   jax.ShapeDtypeStruct((t, LANES), jnp.float32)),
        grid=(t // tm,),
        in_specs=[pl.BlockSpec((tm, N_BRANCHES * d), row),
                  o_spec, o_spec, o_spec, o_spec,
                  _resident(w_b.shape, lambda i: (0, 0, 0)),
                  _resident(w_o.shape, const2),
                  pl.BlockSpec((tm, d), row),
                  _resident((1, d), const2), _resident((1, d), const2),
                  _resident(wr_hi.shape, const2), _resident(wr_lo.shape, const2)],
        out_specs=(pl.BlockSpec((tm, d), row),
                   pl.BlockSpec((tm * _chunk_pitch(d // 2), LANES), row),
                   pl.BlockSpec((tm, d), row), pl.BlockSpec((tm, LANES), row)),
        compiler_params=_cparams(("parallel",)),
        name="merge_ln1",
    )(u, o_d, o_f, o_m, o_s, w_b, w_o, x, ln_g, ln_b, wr_hi, wr_lo)


CHUNK_PAD = 2


def _chunk_pitch(d):
    return d // LANES + CHUNK_PAD


def _to_chunked(ref, x):
    n, d = x.shape
    nc, pitch = d // LANES, _chunk_pitch(d)
    for c in range(nc):
        ref[pl.ds(c, n, stride=pitch), :] = x[:, c * LANES:(c + 1) * LANES]
    for c in range(nc, pitch):
        ref[pl.ds(c, n, stride=pitch), :] = jnp.zeros((n, LANES), x.dtype)


def _from_chunked(ref, n, d):
    pitch = _chunk_pitch(d)
    return jnp.concatenate([ref[pl.ds(c, n, stride=pitch), :] for c in range(d // LANES)],
                           axis=1)


def _gather_rows(idx_ref, base, src_hbm, dst, sem, n, d):
    nc, pitch = d // LANES, _chunk_pitch(d)
    for r in range(n):
        row = pl.multiple_of(idx_ref[base + r] * pitch, CHUNK_PAD)
        pltpu.make_async_copy(src_hbm.at[pl.ds(row, nc), :], dst.at[pl.ds(r * pitch, nc), :],
                              sem).start(priority=r % 2)


def _wait_rows(src_hbm, dst, sem, n, d):
    rows = n * (d // LANES)
    pltpu.make_async_copy(src_hbm.at[pl.ds(0, rows), :], dst.at[pl.ds(0, rows), :], sem).wait()


def _pack_bf16_halves(x):
    bits = pltpu.bitcast(x.astype(jnp.float32), jnp.uint32)
    half = x.shape[1] // 2
    return bits[:, :half] | (bits[:, half:] >> 16)


def _unpack_bf16_halves(w):
    left = pltpu.bitcast(w & jnp.uint32(0xFFFF0000), jnp.float32)
    right = pltpu.bitcast(w << 16, jnp.float32)
    return jnp.concatenate([left, right], axis=1).astype(jnp.bfloat16)


def _moe_body(sched_ref, n_used_ref, tok_ref, h_hbm, wg_hbm, wu_hbm, wd_hbm, y_ref,
              xbuf, sem, wg_f, wu_f, wd_f, wsem, wg_b, wu_b, wd_b, *, layer):
    i = pl.program_id(0)
    n_used = n_used_ref[0]
    d = wg_b.shape[0] // 2
    bm = xbuf.shape[1] // _chunk_pitch(d)
    slot = i % 2

    def weight_copies(e, s):
        return (pltpu.make_async_copy(wg_hbm.at[layer, e], wg_f.at[s], wsem.at[s]),
                pltpu.make_async_copy(wu_hbm.at[layer, e], wu_f.at[s], wsem.at[s]),
                pltpu.make_async_copy(wd_hbm.at[layer, e], wd_f.at[s], wsem.at[s]))

    @pl.when((i == 0) & (n_used > 0))
    def _():
        for cp in weight_copies(sched_ref[0, 0], 0):
            cp.start(priority=1)
        _gather_rows(tok_ref, 0, h_hbm, xbuf.at[0], sem.at[0], bm, d)

    @pl.when(i + 1 < n_used)
    def _():
        _gather_rows(tok_ref, (i + 1) * bm, h_hbm, xbuf.at[1 - slot], sem.at[1 - slot], bm, d)

    @pl.when(i < n_used)
    def _():
        @pl.when(sched_ref[1, i] == 1)
        def _():
            ws = sched_ref[2, i]
            for cp in weight_copies(sched_ref[0, i], ws):
                cp.wait()
            wg_b[...] = wg_f[ws].astype(jnp.bfloat16)
            wu_b[...] = wu_f[ws].astype(jnp.bfloat16)
            wd_b[...] = wd_f[ws].astype(jnp.bfloat16)

            @pl.when(sched_ref[3, i] >= 0)
            def _():
                for cp in weight_copies(sched_ref[3, i], 1 - ws):
                    cp.start(priority=1)

        _wait_rows(h_hbm, xbuf.at[slot], sem.at[slot], bm, d)
        x = _unpack_bf16_halves(_from_chunked(xbuf.at[slot], bm, d))
        a = _dot(x, wg_b[...])
        hid = (a * _sigmoid(a)) * _dot(x, wu_b[...])
        _to_chunked(y_ref, _dot(hid.astype(jnp.bfloat16), wd_b[...]))

    @pl.when(i >= n_used)
    def _():
        y_ref[...] = jnp.zeros_like(y_ref)


def _moe(sched, n_used, row_tok, h_rows, w_g, w_u, w_d, layer):
    n_rows = row_tok.shape[0]
    d, hid = w_g.shape[2], w_g.shape[3]
    pitch = _chunk_pitch(d)
    bm = MOE_ROWS
    any_spec = pl.BlockSpec(memory_space=pl.ANY)
    return pl.pallas_call(
        functools.partial(_moe_body, layer=layer),
        out_shape=jax.ShapeDtypeStruct((n_rows * pitch, LANES), jnp.float32),
        grid_spec=pltpu.PrefetchScalarGridSpec(
            num_scalar_prefetch=3, grid=(n_rows // bm,),
            in_specs=[any_spec, any_spec, any_spec, any_spec],
            out_specs=pl.BlockSpec((bm * pitch, LANES), lambda i, s, n, tok: (i, 0)),
            scratch_shapes=[pltpu.VMEM((2, bm * _chunk_pitch(d // 2), LANES), jnp.uint32),
                            pltpu.SemaphoreType.DMA((2,)),
                            pltpu.VMEM((2, d, hid), jnp.float32),
                            pltpu.VMEM((2, d, hid), jnp.float32),
                            pltpu.VMEM((2, hid, d), jnp.float32),
                            pltpu.SemaphoreType.DMA((2,)),
                            pltpu.VMEM((d, hid), jnp.bfloat16),
                            pltpu.VMEM((d, hid), jnp.bfloat16),
                            pltpu.VMEM((hid, d), jnp.bfloat16)]),
        compiler_params=_cparams(("arbitrary",)),
        name="moe_experts",
    )(sched, n_used, row_tok, h_rows, w_g, w_u, w_d)


def _final_body(d0_ref, d1_ref, h_ref, hb_ref, y_hbm, gw_ref, p_ref, wg_ref, wp_ref, lg_ref,
                lb_ref, y_o, yb_o, gbuf, sem, *, alpha):
    i = pl.program_id(0)
    n = pl.num_programs(0)
    tm, d = h_ref.shape
    slot = i % 2

    def gather(step, s):
        _gather_rows(d0_ref, step * tm, y_hbm, gbuf.at[s, 0], sem.at[s], tm, d)
        _gather_rows(d1_ref, step * tm, y_hbm, gbuf.at[s, 1], sem.at[s], tm, d)

    @pl.when(i == 0)
    def _():
        gather(0, 0)

    @pl.when(i + 1 < n)
    def _():
        gather(i + 1, 1 - slot)

    gate = _sigmoid(_dot(hb_ref[...], wg_ref[...]))
    proj = _dot(p_ref[...].astype(jnp.bfloat16), wp_ref[...])
    _wait_rows(y_hbm, gbuf.at[slot, 0], sem.at[slot], tm, d)
    _wait_rows(y_hbm, gbuf.at[slot, 1], sem.at[slot], tm, d)
    gw = gw_ref[...]
    ffn = (_from_chunked(gbuf.at[slot, 0], tm, d) * gw[:, 0:1]
           + _from_chunked(gbuf.at[slot, 1], tm, d) * gw[:, 1:2])
    y = _layer_norm(alpha * h_ref[...] + ffn + gate * proj, lg_ref[...], lb_ref[...])
    y_o[...] = y
    yb_o[...] = y.astype(jnp.bfloat16)


def _final(dest0, dest1, h, hb, y_rows, gate_w, p, w_pg, w_pp, ln_g, ln_b, alpha):
    t, d = h.shape
    tm = min(256, t)
    row = lambda i, *_: (i, 0)
    const2 = lambda i, *_: (0, 0)
    return pl.pallas_call(
        functools.partial(_final_body, alpha=alpha),
        out_shape=(jax.ShapeDtypeStruct((t, d), jnp.float32),
                   jax.ShapeDtypeStruct((t, d), jnp.bfloat16)),
        grid_spec=pltpu.PrefetchScalarGridSpec(
            num_scalar_prefetch=2, grid=(t // tm,),
            in_specs=[pl.BlockSpec((tm, d), row), pl.BlockSpec((tm, d), row),
                      pl.BlockSpec(memory_space=pl.ANY),
                      pl.BlockSpec((tm, TOP_K), row), pl.BlockSpec((tm, p.shape[1]), row),
                      _resident(w_pg.shape, const2), _resident(w_pp.shape, const2),
                      _resident((1, d), const2), _resident((1, d), const2)],
            out_specs=(pl.BlockSpec((tm, d), row), pl.BlockSpec((tm, d), row)),
            scratch_shapes=[pltpu.VMEM((2, TOP_K, tm * _chunk_pitch(d), LANES), jnp.float32),
                            pltpu.SemaphoreType.DMA((2,))]),
        compiler_params=_cparams(("arbitrary",)),
        name="final_ln2",
    )(dest0, dest1, h, hb, y_rows, gate_w, p, w_pg, w_pp, ln_g, ln_b)


def _rope_tables(positions):
    pos = positions.reshape(-1).astype(jnp.float32)
    lane = jnp.arange(LANES)

    def tables(rot, period):
        half = rot // 2
        inv_freq = ROPE_THETA ** (-jnp.arange(half, dtype=jnp.float32) / half)
        ang = pos[:, None] * inv_freq
        cos, sin = jnp.cos(ang), jnp.sin(ang)
        lp = lane % period
        idx = lp % half
        in_rot = lp < rot
        c = jnp.where(in_rot[None, :], cos[:, idx], 1.0)
        s_up = jnp.where((lp < half)[None, :], -sin[:, idx], 0.0)
        s_dn = jnp.where(((lp >= half) & in_rot)[None, :], sin[:, idx], 0.0)
        return c, s_up, s_dn

    return tables(DIFF_ROT, DIFF_QK) + tables(MLA_ROPE, LANES)


def _w_in_segments(d):
    widths = [512, 512, 512, 512, 512, 512, HEADS, MLA_Q_LORA, MLA_KV_LORA, MLA_ROPE,
              512, 512, 512, N_BRANCHES * d]
    dsts = [COL_DQ, COL_DQ + 512, COL_DV, COL_FQ, COL_FK, COL_FV, COL_FF, COL_CQ, COL_CKV,
            COL_KR, COL_SQ, COL_SK, COL_SV, COL_GATES]
    segs, src = [], 0
    for wd, dst in zip(widths, dsts):
        segs.append((src, dst, wd))
        src += wd
    return segs


def _w_in_body(src_ref, valid_ref, wt_hbm, o_ref, buf, sem, *, layer):
    b = pl.program_id(0)
    slot = b % 2
    rows = buf.shape[1]

    def copy(blk, s):
        return pltpu.make_async_copy(wt_hbm.at[pl.ds(src_ref[blk], rows), layer, :],
                                     buf.at[s], sem.at[s])

    @pl.when(b == 0)
    def _():
        copy(0, 0).start()

    @pl.when(b + 1 < pl.num_programs(0))
    def _():
        copy(b + 1, 1 - slot).start()

    copy(b, slot).wait()
    r = lax.broadcasted_iota(jnp.int32, buf.shape[1:], 0)
    o_ref[...] = jnp.where(r < valid_ref[b], buf[slot], 0.0).astype(o_ref.dtype)


def _reorder_w_in(w_t, layer):
    n_in, _, d = w_t.shape
    src, valid = [0] * (U_WIDTH // LANES), [0] * (U_WIDTH // LANES)
    for s0, dst, wd in _w_in_segments(d):
        for off in range(0, wd, LANES):
            blk = (dst + off) // LANES
            src[blk], valid[blk] = s0 + off, min(LANES, wd - off)
    assert max(s + LANES for s in src) <= n_in
    return pl.pallas_call(
        functools.partial(_w_in_body, layer=layer),
        out_shape=jax.ShapeDtypeStruct((U_WIDTH, d), jnp.bfloat16),
        grid_spec=pltpu.PrefetchScalarGridSpec(
            num_scalar_prefetch=2, grid=(U_WIDTH // LANES,),
            in_specs=[pl.BlockSpec(memory_space=pl.ANY)],
            out_specs=pl.BlockSpec((LANES, d), lambda b, s, v: (b, 0)),
            scratch_shapes=[pltpu.VMEM((2, LANES, d), jnp.float32),
                            pltpu.SemaphoreType.DMA((2,))]),
        compiler_params=_cparams(("arbitrary",)),
        name="w_in_relayout",
    )(jnp.asarray(src, jnp.int32), jnp.asarray(valid, jnp.int32), w_t)


def _route(logits, b_rg, b_re, t):
    def first_max(v, n):
        top = jnp.max(v, axis=-1, keepdims=True)
        ids = jnp.arange(n, dtype=jnp.int32)
        return top, jnp.min(jnp.where(v == top, ids, n), axis=-1, keepdims=True)

    g_logits = logits[:, :N_GROUPS] + b_rg
    g_max, grp = first_max(g_logits, N_GROUPS)
    p_grp = 1.0 / jnp.sum(jnp.exp(g_logits - g_max), axis=-1, keepdims=True)
    e_logits = (logits[:, N_GROUPS:N_GROUPS + N_EXPERTS] + b_re).reshape(
        t, N_GROUPS, EXPERTS_PER_GROUP)
    in_grp = jnp.arange(N_GROUPS, dtype=jnp.int32)[None, :, None] == grp[:, :, None]
    e_in = jnp.sum(jnp.where(in_grp, e_logits, 0.0), axis=1)
    probs = jax.nn.softmax(e_in, axis=-1)
    p1, i1 = first_max(probs, EXPERTS_PER_GROUP)
    rest = jnp.where(jnp.arange(EXPERTS_PER_GROUP, dtype=jnp.int32)[None, :] == i1, -1.0, probs)
    p2, i2 = first_max(rest, EXPERTS_PER_GROUP)
    top_p = jnp.concatenate([p1, p2], axis=-1)
    weights = p_grp * top_p / jnp.sum(top_p, axis=-1, keepdims=True)
    expert_idx = grp * EXPERTS_PER_GROUP + jnp.concatenate([i1, i2], axis=-1)
    return expert_idx.astype(jnp.int32), weights


def _dispatch_plan(expert_idx, t):
    a = t * TOP_K
    flat_e = expert_idx.reshape(a)
    onehot = (flat_e[:, None] == jnp.arange(N_EXPERTS)[None, :]).astype(jnp.int32)
    sizes = jnp.sum(onehot, axis=0)
    padded = (sizes + MOE_ROWS - 1) // MOE_ROWS * MOE_ROWS
    pad_end = jnp.cumsum(padded)
    pad_start = pad_end - padded
    dest = (jnp.sum(onehot * (jnp.cumsum(onehot, axis=0) + pad_start[None, :]), axis=1)
            - 1).astype(jnp.int32)
    n_rows = a + N_EXPERTS * MOE_ROWS
    n_blk = n_rows // MOE_ROWS
    flat_tok = jnp.arange(a, dtype=jnp.int32) // TOP_K
    row_tok = jnp.zeros((n_rows,), jnp.int32).at[dest].set(flat_tok)
    blk_start = jnp.arange(n_blk, dtype=pad_end.dtype) * MOE_ROWS
    blk_e = jnp.minimum(jnp.sum(pad_end[None, :] <= blk_start[:, None], axis=1),
                        N_EXPERTS - 1).astype(jnp.int32)
    n_used = (pad_end[-1] // MOE_ROWS).astype(jnp.int32).reshape(1)
    first = jnp.concatenate([jnp.ones((1,), jnp.int32),
                             (blk_e[1:] != blk_e[:-1]).astype(jnp.int32)])
    parity = (jnp.cumsum(first) - 1) % 2
    experts = jnp.arange(N_EXPERTS, dtype=jnp.int32)
    used = jnp.where(padded > 0, experts, N_EXPERTS)
    later = jnp.flip(lax.cummin(jnp.flip(used)))
    next_used = jnp.concatenate([later[1:], jnp.full((1,), N_EXPERTS, jnp.int32)])
    next_used = jnp.where(next_used < N_EXPERTS, next_used, -1)
    sched = jnp.stack([blk_e, first, parity.astype(jnp.int32),
                       next_used[blk_e].astype(jnp.int32)])
    return dest.reshape(t, TOP_K), row_tok, sched, n_used


def kernel(x, p, positions, w_in, fox_f_bias, diff_lambda, diff_subln, mla_q_norm, mla_kv_norm,
           mla_w_uq, mla_w_ukv, w_branch, w_o, ln1_g, ln1_b, w_router_group, b_router_group,
           w_router_expert, b_router_expert, w_expert_gate, w_expert_up, w_expert_down,
           w_ple_gate, w_ple_proj, ln2_g, ln2_b):
    batch, seq, d = x.shape
    depth = w_in.shape[0]
    t = batch * seq
    alpha = (2 * depth) ** 0.25
    bf = jnp.bfloat16
    tabs = _rope_tables(positions)
    w_t = jnp.transpose(w_in, (2, 0, 1))
    col_scale = jnp.ones((1, U_WIDTH), jnp.float32)
    col_scale = col_scale.at[0, COL_FQ:COL_FQ + BRANCH_WIDTH].set(HEAD_DIM ** -0.5 * LOG2E)
    col_scale = col_scale.at[0, COL_SQ:COL_SQ + BRANCH_WIDTH].set(HEAD_DIM ** -0.5)
    xf = x.reshape(t, d)
    xb = xf.astype(bf)

    for i in range(depth):
        w1 = _reorder_w_in(w_t, i)
        u, f_logits = _matmul(xb, w1, col_scale, bf, 1024, 1536, COL_FF, "in_proj")

        b_f = jnp.zeros((1, LANES), jnp.float32).at[0, :HEADS].set(fox_f_bias[i])
        cum = _fgate_cumsum(f_logits, b_f, batch, seq)[:, :HEADS] * LOG2E
        cum_bh = jnp.transpose(cum.reshape(batch, seq, HEADS), (0, 2, 1)).reshape(
            batch * HEADS, seq)
        cq_rows = cum_bh.reshape(batch * HEADS, 1, seq)
        ck_rep = jnp.broadcast_to(cum_bh[:, :, None], (batch * HEADS, seq, LANES))

        uq = mla_w_uq[i].reshape(MLA_Q_LORA, HEADS, MLA_NOPE + MLA_ROPE)
        uq = jnp.pad(uq, ((0, 0), (0, 0), (0, MLA_QK_PAD - MLA_NOPE - MLA_ROPE)))
        uq = uq.reshape(MLA_Q_LORA, HEADS * MLA_QK_PAD).astype(bf)
        ukv = mla_w_ukv[i].reshape(MLA_KV_LORA, HEADS, MLA_NOPE + HEAD_DIM)
        uk = ukv[:, :, :MLA_NOPE].reshape(MLA_KV_LORA, HEADS * MLA_NOPE).astype(bf)
        uv = ukv[:, :, MLA_NOPE:].reshape(MLA_KV_LORA, HEADS * HEAD_DIM).astype(bf)
        dqk, mq, mk, mv = _prep(u, tabs, mla_q_norm[i].reshape(1, -1),
                                mla_kv_norm[i].reshape(1, -1), uq, uk, uv)

        lam_init = 0.8 - 0.6 * math.exp(-0.3 * i)
        lp = diff_lambda[i].astype(jnp.float32)
        lam = (jnp.exp(jnp.sum(lp[0] * lp[1])) - jnp.exp(jnp.sum(lp[2] * lp[3]))
               + lam_init).reshape(1)
        blk = lambda col: col // HEAD_DIM
        o_d = _attention("diff", batch, seq, dqk, 0, dqk, HEADS, u, blk(COL_DV), HEAD_DIM,
                         extras=(diff_subln[i].reshape(1, HEAD_DIM),),
                         lam=lam, lam_init=lam_init)
        o_f = _attention("fox", batch, seq, u, blk(COL_FQ), u, blk(COL_FK), u, blk(COL_FV),
                         HEAD_DIM, extras=(cq_rows, ck_rep))
        o_m = _attention("mla", batch, seq, mq, 0, mk, 0, mv, 0, MLA_QK_PAD)
        o_s = _attention("sb", batch, seq, u, blk(COL_SQ), u, blk(COL_SK), u, blk(COL_SV),
                         HEAD_DIM)

        w_r = jnp.concatenate([w_router_group[i], w_router_expert[i]], axis=1)
        w_r = jnp.pad(w_r, ((0, 0), (0, LANES - w_r.shape[1])))
        wr_hi = w_r.astype(bf)
        wr_lo = (w_r - wr_hi.astype(jnp.float32)).astype(bf)
        h, h_rows, hb, logits = _merge(
            u, o_d, o_f, o_m, o_s, w_branch[i].astype(bf), w_o[i].astype(bf), xf,
            ln1_g[i].reshape(1, d), ln1_b[i].reshape(1, d), wr_hi, wr_lo, alpha)

        expert_idx, weights = _route(logits, b_router_group[i], b_router_expert[i], t)
        dest, row_tok, sched, n_used = _dispatch_plan(expert_idx, t)
        y_rows = _moe(sched, n_used, row_tok, h_rows, w_expert_gate, w_expert_up,
                      w_expert_down, i)
        xf, xb = _final(dest[:, 0], dest[:, 1], h, hb, y_rows, weights.astype(jnp.float32),
                        p[i].reshape(t, -1), w_ple_gate[i].astype(bf),
                        w_ple_proj[i].astype(bf), ln2_g[i].reshape(1, d),
                        ln2_b[i].reshape(1, d), alpha)
    return xf.reshape(batch, seq, d)
```

```python
import functools
import math

import jax
import jax.numpy as jnp
from jax import lax
from jax.experimental import pallas as pl
from jax.experimental.pallas import tpu as pltpu

HEAD_DIM = 128
HEADS = 4
N_BRANCHES = 4
BRANCH_WIDTH = HEADS * HEAD_DIM
DIFF_QK = 64
DIFF_ROT = DIFF_QK // 4
MLA_Q_LORA = 512
MLA_KV_LORA = 256
MLA_NOPE = 128
MLA_ROPE = 64
MLA_QK_PAD = 256
N_GROUPS = 4
EXPERTS_PER_GROUP = 8
N_EXPERTS = N_GROUPS * EXPERTS_PER_GROUP
TOP_K = 2
CHUNK = 64
ROPE_THETA = 500000.0
LN_EPS = 1e-5
RMS_EPS = 1e-6
NEG_BIG = -1e30
LOG2E = 1.4426950408889634
SB_EXP_UNDERFLOW = -104.0
LANES = 128
V7X_VMEM_LIMIT = 56 * 1024 * 1024

COL_GATES = 0
COL_DQ = 8192
COL_DV = COL_DQ + 1024
COL_FQ = COL_DV + 512
COL_FK = COL_FQ + 512
COL_FV = COL_FK + 512
COL_SQ = COL_FV + 512
COL_SK = COL_SQ + 512
COL_SV = COL_SK + 512
COL_CQ = COL_SV + 512
COL_CKV = COL_CQ + 512
COL_KR = COL_CKV + 256
COL_FF = COL_KR + 128
U_WIDTH = COL_FF + 128

ATT_TQ = 512
MOE_ROWS = 256
MOE_ROW_BUFS = 3


def _cparams(sem, vmem=V7X_VMEM_LIMIT):
    return pltpu.CompilerParams(dimension_semantics=sem, vmem_limit_bytes=vmem)


def _resident(shape, index_map):
    return pl.BlockSpec(shape, index_map, pipeline_mode=pl.Buffered(1))


def _matmul_body(x_ref, w_ref, c_ref, o_ref, f_ref, *, f_block, f_off):
    acc = lax.dot_general(x_ref[...], w_ref[...], (((1,), (1,)), ((), ())),
                          preferred_element_type=jnp.float32)
    o_ref[...] = (acc * c_ref[...]).astype(o_ref.dtype)

    @pl.when(pl.program_id(1) == f_block)
    def _():
        f_ref[...] = acc[:, f_off:f_off + LANES]


def _matmul(x, w_t, col_scale, out_dtype, tm, tn, f32_col, name):
    m, k = x.shape
    n = w_t.shape[0]
    tm, tn = min(tm, m), min(tn, n)
    body = functools.partial(_matmul_body, f_block=f32_col // tn, f_off=f32_col % tn)
    return pl.pallas_call(
        body,
        out_shape=(jax.ShapeDtypeStruct((m, n), out_dtype),
                   jax.ShapeDtypeStruct((m, LANES), jnp.float32)),
        grid=(m // tm, n // tn),
        in_specs=[pl.BlockSpec((tm, k), lambda i, j: (i, 0)),
                  pl.BlockSpec((tn, k), lambda i, j: (j, 0)),
                  pl.BlockSpec((1, tn), lambda i, j: (0, j))],
        out_specs=(pl.BlockSpec((tm, tn), lambda i, j: (i, j)),
                   pl.BlockSpec((tm, LANES), lambda i, j: (i, 0))),
        compiler_params=_cparams(("parallel", "arbitrary")),
        name=name,
    )(x, w_t, col_scale)


def _split3(x):
    hi = x.astype(jnp.bfloat16)
    r1 = x - hi.astype(jnp.float32)
    mid = r1.astype(jnp.bfloat16)
    lo = (r1 - mid.astype(jnp.float32)).astype(jnp.bfloat16)
    return hi, mid, lo


def _split2(x):
    hi = x.astype(jnp.bfloat16)
    mid = (x - hi.astype(jnp.float32)).astype(jnp.bfloat16)
    return hi, mid


def _dot(a, b):
    return jnp.dot(a, b, preferred_element_type=jnp.float32)


def _dot_nt(a, b):
    return lax.dot_general(a, b, (((1,), (1,)), ((), ())), preferred_element_type=jnp.float32)


def _log_sigmoid(z):
    neg_abs = pltpu.bitcast(pltpu.bitcast(z, jnp.uint32) | jnp.uint32(0x80000000), jnp.float32)
    return jnp.minimum(z, 0.0) - jnp.log(1.0 + jnp.exp(neg_abs))


def _sigmoid(z):
    return 1.0 / (1.0 + jnp.exp(-z))


def _fgate_body(f_ref, b_ref, o_ref, carry_ref):
    @pl.when(pl.program_id(1) == 0)
    def _():
        carry_ref[...] = jnp.zeros_like(carry_ref)

    log_f = _log_sigmoid(f_ref[...] + b_ref[...])
    ts = log_f.shape[0]
    row = lax.broadcasted_iota(jnp.int32, (ts, ts), 0)
    col = lax.broadcasted_iota(jnp.int32, (ts, ts), 1)
    tri = (col <= row).astype(jnp.bfloat16)
    hi, mid, lo = _split3(log_f)
    cum = _dot(tri, hi) + _dot(tri, mid) + _dot(tri, lo) + carry_ref[...]
    o_ref[...] = cum
    carry_ref[...] = cum[ts - 1:ts, :]


def _fgate_cumsum(f_logits, b_f, batch, seq):
    t = f_logits.shape[0]
    ts = min(512, seq)
    ns = seq // ts
    return pl.pallas_call(
        _fgate_body,
        out_shape=jax.ShapeDtypeStruct((t, LANES), jnp.float32),
        grid=(batch, ns),
        in_specs=[pl.BlockSpec((ts, LANES), lambda b, s: (b * ns + s, 0)),
                  _resident((1, LANES), lambda b, s: (0, 0))],
        out_specs=pl.BlockSpec((ts, LANES), lambda b, s: (b * ns + s, 0)),
        scratch_shapes=[pltpu.VMEM((1, LANES), jnp.float32)],
        compiler_params=_cparams(("parallel", "arbitrary")),
        name="fgate_cumsum",
    )(f_logits, b_f)


def _rope_lanes(x, c, s_up, s_dn, half):
    return x * c + pltpu.roll(x, LANES - half, 1) * s_up + pltpu.roll(x, half, 1) * s_dn


def _prep_body(dqk_ref, cq_ref, ckv_ref, kr_ref, c16_ref, su16_ref, sd16_ref,
               c64_ref, su64_ref, sd64_ref, qg_ref, kvg_ref, wuq_ref, wuk_ref, wuv_ref,
               dqk_o, mq_o, mk_o, mv_o):
    c16, su16, sd16 = c16_ref[...], su16_ref[...], sd16_ref[...]
    c64, su64, sd64 = c64_ref[...], su64_ref[...], sd64_ref[...]
    for j in range(dqk_ref.shape[1] // LANES):
        sl = slice(j * LANES, (j + 1) * LANES)
        x = _rope_lanes(dqk_ref[:, sl].astype(jnp.float32), c16, su16, sd16, DIFF_ROT // 2)
        if j < HEADS:
            x = x * (DIFF_QK ** -0.5 * LOG2E)
        dqk_o[:, sl] = x.astype(dqk_o.dtype)

    cq = cq_ref[...].astype(jnp.float32)
    cqn = cq * lax.rsqrt(jnp.mean(cq * cq, axis=-1, keepdims=True) + RMS_EPS) * qg_ref[...]
    q = _dot(cqn.astype(jnp.bfloat16), wuq_ref[...])
    q = q * ((MLA_NOPE + MLA_ROPE) ** -0.5 * LOG2E)
    ckv = ckv_ref[...].astype(jnp.float32)
    ckvn = (ckv * lax.rsqrt(jnp.mean(ckv * ckv, axis=-1, keepdims=True) + RMS_EPS)
            * kvg_ref[...]).astype(jnp.bfloat16)
    kn = _dot(ckvn, wuk_ref[...])
    mv_o[...] = _dot(ckvn, wuv_ref[...]).astype(mv_o.dtype)
    kr = _rope_lanes(kr_ref[...].astype(jnp.float32), c64, su64, sd64, MLA_ROPE // 2)
    for h in range(HEADS):
        base = h * MLA_QK_PAD
        mq_o[:, base:base + LANES] = q[:, base:base + LANES].astype(mq_o.dtype)
        qr = _rope_lanes(q[:, base + LANES:base + 2 * LANES], c64, su64, sd64, MLA_ROPE // 2)
        mq_o[:, base + LANES:base + 2 * LANES] = qr.astype(mq_o.dtype)
        mk_o[:, base:base + LANES] = kn[:, h * LANES:(h + 1) * LANES].astype(mk_o.dtype)
        mk_o[:, base + LANES:base + 2 * LANES] = kr.astype(mk_o.dtype)


def _prep(u, tabs, q_norm, kv_norm, w_uq, w_uk, w_uv):
    t = u.shape[0]
    tm = min(512, t)
    row = lambda blk: (lambda i: (i, blk))
    const = lambda i: (0, 0)
    tab_spec = pl.BlockSpec((tm, LANES), lambda i: (i, 0))
    outs = pl.pallas_call(
        _prep_body,
        out_shape=(jax.ShapeDtypeStruct((t, 1024), jnp.bfloat16),
                   jax.ShapeDtypeStruct((t, HEADS * MLA_QK_PAD), jnp.bfloat16),
                   jax.ShapeDtypeStruct((t, HEADS * MLA_QK_PAD), jnp.bfloat16),
                   jax.ShapeDtypeStruct((t, BRANCH_WIDTH), jnp.bfloat16)),
        grid=(t // tm,),
        in_specs=[pl.BlockSpec((tm, 1024), row(COL_DQ // 1024)),
                  pl.BlockSpec((tm, MLA_Q_LORA), row(COL_CQ // MLA_Q_LORA)),
                  pl.BlockSpec((tm, MLA_KV_LORA), row(COL_CKV // MLA_KV_LORA)),
                  pl.BlockSpec((tm, LANES), row(COL_KR // LANES)),
                  tab_spec, tab_spec, tab_spec, tab_spec, tab_spec, tab_spec,
                  _resident((1, MLA_Q_LORA), const),
                  _resident((1, MLA_KV_LORA), const),
                  _resident(w_uq.shape, const),
                  _resident(w_uk.shape, const),
                  _resident(w_uv.shape, const)],
        out_specs=(pl.BlockSpec((tm, 1024), lambda i: (i, 0)),
                   pl.BlockSpec((tm, HEADS * MLA_QK_PAD), lambda i: (i, 0)),
                   pl.BlockSpec((tm, HEADS * MLA_QK_PAD), lambda i: (i, 0)),
                   pl.BlockSpec((tm, BRANCH_WIDTH), lambda i: (i, 0))),
        compiler_params=_cparams(("parallel",)),
        name="prep",
    )(u, u, u, u, *tabs, q_norm, kv_norm, w_uq, w_uk, w_uv)
    return outs


def _attn_body(*refs, kind, tq, tk, lam_init):
    softmax = kind != "sb"
    if kind == "diff":
        lam_ref, q_ref, k_ref, v_ref, g_ref, o_ref = refs[:6]
    elif kind == "fox":
        q_ref, k_ref, v_ref, cq_ref, ck_ref, o_ref = refs[:6]
    else:
        q_ref, k_ref, v_ref, o_ref = refs[:4]
    if softmax:
        s_bufs, p_bufs = refs[-8:-6], refs[-6:-4]
        vt_ref, m_ref, a_ref, acc_ref = refs[-4:]
    else:
        s_bufs, p_bufs = refs[-7:-5], refs[-5:-3]
        vt_ref, r_ref, acc_ref = refs[-3:]
    qi = pl.program_id(2)

    @pl.when(qi == 0)
    def _():
        for kb in range(vt_ref.shape[0]):
            vt_ref[kb, 0:HEAD_DIM, :] = v_ref[kb * tk:(kb + 1) * tk, :].T
            if softmax:
                vt_ref[kb, HEAD_DIM:, :] = jnp.ones((vt_ref.shape[1] - HEAD_DIM, tk),
                                                    vt_ref.dtype)

    q0 = qi * tq
    last_kb = 2 * qi + 1
    q = q_ref[...]
    n_comp = 2 if kind == "diff" else 1
    if kind == "diff":
        lane = lax.broadcasted_iota(jnp.int32, q.shape, 1)
        qs = [jnp.where(lane < DIFF_QK, q, jnp.zeros_like(q)),
              jnp.where(lane >= DIFF_QK, q, jnp.zeros_like(q))]
    else:
        qs = [q]

    acc_ref[...] = jnp.zeros_like(acc_ref)
    p_bufs[1][...] = jnp.zeros_like(p_bufs[1])
    if softmax:
        m_ref[...] = jnp.full_like(m_ref, NEG_BIG)
        a_ref[...] = jnp.zeros_like(a_ref)
    else:
        r_ref[...] = jnp.zeros_like(r_ref)

    def key_rows(kb):
        if isinstance(kb, int):
            return pl.ds(kb * tk, tk)
        return pl.ds(pl.multiple_of(kb * tk, tk), tk)

    def scores_stage(kb, s_dst):
        k_t = k_ref[key_rows(kb), :]
        for c in range(n_comp):
            s_dst[c] = _dot_nt(k_t, qs[c])

    def values_stage(kb, p_src):
        v_t = vt_ref[kb]
        for c in range(n_comp):
            pv = _dot(v_t, p_src[c])[:acc_ref.shape[1]]
            if softmax:
                acc_ref[c] = a_ref[c] * acc_ref[c] + pv
            else:
                acc_ref[c] = acc_ref[c] + pv

    def weights_stage(kb, s_src, p_dst, masked):
        if masked:
            j = lax.broadcasted_iota(jnp.int32, (tk, tq), 0)
            i = lax.broadcasted_iota(jnp.int32, (tk, tq), 1)
            off = q0 - kb * tk
            if kind in ("diff", "mla"):
                mask = (j // CHUNK - i // CHUNK) <= off // CHUNK
            elif kind == "fox":
                mask = (j - i) <= off
            else:
                mask = (j - i) < off
        if not softmax:
            z_t = s_src[0]
            ls = _log_sigmoid(z_t)
            lk = ls - z_t
            if masked:
                lk = jnp.where(mask, lk, 0.0)
            jj = lax.broadcasted_iota(jnp.int32, (tk, tk), 0)
            mm = lax.broadcasted_iota(jnp.int32, (tk, tk), 1)
            upper = (mm > jj).astype(jnp.bfloat16)
            hi, mid = _split2(lk)
            suffix = _dot(jnp.concatenate([upper, upper], axis=1),
                          jnp.concatenate([hi, mid], axis=0))
            r_old = r_ref[...]
            w_t = jnp.exp(ls + (r_old + suffix))
            if masked:
                w_t = jnp.where(mask, w_t, 0.0)
            p_dst[0] = w_t.astype(jnp.bfloat16)
            r_ref[...] = r_old + suffix[0:1, :] + lk[0:1, :]
            return
        for c in range(n_comp):
            s_t = s_src[c]
            if kind == "fox":
                ck = ck_ref[key_rows(kb), :]
                s_t = s_t + cq_ref[...] - jnp.concatenate([ck] * (tq // LANES), axis=1)
            if masked:
                s_t = jnp.where(mask, s_t, NEG_BIG)
            m_old = m_ref[c]
            m_new = jnp.maximum(m_old, jnp.max(s_t, axis=0, keepdims=True))
            alpha = jnp.exp2(m_old - m_new)
            p_t = jnp.exp2(s_t - m_new)
            m_ref[c] = m_new
            a_ref[c] = alpha
            p_dst[c] = p_t.astype(jnp.bfloat16)

    def step(kb, parity, masked, next_kb, prev_kb):
        if next_kb is not None:
            scores_stage(next_kb, s_bufs[1 - parity])
        values_stage(prev_kb, p_bufs[1 - parity])
        weights_stage(kb, s_bufs[parity], p_bufs[parity], masked)

    if softmax and tq == 2 * tk:
        scores_stage(0, s_bufs[0])

        def body(t, carry):
            kb = 2 * t
            step(kb, 0, False, kb + 1, jnp.maximum(kb - 1, 0))
            step(kb + 1, 1, False, kb + 2, kb)
            return carry
        lax.fori_loop(0, qi, body, 0)
        step(last_kb - 1, 0, True, last_kb, jnp.maximum(last_kb - 2, 0))
        step(last_kb, 1, True, None, last_kb - 1)
        values_stage(last_kb, p_bufs[1])
    elif softmax:
        scores_stage(0, s_bufs[0])

        def body(t, carry):
            kb = 2 * t
            step(kb, 0, False, kb + 1, jnp.maximum(kb - 1, 0))
            step(kb + 1, 1, False, kb + 2, kb)
            return carry
        lax.fori_loop(0, qi // 2, body, 0)

        @pl.when(qi % 2 == 0)
        def _():
            step(qi, 0, True, None, jnp.maximum(qi - 1, 0))
            values_stage(qi, p_bufs[0])

        @pl.when(qi % 2 == 1)
        def _():
            step(qi - 1, 0, False, qi, jnp.maximum(qi - 2, 0))
            step(qi, 1, True, None, qi - 1)
            values_stage(qi, p_bufs[1])
    else:
        scores_stage(last_kb, s_bufs[0])
        step(last_kb, 0, True, last_kb - 1, last_kb)
        step(last_kb - 1, 1, True, jnp.maximum(last_kb - 2, 0), last_kb)

        def live(carry):
            t, _, r_max = carry
            return (t < qi) & (r_max >= SB_EXP_UNDERFLOW)

        def body(carry):
            t = carry[0]
            kb = last_kb - 2 - 2 * t
            step(kb, 0, False, kb - 1, kb + 1)
            step(kb - 1, 1, False, jnp.maximum(kb - 2, 0), kb)
            return t + 1, kb - 1, jnp.max(r_ref[...])
        done = lax.while_loop(live, body, (0, last_kb - 1, jnp.max(r_ref[...])))
        values_stage(done[1], p_bufs[1])

    def normalised(c):
        return acc_ref[c, 0:HEAD_DIM, :] / acc_ref[c, HEAD_DIM:HEAD_DIM + 1, :]

    if kind == "sb":
        o_t = acc_ref[0]
    elif kind == "diff":
        o_t = normalised(0) - lam_ref[0] * normalised(1)
    else:
        o_t = normalised(0)
    o = o_t.T
    if kind == "diff":
        o = (o * lax.rsqrt(jnp.mean(o * o, axis=-1, keepdims=True) + RMS_EPS)
             * g_ref[...] * (1.0 - lam_init))
    o_ref[...] = o.astype(o_ref.dtype)


def _attention(kind, batch, seq, q_arr, q_col, k_arr, k_col, v_arr, v_col, dq,
               extras=(), lam=None, lam_init=0.0):
    t = q_arr.shape[0]
    tq = min(ATT_TQ, seq)
    tk = tq if kind in ("diff", "mla") else tq // 2
    nq = seq // tq
    q_spec = pl.BlockSpec((tq, dq), lambda b, h, i, *_: (b * nq + i, q_col + h))
    k_spec = pl.BlockSpec((seq, dq), lambda b, h, i, *_: (b, k_col + h))
    v_spec = pl.BlockSpec((seq, HEAD_DIM), lambda b, h, i, *_: (b, v_col + h))
    in_specs = [q_spec, k_spec, v_spec]
    args = [q_arr, k_arr, v_arr]
    n_comp = 2 if kind == "diff" else 1
    if kind == "diff":
        in_specs.append(_resident((1, HEAD_DIM), lambda b, h, i, *_: (0, 0)))
        args.append(extras[0])
    elif kind == "fox":
        cq_rows, ck_rep = extras
        in_specs.append(pl.BlockSpec((None, 1, tq), lambda b, h, i: (b * HEADS + h, 0, i)))
        in_specs.append(pl.BlockSpec((None, seq, LANES), lambda b, h, i: (b * HEADS + h, 0, 0)))
        args += [cq_rows, ck_rep]
    s_buf = pltpu.VMEM((n_comp, tk, tq), jnp.float32)
    p_buf = pltpu.VMEM((n_comp, tk, tq), jnp.bfloat16)
    row = pltpu.VMEM((n_comp, 1, tq), jnp.float32)
    nk = seq // tk
    if kind == "sb":
        scratch = [s_buf, s_buf, p_buf, p_buf,
                   pltpu.VMEM((nk, HEAD_DIM, tk), jnp.bfloat16),
                   pltpu.VMEM((1, tq), jnp.float32),
                   pltpu.VMEM((1, HEAD_DIM, tq), jnp.float32)]
    else:
        scratch = [s_buf, s_buf, p_buf, p_buf,
                   pltpu.VMEM((nk, HEAD_DIM + 16, tk), jnp.bfloat16), row, row,
                   pltpu.VMEM((n_comp, HEAD_DIM + 8, tq), jnp.float32)]
    body = functools.partial(_attn_body, kind=kind, tq=tq, tk=tk, lam_init=lam_init)
    grid = (batch, HEADS, nq)
    out_shape = jax.ShapeDtypeStruct((t, BRANCH_WIDTH), jnp.bfloat16)
    out_spec = pl.BlockSpec((tq, HEAD_DIM), lambda b, h, i, *_: (b * nq + i, h))
    cp = _cparams(("parallel", "parallel", "arbitrary"))
    if kind == "diff":
        return pl.pallas_call(
            body, out_shape=out_shape,
            grid_spec=pltpu.PrefetchScalarGridSpec(
                num_scalar_prefetch=1, grid=grid, in_specs=in_specs, out_specs=out_spec,
                scratch_shapes=scratch),
            compiler_params=cp, name="attn_" + kind)(lam, *args)
    return pl.pallas_call(
        body, out_shape=out_shape, grid=grid, in_specs=in_specs, out_specs=out_spec,
        scratch_shapes=scratch, compiler_params=cp, name="attn_" + kind)(*args)


def _layer_norm(y, g, b):
    mu = jnp.mean(y, axis=-1, keepdims=True)
    yc = y - mu
    var = jnp.mean(yc * yc, axis=-1, keepdims=True)
    return yc * lax.rsqrt(var + LN_EPS) * g + b


def _merge_body(g_ref, od_ref, of_ref, om_ref, os_ref, wb_ref, wo_ref, x_ref, lg_ref, lb_ref,
                wrh_ref, wrl_ref, h_o, hr_o, hb_o, lgt_o, *, alpha):
    d = x_ref.shape[1]
    merged = None
    for n, o_ref in enumerate((od_ref, of_ref, om_ref, os_ref)):
        gate = _sigmoid(g_ref[:, n * d:(n + 1) * d].astype(jnp.float32))
        term = gate * _dot(o_ref[...], wb_ref[n])
        merged = term if merged is None else merged + term
    mix = _dot(merged.astype(jnp.bfloat16), wo_ref[...])
    h = _layer_norm(alpha * x_ref[...] + mix, lg_ref[...], lb_ref[...])
    h_o[...] = h
    h_hi = h.astype(jnp.bfloat16)
    hb_o[...] = h_hi
    _to_chunked(hr_o, _pack_bf16_halves(h_hi))
    h_lo = (h - h_hi.astype(jnp.float32)).astype(jnp.bfloat16)
    lgt_o[...] = (_dot(h_hi, wrh_ref[...]) + _dot(h_lo, wrh_ref[...]) + _dot(h_hi, wrl_ref[...]))


def _merge(u, o_d, o_f, o_m, o_s, w_b, w_o, x, ln_g, ln_b, wr_hi, wr_lo, alpha):
    t, d = x.shape
    tm = min(256, t)
    row = lambda i: (i, 0)
    const2 = lambda i: (0, 0)
    o_spec = pl.BlockSpec((tm, BRANCH_WIDTH), row)
    return pl.pallas_call(
        functools.partial(_merge_body, alpha=alpha),
        out_shape=(jax.ShapeDtypeStruct((t, d), jnp.float32),
                   jax.ShapeDtypeStruct((t * _chunk_pitch(d // 2), LANES), jnp.uint32),
                   jax.ShapeDtypeStruct((t, d), jnp.bfloat16),
                   jax.ShapeDtypeStruct((t, LANES), jnp.float32)),
        grid=(t // tm,),
        in_specs=[pl.BlockSpec((tm, N_BRANCHES * d), row),
                  o_spec, o_spec, o_spec, o_spec,
                  _resident(w_b.shape, lambda i: (0, 0, 0)),
                  _resident(w_o.shape, const2),
                  pl.BlockSpec((tm, d), row),
                  _resident((1, d), const2), _resident((1, d), const2),
                  _resident(wr_hi.shape, const2), _resident(wr_lo.shape, const2)],
        out_specs=(pl.BlockSpec((tm, d), row),
                   pl.BlockSpec((tm * _chunk_pitch(d // 2), LANES), row),
                   pl.BlockSpec((tm, d), row), pl.BlockSpec((tm, LANES), row)),
        compiler_params=_cparams(("parallel",)),
        name="merge_ln1",
    )(u, o_d, o_f, o_m, o_s, w_b, w_o, x, ln_g, ln_b, wr_hi, wr_lo)


CHUNK_PAD = 2


def _chunk_pitch(d):
    return d // LANES + CHUNK_PAD


def _to_chunked(ref, x):
    n, d = x.shape
    nc, pitch = d // LANES, _chunk_pitch(d)
    for c in range(nc):
        ref[pl.ds(c, n, stride=pitch), :] = x[:, c * LANES:(c + 1) * LANES]
    for c in range(nc, pitch):
        ref[pl.ds(c, n, stride=pitch), :] = jnp.zeros((n, LANES), x.dtype)


def _from_chunked(ref, n, d):
    pitch = _chunk_pitch(d)
    return jnp.concatenate([ref[pl.ds(c, n, stride=pitch), :] for c in range(d // LANES)],
                           axis=1)


def _gather_rows(idx_ref, base, src_hbm, dst, sem, n, d):
    nc, pitch = d // LANES, _chunk_pitch(d)
    for r in range(n):
        row = pl.multiple_of(idx_ref[base + r] * pitch, CHUNK_PAD)
        pltpu.make_async_copy(src_hbm.at[pl.ds(row, nc), :], dst.at[pl.ds(r * pitch, nc), :],
                              sem).start(priority=r % 2)


def _wait_rows(src_hbm, dst, sem, n, d):
    rows = n * (d // LANES)
    pltpu.make_async_copy(src_hbm.at[pl.ds(0, rows), :], dst.at[pl.ds(0, rows), :], sem).wait()


def _pack_bf16_halves(x):
    bits = pltpu.bitcast(x.astype(jnp.float32), jnp.uint32)
    half = x.shape[1] // 2
    return bits[:, :half] | (bits[:, half:] >> 16)


def _unpack_bf16_halves(w):
    left = pltpu.bitcast(w & jnp.uint32(0xFFFF0000), jnp.float32)
    right = pltpu.bitcast(w << 16, jnp.float32)
    return jnp.concatenate([left, right], axis=1).astype(jnp.bfloat16)


def _moe_body(sched_ref, n_used_ref, tok_ref, h_hbm, wg_hbm, wu_hbm, wd_hbm, y_ref,
              xbuf, sem, wg_f, wu_f, wd_f, wsem, wg_b, wu_b, wd_b, *, layer):
    i = pl.program_id(0)
    n_used = n_used_ref[0]
    d = wg_b.shape[0] // 2
    bm = xbuf.shape[1] // _chunk_pitch(d)
    slot = i % MOE_ROW_BUFS

    def weight_copies(e, s):
        return (pltpu.make_async_copy(wg_hbm.at[layer, e], wg_f.at[s], wsem.at[s]),
                pltpu.make_async_copy(wu_hbm.at[layer, e], wu_f.at[s], wsem.at[s]),
                pltpu.make_async_copy(wd_hbm.at[layer, e], wd_f.at[s], wsem.at[s]))

    @pl.when((i == 0) & (n_used > 0))
    def _():
        for cp in weight_copies(sched_ref[0, 0], 0):
            cp.start(priority=1)
        _gather_rows(tok_ref, 0, h_hbm, xbuf.at[0], sem.at[0], bm, d)

    for ahead in range(1, MOE_ROW_BUFS - 1):
        @pl.when((i == 0) & (ahead < n_used))
        def _():
            _gather_rows(tok_ref, ahead * bm, h_hbm, xbuf.at[ahead], sem.at[ahead], bm, d)

    nxt = i + MOE_ROW_BUFS - 1

    @pl.when(nxt < n_used)
    def _():
        s = nxt % MOE_ROW_BUFS
        _gather_rows(tok_ref, nxt * bm, h_hbm, xbuf.at[s], sem.at[s], bm, d)

    @pl.when(i < n_used)
    def _():
        @pl.when(sched_ref[1, i] == 1)
        def _():
            ws = sched_ref[2, i]
            for cp in weight_copies(sched_ref[0, i], ws):
                cp.wait()
            wg_b[...] = wg_f[ws].astype(jnp.bfloat16)
            wu_b[...] = wu_f[ws].astype(jnp.bfloat16)
            wd_b[...] = wd_f[ws].astype(jnp.bfloat16)

            @pl.when(sched_ref[3, i] >= 0)
            def _():
                for cp in weight_copies(sched_ref[3, i], 1 - ws):
                    cp.start(priority=1)

        _wait_rows(h_hbm, xbuf.at[slot], sem.at[slot], bm, d)
        x = _unpack_bf16_halves(_from_chunked(xbuf.at[slot], bm, d))
        a = _dot(x, wg_b[...])
        hid = (a * _sigmoid(a)) * _dot(x, wu_b[...])
        _to_chunked(y_ref, _dot(hid.astype(jnp.bfloat16), wd_b[...]))

    @pl.when(i >= n_used)
    def _():
        y_ref[...] = jnp.zeros_like(y_ref)


def _moe(sched, n_used, row_tok, h_rows, w_g, w_u, w_d, layer):
    n_rows = row_tok.shape[0]
    d, hid = w_g.shape[2], w_g.shape[3]
    pitch = _chunk_pitch(d)
    bm = MOE_ROWS
    any_spec = pl.BlockSpec(memory_space=pl.ANY)
    return pl.pallas_call(
        functools.partial(_moe_body, layer=layer),
        out_shape=jax.ShapeDtypeStruct((n_rows * pitch, LANES), jnp.float32),
        grid_spec=pltpu.PrefetchScalarGridSpec(
            num_scalar_prefetch=3, grid=(n_rows // bm,),
            in_specs=[any_spec, any_spec, any_spec, any_spec],
            out_specs=pl.BlockSpec((bm * pitch, LANES), lambda i, s, n, tok: (i, 0)),
            scratch_shapes=[pltpu.VMEM((MOE_ROW_BUFS, bm * _chunk_pitch(d // 2), LANES),
                                       jnp.uint32),
                            pltpu.SemaphoreType.DMA((MOE_ROW_BUFS,)),
                            pltpu.VMEM((2, d, hid), jnp.float32),
                            pltpu.VMEM((2, d, hid), jnp.float32),
                            pltpu.VMEM((2, hid, d), jnp.float32),
                            pltpu.SemaphoreType.DMA((2,)),
                            pltpu.VMEM((d, hid), jnp.bfloat16),
                            pltpu.VMEM((d, hid), jnp.bfloat16),
                            pltpu.VMEM((hid, d), jnp.bfloat16)]),
        compiler_params=_cparams(("arbitrary",)),
        name="moe_experts",
    )(sched, n_used, row_tok, h_rows, w_g, w_u, w_d)


def _final_body(d0_ref, d1_ref, h_ref, hb_ref, y_hbm, gw_ref, p_ref, wg_ref, wp_ref, lg_ref,
                lb_ref, y_o, yb_o, gbuf, sem, *, alpha):
    i = pl.program_id(0)
    n = pl.num_programs(0)
    tm, d = h_ref.shape
    slot = i % 2

    def gather(step, s):
        _gather_rows(d0_ref, step * tm, y_hbm, gbuf.at[s, 0], sem.at[s], tm, d)
        _gather_rows(d1_ref, step * tm, y_hbm, gbuf.at[s, 1], sem.at[s], tm, d)

    @pl.when(i == 0)
    def _():
        gather(0, 0)

    @pl.when(i + 1 < n)
    def _():
        gather(i + 1, 1 - slot)

    gate = _sigmoid(_dot(hb_ref[...], wg_ref[...]))
    proj = _dot(p_ref[...].astype(jnp.bfloat16), wp_ref[...])
    _wait_rows(y_hbm, gbuf.at[slot, 0], sem.at[slot], tm, d)
    _wait_rows(y_hbm, gbuf.at[slot, 1], sem.at[slot], tm, d)
    gw = gw_ref[...]
    ffn = (_from_chunked(gbuf.at[slot, 0], tm, d) * gw[:, 0:1]
           + _from_chunked(gbuf.at[slot, 1], tm, d) * gw[:, 1:2])
    y = _layer_norm(alpha * h_ref[...] + ffn + gate * proj, lg_ref[...], lb_ref[...])
    y_o[...] = y
    yb_o[...] = y.astype(jnp.bfloat16)


def _final(dest0, dest1, h, hb, y_rows, gate_w, p, w_pg, w_pp, ln_g, ln_b, alpha):
    t, d = h.shape
    tm = min(256, t)
    row = lambda i, *_: (i, 0)
    const2 = lambda i, *_: (0, 0)
    return pl.pallas_call(
        functools.partial(_final_body, alpha=alpha),
        out_shape=(jax.ShapeDtypeStruct((t, d), jnp.float32),
                   jax.ShapeDtypeStruct((t, d), jnp.bfloat16)),
        grid_spec=pltpu.PrefetchScalarGridSpec(
            num_scalar_prefetch=2, grid=(t // tm,),
            in_specs=[pl.BlockSpec((tm, d), row), pl.BlockSpec((tm, d), row),
                      pl.BlockSpec(memory_space=pl.ANY),
                      pl.BlockSpec((tm, TOP_K), row), pl.BlockSpec((tm, p.shape[1]), row),
                      _resident(w_pg.shape, const2), _resident(w_pp.shape, const2),
                      _resident((1, d), const2), _resident((1, d), const2)],
            out_specs=(pl.BlockSpec((tm, d), row), pl.BlockSpec((tm, d), row)),
            scratch_shapes=[pltpu.VMEM((2, TOP_K, tm * _chunk_pitch(d), LANES), jnp.float32),
                            pltpu.SemaphoreType.DMA((2,))]),
        compiler_params=_cparams(("arbitrary",)),
        name="final_ln2",
    )(dest0, dest1, h, hb, y_rows, gate_w, p, w_pg, w_pp, ln_g, ln_b)


def _rope_tables(positions):
    pos = positions.reshape(-1).astype(jnp.float32)
    lane = jnp.arange(LANES)

    def tables(rot, period):
        half = rot // 2
        inv_freq = ROPE_THETA ** (-jnp.arange(half, dtype=jnp.float32) / half)
        ang = pos[:, None] * inv_freq
        cos, sin = jnp.cos(ang), jnp.sin(ang)
        lp = lane % period
        idx = lp % half
        in_rot = lp < rot
        c = jnp.where(in_rot[None, :], cos[:, idx], 1.0)
        s_up = jnp.where((lp < half)[None, :], -sin[:, idx], 0.0)
        s_dn = jnp.where(((lp >= half) & in_rot)[None, :], sin[:, idx], 0.0)
        return c, s_up, s_dn

    return tables(DIFF_ROT, DIFF_QK) + tables(MLA_ROPE, LANES)


def _w_in_segments(d):
    widths = [512, 512, 512, 512, 512, 512, HEADS, MLA_Q_LORA, MLA_KV_LORA, MLA_ROPE,
              512, 512, 512, N_BRANCHES * d]
    dsts = [COL_DQ, COL_DQ + 512, COL_DV, COL_FQ, COL_FK, COL_FV, COL_FF, COL_CQ, COL_CKV,
            COL_KR, COL_SQ, COL_SK, COL_SV, COL_GATES]
    segs, src = [], 0
    for wd, dst in zip(widths, dsts):
        segs.append((src, dst, wd))
        src += wd
    return segs


def _w_in_body(src_ref, valid_ref, wt_hbm, o_ref, buf, sem, *, layer):
    b = pl.program_id(0)
    slot = b % 2
    rows = buf.shape[1]

    def copy(blk, s):
        return pltpu.make_async_copy(wt_hbm.at[pl.ds(src_ref[blk], rows), layer, :],
                                     buf.at[s], sem.at[s])

    @pl.when(b == 0)
    def _():
        copy(0, 0).start()

    @pl.when(b + 1 < pl.num_programs(0))
    def _():
        copy(b + 1, 1 - slot).start()

    copy(b, slot).wait()
    r = lax.broadcasted_iota(jnp.int32, buf.shape[1:], 0)
    o_ref[...] = jnp.where(r < valid_ref[b], buf[slot], 0.0).astype(o_ref.dtype)


def _reorder_w_in(w_t, layer):
    n_in, _, d = w_t.shape
    src, valid = [0] * (U_WIDTH // LANES), [0] * (U_WIDTH // LANES)
    for s0, dst, wd in _w_in_segments(d):
        for off in range(0, wd, LANES):
            blk = (dst + off) // LANES
            src[blk], valid[blk] = s0 + off, min(LANES, wd - off)
    assert max(s + LANES for s in src) <= n_in
    return pl.pallas_call(
        functools.partial(_w_in_body, layer=layer),
        out_shape=jax.ShapeDtypeStruct((U_WIDTH, d), jnp.bfloat16),
        grid_spec=pltpu.PrefetchScalarGridSpec(
            num_scalar_prefetch=2, grid=(U_WIDTH // LANES,),
            in_specs=[pl.BlockSpec(memory_space=pl.ANY)],
            out_specs=pl.BlockSpec((LANES, d), lambda b, s, v: (b, 0)),
            scratch_shapes=[pltpu.VMEM((2, LANES, d), jnp.float32),
                            pltpu.SemaphoreType.DMA((2,))]),
        compiler_params=_cparams(("arbitrary",)),
        name="w_in_relayout",
    )(jnp.asarray(src, jnp.int32), jnp.asarray(valid, jnp.int32), w_t)


def _route(logits, b_rg, b_re, t):
    def first_max(v, n):
        top = jnp.max(v, axis=-1, keepdims=True)
        ids = jnp.arange(n, dtype=jnp.int32)
        return top, jnp.min(jnp.where(v == top, ids, n), axis=-1, keepdims=True)

    g_logits = logits[:, :N_GROUPS] + b_rg
    g_max, grp = first_max(g_logits, N_GROUPS)
    p_grp = 1.0 / jnp.sum(jnp.exp(g_logits - g_max), axis=-1, keepdims=True)
    e_logits = (logits[:, N_GROUPS:N_GROUPS + N_EXPERTS] + b_re).reshape(
        t, N_GROUPS, EXPERTS_PER_GROUP)
    in_grp = jnp.arange(N_GROUPS, dtype=jnp.int32)[None, :, None] == grp[:, :, None]
    e_in = jnp.sum(jnp.where(in_grp, e_logits, 0.0), axis=1)
    probs = jax.nn.softmax(e_in, axis=-1)
    p1, i1 = first_max(probs, EXPERTS_PER_GROUP)
    rest = jnp.where(jnp.arange(EXPERTS_PER_GROUP, dtype=jnp.int32)[None, :] == i1, -1.0, probs)
    p2, i2 = first_max(rest, EXPERTS_PER_GROUP)
    top_p = jnp.concatenate([p1, p2], axis=-1)
    weights = p_grp * top_p / jnp.sum(top_p, axis=-1, keepdims=True)
    expert_idx = grp * EXPERTS_PER_GROUP + jnp.concatenate([i1, i2], axis=-1)
    return expert_idx.astype(jnp.int32), weights


def _dispatch_plan(expert_idx, t):
    a = t * TOP_K
    flat_e = expert_idx.reshape(a)
    onehot = (flat_e[:, None] == jnp.arange(N_EXPERTS)[None, :]).astype(jnp.int32)
    sizes = jnp.sum(onehot, axis=0)
    padded = (sizes + MOE_ROWS - 1) // MOE_ROWS * MOE_ROWS
    pad_end = jnp.cumsum(padded)
    pad_start = pad_end - padded
    dest = (jnp.sum(onehot * (jnp.cumsum(onehot, axis=0) + pad_start[None, :]), axis=1)
            - 1).astype(jnp.int32)
    n_rows = a + N_EXPERTS * MOE_ROWS
    n_blk = n_rows // MOE_ROWS
    flat_tok = jnp.arange(a, dtype=jnp.int32) // TOP_K
    row_tok = jnp.zeros((n_rows,), jnp.int32).at[dest].set(flat_tok)
    blk_start = jnp.arange(n_blk, dtype=pad_end.dtype) * MOE_ROWS
    blk_e = jnp.minimum(jnp.sum(pad_end[None, :] <= blk_start[:, None], axis=1),
                        N_EXPERTS - 1).astype(jnp.int32)
    n_used = (pad_end[-1] // MOE_ROWS).astype(jnp.int32).reshape(1)
    first = jnp.concatenate([jnp.ones((1,), jnp.int32),
                             (blk_e[1:] != blk_e[:-1]).astype(jnp.int32)])
    parity = (jnp.cumsum(first) - 1) % 2
    experts = jnp.arange(N_EXPERTS, dtype=jnp.int32)
    used = jnp.where(padded > 0, experts, N_EXPERTS)
    later = jnp.flip(lax.cummin(jnp.flip(used)))
    next_used = jnp.concatenate([later[1:], jnp.full((1,), N_EXPERTS, jnp.int32)])
    next_used = jnp.where(next_used < N_EXPERTS, next_used, -1)
    sched = jnp.stack([blk_e, first, parity.astype(jnp.int32),
                       next_used[blk_e].astype(jnp.int32)])
    return dest.reshape(t, TOP_K), row_tok, sched, n_used


def kernel(x, p, positions, w_in, fox_f_bias, diff_lambda, diff_subln, mla_q_norm, mla_kv_norm,
           mla_w_uq, mla_w_ukv, w_branch, w_o, ln1_g, ln1_b, w_router_group, b_router_group,
           w_router_expert, b_router_expert, w_expert_gate, w_expert_up, w_expert_down,
           w_ple_gate, w_ple_proj, ln2_g, ln2_b):
    batch, seq, d = x.shape
    depth = w_in.shape[0]
    t = batch * seq
    alpha = (2 * depth) ** 0.25
    bf = jnp.bfloat16
    tabs = _rope_tables(positions)
    w_t = jnp.transpose(w_in, (2, 0, 1))
    col_scale = jnp.ones((1, U_WIDTH), jnp.float32)
    col_scale = col_scale.at[0, COL_FQ:COL_FQ + BRANCH_WIDTH].set(HEAD_DIM ** -0.5 * LOG2E)
    col_scale = col_scale.at[0, COL_SQ:COL_SQ + BRANCH_WIDTH].set(HEAD_DIM ** -0.5)
    xf = x.reshape(t, d)
    xb = xf.astype(bf)

    for i in range(depth):
        w1 = _reorder_w_in(w_t, i)
        u, f_logits = _matmul(xb, w1, col_scale, bf, 1024, 1536, COL_FF, "in_proj")

        b_f = jnp.zeros((1, LANES), jnp.float32).at[0, :HEADS].set(fox_f_bias[i])
        cum = _fgate_cumsum(f_logits, b_f, batch, seq)[:, :HEADS] * LOG2E
        cum_bh = jnp.transpose(cum.reshape(batch, seq, HEADS), (0, 2, 1)).reshape(
            batch * HEADS, seq)
        cq_rows = cum_bh.reshape(batch * HEADS, 1, seq)
        ck_rep = jnp.broadcast_to(cum_bh[:, :, None], (batch * HEADS, seq, LANES))

        uq = mla_w_uq[i].reshape(MLA_Q_LORA, HEADS, MLA_NOPE + MLA_ROPE)
        uq = jnp.pad(uq, ((0, 0), (0, 0), (0, MLA_QK_PAD - MLA_NOPE - MLA_ROPE)))
        uq = uq.reshape(MLA_Q_LORA, HEADS * MLA_QK_PAD).astype(bf)
        ukv = mla_w_ukv[i].reshape(MLA_KV_LORA, HEADS, MLA_NOPE + HEAD_DIM)
        uk = ukv[:, :, :MLA_NOPE].reshape(MLA_KV_LORA, HEADS * MLA_NOPE).astype(bf)
        uv = ukv[:, :, MLA_NOPE:].reshape(MLA_KV_LORA, HEADS * HEAD_DIM).astype(bf)
        dqk, mq, mk, mv = _prep(u, tabs, mla_q_norm[i].reshape(1, -1),
                                mla_kv_norm[i].reshape(1, -1), uq, uk, uv)

        lam_init = 0.8 - 0.6 * math.exp(-0.3 * i)
        lp = diff_lambda[i].astype(jnp.float32)
        lam = (jnp.exp(jnp.sum(lp[0] * lp[1])) - jnp.exp(jnp.sum(lp[2] * lp[3]))
               + lam_init).reshape(1)
        blk = lambda col: col // HEAD_DIM
        o_d = _attention("diff", batch, seq, dqk, 0, dqk, HEADS, u, blk(COL_DV), HEAD_DIM,
                         extras=(diff_subln[i].reshape(1, HEAD_DIM),),
                         lam=lam, lam_init=lam_init)
        o_f = _attention("fox", batch, seq, u, blk(COL_FQ), u, blk(COL_FK), u, blk(COL_FV),
                         HEAD_DIM, extras=(cq_rows, ck_rep))
        o_m = _attention("mla", batch, seq, mq, 0, mk, 0, mv, 0, MLA_QK_PAD)
        o_s = _attention("sb", batch, seq, u, blk(COL_SQ), u, blk(COL_SK), u, blk(COL_SV),
                         HEAD_DIM)

        w_r = jnp.concatenate([w_router_group[i], w_router_expert[i]], axis=1)
        w_r = jnp.pad(w_r, ((0, 0), (0, LANES - w_r.shape[1])))
        wr_hi = w_r.astype(bf)
        wr_lo = (w_r - wr_hi.astype(jnp.float32)).astype(bf)
        h, h_rows, hb, logits = _merge(
            u, o_d, o_f, o_m, o_s, w_branch[i].astype(bf), w_o[i].astype(bf), xf,
            ln1_g[i].reshape(1, d), ln1_b[i].reshape(1, d), wr_hi, wr_lo, alpha)

        expert_idx, weights = _route(logits, b_router_group[i], b_router_expert[i], t)
        dest, row_tok, sched, n_used = _dispatch_plan(expert_idx, t)
        y_rows = _moe(sched, n_used, row_tok, h_rows, w_expert_gate, w_expert_up,
                      w_expert_down, i)
        xf, xb = _final(dest[:, 0], dest[:, 1], h, hb, y_rows, weights.astype(jnp.float32),
                        p[i].reshape(t, -1), w_ple_gate[i].astype(bf),
                        w_ple_proj[i].astype(bf), ln2_g[i].reshape(1, d),
                        ln2_b[i].reshape(1, d), alpha)
    return xf.reshape(batch, seq, d)
```
